```python
import math
import jax, jax.numpy as jnp
from jax import lax
import numpy as np

D_MODEL = 1024
BATCH = 16
SEQ = 2048
DEPTH = 2

GRID_W = 64
ROPE_THETA = 10000.0
Q_BLOCK = 128
EPS = 1e-6
PLE_DIM = 256

MLA_HEADS = 8
MLA_Q_LORA = 256
MLA_KV_LORA = 128
MLA_NOPE = 64
MLA_ROPE = 32
MLA_V = 64

GLA_HEADS = 4
GLA_DK = 64
GLA_DV = 128
GLA_GATE_RANK = 16
GLA_GATE_NORM = 16.0
GLA_CHUNK = 64

GQA_HEADS = 16
GQA_KV_HEADS = 4
GQA_HEAD_DIM = 64

N_EXPERTS = 16
EC_CAPACITY_FACTOR = 2
EXPERT_FF = 1024

D_MIX = MLA_HEADS * MLA_V + GLA_HEADS * GLA_DV
EVEN_SPLITS = (MLA_Q_LORA, MLA_KV_LORA, MLA_ROPE,
               GLA_HEADS * GLA_DK, GLA_HEADS * GLA_DK, GLA_HEADS * GLA_DV,
               2 * GLA_GATE_RANK, GLA_HEADS * GLA_DV)
ODD_SPLITS = (GQA_HEADS * GQA_HEAD_DIM, GQA_KV_HEADS * GQA_HEAD_DIM, GQA_KV_HEADS * GQA_HEAD_DIM)
N_EVEN = (DEPTH + 1) // 2
N_ODD = DEPTH // 2
DEEPNORM_ALPHA = (2.0 * DEPTH) ** 0.25
DEEPNORM_BETA = (8.0 * DEPTH) ** -0.25

kernel_name = "hybrid_mla_gla_gqa_ecmoe_deepnorm"


def _split(h, sizes):
    offs = np.cumsum(sizes)[:-1].tolist()
    return jnp.split(h, offs, axis=-1)


def rms_norm(x, g):
    x32 = x.astype(jnp.float32)
    y = x32 * lax.rsqrt(jnp.mean(x32 * x32, axis=-1, keepdims=True) + EPS)
    return (y * g.astype(jnp.float32)).astype(x.dtype)


def layer_norm(x, g, b):
    x32 = x.astype(jnp.float32)
    mu = jnp.mean(x32, axis=-1, keepdims=True)
    xc = x32 - mu
    var = jnp.mean(xc * xc, axis=-1, keepdims=True)
    y = xc * lax.rsqrt(var + EPS) * g.astype(jnp.float32) + b.astype(jnp.float32)
    return y.astype(x.dtype)


def axial_rope(seq, rot_dim):
    rows = seq // GRID_W
    row = jnp.repeat(jnp.arange(rows, dtype=jnp.float32), GRID_W)
    col = jnp.tile(jnp.arange(GRID_W, dtype=jnp.float32), rows)
    axis_dim = rot_dim // 2
    inv = ROPE_THETA ** (-jnp.arange(0, axis_dim, 2, dtype=jnp.float32) / axis_dim)
    ang = jnp.concatenate([row[:, None] * inv, col[:, None] * inv], axis=-1)
    return jnp.cos(ang), jnp.sin(ang)


def apply_rope(x, cos, sin):
    half = x.shape[-1] // 2
    c = cos.astype(x.dtype)
    s = sin.astype(x.dtype)
    x1, x2 = x[..., :half], x[..., half:]
    return jnp.concatenate([x1 * c - x2 * s, x1 * s + x2 * c], axis=-1)


def _to_blocks(t):
    b, s = t.shape[:2]
    nb = s // Q_BLOCK
    return t.reshape((b, nb, Q_BLOCK) + t.shape[2:]).swapaxes(0, 1)


def _from_blocks(t):
    nb, b, qb = t.shape[:3]
    return t.swapaxes(0, 1).reshape((b, nb * qb) + t.shape[3:])


def mla_attention(q_nope, q_pe, k_nope, k_pe, v):
    scale = (MLA_NOPE + MLA_ROPE) ** -0.5

    def block(args):
        qn, qp = args
        s = (jnp.einsum('bqhd,bkhd->bhqk', qn, k_nope)
             + jnp.einsum('bqhd,bkd->bhqk', qp, k_pe)) * scale
        pr = jax.nn.softmax(s.astype(jnp.float32), axis=-1).astype(v.dtype)
        return jnp.einsum('bhqk,bkhd->bqhd', pr, v)

    o = lax.map(block, (_to_blocks(q_nope), _to_blocks(q_pe)))
    return _from_blocks(o)


def gla_chunked(q, k, v, log_a):
    b, s, h, dk = q.shape
    dv = v.shape[-1]
    n = s // GLA_CHUNK
    L = GLA_CHUNK

    def chunk(t):
        return t.astype(jnp.float32).reshape(b, n, L, h, t.shape[-1]).transpose(0, 3, 1, 2, 4)

    qc, kc, vc, lac = chunk(q), chunk(k), chunk(v), chunk(log_a)
    cum = jnp.cumsum(lac, axis=-2)
    qg = qc * jnp.exp(cum)
    kg = kc * jnp.exp(-cum)
    tri = jnp.tril(jnp.ones((L, L), dtype=bool))
    att = jnp.where(tri, jnp.einsum('bhnld,bhnmd->bhnlm', qg, kg), 0.0)
    o_intra = jnp.einsum('bhnlm,bhnme->bhnle', att, vc)
    last = cum[..., -1:, :]
    u = jnp.einsum('bhnld,bhnle->bhnde', kc * jnp.exp(last - cum), vc)
    decay = jnp.exp(last[..., 0, :])

    def step(state, inp):
        d, du = inp
        return d[..., None] * state + du, state

    init = jnp.zeros((b, h, dk, dv), jnp.float32)
    _, s_prev = lax.scan(step, init, (jnp.moveaxis(decay, 2, 0), jnp.moveaxis(u, 2, 0)))
    o_inter = jnp.einsum('bhnld,nbhde->bhnle', qg, s_prev)
    o = (o_intra + o_inter).transpose(0, 2, 3, 1, 4).reshape(b, s, h, dv)
    return o.astype(v.dtype)


def even_mixer(x, w_in, mla_q_norm, w_uq, mla_kv_norm, w_ukv,
               gla_gate_w_fwd, gla_gate_b_fwd, gla_gate_w_bwd, gla_gate_b_bwd, gla_norm,
               cos_a, sin_a):
    b, s, _ = x.shape
    h = x @ w_in
    c_q, c_kv, k_pe, gq, gk, gv, g_lr, gr = _split(h, EVEN_SPLITS)

    q = (rms_norm(c_q, mla_q_norm) @ w_uq).reshape(b, s, MLA_HEADS, MLA_NOPE + MLA_ROPE)
    q_nope = q[..., :MLA_NOPE]
    q_pe = apply_rope(q[..., MLA_NOPE:], cos_a[:, None, :], sin_a[:, None, :])
    kv = (rms_norm(c_kv, mla_kv_norm) @ w_ukv).reshape(b, s, MLA_HEADS, MLA_NOPE + MLA_V)
    k_nope, v_mla = kv[..., :MLA_NOPE], kv[..., MLA_NOPE:]
    k_pe = apply_rope(k_pe, cos_a, sin_a)
    o_mla = mla_attention(q_nope, q_pe, k_nope, k_pe, v_mla).reshape(b, s, MLA_HEADS * MLA_V)

    qh = (gq * (GLA_DK ** -0.5)).reshape(b, s, GLA_HEADS, GLA_DK)
    kh = gk.reshape(b, s, GLA_HEADS, GLA_DK)
    vh = gv.reshape(b, s, GLA_HEADS, GLA_DV)
    lr_f, lr_b = g_lr[..., :GLA_GATE_RANK], g_lr[..., GLA_GATE_RANK:]
    la_f = (jax.nn.log_sigmoid((lr_f @ gla_gate_w_fwd + gla_gate_b_fwd).astype(jnp.float32))
            / GLA_GATE_NORM).reshape(b, s, GLA_HEADS, GLA_DK)
    la_b = (jax.nn.log_sigmoid((lr_b @ gla_gate_w_bwd + gla_gate_b_bwd).astype(jnp.float32))
            / GLA_GATE_NORM).reshape(b, s, GLA_HEADS, GLA_DK)
    o_f = gla_chunked(qh, kh, vh, la_f)
    flip = lambda t: jnp.flip(t, axis=1)
    o_b = flip(gla_chunked(flip(qh), flip(kh), flip(vh), flip(la_b)))
    o_gla = rms_norm(o_f + o_b, gla_norm).reshape(b, s, GLA_HEADS * GLA_DV) * jax.nn.silu(gr)

    return jnp.concatenate([o_mla, o_gla], axis=-1)


def odd_mixer(x, w_in, gqa_q_norm, gqa_k_norm, cos_c, sin_c):
    b, s, _ = x.shape
    g = GQA_HEADS // GQA_KV_HEADS
    q, k, v = _split(x @ w_in, ODD_SPLITS)
    q = rms_norm(q.reshape(b, s, GQA_HEADS, GQA_HEAD_DIM), gqa_q_norm)
    k = rms_norm(k.reshape(b, s, GQA_KV_HEADS, GQA_HEAD_DIM), gqa_k_norm)
    v = v.reshape(b, s, GQA_KV_HEADS, GQA_HEAD_DIM)
    q = apply_rope(q, cos_c[:, None, :], sin_c[:, None, :]).reshape(b, s, GQA_KV_HEADS, g, GQA_HEAD_DIM)
    k = apply_rope(k, cos_c[:, None, :], sin_c[:, None, :])
    scale = GQA_HEAD_DIM ** -0.5

    def block(qb):
        sc = jnp.einsum('bqkgd,bskd->bkgqs', qb, k) * scale
        pr = jax.nn.softmax(sc.astype(jnp.float32), axis=-1).astype(v.dtype)
        return jnp.einsum('bkgqs,bskd->bqkgd', pr, v)

    o = _from_blocks(lax.map(block, _to_blocks(q)))
    return o.reshape(b, s, GQA_HEADS * GQA_HEAD_DIM)


def expert_choice_moe(x, router_w, w1, w3, w2):
    b, s, d = x.shape
    cap = EC_CAPACITY_FACTOR * s // N_EXPERTS
    aff = jax.nn.softmax(jnp.einsum('bsd,de->bse', x, router_w).astype(jnp.float32), axis=-1)
    gates, idx = lax.top_k(aff.transpose(0, 2, 1), cap)
    xg = jax.vmap(lambda xb, ib: xb[ib])(x, idx)
    hid = jax.nn.silu(jnp.einsum('becd,edf->becf', xg, w1)) * jnp.einsum('becd,edf->becf', xg, w3)
    ye = jnp.einsum('becf,efd->becd', hid, w2) * gates[..., None].astype(x.dtype)
    return jax.vmap(lambda yb, ib: jnp.zeros((s, d), x.dtype).at[ib.reshape(-1)].add(yb.reshape(-1, d)))(ye, idx)


def setup_inputs(seed: int = 0) -> dict:
    key = jax.random.key(seed)
    ks = iter(jax.random.split(key, 40))
    f32 = jnp.float32

    def nrm(shape, scale):
        return jax.random.normal(next(ks), shape, f32) * scale

    def gain(shape):
        return 1.0 + 0.02 * jax.random.normal(next(ks), shape, f32)

    d_even_in = sum(EVEN_SPLITS)
    d_odd_in = sum(ODD_SPLITS)
    NE, NO = N_EVEN, N_ODD
    return {
        "x": nrm((BATCH, SEQ, D_MODEL), 1.0),
        "p": nrm((DEPTH, BATCH, SEQ, PLE_DIM), 1.0),
        "w_in_even": nrm((NE, D_MODEL, d_even_in), D_MODEL ** -0.5),
        "mla_q_norm": gain((NE, MLA_Q_LORA)),
        "w_uq": nrm((NE, MLA_Q_LORA, MLA_HEADS * (MLA_NOPE + MLA_ROPE)), MLA_Q_LORA ** -0.5),
        "mla_kv_norm": gain((NE, MLA_KV_LORA)),
        "w_ukv": nrm((NE, MLA_KV_LORA, MLA_HEADS * (MLA_NOPE + MLA_V)), MLA_KV_LORA ** -0.5),
        "gla_gate_w_fwd": nrm((NE, GLA_GATE_RANK, GLA_HEADS * GLA_DK), GLA_GATE_RANK ** -0.5),
        "gla_gate_b_fwd": nrm((NE, GLA_HEADS * GLA_DK), 0.1),
        "gla_gate_w_bwd": nrm((NE, GLA_GATE_RANK, GLA_HEADS * GLA_DK), GLA_GATE_RANK ** -0.5),
        "gla_gate_b_bwd": nrm((NE, GLA_HEADS * GLA_DK), 0.1),
        "gla_norm": gain((NE, GLA_DV)),
        "w_in_odd": nrm((NO, D_MODEL, d_odd_in), D_MODEL ** -0.5),
        "gqa_q_norm": gain((NO, GQA_HEAD_DIM)),
        "gqa_k_norm": gain((NO, GQA_HEAD_DIM)),
        "w_o": nrm((DEPTH, D_MIX, D_MODEL), D_MIX ** -0.5 * DEEPNORM_BETA),
        "ln1_g": gain((DEPTH, D_MODEL)),
        "ln1_b": nrm((DEPTH, D_MODEL), 0.02),
        "router_w": nrm((DEPTH, D_MODEL, N_EXPERTS), D_MODEL ** -0.5),
        "w1": nrm((DEPTH, N_EXPERTS, D_MODEL, EXPERT_FF), D_MODEL ** -0.5),
        "w3": nrm((DEPTH, N_EXPERTS, D_MODEL, EXPERT_FF), D_MODEL ** -0.5),
        "w2": nrm((DEPTH, N_EXPERTS, EXPERT_FF, D_MODEL), EXPERT_FF ** -0.5 * DEEPNORM_BETA),
        "ple_gate_w": nrm((DEPTH, D_MODEL, D_MODEL), D_MODEL ** -0.5),
        "ple_gate_b": nrm((DEPTH, D_MODEL), 0.02),
        "ple_w": nrm((DEPTH, PLE_DIM, D_MODEL), PLE_DIM ** -0.5 * DEEPNORM_BETA),
        "ln2_g": gain((DEPTH, D_MODEL)),
        "ln2_b": nrm((DEPTH, D_MODEL), 0.02),
    }


def reference(x, p, w_in_even, mla_q_norm, w_uq, mla_kv_norm, w_ukv,
              gla_gate_w_fwd, gla_gate_b_fwd, gla_gate_w_bwd, gla_gate_b_bwd, gla_norm,
              w_in_odd, gqa_q_norm, gqa_k_norm,
              w_o, ln1_g, ln1_b, router_w, w1, w3, w2,
              ple_gate_w, ple_gate_b, ple_w, ln2_g, ln2_b):
    seq = x.shape[1]
    cos_a, sin_a = axial_rope(seq, MLA_ROPE)
    cos_c, sin_c = axial_rope(seq, GQA_HEAD_DIM)
    for i in range(DEPTH):
        if i % 2 == 0:
            j = i // 2
            mix = even_mixer(x, w_in_even[j], mla_q_norm[j], w_uq[j], mla_kv_norm[j], w_ukv[j],
                             gla_gate_w_fwd[j], gla_gate_b_fwd[j], gla_gate_w_bwd[j], gla_gate_b_bwd[j],
                             gla_norm[j], cos_a, sin_a)
        else:
            j = i // 2
            mix = odd_mixer(x, w_in_odd[j], gqa_q_norm[j], gqa_k_norm[j], cos_c, sin_c)
        x = layer_norm(DEEPNORM_ALPHA * x + mix @ w_o[i], ln1_g[i], ln1_b[i])
        ffn = expert_choice_moe(x, router_w[i], w1[i], w3[i], w2[i])
        ple = jax.nn.sigmoid(x @ ple_gate_w[i] + ple_gate_b[i]) * (p[i] @ ple_w[i])
        x = layer_norm(DEEPNORM_ALPHA * x + ffn + ple, ln2_g[i], ln2_b[i])
    return x
```

```python
import functools
import math

import jax
import jax.numpy as jnp
import numpy as np
from jax import lax
from jax.experimental import pallas as pl
from jax.experimental.pallas import tpu as pltpu

F32 = jnp.float32
BF16 = jnp.bfloat16

V7X_LANES = 128
V7X_VMEM_BYTES = 64 * 1024 * 1024
VMEM_LIMIT = 56 * 1024 * 1024

GRID_W = 64
ROPE_THETA = 10000.0
EPS = 1e-6
MLA_HEADS, MLA_Q_LORA, MLA_KV_LORA = 8, 256, 128
MLA_NOPE, MLA_ROPE, MLA_V = 64, 32, 64
GLA_HEADS, GLA_DK, GLA_DV = 4, 64, 128
GLA_GATE_RANK, GLA_GATE_NORM, GLA_CHUNK = 16, 16.0, 64
GQA_HEADS, GQA_KV_HEADS, GQA_HEAD_DIM = 16, 4, 64
N_EXPERTS, EC_CAPACITY_FACTOR = 16, 2
EVEN_SPLITS = (MLA_Q_LORA, MLA_KV_LORA, MLA_ROPE, GLA_HEADS * GLA_DK, GLA_HEADS * GLA_DK,
               GLA_HEADS * GLA_DV, 2 * GLA_GATE_RANK, GLA_HEADS * GLA_DV)
ODD_SPLITS = (GQA_HEADS * GQA_HEAD_DIM, GQA_KV_HEADS * GQA_HEAD_DIM, GQA_KV_HEADS * GQA_HEAD_DIM)
MLA_PAD = V7X_LANES
PE_LO, PE_HI = MLA_NOPE, MLA_NOPE + MLA_ROPE


def _dot(a, b):
    return jnp.dot(a, b, preferred_element_type=F32)


def _dot_nt(a, b):
    return lax.dot_general(a, b, (((1,), (1,)), ((), ())), preferred_element_type=F32)


def _dot_tn(a, b):
    return lax.dot_general(a, b, (((0,), (0,)), ((), ())), preferred_element_type=F32)


def _split3(x):
    hi = x.astype(BF16)
    r = x - hi.astype(F32)
    mid = r.astype(BF16)
    lo = (r - mid.astype(F32)).astype(BF16)
    return hi, mid, lo


def _dot_sel(sel, x):
    hi, mid, lo = _split3(x)
    return _dot(sel, hi) + _dot(sel, mid) + _dot(sel, lo)


def _dot_x_sel(x, sel):
    hi, mid, lo = _split3(x)
    return _dot(hi, sel) + _dot(mid, sel) + _dot(lo, sel)


def _params(sem):
    return pltpu.CompilerParams(dimension_semantics=sem, vmem_limit_bytes=VMEM_LIMIT)


def _rot_half(x, half):
    w = x.shape[-1]
    lane = lax.broadcasted_iota(jnp.int32, x.shape, x.ndim - 1)
    first = (lane % (2 * half)) < half
    return jnp.where(first, pltpu.roll(x, w - half, x.ndim - 1), pltpu.roll(x, half, x.ndim - 1))


def _layer_norm(y, g, b):
    mu = jnp.mean(y, axis=-1, keepdims=True)
    yc = y - mu
    var = jnp.mean(yc * yc, axis=-1, keepdims=True)
    return yc * lax.rsqrt(var + EPS) * g + b


def _rms(x, g):
    return x * lax.rsqrt(jnp.mean(x * x, axis=-1, keepdims=True) + EPS) * g


def _log_sigmoid(z):
    return jnp.minimum(z, 0.0) - jnp.log1p(jnp.exp(-jnp.abs(z)))


def _sigmoid(z):
    return 1.0 / (1.0 + jnp.exp(-z))


def _even_proj_kernel(x_ref, w_in_ref, qn_ref, w_uq_ref, kvn_ref, w_uk_ref, w_uv_ref,
                      w_gate_ref, b_gate_ref, cq_ref, sq_ref,
                      q_ref, k_ref, v_ref, gq_ref, gk_ref, gv_ref, la_ref, gr_ref):
    h = _dot(x_ref[...].astype(BF16), w_in_ref[...])
    cos = cq_ref[...]
    sin = sq_ref[...]
    lane = lax.broadcasted_iota(jnp.int32, cos.shape, 1)
    pe_lane = (lane >= PE_LO) & (lane < PE_HI)

    def rope(t):
        return t * cos + _rot_half(t, MLA_ROPE // 2) * sin

    c_q = _rms(h[:, 0:256], qn_ref[...])
    q = _dot(c_q.astype(BF16), w_uq_ref[...]) * ((MLA_NOPE + MLA_ROPE) ** -0.5)
    c_kv = _rms(h[:, 256:384], kvn_ref[...]).astype(BF16)
    kn = _dot(c_kv, w_uk_ref[...])
    v_ref[...] = _dot(c_kv, w_uv_ref[...]).astype(v_ref.dtype)
    chunk = h[:, 384:512]
    k_pe = jnp.where(pe_lane, rope(chunk), 0.0)
    for hd in range(MLA_HEADS):
        sl = slice(hd * MLA_PAD, (hd + 1) * MLA_PAD)
        q_ref[:, sl] = rope(q[:, sl]).astype(q_ref.dtype)
        k_ref[:, sl] = (kn[:, sl] + k_pe).astype(k_ref.dtype)
    z = _dot(chunk.astype(BF16), w_gate_ref[...]) + b_gate_ref[...]
    la_ref[...] = _log_sigmoid(z) * (1.0 / GLA_GATE_NORM)
    gq_ref[...] = h[:, 512:768] * (GLA_DK ** -0.5)
    gk_ref[...] = h[:, 768:1024]
    gv_ref[...] = h[:, 1024:1536]
    gr = h[:, 1536:2048]
    gr_ref[...] = gr * _sigmoid(gr)


def _even_proj(x2, w_in_p, qn, w_uq_p, kvn, w_uk_p, w_uv, w_gate, b_gate, cq, sq, seq, tm):
    t, d = x2.shape
    nsb = seq // tm
    row = lambda i: (i, 0)
    fixed = lambda i: (0, 0)
    pos = lambda i: (i % nsb, 0)
    full = lambda a: pl.BlockSpec(a.shape, fixed)
    outs = [(1024, BF16), (1024, BF16), (512, BF16), (256, F32), (256, F32), (512, F32), (512, F32), (512, F32)]
    return pl.pallas_call(
        _even_proj_kernel,
        grid=(t // tm,),
        in_specs=[pl.BlockSpec((tm, d), row), full(w_in_p), full(qn), full(w_uq_p), full(kvn),
                  full(w_uk_p), full(w_uv), full(w_gate), full(b_gate),
                  pl.BlockSpec((tm, MLA_PAD), pos), pl.BlockSpec((tm, MLA_PAD), pos)],
        out_specs=[pl.BlockSpec((tm, n), row) for n, _ in outs],
        out_shape=[jax.ShapeDtypeStruct((t, n), dt) for n, dt in outs],
        compiler_params=_params(("parallel",)),
        name="even_proj",
    )(x2, w_in_p, qn, w_uq_p, kvn, w_uk_p, w_uv, w_gate, b_gate, cq, sq)


def _mla_attn_kernel(q_ref, k_ref, v_ref, o_ref):
    v = v_ref[0]
    lane = lax.broadcasted_iota(jnp.int32, v.shape, 1)
    acc = None
    for a in range(2):
        sl = slice(a * MLA_PAD, (a + 1) * MLA_PAD)
        s = _dot_nt(q_ref[0, :, sl], k_ref[0, :, sl])
        e = jnp.exp(s - jnp.max(s, axis=-1, keepdims=True))
        inv = 1.0 / jnp.sum(e, axis=-1, keepdims=True)
        v_a = jnp.where((lane // MLA_V) == a, v, jnp.zeros_like(v))
        o = _dot(e.astype(BF16), v_a) * inv
        acc = o if acc is None else acc + o
    o_ref[0] = acc.astype(o_ref.dtype)


def _mla_attn(q, k, v, tq):
    b, s, _ = q.shape
    return pl.pallas_call(
        _mla_attn_kernel,
        grid=(b, MLA_HEADS // 2, s // tq),
        in_specs=[pl.BlockSpec((1, tq, 2 * MLA_PAD), lambda i, j, t: (i, t, j)),
                  pl.BlockSpec((1, s, 2 * MLA_PAD), lambda i, j, t: (i, 0, j)),
                  pl.BlockSpec((1, s, 2 * MLA_V), lambda i, j, t: (i, 0, j))],
        out_specs=pl.BlockSpec((1, tq, 2 * MLA_V), lambda i, j, t: (i, t, j)),
        out_shape=jax.ShapeDtypeStruct((b, s, MLA_HEADS * MLA_V), BF16),
        compiler_params=_params(("parallel", "parallel", "parallel")),
        name="mla_attn",
    )(q, k, v)


def _gla_kernel(q_ref, k_ref, la_ref, v_ref, gr_ref, g_ref, o_ref, state_ref):
    hd = pl.program_id(1)
    seq = q_ref.shape[1]
    n_chunks = seq // GLA_CHUNK
    L = GLA_CHUNK
    dkw = GLA_HEADS * GLA_DK
    r_i = lax.broadcasted_iota(jnp.int32, (L, L), 0)
    c_i = lax.broadcasted_iota(jnp.int32, (L, L), 1)
    lane = lax.broadcasted_iota(jnp.int32, (L, dkw), 1)
    head_lane = (lane // GLA_DK) == hd

    def direction(fwd):
        keep = (c_i <= r_i) if fwd else (c_i >= r_i)
        tri = jnp.where(keep, 1.0, 0.0).astype(BF16)
        la_off = 0 if fwd else dkw
        last_row = L - 1 if fwd else 0
        state_ref[...] = jnp.zeros_like(state_ref)

        def body(i, carry):
            n = i if fwd else n_chunks - 1 - i
            rows = pl.ds(pl.multiple_of(n * L, L), L)
            la = la_ref[0, rows, la_off:la_off + dkw]
            cum = _dot_sel(tri, la)
            last = cum[last_row:last_row + 1, :]
            q = q_ref[0, rows, :]
            k = k_ref[0, rows, :]
            v = v_ref[0, rows, :].astype(BF16)
            qg = jnp.where(head_lane, q * jnp.exp(cum), 0.0).astype(BF16)
            kg = (k * jnp.exp(-cum)).astype(BF16)
            att = jnp.where(keep, _dot_nt(qg, kg), 0.0).astype(BF16)
            st = state_ref[...]
            o = _dot(att, v) + _dot_nt(qg, st.astype(BF16))
            kdec = (k * jnp.exp(last - cum)).astype(BF16)
            state_ref[...] = st * jnp.exp(last) + _dot_tn(v, kdec)
            if fwd:
                o_ref[0, rows, :] = o
            else:
                tot = o_ref[0, rows, :] + o
                o_ref[0, rows, :] = _rms(tot, g_ref[...]) * gr_ref[0, rows, :]
            return carry

        lax.fori_loop(0, n_chunks, body, 0)

    direction(True)
    direction(False)


def _gla(gq, gk, la, gv, gr, g_norm):
    b, s, _ = gq.shape
    dkw = GLA_HEADS * GLA_DK
    return pl.pallas_call(
        _gla_kernel,
        grid=(b, GLA_HEADS),
        in_specs=[pl.BlockSpec((1, s, dkw), lambda i, j: (i, 0, 0)),
                  pl.BlockSpec((1, s, dkw), lambda i, j: (i, 0, 0)),
                  pl.BlockSpec((1, s, 2 * dkw), lambda i, j: (i, 0, 0)),
                  pl.BlockSpec((1, s, GLA_DV), lambda i, j: (i, 0, j)),
                  pl.BlockSpec((1, s, GLA_DV), lambda i, j: (i, 0, j)),
                  pl.BlockSpec((1, GLA_DV), lambda i, j: (0, 0))],
        out_specs=pl.BlockSpec((1, s, GLA_DV), lambda i, j: (i, 0, j)),
        out_shape=jax.ShapeDtypeStruct((b, s, GLA_HEADS * GLA_DV), F32),
        scratch_shapes=[pltpu.VMEM((GLA_DV, dkw), F32)],
        compiler_params=_params(("parallel", "parallel")),
        name="gla",
    )(gq, gk, la, gv, gr, g_norm)


def _odd_proj_kernel(x_ref, w_ref, gq_ref, gk_ref, avg_ref, cos_ref, sin_ref, q_ref, k_ref, v_ref):
    h = _dot(x_ref[...].astype(BF16), w_ref[...])
    cos = cos_ref[...]
    sin = sin_ref[...]
    avg = avg_ref[...]
    nq = GQA_HEADS * GQA_HEAD_DIM
    nk = GQA_KV_HEADS * GQA_HEAD_DIM

    def norm_rope(t, g):
        ms = _dot_x_sel(t * t, avg)
        t = t * lax.rsqrt(ms + EPS) * g
        return t * cos + _rot_half(t, GQA_HEAD_DIM // 2) * sin

    for c in range(nq // nk):
        sl = slice(c * nk, (c + 1) * nk)
        q_ref[:, sl] = (norm_rope(h[:, sl], gq_ref[...]) * (GQA_HEAD_DIM ** -0.5)).astype(q_ref.dtype)
    k_ref[...] = norm_rope(h[:, nq:nq + nk], gk_ref[...]).astype(k_ref.dtype)
    v_ref[...] = h[:, nq + nk:].astype(v_ref.dtype)


def _odd_proj(x2, w, gq, gk, avg, cos, sin, seq, tm):
    t, d = x2.shape
    nsb = seq // tm
    row = lambda i: (i, 0)
    fixed = lambda i: (0, 0)
    pos = lambda i: (i % nsb, 0)
    full = lambda a: pl.BlockSpec(a.shape, fixed)
    outs = [ODD_SPLITS[0], ODD_SPLITS[1], ODD_SPLITS[2]]
    return pl.pallas_call(
        _odd_proj_kernel,
        grid=(t // tm,),
        in_specs=[pl.BlockSpec((tm, d), row), full(w), full(gq), full(gk), full(avg),
                  pl.BlockSpec((tm, cos.shape[1]), pos), pl.BlockSpec((tm, sin.shape[1]), pos)],
        out_specs=[pl.BlockSpec((tm, n), row) for n in outs],
        out_shape=[jax.ShapeDtypeStruct((t, n), BF16) for n in outs],
        compiler_params=_params(("parallel",)),
        name="odd_proj",
    )(x2, w, gq, gk, avg, cos, sin)


def _gqa_attn_kernel(q_ref, k_ref, v_ref, o_ref, krep_ref, vexp_ref):
    j = pl.program_id(1)
    group = GQA_HEADS // GQA_KV_HEADS
    width = GQA_KV_HEADS * GQA_HEAD_DIM

    @pl.when(pl.program_id(2) == 0)
    def _():
        r = lax.broadcasted_iota(jnp.int32, (width, width), 0)
        c = lax.broadcasted_iota(jnp.int32, (width, width), 1)
        pick = (r // GQA_HEAD_DIM == j) & (r % GQA_HEAD_DIM == c % GQA_HEAD_DIM)
        for g in range(group):
            only_g = pick & (c // GQA_HEAD_DIM == g)
            vexp_ref[g] = _dot(v_ref[0], jnp.where(only_g, 1.0, 0.0).astype(BF16)).astype(BF16)
        krep_ref[...] = _dot(k_ref[0], jnp.where(pick, 1.0, 0.0).astype(BF16)).astype(BF16)

    q = q_ref[0]
    lane = lax.broadcasted_iota(jnp.int32, q.shape, 1)
    acc = None
    for g in range(group):
        qg = jnp.where(lane // GQA_HEAD_DIM == g, q, jnp.zeros_like(q))
        s = _dot_nt(qg, krep_ref[...])
        e = jnp.exp(s - jnp.max(s, axis=-1, keepdims=True))
        inv = 1.0 / jnp.sum(e, axis=-1, keepdims=True)
        o = _dot(e.astype(BF16), vexp_ref[g]) * inv
        acc = o if acc is None else acc + o
    o_ref[0] = acc.astype(o_ref.dtype)


def _gqa_attn(q, k, v, tq):
    b, s, _ = q.shape
    width = GQA_KV_HEADS * GQA_HEAD_DIM
    group = GQA_HEADS // GQA_KV_HEADS
    return pl.pallas_call(
        _gqa_attn_kernel,
        grid=(b, GQA_KV_HEADS, s // tq),
        in_specs=[pl.BlockSpec((1, tq, width), lambda i, j, t: (i, t, j)),
                  pl.BlockSpec((1, s, width), lambda i, j, t: (i, 0, 0)),
                  pl.BlockSpec((1, s, width), lambda i, j, t: (i, 0, 0))],
        out_specs=pl.BlockSpec((1, tq, width), lambda i, j, t: (i, t, j)),
        out_shape=jax.ShapeDtypeStruct((b, s, GQA_HEADS * GQA_HEAD_DIM), BF16),
        scratch_shapes=[pltpu.VMEM((s, width), BF16), pltpu.VMEM((group, s, width), BF16)],
        compiler_params=_params(("parallel", "parallel", "arbitrary")),
        name="gqa_attn",
    )(q, k, v)


def _out_ln_kernel(alpha, x_ref, ma_ref, mb_ref, wa_ref, wb_ref, g_ref, b_ref, y_ref, yb_ref):
    y = alpha * x_ref[...] + _dot(ma_ref[...].astype(BF16), wa_ref[...]) \
        + _dot(mb_ref[...].astype(BF16), wb_ref[...])
    y = _layer_norm(y, g_ref[...], b_ref[...])
    y_ref[...] = y
    yb_ref[...] = y.astype(BF16)


def _out_ln(x2, mix_a, mix_b, col_a, col_b, w_a, w_b, g, b, alpha, tm):
    t, d = x2.shape
    half = w_a.shape[0]
    row = lambda i: (i, 0)
    fixed = lambda i: (0, 0)
    return pl.pallas_call(
        functools.partial(_out_ln_kernel, alpha),
        grid=(t // tm,),
        in_specs=[pl.BlockSpec((tm, d), row),
                  pl.BlockSpec((tm, half), lambda i: (i, col_a)),
                  pl.BlockSpec((tm, half), lambda i: (i, col_b)),
                  pl.BlockSpec(w_a.shape, fixed), pl.BlockSpec(w_b.shape, fixed),
                  pl.BlockSpec(g.shape, fixed), pl.BlockSpec(b.shape, fixed)],
        out_specs=[pl.BlockSpec((tm, d), row), pl.BlockSpec((tm, d), row)],
        out_shape=[jax.ShapeDtypeStruct((t, d), F32), jax.ShapeDtypeStruct((t, d), BF16)],
        compiler_params=_params(("parallel",)),
        name="out_ln",
    )(x2, mix_a, mix_b, w_a, w_b, g, b)


def _route_kernel(cap, x_ref, rw_ref, aff_ref, slot_ref):
    x = x_ref[0]
    seq = x.shape[0]
    xh, xm, _ = _split3(x)
    wh, wm, _ = _split3(rw_ref[...])
    logits = _dot_nt(wh, xh) + (_dot_nt(wh, xm) + _dot_nt(wm, xh))
    e = jnp.exp(logits - jnp.max(logits, axis=0, keepdims=True))
    aff = e / jnp.sum(e, axis=0, keepdims=True)
    aff_ref[0] = aff
    bits = pltpu.bitcast(aff, jnp.int32)

    def pick(i, thr):
        cand = thr | (jnp.int32(1) << (30 - i))
        cnt = jnp.sum((bits >= cand).astype(jnp.int32), axis=1, keepdims=True)
        return jnp.where(cnt >= cap, cand, thr)

    thr = lax.fori_loop(0, 31, pick, jnp.zeros((bits.shape[0], 1), jnp.int32))
    above = bits > thr
    tie = bits == thr
    need = cap - jnp.sum(above.astype(jnp.int32), axis=1, keepdims=True)

    blk = 256 if seq % 256 == 0 else V7X_LANES
    r = lax.broadcasted_iota(jnp.int32, (blk, blk), 0)
    c = lax.broadcasted_iota(jnp.int32, (blk, blk), 1)
    before = jnp.where(r < c, 1.0, 0.0).astype(BF16)

    def prefix(mask):
        m = jnp.where(mask, 1.0, 0.0).astype(BF16)
        run = jnp.zeros((mask.shape[0], 1), F32)
        parts = []
        for t in range(seq // blk):
            mb = m[:, t * blk:(t + 1) * blk]
            parts.append(_dot(mb, before) + run)
            run = run + jnp.sum(mb.astype(F32), axis=1, keepdims=True)
        return jnp.concatenate(parts, axis=1).astype(jnp.int32)

    chosen = above | (tie & (prefix(tie) < need))
    slot_ref[0] = jnp.where(chosen, prefix(chosen), -1)


def _route(x1, rw_t, cap):
    b, s, d = x1.shape
    e = rw_t.shape[0]
    return pl.pallas_call(
        functools.partial(_route_kernel, cap),
        grid=(b,),
        in_specs=[pl.BlockSpec((1, s, d), lambda i: (i, 0, 0)), pl.BlockSpec((e, d), lambda i: (0, 0))],
        out_specs=[pl.BlockSpec((1, e, s), lambda i: (i, 0, 0)), pl.BlockSpec((1, e, s), lambda i: (i, 0, 0))],
        out_shape=[jax.ShapeDtypeStruct((b, e, s), F32), jax.ShapeDtypeStruct((b, e, s), jnp.int32)],
        compiler_params=_params(("parallel",)),
        name="route",
    )(x1, rw_t)


def _moe_kernel(cap, ts, xb_ref, slot_row_ref, slot_col_ref, aff_col_ref, w1_ref, w3_ref, w2_ref, o_ref):
    e_id = pl.program_id(1)
    seq = xb_ref.shape[1]
    slot_row = slot_row_ref[0, 0]
    c_row = lax.broadcasted_iota(jnp.int32, (cap, seq), 0)
    gather = jnp.where(slot_row == c_row, 1.0, 0.0).astype(BF16)
    xg = _dot(gather, xb_ref[0]).astype(BF16)
    h1 = _dot(xg, w1_ref[0])
    h3 = _dot(xg, w3_ref[0])
    hid = (h1 * _sigmoid(h1) * h3).astype(BF16)
    ye = _dot(hid, w2_ref[0]).astype(BF16)
    c_col = lax.broadcasted_iota(jnp.int32, (ts, cap), 1)
    for t in range(seq // ts):
        rows = slice(t * ts, (t + 1) * ts)
        scatter = jnp.where(slot_col_ref[0, 0, rows, :] == c_col, 1.0, 0.0).astype(BF16)
        contrib = aff_col_ref[0, 0, rows, :] * _dot(scatter, ye)

        @pl.when(e_id == 0)
        def _():
            o_ref[0, rows, :] = contrib

        @pl.when(e_id != 0)
        def _():
            o_ref[0, rows, :] += contrib


def _moe(x1b, slot, aff, w1, w3, w2, cap):
    b, s, d = x1b.shape
    e = slot.shape[1]
    ff = w1.shape[2]
    ts = min(512, s)
    slot_row = slot.reshape(b, e, 1, s)
    slot_col = slot.reshape(b, e, s, 1)
    aff_col = aff.reshape(b, e, s, 1)
    return pl.pallas_call(
        functools.partial(_moe_kernel, cap, ts),
        grid=(b, e),
        in_specs=[pl.BlockSpec((1, s, d), lambda i, j: (i, 0, 0)),
                  pl.BlockSpec((1, 1, 1, s), lambda i, j: (i, j, 0, 0)),
                  pl.BlockSpec((1, 1, s, 1), lambda i, j: (i, j, 0, 0)),
                  pl.BlockSpec((1, 1, s, 1), lambda i, j: (i, j, 0, 0)),
                  pl.BlockSpec((1, d, ff), lambda i, j: (j, 0, 0)),
                  pl.BlockSpec((1, d, ff), lambda i, j: (j, 0, 0)),
                  pl.BlockSpec((1, ff, d), lambda i, j: (j, 0, 0))],
        out_specs=pl.BlockSpec((1, s, d), lambda i, j: (i, 0, 0)),
        out_shape=jax.ShapeDtypeStruct((b, s, d), F32),
        compiler_params=_params(("parallel", "arbitrary")),
        name="moe",
    )(x1b, slot_row, slot_col, aff_col, w1, w3, w2)


def _ple_ln_kernel(alpha, x_ref, xb_ref, f_ref, p_ref, wg_ref, bg_ref, wp_ref, g_ref, b_ref, y_ref):
    gate = _sigmoid(_dot(xb_ref[...], wg_ref[...]) + bg_ref[...])
    ple = gate * _dot(p_ref[...].astype(BF16), wp_ref[...])
    y_ref[...] = _layer_norm(alpha * x_ref[...] + f_ref[...] + ple, g_ref[...], b_ref[...])


def _ple_ln(x1, x1b, ffn, p2, wg, bg, wp, g, b, alpha, tm):
    t, d = x1.shape
    row = lambda i: (i, 0)
    fixed = lambda i: (0, 0)
    full = lambda a: pl.BlockSpec(a.shape, fixed)
    return pl.pallas_call(
        functools.partial(_ple_ln_kernel, alpha),
        grid=(t // tm,),
        in_specs=[pl.BlockSpec((tm, d), row), pl.BlockSpec((tm, d), row), pl.BlockSpec((tm, d), row),
                  pl.BlockSpec((tm, p2.shape[1]), row), full(wg), full(bg), full(wp), full(g), full(b)],
        out_specs=pl.BlockSpec((tm, d), row),
        out_shape=jax.ShapeDtypeStruct((t, d), F32),
        compiler_params=_params(("parallel",)),
        name="ple_ln",
    )(x1, x1b, ffn, p2, wg, bg, wp, g, b)


def _rope_tables(seq, rot_dim, lo, width):
    rows = seq // GRID_W
    row = jnp.repeat(jnp.arange(rows, dtype=F32), GRID_W)
    col = jnp.tile(jnp.arange(GRID_W, dtype=F32), rows)
    axis_dim = rot_dim // 2
    inv = ROPE_THETA ** (-jnp.arange(0, axis_dim, 2, dtype=F32) / axis_dim)
    ang = jnp.concatenate([row[:, None] * inv, col[:, None] * inv], axis=-1)
    cos, sin = jnp.cos(ang), jnp.sin(ang)
    cos2 = jnp.concatenate([cos, cos], axis=-1)
    sin2 = jnp.concatenate([-sin, sin], axis=-1)
    if lo == 0:
        reps = width // rot_dim
        return jnp.tile(cos2, (1, reps)), jnp.tile(sin2, (1, reps))
    pad_l = jnp.ones((seq, lo), F32)
    pad_r = jnp.ones((seq, width - lo - rot_dim), F32)
    cos_t = jnp.concatenate([pad_l, cos2, pad_r], axis=-1)
    sin_t = jnp.concatenate([0 * pad_l, sin2, 0 * pad_r], axis=-1)
    return cos_t, sin_t


def _prep_even(w_in, w_uq, w_ukv, gw_f, gb_f, gw_b, gb_b):
    d = w_in.shape[0]
    offs = np.cumsum(EVEN_SPLITS)[:-1].tolist()
    c_q, c_kv, k_pe, gq, gk, gv, g_lr, gr = jnp.split(w_in, offs, axis=-1)
    z32 = jnp.zeros((d, 32), w_in.dtype)
    chunk = jnp.concatenate([g_lr, z32, k_pe, z32], axis=-1)
    w_in_p = jnp.concatenate([c_q, c_kv, chunk, gq, gk, gv, gr], axis=-1).astype(BF16)
    uq = w_uq.reshape(MLA_Q_LORA, MLA_HEADS, MLA_NOPE + MLA_ROPE)
    uq = jnp.pad(uq, ((0, 0), (0, 0), (0, MLA_PAD - MLA_NOPE - MLA_ROPE)))
    w_uq_p = uq.reshape(MLA_Q_LORA, MLA_HEADS * MLA_PAD).astype(BF16)
    ukv = w_ukv.reshape(MLA_KV_LORA, MLA_HEADS, MLA_NOPE + MLA_V)
    uk = jnp.pad(ukv[:, :, :MLA_NOPE], ((0, 0), (0, 0), (0, MLA_PAD - MLA_NOPE)))
    w_uk_p = uk.reshape(MLA_KV_LORA, MLA_HEADS * MLA_PAD).astype(BF16)
    w_uv = ukv[:, :, MLA_NOPE:].reshape(MLA_KV_LORA, MLA_HEADS * MLA_V).astype(BF16)
    dkw = GLA_HEADS * GLA_DK
    w_gate = jnp.zeros((MLA_PAD, 2 * dkw), F32)
    w_gate = w_gate.at[0:GLA_GATE_RANK, 0:dkw].set(gw_f)
    w_gate = w_gate.at[GLA_GATE_RANK:2 * GLA_GATE_RANK, dkw:].set(gw_b).astype(BF16)
    b_gate = jnp.concatenate([gb_f, gb_b])[None, :]
    return w_in_p, w_uq_p, w_uk_p, w_uv, w_gate, b_gate


def kernel(x, p, w_in_even, mla_q_norm, w_uq, mla_kv_norm, w_ukv, gla_gate_w_fwd, gla_gate_b_fwd,
           gla_gate_w_bwd, gla_gate_b_bwd, gla_norm, w_in_odd, gqa_q_norm, gqa_k_norm, w_o, ln1_g,
           ln1_b, router_w, w1, w3, w2, ple_gate_w, ple_gate_b, ple_w, ln2_g, ln2_b):
    b, s, d = x.shape
    depth = w_o.shape[0]
    t = b * s
    alpha = (2.0 * depth) ** 0.25
    cap = EC_CAPACITY_FACTOR * s // N_EXPERTS
    tm = min(256, s)
    tq = min(256, s)
    half = w_o.shape[1] // 2
    cos_a, sin_a = _rope_tables(s, MLA_ROPE, PE_LO, MLA_PAD)
    cos_c, sin_c = _rope_tables(s, GQA_HEAD_DIM, 0, GQA_KV_HEADS * GQA_HEAD_DIM)
    hw = GQA_KV_HEADS * GQA_HEAD_DIM
    head_of = np.arange(hw) // GQA_HEAD_DIM
    avg = jnp.asarray((head_of[:, None] == head_of[None, :]) / GQA_HEAD_DIM, BF16)

    x2 = x.reshape(t, d)
    for i in range(depth):
        j = i // 2
        if i % 2 == 0:
            w_in_p, w_uq_p, w_uk_p, w_uv, w_gate, b_gate = _prep_even(
                w_in_even[j], w_uq[j], w_ukv[j], gla_gate_w_fwd[j], gla_gate_b_fwd[j],
                gla_gate_w_bwd[j], gla_gate_b_bwd[j])
            q, k, v, gq, gk, gv, la, gr = _even_proj(
                x2, w_in_p, mla_q_norm[j][None, :], w_uq_p, mla_kv_norm[j][None, :], w_uk_p, w_uv,
                w_gate, b_gate, cos_a, sin_a, s, tm)
            r3 = lambda a: a.reshape(b, s, a.shape[-1])
            o_mla = _mla_attn(r3(q), r3(k), r3(v), tq).reshape(t, -1)
            o_gla = _gla(r3(gq), r3(gk), r3(la), r3(gv), r3(gr), gla_norm[j][None, :]).reshape(t, -1)
            mix_a, mix_b, col_a, col_b = o_mla, o_gla, 0, 0
        else:
            q, k, v = _odd_proj(x2, w_in_odd[j].astype(BF16),
                                jnp.tile(gqa_q_norm[j], GQA_KV_HEADS)[None, :],
                                jnp.tile(gqa_k_norm[j], GQA_KV_HEADS)[None, :],
                                avg, cos_c, sin_c, s, tm)
            r3 = lambda a: a.reshape(b, s, a.shape[-1])
            o = _gqa_attn(r3(q), r3(k), r3(v), tq).reshape(t, -1)
            mix_a, mix_b, col_a, col_b = o, o, 0, 1
        wo = w_o[i].astype(BF16)
        x1, x1b = _out_ln(x2, mix_a, mix_b, col_a, col_b, wo[:half], wo[half:],
                          ln1_g[i][None, :], ln1_b[i][None, :], alpha, tm)
        aff, slot = _route(x1.reshape(b, s, d), router_w[i].T, cap)
        ffn = _moe(x1b.reshape(b, s, d), slot, aff, w1[i].astype(BF16), w3[i].astype(BF16),
                   w2[i].astype(BF16), cap)
        x2 = _ple_ln(x1, x1b, ffn.reshape(t, d), p[i].reshape(t, -1), ple_gate_w[i].astype(BF16),
                     ple_gate_b[i][None, :], ple_w[i].astype(BF16), ln2_g[i][None, :], ln2_b[i][None, :],
                     alpha, tm)
    return x2.reshape(b, s, d)
```

```python
import functools
import math

import jax
import jax.numpy as jnp
import numpy as np
from jax import lax
from jax.experimental import pallas as pl
from jax.experimental.pallas import tpu as pltpu

F32 = jnp.float32
BF16 = jnp.bfloat16

V7X_LANES = 128
V7X_VMEM_BYTES = 64 * 1024 * 1024
VMEM_LIMIT = 56 * 1024 * 1024

GRID_W = 64
ROPE_THETA = 10000.0
EPS = 1e-6
MLA_HEADS, MLA_Q_LORA, MLA_KV_LORA = 8, 256, 128
MLA_NOPE, MLA_ROPE, MLA_V = 64, 32, 64
GLA_HEADS, GLA_DK, GLA_DV = 4, 64, 128
GLA_GATE_RANK, GLA_GATE_NORM, GLA_CHUNK = 16, 16.0, 64
GQA_HEADS, GQA_KV_HEADS, GQA_HEAD_DIM = 16, 4, 64
N_EXPERTS, EC_CAPACITY_FACTOR = 16, 2
EVEN_SPLITS = (MLA_Q_LORA, MLA_KV_LORA, MLA_ROPE, GLA_HEADS * GLA_DK, GLA_HEADS * GLA_DK,
               GLA_HEADS * GLA_DV, 2 * GLA_GATE_RANK, GLA_HEADS * GLA_DV)
ODD_SPLITS = (GQA_HEADS * GQA_HEAD_DIM, GQA_KV_HEADS * GQA_HEAD_DIM, GQA_KV_HEADS * GQA_HEAD_DIM)
MLA_PAD = V7X_LANES
PE_LO, PE_HI = MLA_NOPE, MLA_NOPE + MLA_ROPE


def _dot(a, b):
    return jnp.dot(a, b, preferred_element_type=F32)


def _dot_nt(a, b):
    return lax.dot_general(a, b, (((1,), (1,)), ((), ())), preferred_element_type=F32)


def _dot_tn(a, b):
    return lax.dot_general(a, b, (((0,), (0,)), ((), ())), preferred_element_type=F32)


def _split3(x):
    hi = x.astype(BF16)
    r = x - hi.astype(F32)
    mid = r.astype(BF16)
    lo = (r - mid.astype(F32)).astype(BF16)
    return hi, mid, lo


def _dot_sel(sel, x):
    hi, mid, lo = _split3(x)
    return _dot(sel, hi) + _dot(sel, mid) + _dot(sel, lo)


def _dot_x_sel(x, sel):
    hi, mid, lo = _split3(x)
    return _dot(hi, sel) + _dot(mid, sel) + _dot(lo, sel)


def _params(sem):
    return pltpu.CompilerParams(dimension_semantics=sem, vmem_limit_bytes=VMEM_LIMIT)


def _rot_half(x, half):
    w = x.shape[-1]
    lane = lax.broadcasted_iota(jnp.int32, x.shape, x.ndim - 1)
    first = (lane % (2 * half)) < half
    return jnp.where(first, pltpu.roll(x, w - half, x.ndim - 1), pltpu.roll(x, half, x.ndim - 1))


def _layer_norm(y, g, b):
    mu = jnp.mean(y, axis=-1, keepdims=True)
    yc = y - mu
    var = jnp.mean(yc * yc, axis=-1, keepdims=True)
    return yc * lax.rsqrt(var + EPS) * g + b


def _rms(x, g):
    return x * lax.rsqrt(jnp.mean(x * x, axis=-1, keepdims=True) + EPS) * g


def _log_sigmoid(z):
    return jnp.minimum(z, 0.0) - jnp.log1p(jnp.exp(-jnp.abs(z)))


def _sigmoid(z):
    return 1.0 / (1.0 + jnp.exp(-z))


def _even_proj_kernel(x_ref, w_in_ref, qn_ref, w_uq_ref, kvn_ref, w_uk_ref, w_uv_ref,
                      w_gate_ref, b_gate_ref, cq_ref, sq_ref,
                      q_ref, k_ref, v_ref, gq_ref, gk_ref, gv_ref, la_ref, gr_ref):
    h = _dot(x_ref[...].astype(BF16), w_in_ref[...])
    cos = cq_ref[...]
    sin = sq_ref[...]
    lane = lax.broadcasted_iota(jnp.int32, cos.shape, 1)
    pe_lane = (lane >= PE_LO) & (lane < PE_HI)

    def rope(t):
        return t * cos + _rot_half(t, MLA_ROPE // 2) * sin

    c_q = _rms(h[:, 0:256], qn_ref[...])
    q = _dot(c_q.astype(BF16), w_uq_ref[...]) * ((MLA_NOPE + MLA_ROPE) ** -0.5)
    c_kv = _rms(h[:, 256:384], kvn_ref[...]).astype(BF16)
    kn = _dot(c_kv, w_uk_ref[...])
    v_ref[...] = _dot(c_kv, w_uv_ref[...]).astype(v_ref.dtype)
    chunk = h[:, 384:512]
    k_pe = jnp.where(pe_lane, rope(chunk), 0.0)
    for hd in range(MLA_HEADS):
        sl = slice(hd * MLA_PAD, (hd + 1) * MLA_PAD)
        q_ref[:, sl] = rope(q[:, sl]).astype(q_ref.dtype)
        k_ref[:, sl] = (kn[:, sl] + k_pe).astype(k_ref.dtype)
    z = _dot(chunk.astype(BF16), w_gate_ref[...]) + b_gate_ref[...]
    la = _log_sigmoid(z) * (1.0 / GLA_GATE_NORM)
    tm = la.shape[0]
    r = lax.broadcasted_iota(jnp.int32, (tm, tm), 0)
    c = lax.broadcasted_iota(jnp.int32, (tm, tm), 1)
    same = (r // GLA_CHUNK) == (c // GLA_CHUNK)
    dkw = GLA_HEADS * GLA_DK
    la_ref[:, :dkw] = _dot_sel(jnp.where(same & (c <= r), 1.0, 0.0).astype(BF16), la[:, :dkw])
    la_ref[:, dkw:] = _dot_sel(jnp.where(same & (c >= r), 1.0, 0.0).astype(BF16), la[:, dkw:])
    gq_ref[...] = h[:, 512:768] * (GLA_DK ** -0.5)
    gk_ref[...] = h[:, 768:1024]
    gv_ref[...] = h[:, 1024:1536].astype(gv_ref.dtype)
    gr = h[:, 1536:2048]
    gr_ref[...] = gr * _sigmoid(gr)


def _even_proj(x2, w_in_p, qn, w_uq_p, kvn, w_uk_p, w_uv, w_gate, b_gate, cq, sq, seq, tm):
    t, d = x2.shape
    nsb = seq // tm
    row = lambda i: (i, 0)
    fixed = lambda i: (0, 0)
    pos = lambda i: (i % nsb, 0)
    full = lambda a: pl.BlockSpec(a.shape, fixed)
    outs = [(1024, BF16), (1024, BF16), (512, BF16), (256, F32), (256, F32), (512, BF16), (512, F32), (512, F32)]
    return pl.pallas_call(
        _even_proj_kernel,
        grid=(t // tm,),
        in_specs=[pl.BlockSpec((tm, d), row), full(w_in_p), full(qn), full(w_uq_p), full(kvn),
                  full(w_uk_p), full(w_uv), full(w_gate), full(b_gate),
                  pl.BlockSpec((tm, MLA_PAD), pos), pl.BlockSpec((tm, MLA_PAD), pos)],
        out_specs=[pl.BlockSpec((tm, n), row) for n, _ in outs],
        out_shape=[jax.ShapeDtypeStruct((t, n), dt) for n, dt in outs],
        compiler_params=_params(("parallel",)),
        name="even_proj",
    )(x2, w_in_p, qn, w_uq_p, kvn, w_uk_p, w_uv, w_gate, b_gate, cq, sq)


def _mla_attn_kernel(q_ref, k_ref, v_ref, o_ref):
    v = v_ref[0]
    lane = lax.broadcasted_iota(jnp.int32, v.shape, 1)
    acc = None
    for a in range(2):
        sl = slice(a * MLA_PAD, (a + 1) * MLA_PAD)
        s = _dot_nt(q_ref[0, :, sl], k_ref[0, :, sl])
        e = jnp.exp(s - jnp.max(s, axis=-1, keepdims=True))
        inv = 1.0 / jnp.sum(e, axis=-1, keepdims=True)
        v_a = jnp.where((lane // MLA_V) == a, v, jnp.zeros_like(v))
        o = _dot(e.astype(BF16), v_a) * inv
        acc = o if acc is None else acc + o
    o_ref[0] = acc.astype(o_ref.dtype)


def _mla_attn(q, k, v, tq):
    b, s, _ = q.shape
    return pl.pallas_call(
        _mla_attn_kernel,
        grid=(b, MLA_HEADS // 2, s // tq),
        in_specs=[pl.BlockSpec((1, tq, 2 * MLA_PAD), lambda i, j, t: (i, t, j)),
                  pl.BlockSpec((1, s, 2 * MLA_PAD), lambda i, j, t: (i, 0, j)),
                  pl.BlockSpec((1, s, 2 * MLA_V), lambda i, j, t: (i, 0, j))],
        out_specs=pl.BlockSpec((1, tq, 2 * MLA_V), lambda i, j, t: (i, t, j)),
        out_shape=jax.ShapeDtypeStruct((b, s, MLA_HEADS * MLA_V), BF16),
        compiler_params=_params(("parallel", "parallel", "parallel")),
        name="mla_attn",
    )(q, k, v)


def _gla_kernel(q_ref, k_ref, cum_ref, v_ref, gr_ref, g_ref, o_ref, state_ref, of_ref, ob_ref):
    seq = q_ref.shape[1]
    n_chunks = seq // GLA_CHUNK
    L = GLA_CHUNK
    dkw = GLA_HEADS * GLA_DK
    dvw = GLA_HEADS * GLA_DV

    def iota(shape, dim):
        return lax.broadcasted_iota(jnp.int32, shape, dim)

    row_l = iota((L, dkw), 0)
    col_m = iota((L, dkw), 1) % L
    k_own = (iota((dkw, dkw), 0) // L) == (iota((dkw, dkw), 1) // GLA_DK)
    v_own = (iota((dkw, dvw), 0) // L) == (iota((dkw, dvw), 1) // GLA_DV)
    s_own = (iota((dvw, dkw), 0) // GLA_DV) == (iota((dvw, dkw), 1) // GLA_DK)
    state_ref[...] = jnp.zeros_like(state_ref)

    def body(i, carry):
        for d, fwd in enumerate((True, False)):
            n = i if fwd else n_chunks - 1 - i
            rows = pl.ds(pl.multiple_of(n * L, L), L)
            cum = cum_ref[0, rows, d * dkw:(d + 1) * dkw]
            last = cum[L - 1:L, :] if fwd else cum[0:1, :]
            q = q_ref[0, rows, :]
            k = k_ref[0, rows, :]
            v = v_ref[0, rows, :]
            qe = (q * jnp.exp(cum)).astype(BF16)
            kg = (k * jnp.exp(-cum)).astype(BF16)
            kdec = (k * jnp.exp(last - cum)).astype(BF16)
            k_blk = jnp.where(k_own, jnp.concatenate([kg] * GLA_HEADS, axis=0), jnp.zeros((), BF16))
            att = _dot_nt(qe, k_blk)
            keep = (col_m <= row_l) if fwd else (col_m >= row_l)
            att = jnp.where(keep, att, 0.0).astype(BF16)
            v_blk = jnp.where(v_own, jnp.concatenate([v] * GLA_HEADS, axis=0), jnp.zeros((), BF16))
            st = state_ref[d]
            o = _dot(att, v_blk) + _dot_nt(qe, st.astype(BF16))
            state_ref[d] = st * jnp.exp(last) + jnp.where(s_own, _dot_tn(v, kdec), 0.0)
            if fwd:
                of_ref[rows, :] = o
            else:
                ob_ref[rows, :] = o
        return carry

    lax.fori_loop(0, n_chunks, body, 0, unroll=2)

    def finish(n, carry):
        rows = pl.ds(pl.multiple_of(n * L, L), L)
        tot = of_ref[rows, :] + ob_ref[rows, :]
        for hd in range(GLA_HEADS):
            sl = slice(hd * GLA_DV, (hd + 1) * GLA_DV)
            o_ref[0, rows, sl] = (_rms(tot[:, sl], g_ref[...]) * gr_ref[0, rows, sl]).astype(o_ref.dtype)
        return carry

    lax.fori_loop(0, n_chunks, finish, 0)


def _gla(gq, gk, cum, gv, gr, g_norm):
    b, s, _ = gq.shape
    dkw = GLA_HEADS * GLA_DK
    dvw = GLA_HEADS * GLA_DV
    blk = lambda w: pl.BlockSpec((1, s, w), lambda i: (i, 0, 0))
    return pl.pallas_call(
        _gla_kernel,
        grid=(b,),
        in_specs=[blk(dkw), blk(dkw), blk(2 * dkw), blk(dvw), blk(dvw),
                  pl.BlockSpec((1, GLA_DV), lambda i: (0, 0))],
        out_specs=blk(dvw),
        out_shape=jax.ShapeDtypeStruct((b, s, dvw), BF16),
        scratch_shapes=[pltpu.VMEM((2, dvw, dkw), F32), pltpu.VMEM((s, dvw), F32), pltpu.VMEM((s, dvw), F32)],
        compiler_params=_params(("parallel",)),
        name="gla",
    )(gq, gk, cum, gv, gr, g_norm)


def _odd_proj_kernel(x_ref, w_ref, gq_ref, gk_ref, avg_ref, cos_ref, sin_ref, q_ref, k_ref, v_ref):
    h = _dot(x_ref[...].astype(BF16), w_ref[...])
    cos = cos_ref[...]
    sin = sin_ref[...]
    avg = avg_ref[...]
    nq = GQA_HEADS * GQA_HEAD_DIM
    nk = GQA_KV_HEADS * GQA_HEAD_DIM

    def norm_rope(t, g):
        ms = _dot_x_sel(t * t, avg)
        t = t * lax.rsqrt(ms + EPS) * g
        return t * cos + _rot_half(t, GQA_HEAD_DIM // 2) * sin

    for c in range(nq // nk):
        sl = slice(c * nk, (c + 1) * nk)
        q_ref[:, sl] = (norm_rope(h[:, sl], gq_ref[...]) * (GQA_HEAD_DIM ** -0.5)).astype(q_ref.dtype)
    k_ref[...] = norm_rope(h[:, nq:nq + nk], gk_ref[...]).astype(k_ref.dtype)
    v_ref[...] = h[:, nq + nk:].astype(v_ref.dtype)


def _odd_proj(x2, w, gq, gk, avg, cos, sin, seq, tm):
    t, d = x2.shape
    nsb = seq // tm
    row = lambda i: (i, 0)
    fixed = lambda i: (0, 0)
    pos = lambda i: (i % nsb, 0)
    full = lambda a: pl.BlockSpec(a.shape, fixed)
    outs = [ODD_SPLITS[0], ODD_SPLITS[1], ODD_SPLITS[2]]
    return pl.pallas_call(
        _odd_proj_kernel,
        grid=(t // tm,),
        in_specs=[pl.BlockSpec((tm, d), row), full(w), full(gq), full(gk), full(avg),
                  pl.BlockSpec((tm, cos.shape[1]), pos), pl.BlockSpec((tm, sin.shape[1]), pos)],
        out_specs=[pl.BlockSpec((tm, n), row) for n in outs],
        out_shape=[jax.ShapeDtypeStruct((t, n), BF16) for n in outs],
        compiler_params=_params(("parallel",)),
        name="odd_proj",
    )(x2, w, gq, gk, avg, cos, sin)


def _gqa_attn_kernel(q_ref, k_ref, v_ref, o_ref, krep_ref, vexp_ref):
    j = pl.program_id(1)
    group = GQA_HEADS // GQA_KV_HEADS
    width = GQA_KV_HEADS * GQA_HEAD_DIM

    @pl.when(pl.program_id(2) == 0)
    def _():
        r = lax.broadcasted_iota(jnp.int32, (width, width), 0)
        c = lax.broadcasted_iota(jnp.int32, (width, width), 1)
        pick = (r // GQA_HEAD_DIM == j) & (r % GQA_HEAD_DIM == c % GQA_HEAD_DIM)
        for g in range(group):
            only_g = pick & (c // GQA_HEAD_DIM == g)
            vexp_ref[g] = _dot(v_ref[0], jnp.where(only_g, 1.0, 0.0).astype(BF16)).astype(BF16)
        krep_ref[...] = _dot(k_ref[0], jnp.where(pick, 1.0, 0.0).astype(BF16)).astype(BF16)

    q = q_ref[0]
    lane = lax.broadcasted_iota(jnp.int32, q.shape, 1)
    acc = None
    for g in range(group):
        qg = jnp.where(lane // GQA_HEAD_DIM == g, q, jnp.zeros_like(q))
        s = _dot_nt(qg, krep_ref[...])
        e = jnp.exp(s - jnp.max(s, axis=-1, keepdims=True))
        inv = 1.0 / jnp.sum(e, axis=-1, keepdims=True)
        o = _dot(e.astype(BF16), vexp_ref[g]) * inv
        acc = o if acc is None else acc + o
    o_ref[0] = acc.astype(o_ref.dtype)


def _gqa_attn(q, k, v, tq):
    b, s, _ = q.shape
    width = GQA_KV_HEADS * GQA_HEAD_DIM
    group = GQA_HEADS // GQA_KV_HEADS
    return pl.pallas_call(
        _gqa_attn_kernel,
        grid=(b, GQA_KV_HEADS, s // tq),
        in_specs=[pl.BlockSpec((1, tq, width), lambda i, j, t: (i, t, j)),
                  pl.BlockSpec((1, s, width), lambda i, j, t: (i, 0, 0)),
                  pl.BlockSpec((1, s, width), lambda i, j, t: (i, 0, 0))],
        out_specs=pl.BlockSpec((1, tq, width), lambda i, j, t: (i, t, j)),
        out_shape=jax.ShapeDtypeStruct((b, s, GQA_HEADS * GQA_HEAD_DIM), BF16),
        scratch_shapes=[pltpu.VMEM((s, width), BF16), pltpu.VMEM((group, s, width), BF16)],
        compiler_params=_params(("parallel", "parallel", "arbitrary")),
        name="gqa_attn",
    )(q, k, v)


def _out_ln_kernel(alpha, x_ref, ma_ref, mb_ref, wa_ref, wb_ref, g_ref, b_ref, y_ref, yb_ref):
    y = alpha * x_ref[...] + _dot(ma_ref[...].astype(BF16), wa_ref[...]) \
        + _dot(mb_ref[...].astype(BF16), wb_ref[...])
    y = _layer_norm(y, g_ref[...], b_ref[...])
    y_ref[...] = y
    yb_ref[...] = y.astype(BF16)


def _out_ln(x2, mix_a, mix_b, col_a, col_b, w_a, w_b, g, b, alpha, tm):
    t, d = x2.shape
    half = w_a.shape[0]
    row = lambda i: (i, 0)
    fixed = lambda i: (0, 0)
    return pl.pallas_call(
        functools.partial(_out_ln_kernel, alpha),
        grid=(t // tm,),
        in_specs=[pl.BlockSpec((tm, d), row),
                  pl.BlockSpec((tm, half), lambda i: (i, col_a)),
                  pl.BlockSpec((tm, half), lambda i: (i, col_b)),
                  pl.BlockSpec(w_a.shape, fixed), pl.BlockSpec(w_b.shape, fixed),
                  pl.BlockSpec(g.shape, fixed), pl.BlockSpec(b.shape, fixed)],
        out_specs=[pl.BlockSpec((tm, d), row), pl.BlockSpec((tm, d), row)],
        out_shape=[jax.ShapeDtypeStruct((t, d), F32), jax.ShapeDtypeStruct((t, d), BF16)],
        compiler_params=_params(("parallel",)),
        name="out_ln",
    )(x2, mix_a, mix_b, w_a, w_b, g, b)


def _route_kernel(cap, x_ref, rw_ref, slot_ref, slot_t_ref, aff_t_ref):
    x = x_ref[0]
    seq = x.shape[0]
    xh, xm, _ = _split3(x)
    wh, wm, _ = _split3(rw_ref[...])
    logits = _dot_nt(wh, xh) + (_dot_nt(wh, xm) + _dot_nt(wm, xh))
    e = jnp.exp(logits - jnp.max(logits, axis=0, keepdims=True))
    aff = e / jnp.sum(e, axis=0, keepdims=True)
    bits = pltpu.bitcast(aff, jnp.int32)

    def pick(i, thr):
        cand = thr | (jnp.int32(1) << (30 - i))
        cnt = jnp.sum((bits >= cand).astype(jnp.int32), axis=1, keepdims=True)
        return jnp.where(cnt >= cap, cand, thr)

    thr = lax.fori_loop(0, 31, pick, jnp.zeros((bits.shape[0], 1), jnp.int32))
    above = bits > thr
    tie = bits == thr
    need = cap - jnp.sum(above.astype(jnp.int32), axis=1, keepdims=True)

    blk = 256 if seq % 256 == 0 else V7X_LANES
    r = lax.broadcasted_iota(jnp.int32, (blk, blk), 0)
    c = lax.broadcasted_iota(jnp.int32, (blk, blk), 1)
    before = jnp.where(r < c, 1.0, 0.0).astype(BF16)

    def prefix(mask):
        m = jnp.where(mask, 1.0, 0.0).astype(BF16)
        run = jnp.zeros((mask.shape[0], 1), F32)
        parts = []
        for t in range(seq // blk):
            mb = m[:, t * blk:(t + 1) * blk]
            parts.append(_dot(mb, before) + run)
            run = run + jnp.sum(mb.astype(F32), axis=1, keepdims=True)
        return jnp.concatenate(parts, axis=1).astype(jnp.int32)

    chosen = above | (tie & (prefix(tie) < need))
    slot = jnp.where(chosen, prefix(chosen), -1)
    slot_ref[0] = slot
    n_e = aff.shape[0]
    eye = jnp.where(lax.broadcasted_iota(jnp.int32, (n_e, n_e), 0)
                    == lax.broadcasted_iota(jnp.int32, (n_e, n_e), 1), 1.0, 0.0).astype(BF16)

    def transpose(t):
        hi, mid, lo = _split3(t)
        return _dot_tn(hi, eye) + _dot_tn(mid, eye) + _dot_tn(lo, eye)

    slot_t_ref[0] = transpose(slot.astype(F32))
    aff_t_ref[0] = transpose(aff)


def _route(x1, rw_t, cap):
    b, s, d = x1.shape
    e = rw_t.shape[0]
    return pl.pallas_call(
        functools.partial(_route_kernel, cap),
        grid=(b,),
        in_specs=[pl.BlockSpec((1, s, d), lambda i: (i, 0, 0)), pl.BlockSpec((e, d), lambda i: (0, 0))],
        out_specs=[pl.BlockSpec((1, e, s), lambda i: (i, 0, 0)), pl.BlockSpec((1, s, e), lambda i: (i, 0, 0)),
                   pl.BlockSpec((1, s, e), lambda i: (i, 0, 0))],
        out_shape=[jax.ShapeDtypeStruct((b, e, s), jnp.int32), jax.ShapeDtypeStruct((b, s, e), F32),
                   jax.ShapeDtypeStruct((b, s, e), F32)],
        compiler_params=_params(("parallel",)),
        name="route",
    )(x1, rw_t)


def _moe_kernel(cap, ts, xb_ref, slot_row_ref, slot_t_ref, aff_t_ref, w1_ref, w3_ref, w2_ref, o_ref):
    e_id = pl.program_id(1)
    seq = xb_ref.shape[1]
    n_e = slot_t_ref.shape[2]
    mine = lax.broadcasted_iota(jnp.int32, (ts, n_e), 1) == e_id
    slot_row = slot_row_ref[0, 0]
    c_row = lax.broadcasted_iota(jnp.int32, (cap, seq), 0)
    gather = jnp.where(slot_row == c_row, 1.0, 0.0).astype(BF16)
    xg = _dot(gather, xb_ref[0]).astype(BF16)
    h1 = _dot(xg, w1_ref[0])
    h3 = _dot(xg, w3_ref[0])
    hid = (h1 * _sigmoid(h1) * h3).astype(BF16)
    ye = _dot(hid, w2_ref[0]).astype(BF16)
    c_col = lax.broadcasted_iota(jnp.int32, (ts, cap), 1).astype(F32)
    for t in range(seq // ts):
        rows = slice(t * ts, (t + 1) * ts)
        slot_col = jnp.sum(jnp.where(mine, slot_t_ref[0, rows, :], 0.0), axis=1, keepdims=True)
        aff_col = jnp.sum(jnp.where(mine, aff_t_ref[0, rows, :], 0.0), axis=1, keepdims=True)
        scatter = jnp.where(slot_col == c_col, 1.0, 0.0).astype(BF16)
        contrib = aff_col * _dot(scatter, ye)

        @pl.when(e_id == 0)
        def _():
            o_ref[0, rows, :] = contrib

        @pl.when(e_id != 0)
        def _():
            o_ref[0, rows, :] += contrib


def _moe(x1b, slot, slot_t, aff_t, w1, w3, w2, cap):
    b, s, d = x1b.shape
    e = slot.shape[1]
    ff = w1.shape[2]
    ts = min(512, s)
    slot_row = slot.reshape(b, e, 1, s)
    return pl.pallas_call(
        functools.partial(_moe_kernel, cap, ts),
        grid=(b, e),
        in_specs=[pl.BlockSpec((1, s, d), lambda i, j: (i, 0, 0)),
                  pl.BlockSpec((1, 1, 1, s), lambda i, j: (i, j, 0, 0)),
                  pl.BlockSpec((1, s, e), lambda i, j: (i, 0, 0)),
                  pl.BlockSpec((1, s, e), lambda i, j: (i, 0, 0)),
                  pl.BlockSpec((1, d, ff), lambda i, j: (j, 0, 0)),
                  pl.BlockSpec((1, d, ff), lambda i, j: (j, 0, 0)),
                  pl.BlockSpec((1, ff, d), lambda i, j: (j, 0, 0))],
        out_specs=pl.BlockSpec((1, s, d), lambda i, j: (i, 0, 0)),
        out_shape=jax.ShapeDtypeStruct((b, s, d), F32),
        compiler_params=_params(("parallel", "arbitrary")),
        name="moe",
    )(x1b, slot_row, slot_t, aff_t, w1, w3, w2)


def _ple_ln_kernel(alpha, x_ref, xb_ref, f_ref, p_ref, wg_ref, bg_ref, wp_ref, g_ref, b_ref, y_ref):
    gate = _sigmoid(_dot(xb_ref[...], wg_ref[...]) + bg_ref[...])
    ple = gate * _dot(p_ref[...].astype(BF16), wp_ref[...])
    y_ref[...] = _layer_norm(alpha * x_ref[...] + f_ref[...] + ple, g_ref[...], b_ref[...])


def _ple_ln(x1, x1b, ffn, p2, wg, bg, wp, g, b, alpha, tm):
    t, d = x1.shape
    row = lambda i: (i, 0)
    fixed = lambda i: (0, 0)
    full = lambda a: pl.BlockSpec(a.shape, fixed)
    return pl.pallas_call(
        functools.partial(_ple_ln_kernel, alpha),
        grid=(t // tm,),
        in_specs=[pl.BlockSpec((tm, d), row), pl.BlockSpec((tm, d), row), pl.BlockSpec((tm, d), row),
                  pl.BlockSpec((tm, p2.shape[1]), row), full(wg), full(bg), full(wp), full(g), full(b)],
        out_specs=pl.BlockSpec((tm, d), row),
        out_shape=jax.ShapeDtypeStruct((t, d), F32),
        compiler_params=_params(("parallel",)),
        name="ple_ln",
    )(x1, x1b, ffn, p2, wg, bg, wp, g, b)


def _rope_tables(seq, rot_dim, lo, width):
    rows = seq // GRID_W
    row = jnp.repeat(jnp.arange(rows, dtype=F32), GRID_W)
    col = jnp.tile(jnp.arange(GRID_W, dtype=F32), rows)
    axis_dim = rot_dim // 2
    inv = ROPE_THETA ** (-jnp.arange(0, axis_dim, 2, dtype=F32) / axis_dim)
    ang = jnp.concatenate([row[:, None] * inv, col[:, None] * inv], axis=-1)
    cos, sin = jnp.cos(ang), jnp.sin(ang)
    cos2 = jnp.concatenate([cos, cos], axis=-1)
    sin2 = jnp.concatenate([-sin, sin], axis=-1)
    if lo == 0:
        reps = width // rot_dim
        return jnp.tile(cos2, (1, reps)), jnp.tile(sin2, (1, reps))
    pad_l = jnp.ones((seq, lo), F32)
    pad_r = jnp.ones((seq, width - lo - rot_dim), F32)
    cos_t = jnp.concatenate([pad_l, cos2, pad_r], axis=-1)
    sin_t = jnp.concatenate([0 * pad_l, sin2, 0 * pad_r], axis=-1)
    return cos_t, sin_t


def _prep_even(w_in, w_uq, w_ukv, gw_f, gb_f, gw_b, gb_b):
    d = w_in.shape[0]
    offs = np.cumsum(EVEN_SPLITS)[:-1].tolist()
    c_q, c_kv, k_pe, gq, gk, gv, g_lr, gr = jnp.split(w_in, offs, axis=-1)
    z32 = jnp.zeros((d, 32), w_in.dtype)
    chunk = jnp.concatenate([g_lr, z32, k_pe, z32], axis=-1)
    w_in_p = jnp.concatenate([c_q, c_kv, chunk, gq, gk, gv, gr], axis=-1).astype(BF16)
    uq = w_uq.reshape(MLA_Q_LORA, MLA_HEADS, MLA_NOPE + MLA_ROPE)
    uq = jnp.pad(uq, ((0, 0), (0, 0), (0, MLA_PAD - MLA_NOPE - MLA_ROPE)))
    w_uq_p = uq.reshape(MLA_Q_LORA, MLA_HEADS * MLA_PAD).astype(BF16)
    ukv = w_ukv.reshape(MLA_KV_LORA, MLA_HEADS, MLA_NOPE + MLA_V)
    uk = jnp.pad(ukv[:, :, :MLA_NOPE], ((0, 0), (0, 0), (0, MLA_PAD - MLA_NOPE)))
    w_uk_p = uk.reshape(MLA_KV_LORA, MLA_HEADS * MLA_PAD).astype(BF16)
    w_uv = ukv[:, :, MLA_NOPE:].reshape(MLA_KV_LORA, MLA_HEADS * MLA_V).astype(BF16)
    dkw = GLA_HEADS * GLA_DK
    w_gate = jnp.zeros((MLA_PAD, 2 * dkw), F32)
    w_gate = w_gate.at[0:GLA_GATE_RANK, 0:dkw].set(gw_f)
    w_gate = w_gate.at[GLA_GATE_RANK:2 * GLA_GATE_RANK, dkw:].set(gw_b).astype(BF16)
    b_gate = jnp.concatenate([gb_f, gb_b])[None, :]
    return w_in_p, w_uq_p, w_uk_p, w_uv, w_gate, b_gate


def kernel(x, p, w_in_even, mla_q_norm, w_uq, mla_kv_norm, w_ukv, gla_gate_w_fwd, gla_gate_b_fwd,
           gla_gate_w_bwd, gla_gate_b_bwd, gla_norm, w_in_odd, gqa_q_norm, gqa_k_norm, w_o, ln1_g,
           ln1_b, router_w, w1, w3, w2, ple_gate_w, ple_gate_b, ple_w, ln2_g, ln2_b):
    b, s, d = x.shape
    depth = w_o.shape[0]
    t = b * s
    alpha = (2.0 * depth) ** 0.25
    cap = EC_CAPACITY_FACTOR * s // N_EXPERTS
    tm = min(256, s)
    tq = min(256, s)
    half = w_o.shape[1] // 2
    cos_a, sin_a = _rope_tables(s, MLA_ROPE, PE_LO, MLA_PAD)
    cos_c, sin_c = _rope_tables(s, GQA_HEAD_DIM, 0, GQA_KV_HEADS * GQA_HEAD_DIM)
    hw = GQA_KV_HEADS * GQA_HEAD_DIM
    head_of = np.arange(hw) // GQA_HEAD_DIM
    avg = jnp.asarray((head_of[:, None] == head_of[None, :]) / GQA_HEAD_DIM, BF16)

    x2 = x.reshape(t, d)
    for i in range(depth):
        j = i // 2
        if i % 2 == 0:
            w_in_p, w_uq_p, w_uk_p, w_uv, w_gate, b_gate = _prep_even(
                w_in_even[j], w_uq[j], w_ukv[j], gla_gate_w_fwd[j], gla_gate_b_fwd[j],
                gla_gate_w_bwd[j], gla_gate_b_bwd[j])
            q, k, v, gq, gk, gv, la, gr = _even_proj(
                x2, w_in_p, mla_q_norm[j][None, :], w_uq_p, mla_kv_norm[j][None, :], w_uk_p, w_uv,
                w_gate, b_gate, cos_a, sin_a, s, tm)
            r3 = lambda a: a.reshape(b, s, a.shape[-1])
            o_mla = _mla_attn(r3(q), r3(k), r3(v), tq).reshape(t, -1)
            o_gla = _gla(r3(gq), r3(gk), r3(la), r3(gv), r3(gr), gla_norm[j][None, :]).reshape(t, -1)
            mix_a, mix_b, col_a, col_b = o_mla, o_gla, 0, 0
        else:
            q, k, v = _odd_proj(x2, w_in_odd[j].astype(BF16),
                                jnp.tile(gqa_q_norm[j], GQA_KV_HEADS)[None, :],
                                jnp.tile(gqa_k_norm[j], GQA_KV_HEADS)[None, :],
                                avg, cos_c, sin_c, s, tm)
            r3 = lambda a: a.reshape(b, s, a.shape[-1])
            o = _gqa_attn(r3(q), r3(k), r3(v), tq).reshape(t, -1)
            mix_a, mix_b, col_a, col_b = o, o, 0, 1
        wo = w_o[i].astype(BF16)
        x1, x1b = _out_ln(x2, mix_a, mix_b, col_a, col_b, wo[:half], wo[half:],
                          ln1_g[i][None, :], ln1_b[i][None, :], alpha, tm)
        slot, slot_t, aff_t = _route(x1.reshape(b, s, d), router_w[i].T, cap)
        ffn = _moe(x1b.reshape(b, s, d), slot, slot_t, aff_t, w1[i].astype(BF16), w3[i].astype(BF16),
                   w2[i].astype(BF16), cap)
        x2 = _ple_ln(x1, x1b, ffn.reshape(t, d), p[i].reshape(t, -1), ple_gate_w[i].astype(BF16),
                     ple_gate_b[i][None, :], ple_w[i].astype(BF16), ln2_g[i][None, :], ln2_b[i][None, :],
                     alpha, tm)
    return x2.reshape(b, s, d)
```

```python
import functools
import math

import jax
import jax.numpy as jnp
import numpy as np
from jax import lax
from jax.experimental import pallas as pl
from jax.experimental.pallas import tpu as pltpu

F32 = jnp.float32
BF16 = jnp.bfloat16

V7X_LANES = 128
V7X_VMEM_BYTES = 64 * 1024 * 1024
VMEM_LIMIT = 56 * 1024 * 1024

GRID_W = 64
ROPE_THETA = 10000.0
EPS = 1e-6
MLA_HEADS, MLA_Q_LORA, MLA_KV_LORA = 8, 256, 128
MLA_NOPE, MLA_ROPE, MLA_V = 64, 32, 64
GLA_HEADS, GLA_DK, GLA_DV = 4, 64, 128
GLA_GATE_RANK, GLA_GATE_NORM, GLA_CHUNK = 16, 16.0, 64
GQA_HEADS, GQA_KV_HEADS, GQA_HEAD_DIM = 16, 4, 64
N_EXPERTS, EC_CAPACITY_FACTOR = 16, 2
EVEN_SPLITS = (MLA_Q_LORA, MLA_KV_LORA, MLA_ROPE, GLA_HEADS * GLA_DK, GLA_HEADS * GLA_DK,
               GLA_HEADS * GLA_DV, 2 * GLA_GATE_RANK, GLA_HEADS * GLA_DV)
ODD_SPLITS = (GQA_HEADS * GQA_HEAD_DIM, GQA_KV_HEADS * GQA_HEAD_DIM, GQA_KV_HEADS * GQA_HEAD_DIM)
MLA_PAD = V7X_LANES
PE_LO, PE_HI = MLA_NOPE, MLA_NOPE + MLA_ROPE


def _dot(a, b):
    return jnp.dot(a, b, preferred_element_type=F32)


def _dot_nt(a, b):
    return lax.dot_general(a, b, (((1,), (1,)), ((), ())), preferred_element_type=F32)


def _dot_tn(a, b):
    return lax.dot_general(a, b, (((0,), (0,)), ((), ())), preferred_element_type=F32)


def _split3(x):
    hi = x.astype(BF16)
    r = x - hi.astype(F32)
    mid = r.astype(BF16)
    lo = (r - mid.astype(F32)).astype(BF16)
    return hi, mid, lo


def _dot_sel(sel, x):
    hi, mid, lo = _split3(x)
    return _dot(sel, hi) + _dot(sel, mid) + _dot(sel, lo)


def _dot_x_sel(x, sel):
    hi, mid, lo = _split3(x)
    return _dot(hi, sel) + _dot(mid, sel) + _dot(lo, sel)


def _params(sem):
    return pltpu.CompilerParams(dimension_semantics=sem, vmem_limit_bytes=VMEM_LIMIT)


def _rot_half(x, half):
    w = x.shape[-1]
    lane = lax.broadcasted_iota(jnp.int32, x.shape, x.ndim - 1)
    first = (lane % (2 * half)) < half
    return jnp.where(first, pltpu.roll(x, w - half, x.ndim - 1), pltpu.roll(x, half, x.ndim - 1))


def _layer_norm(y, g, b):
    mu = jnp.mean(y, axis=-1, keepdims=True)
    yc = y - mu
    var = jnp.mean(yc * yc, axis=-1, keepdims=True)
    return yc * lax.rsqrt(var + EPS) * g + b


def _rms(x, g):
    return x * lax.rsqrt(jnp.mean(x * x, axis=-1, keepdims=True) + EPS) * g


def _log_sigmoid(z):
    return jnp.minimum(z, 0.0) - jnp.log1p(jnp.exp(-jnp.abs(z)))


def _sigmoid(z):
    return 1.0 / (1.0 + jnp.exp(-z))


def _even_proj_kernel(x_ref, w_in_ref, qn_ref, w_uq_ref, kvn_ref, w_uk_ref, w_uv_ref,
                      w_gate_ref, b_gate_ref, cq_ref, sq_ref,
                      q_ref, k_ref, v_ref, gq_ref, gk_ref, gv_ref, la_ref, gr_ref):
    h = _dot(x_ref[...].astype(BF16), w_in_ref[...])
    cos = cq_ref[...]
    sin = sq_ref[...]
    lane = lax.broadcasted_iota(jnp.int32, cos.shape, 1)
    pe_lane = (lane >= PE_LO) & (lane < PE_HI)

    def rope(t):
        return t * cos + _rot_half(t, MLA_ROPE // 2) * sin

    c_q = _rms(h[:, 0:256], qn_ref[...])
    q = _dot(c_q.astype(BF16), w_uq_ref[...]) * ((MLA_NOPE + MLA_ROPE) ** -0.5)
    c_kv = _rms(h[:, 256:384], kvn_ref[...]).astype(BF16)
    kn = _dot(c_kv, w_uk_ref[...])
    v_ref[...] = _dot(c_kv, w_uv_ref[...]).astype(v_ref.dtype)
    chunk = h[:, 384:512]
    k_pe = jnp.where(pe_lane, rope(chunk), 0.0)
    for hd in range(MLA_HEADS):
        sl = slice(hd * MLA_PAD, (hd + 1) * MLA_PAD)
        q_ref[:, sl] = rope(q[:, sl]).astype(q_ref.dtype)
        k_ref[:, sl] = (kn[:, sl] + k_pe).astype(k_ref.dtype)
    z = _dot(chunk.astype(BF16), w_gate_ref[...]) + b_gate_ref[...]
    la = _log_sigmoid(z) * (1.0 / GLA_GATE_NORM)
    tm = la.shape[0]
    r = lax.broadcasted_iota(jnp.int32, (tm, tm), 0)
    c = lax.broadcasted_iota(jnp.int32, (tm, tm), 1)
    same = (r // GLA_CHUNK) == (c // GLA_CHUNK)
    dkw = GLA_HEADS * GLA_DK
    la_ref[:, :dkw] = _dot_sel(jnp.where(same & (c <= r), 1.0, 0.0).astype(BF16), la[:, :dkw])
    la_ref[:, dkw:] = _dot_sel(jnp.where(same & (c >= r), 1.0, 0.0).astype(BF16), la[:, dkw:])
    gq_ref[...] = h[:, 512:768] * (GLA_DK ** -0.5)
    gk_ref[...] = h[:, 768:1024]
    gv_ref[...] = h[:, 1024:1536].astype(gv_ref.dtype)
    gr = h[:, 1536:2048]
    gr_ref[...] = gr * _sigmoid(gr)


def _even_proj(x2, w_in_p, qn, w_uq_p, kvn, w_uk_p, w_uv, w_gate, b_gate, cq, sq, seq, tm):
    t, d = x2.shape
    nsb = seq // tm
    row = lambda i: (i, 0)
    fixed = lambda i: (0, 0)
    pos = lambda i: (i % nsb, 0)
    full = lambda a: pl.BlockSpec(a.shape, fixed)
    outs = [(1024, BF16), (1024, BF16), (512, BF16), (256, F32), (256, F32), (512, BF16), (512, F32), (512, F32)]
    return pl.pallas_call(
        _even_proj_kernel,
        grid=(t // tm,),
        in_specs=[pl.BlockSpec((tm, d), row), full(w_in_p), full(qn), full(w_uq_p), full(kvn),
                  full(w_uk_p), full(w_uv), full(w_gate), full(b_gate),
                  pl.BlockSpec((tm, MLA_PAD), pos), pl.BlockSpec((tm, MLA_PAD), pos)],
        out_specs=[pl.BlockSpec((tm, n), row) for n, _ in outs],
        out_shape=[jax.ShapeDtypeStruct((t, n), dt) for n, dt in outs],
        compiler_params=_params(("parallel",)),
        name="even_proj",
    )(x2, w_in_p, qn, w_uq_p, kvn, w_uk_p, w_uv, w_gate, b_gate, cq, sq)


def _mla_attn_kernel(q_ref, k_ref, v_ref, o_ref):
    v = v_ref[0]
    lane = lax.broadcasted_iota(jnp.int32, v.shape, 1)
    acc = None
    for a in range(2):
        sl = slice(a * MLA_PAD, (a + 1) * MLA_PAD)
        s = _dot_nt(q_ref[0, :, sl], k_ref[0, :, sl])
        e = jnp.exp(s - jnp.max(s, axis=-1, keepdims=True))
        inv = 1.0 / jnp.sum(e, axis=-1, keepdims=True)
        v_a = jnp.where((lane // MLA_V) == a, v, jnp.zeros_like(v))
        o = _dot(e.astype(BF16), v_a) * inv
        acc = o if acc is None else acc + o
    o_ref[0] = acc.astype(o_ref.dtype)


def _mla_attn(q, k, v, tq):
    b, s, _ = q.shape
    return pl.pallas_call(
        _mla_attn_kernel,
        grid=(b, MLA_HEADS // 2, s // tq),
        in_specs=[pl.BlockSpec((1, tq, 2 * MLA_PAD), lambda i, j, t: (i, t, j)),
                  pl.BlockSpec((1, s, 2 * MLA_PAD), lambda i, j, t: (i, 0, j)),
                  pl.BlockSpec((1, s, 2 * MLA_V), lambda i, j, t: (i, 0, j))],
        out_specs=pl.BlockSpec((1, tq, 2 * MLA_V), lambda i, j, t: (i, t, j)),
        out_shape=jax.ShapeDtypeStruct((b, s, MLA_HEADS * MLA_V), BF16),
        compiler_params=_params(("parallel", "parallel", "parallel")),
        name="mla_attn",
    )(q, k, v)


def _gla_kernel(q_ref, k_ref, cum_ref, v_ref, gr_ref, g_ref, o_ref, state_ref, of_ref, ob_ref):
    seq = q_ref.shape[1]
    n_chunks = seq // GLA_CHUNK
    L = GLA_CHUNK
    dkw = GLA_HEADS * GLA_DK
    dvw = GLA_HEADS * GLA_DV

    def iota(shape, dim):
        return lax.broadcasted_iota(jnp.int32, shape, dim)

    row_l = iota((L, dkw), 0)
    col_m = iota((L, dkw), 1) % L
    k_own = (iota((dkw, dkw), 0) // L) == (iota((dkw, dkw), 1) // GLA_DK)
    v_own = (iota((dkw, dvw), 0) // L) == (iota((dkw, dvw), 1) // GLA_DV)
    s_own = (iota((dvw, dkw), 0) // GLA_DV) == (iota((dvw, dkw), 1) // GLA_DK)
    state_ref[...] = jnp.zeros_like(state_ref)

    def body(i, carry):
        for d, fwd in enumerate((True, False)):
            n = i if fwd else n_chunks - 1 - i
            rows = pl.ds(pl.multiple_of(n * L, L), L)
            cum = cum_ref[0, rows, d * dkw:(d + 1) * dkw]
            last = cum[L - 1:L, :] if fwd else cum[0:1, :]
            q = q_ref[0, rows, :]
            k = k_ref[0, rows, :]
            v = v_ref[0, rows, :]
            qe = (q * jnp.exp(cum)).astype(BF16)
            kg = (k * jnp.exp(-cum)).astype(BF16)
            kdec = (k * jnp.exp(last - cum)).astype(BF16)
            k_blk = jnp.where(k_own, jnp.concatenate([kg] * GLA_HEADS, axis=0), jnp.zeros((), BF16))
            att = _dot_nt(qe, k_blk)
            keep = (col_m <= row_l) if fwd else (col_m >= row_l)
            att = jnp.where(keep, att, 0.0).astype(BF16)
            v_blk = jnp.where(v_own, jnp.concatenate([v] * GLA_HEADS, axis=0), jnp.zeros((), BF16))
            st = state_ref[d]
            o = _dot(att, v_blk) + _dot_nt(qe, st.astype(BF16))
            state_ref[d] = st * jnp.exp(last) + jnp.where(s_own, _dot_tn(v, kdec), 0.0)
            if fwd:
                of_ref[rows, :] = o
            else:
                ob_ref[rows, :] = o
        return carry

    lax.fori_loop(0, n_chunks, body, 0, unroll=2)

    def finish(n, carry):
        rows = pl.ds(pl.multiple_of(n * L, L), L)
        tot = of_ref[rows, :] + ob_ref[rows, :]
        for hd in range(GLA_HEADS):
            sl = slice(hd * GLA_DV, (hd + 1) * GLA_DV)
            o_ref[0, rows, sl] = (_rms(tot[:, sl], g_ref[...]) * gr_ref[0, rows, sl]).astype(o_ref.dtype)
        return carry

    lax.fori_loop(0, n_chunks, finish, 0)


def _gla(gq, gk, cum, gv, gr, g_norm):
    b, s, _ = gq.shape
    dkw = GLA_HEADS * GLA_DK
    dvw = GLA_HEADS * GLA_DV
    blk = lambda w: pl.BlockSpec((1, s, w), lambda i: (i, 0, 0))
    return pl.pallas_call(
        _gla_kernel,
        grid=(b,),
        in_specs=[blk(dkw), blk(dkw), blk(2 * dkw), blk(dvw), blk(dvw),
                  pl.BlockSpec((1, GLA_DV), lambda i: (0, 0))],
        out_specs=blk(dvw),
        out_shape=jax.ShapeDtypeStruct((b, s, dvw), BF16),
        scratch_shapes=[pltpu.VMEM((2, dvw, dkw), F32), pltpu.VMEM((s, dvw), F32), pltpu.VMEM((s, dvw), F32)],
        compiler_params=_params(("parallel",)),
        name="gla",
    )(gq, gk, cum, gv, gr, g_norm)


def _odd_proj_kernel(x_ref, w_ref, gq_ref, gk_ref, avg_ref, cos_ref, sin_ref, q_ref, k_ref, v_ref):
    h = _dot(x_ref[...].astype(BF16), w_ref[...])
    cos = cos_ref[...]
    sin = sin_ref[...]
    avg = avg_ref[...]
    nq = GQA_HEADS * GQA_HEAD_DIM
    nk = GQA_KV_HEADS * GQA_HEAD_DIM

    def norm_rope(t, g):
        ms = _dot_x_sel(t * t, avg)
        t = t * lax.rsqrt(ms + EPS) * g
        return t * cos + _rot_half(t, GQA_HEAD_DIM // 2) * sin

    for c in range(nq // nk):
        sl = slice(c * nk, (c + 1) * nk)
        q_ref[:, sl] = (norm_rope(h[:, sl], gq_ref[...]) * (GQA_HEAD_DIM ** -0.5)).astype(q_ref.dtype)
    k_ref[...] = norm_rope(h[:, nq:nq + nk], gk_ref[...]).astype(k_ref.dtype)
    v_ref[...] = h[:, nq + nk:].astype(v_ref.dtype)


def _odd_proj(x2, w, gq, gk, avg, cos, sin, seq, tm):
    t, d = x2.shape
    nsb = seq // tm
    row = lambda i: (i, 0)
    fixed = lambda i: (0, 0)
    pos = lambda i: (i % nsb, 0)
    full = lambda a: pl.BlockSpec(a.shape, fixed)
    outs = [ODD_SPLITS[0], ODD_SPLITS[1], ODD_SPLITS[2]]
    return pl.pallas_call(
        _odd_proj_kernel,
        grid=(t // tm,),
        in_specs=[pl.BlockSpec((tm, d), row), full(w), full(gq), full(gk), full(avg),
                  pl.BlockSpec((tm, cos.shape[1]), pos), pl.BlockSpec((tm, sin.shape[1]), pos)],
        out_specs=[pl.BlockSpec((tm, n), row) for n in outs],
        out_shape=[jax.ShapeDtypeStruct((t, n), BF16) for n in outs],
        compiler_params=_params(("parallel",)),
        name="odd_proj",
    )(x2, w, gq, gk, avg, cos, sin)


def _gqa_attn_kernel(q_ref, k_ref, v_ref, o_ref, krep_ref, vexp_ref):
    j = pl.program_id(1)
    group = GQA_HEADS // GQA_KV_HEADS
    width = GQA_KV_HEADS * GQA_HEAD_DIM

    @pl.when(pl.program_id(2) == 0)
    def _():
        r = lax.broadcasted_iota(jnp.int32, (width, width), 0)
        c = lax.broadcasted_iota(jnp.int32, (width, width), 1)
        pick = (r // GQA_HEAD_DIM == j) & (r % GQA_HEAD_DIM == c % GQA_HEAD_DIM)
        for g in range(group):
            only_g = pick & (c // GQA_HEAD_DIM == g)
            vexp_ref[g] = _dot(v_ref[0], jnp.where(only_g, 1.0, 0.0).astype(BF16)).astype(BF16)
        krep_ref[...] = _dot(k_ref[0], jnp.where(pick, 1.0, 0.0).astype(BF16)).astype(BF16)

    q = q_ref[0]
    lane = lax.broadcasted_iota(jnp.int32, q.shape, 1)
    acc = None
    for g in range(group):
        qg = jnp.where(lane // GQA_HEAD_DIM == g, q, jnp.zeros_like(q))
        s = _dot_nt(qg, krep_ref[...])
        e = jnp.exp(s - jnp.max(s, axis=-1, keepdims=True))
        inv = 1.0 / jnp.sum(e, axis=-1, keepdims=True)
        o = _dot(e.astype(BF16), vexp_ref[g]) * inv
        acc = o if acc is None else acc + o
    o_ref[0] = acc.astype(o_ref.dtype)


def _gqa_attn(q, k, v, tq):
    b, s, _ = q.shape
    width = GQA_KV_HEADS * GQA_HEAD_DIM
    group = GQA_HEADS // GQA_KV_HEADS
    return pl.pallas_call(
        _gqa_attn_kernel,
        grid=(b, GQA_KV_HEADS, s // tq),
        in_specs=[pl.BlockSpec((1, tq, width), lambda i, j, t: (i, t, j)),
                  pl.BlockSpec((1, s, width), lambda i, j, t: (i, 0, 0)),
                  pl.BlockSpec((1, s, width), lambda i, j, t: (i, 0, 0))],
        out_specs=pl.BlockSpec((1, tq, width), lambda i, j, t: (i, t, j)),
        out_shape=jax.ShapeDtypeStruct((b, s, GQA_HEADS * GQA_HEAD_DIM), BF16),
        scratch_shapes=[pltpu.VMEM((s, width), BF16), pltpu.VMEM((group, s, width), BF16)],
        compiler_params=_params(("parallel", "parallel", "arbitrary")),
        name="gqa_attn",
    )(q, k, v)


def _out_ln_kernel(alpha, x_ref, ma_ref, mb_ref, wa_ref, wb_ref, g_ref, b_ref, y_ref, yb_ref, yt_ref):
    y = alpha * x_ref[...] + _dot(ma_ref[...].astype(BF16), wa_ref[...]) \
        + _dot(mb_ref[...].astype(BF16), wb_ref[...])
    y = _layer_norm(y, g_ref[...], b_ref[...])
    y_ref[...] = y
    yb_ref[...] = y.astype(BF16)
    yt_ref[0] = y.T.astype(BF16)


def _out_ln(x2, mix_a, mix_b, col_a, col_b, w_a, w_b, g, b, alpha, seq, tm):
    t, d = x2.shape
    half = w_a.shape[0]
    nsb = seq // tm
    row = lambda i: (i, 0)
    fixed = lambda i: (0, 0)
    return pl.pallas_call(
        functools.partial(_out_ln_kernel, alpha),
        grid=(t // tm,),
        in_specs=[pl.BlockSpec((tm, d), row),
                  pl.BlockSpec((tm, half), lambda i: (i, col_a)),
                  pl.BlockSpec((tm, half), lambda i: (i, col_b)),
                  pl.BlockSpec(w_a.shape, fixed), pl.BlockSpec(w_b.shape, fixed),
                  pl.BlockSpec(g.shape, fixed), pl.BlockSpec(b.shape, fixed)],
        out_specs=[pl.BlockSpec((tm, d), row), pl.BlockSpec((tm, d), row),
                   pl.BlockSpec((1, d, tm), lambda i: (i // nsb, 0, i % nsb))],
        out_shape=[jax.ShapeDtypeStruct((t, d), F32), jax.ShapeDtypeStruct((t, d), BF16),
                   jax.ShapeDtypeStruct((t // seq, d, seq), BF16)],
        compiler_params=_params(("parallel",)),
        name="out_ln",
    )(x2, mix_a, mix_b, w_a, w_b, g, b)


def _route_kernel(cap, x_ref, rw_ref, aff_ref, slot_ref):
    x = x_ref[0]
    seq = x.shape[0]
    xh, xm, _ = _split3(x)
    wh, wm, _ = _split3(rw_ref[...])
    logits = _dot_nt(wh, xh) + (_dot_nt(wh, xm) + _dot_nt(wm, xh))
    e = jnp.exp(logits - jnp.max(logits, axis=0, keepdims=True))
    aff = e / jnp.sum(e, axis=0, keepdims=True)
    aff_ref[0] = aff
    bits = pltpu.bitcast(aff, jnp.int32)

    def pick(i, thr):
        cand = thr | (jnp.int32(1) << (30 - i))
        cnt = jnp.sum((bits >= cand).astype(jnp.int32), axis=1, keepdims=True)
        return jnp.where(cnt >= cap, cand, thr)

    thr = lax.fori_loop(0, 31, pick, jnp.zeros((bits.shape[0], 1), jnp.int32))
    above = bits > thr
    tie = bits == thr
    need = cap - jnp.sum(above.astype(jnp.int32), axis=1, keepdims=True)

    blk = 256 if seq % 256 == 0 else V7X_LANES
    r = lax.broadcasted_iota(jnp.int32, (blk, blk), 0)
    c = lax.broadcasted_iota(jnp.int32, (blk, blk), 1)
    before = jnp.where(r < c, 1.0, 0.0).astype(BF16)

    def prefix(mask):
        m = jnp.where(mask, 1.0, 0.0).astype(BF16)
        run = jnp.zeros((mask.shape[0], 1), F32)
        parts = []
        for t in range(seq // blk):
            mb = m[:, t * blk:(t + 1) * blk]
            parts.append(_dot(mb, before) + run)
            run = run + jnp.sum(mb.astype(F32), axis=1, keepdims=True)
        return jnp.concatenate(parts, axis=1).astype(jnp.int32)

    chosen = above | (tie & (prefix(tie) < need))
    slot_ref[0] = jnp.where(chosen, prefix(chosen), -1)


def _route(x1, rw_t, cap):
    b, s, d = x1.shape
    e = rw_t.shape[0]
    return pl.pallas_call(
        functools.partial(_route_kernel, cap),
        grid=(b,),
        in_specs=[pl.BlockSpec((1, s, d), lambda i: (i, 0, 0)), pl.BlockSpec((e, d), lambda i: (0, 0))],
        out_specs=[pl.BlockSpec((1, e, s), lambda i: (i, 0, 0)), pl.BlockSpec((1, e, s), lambda i: (i, 0, 0))],
        out_shape=[jax.ShapeDtypeStruct((b, e, s), F32), jax.ShapeDtypeStruct((b, e, s), jnp.int32)],
        compiler_params=_params(("parallel",)),
        name="route",
    )(x1, rw_t)


def _moe_kernel(cap, ts, xt_ref, slot_ref, aff_ref, w13_ref, w2_ref, o_ref):
    e_id = pl.program_id(1)
    seq = xt_ref.shape[2]
    ff = w2_ref.shape[2]

    @pl.when(e_id == 0)
    def _():
        o_ref[...] = jnp.zeros_like(o_ref)

    c_row = lax.broadcasted_iota(jnp.int32, (cap, seq), 0)
    pick = jnp.where(slot_ref[0, 0] == c_row, 1.0, 0.0).astype(BF16)
    hd2 = xt_ref.shape[1] // 2
    xg = jnp.concatenate([_dot_nt(xt_ref[0, :hd2, :], pick), _dot_nt(xt_ref[0, hd2:, :], pick)],
                         axis=0).astype(BF16)
    h1 = _dot(w13_ref[0, :ff, :], xg)
    h3 = _dot(w13_ref[0, ff:, :], xg)
    hid = (h1 * _sigmoid(h1) * h3).astype(BF16)
    ye = jnp.concatenate([_dot(w2_ref[0, :hd2, :], hid), _dot(w2_ref[0, hd2:, :], hid)],
                         axis=0).astype(BF16)
    for t in range(seq // ts):
        cols = slice(t * ts, (t + 1) * ts)
        o_ref[0, :, cols] += _dot(ye, pick[:, cols]) * aff_ref[0, 0, :, cols]


def _moe(xt, slot, aff, w13t, w2t, cap):
    b, d, s = xt.shape
    e = slot.shape[1]
    ff = w2t.shape[2]
    ts = min(512, s)
    return pl.pallas_call(
        functools.partial(_moe_kernel, cap, ts),
        grid=(b, e),
        in_specs=[pl.BlockSpec((1, d, s), lambda i, j: (i, 0, 0)),
                  pl.BlockSpec((1, 1, 1, s), lambda i, j: (i, j, 0, 0)),
                  pl.BlockSpec((1, 1, 1, s), lambda i, j: (i, j, 0, 0)),
                  pl.BlockSpec((1, 2 * ff, d), lambda i, j: (j, 0, 0)),
                  pl.BlockSpec((1, d, ff), lambda i, j: (j, 0, 0))],
        out_specs=pl.BlockSpec((1, d, s), lambda i, j: (i, 0, 0)),
        out_shape=jax.ShapeDtypeStruct((b, d, s), F32),
        compiler_params=_params(("parallel", "arbitrary")),
        name="moe",
    )(xt, slot.reshape(b, e, 1, s), aff.reshape(b, e, 1, s), w13t, w2t)


def _ple_ln_kernel(alpha, x_ref, xb_ref, f_ref, p_ref, wg_ref, bg_ref, wp_ref, g_ref, b_ref, y_ref):
    gate = _sigmoid(_dot(xb_ref[...], wg_ref[...]) + bg_ref[...])
    ple = gate * _dot(p_ref[...].astype(BF16), wp_ref[...])
    ffn = f_ref[0].T
    y_ref[...] = _layer_norm(alpha * x_ref[...] + ffn + ple, g_ref[...], b_ref[...])


def _ple_ln(x1, x1b, ffn_t, p2, wg, bg, wp, g, b, alpha, tm):
    t, d = x1.shape
    nsb = ffn_t.shape[2] // tm
    row = lambda i: (i, 0)
    fixed = lambda i: (0, 0)
    full = lambda a: pl.BlockSpec(a.shape, fixed)
    return pl.pallas_call(
        functools.partial(_ple_ln_kernel, alpha),
        grid=(t // tm,),
        in_specs=[pl.BlockSpec((tm, d), row), pl.BlockSpec((tm, d), row),
                  pl.BlockSpec((1, d, tm), lambda i: (i // nsb, 0, i % nsb)),
                  pl.BlockSpec((tm, p2.shape[1]), row), full(wg), full(bg), full(wp), full(g), full(b)],
        out_specs=pl.BlockSpec((tm, d), row),
        out_shape=jax.ShapeDtypeStruct((t, d), F32),
        compiler_params=_params(("parallel",)),
        name="ple_ln",
    )(x1, x1b, ffn_t, p2, wg, bg, wp, g, b)


def _rope_tables(seq, rot_dim, lo, width):
    rows = seq // GRID_W
    row = jnp.repeat(jnp.arange(rows, dtype=F32), GRID_W)
    col = jnp.tile(jnp.arange(GRID_W, dtype=F32), rows)
    axis_dim = rot_dim // 2
    inv = ROPE_THETA ** (-jnp.arange(0, axis_dim, 2, dtype=F32) / axis_dim)
    ang = jnp.concatenate([row[:, None] * inv, col[:, None] * inv], axis=-1)
    cos, sin = jnp.cos(ang), jnp.sin(ang)
    cos2 = jnp.concatenate([cos, cos], axis=-1)
    sin2 = jnp.concatenate([-sin, sin], axis=-1)
    if lo == 0:
        reps = width // rot_dim
        return jnp.tile(cos2, (1, reps)), jnp.tile(sin2, (1, reps))
    pad_l = jnp.ones((seq, lo), F32)
    pad_r = jnp.ones((seq, width - lo - rot_dim), F32)
    cos_t = jnp.concatenate([pad_l, cos2, pad_r], axis=-1)
    sin_t = jnp.concatenate([0 * pad_l, sin2, 0 * pad_r], axis=-1)
    return cos_t, sin_t


def _prep_even(w_in, w_uq, w_ukv, gw_f, gb_f, gw_b, gb_b):
    d = w_in.shape[0]
    offs = np.cumsum(EVEN_SPLITS)[:-1].tolist()
    c_q, c_kv, k_pe, gq, gk, gv, g_lr, gr = jnp.split(w_in, offs, axis=-1)
    z32 = jnp.zeros((d, 32), w_in.dtype)
    chunk = jnp.concatenate([g_lr, z32, k_pe, z32], axis=-1)
    w_in_p = jnp.concatenate([c_q, c_kv, chunk, gq, gk, gv, gr], axis=-1).astype(BF16)
    uq = w_uq.reshape(MLA_Q_LORA, MLA_HEADS, MLA_NOPE + MLA_ROPE)
    uq = jnp.pad(uq, ((0, 0), (0, 0), (0, MLA_PAD - MLA_NOPE - MLA_ROPE)))
    w_uq_p = uq.reshape(MLA_Q_LORA, MLA_HEADS * MLA_PAD).astype(BF16)
    ukv = w_ukv.reshape(MLA_KV_LORA, MLA_HEADS, MLA_NOPE + MLA_V)
    uk = jnp.pad(ukv[:, :, :MLA_NOPE], ((0, 0), (0, 0), (0, MLA_PAD - MLA_NOPE)))
    w_uk_p = uk.reshape(MLA_KV_LORA, MLA_HEADS * MLA_PAD).astype(BF16)
    w_uv = ukv[:, :, MLA_NOPE:].reshape(MLA_KV_LORA, MLA_HEADS * MLA_V).astype(BF16)
    dkw = GLA_HEADS * GLA_DK
    w_gate = jnp.zeros((MLA_PAD, 2 * dkw), F32)
    w_gate = w_gate.at[0:GLA_GATE_RANK, 0:dkw].set(gw_f)
    w_gate = w_gate.at[GLA_GATE_RANK:2 * GLA_GATE_RANK, dkw:].set(gw_b).astype(BF16)
    b_gate = jnp.concatenate([gb_f, gb_b])[None, :]
    return w_in_p, w_uq_p, w_uk_p, w_uv, w_gate, b_gate


def kernel(x, p, w_in_even, mla_q_norm, w_uq, mla_kv_norm, w_ukv, gla_gate_w_fwd, gla_gate_b_fwd,
           gla_gate_w_bwd, gla_gate_b_bwd, gla_norm, w_in_odd, gqa_q_norm, gqa_k_norm, w_o, ln1_g,
           ln1_b, router_w, w1, w3, w2, ple_gate_w, ple_gate_b, ple_w, ln2_g, ln2_b):
    b, s, d = x.shape
    depth = w_o.shape[0]
    t = b * s
    alpha = (2.0 * depth) ** 0.25
    cap = EC_CAPACITY_FACTOR * s // N_EXPERTS
    tm = min(256, s)
    tq = min(256, s)
    half = w_o.shape[1] // 2
    cos_a, sin_a = _rope_tables(s, MLA_ROPE, PE_LO, MLA_PAD)
    cos_c, sin_c = _rope_tables(s, GQA_HEAD_DIM, 0, GQA_KV_HEADS * GQA_HEAD_DIM)
    hw = GQA_KV_HEADS * GQA_HEAD_DIM
    head_of = np.arange(hw) // GQA_HEAD_DIM
    avg = jnp.asarray((head_of[:, None] == head_of[None, :]) / GQA_HEAD_DIM, BF16)

    x2 = x.reshape(t, d)
    for i in range(depth):
        j = i // 2
        if i % 2 == 0:
            w_in_p, w_uq_p, w_uk_p, w_uv, w_gate, b_gate = _prep_even(
                w_in_even[j], w_uq[j], w_ukv[j], gla_gate_w_fwd[j], gla_gate_b_fwd[j],
                gla_gate_w_bwd[j], gla_gate_b_bwd[j])
            q, k, v, gq, gk, gv, la, gr = _even_proj(
                x2, w_in_p, mla_q_norm[j][None, :], w_uq_p, mla_kv_norm[j][None, :], w_uk_p, w_uv,
                w_gate, b_gate, cos_a, sin_a, s, tm)
            r3 = lambda a: a.reshape(b, s, a.shape[-1])
            o_mla = _mla_attn(r3(q), r3(k), r3(v), tq).reshape(t, -1)
            o_gla = _gla(r3(gq), r3(gk), r3(la), r3(gv), r3(gr), gla_norm[j][None, :]).reshape(t, -1)
            mix_a, mix_b, col_a, col_b = o_mla, o_gla, 0, 0
        else:
            q, k, v = _odd_proj(x2, w_in_odd[j].astype(BF16),
                                jnp.tile(gqa_q_norm[j], GQA_KV_HEADS)[None, :],
                                jnp.tile(gqa_k_norm[j], GQA_KV_HEADS)[None, :],
                                avg, cos_c, sin_c, s, tm)
            r3 = lambda a: a.reshape(b, s, a.shape[-1])
            o = _gqa_attn(r3(q), r3(k), r3(v), tq).reshape(t, -1)
            mix_a, mix_b, col_a, col_b = o, o, 0, 1
        wo = w_o[i].astype(BF16)
        x1, x1b, x1t = _out_ln(x2, mix_a, mix_b, col_a, col_b, wo[:half], wo[half:],
                               ln1_g[i][None, :], ln1_b[i][None, :], alpha, s, tm)
        aff, slot = _route(x1.reshape(b, s, d), router_w[i].T, cap)
        w13t = jnp.swapaxes(jnp.concatenate([w1[i], w3[i]], axis=-1), 1, 2).astype(BF16)
        w2t = jnp.swapaxes(w2[i], 1, 2).astype(BF16)
        ffn_t = _moe(x1t, slot, aff, w13t, w2t, cap)
        x2 = _ple_ln(x1, x1b, ffn_t, p[i].reshape(t, -1), ple_gate_w[i].astype(BF16),
                     ple_gate_b[i][None, :], ple_w[i].astype(BF16), ln2_g[i][None, :], ln2_b[i][None, :],
                     alpha, tm)
    return x2.reshape(b, s, d)
```

```python
import functools
import math

import jax
import jax.numpy as jnp
import numpy as np
from jax import lax
from jax.experimental import pallas as pl
from jax.experimental.pallas import tpu as pltpu

F32 = jnp.float32
BF16 = jnp.bfloat16

V7X_LANES = 128
V7X_VMEM_BYTES = 64 * 1024 * 1024
VMEM_LIMIT = 56 * 1024 * 1024

GRID_W = 64
ROPE_THETA = 10000.0
EPS = 1e-6
MLA_HEADS, MLA_Q_LORA, MLA_KV_LORA = 8, 256, 128
MLA_NOPE, MLA_ROPE, MLA_V = 64, 32, 64
GLA_HEADS, GLA_DK, GLA_DV = 4, 64, 128
GLA_GATE_RANK, GLA_GATE_NORM, GLA_CHUNK = 16, 16.0, 64
GQA_HEADS, GQA_KV_HEADS, GQA_HEAD_DIM = 16, 4, 64
N_EXPERTS, EC_CAPACITY_FACTOR = 16, 2
EVEN_SPLITS = (MLA_Q_LORA, MLA_KV_LORA, MLA_ROPE, GLA_HEADS * GLA_DK, GLA_HEADS * GLA_DK,
               GLA_HEADS * GLA_DV, 2 * GLA_GATE_RANK, GLA_HEADS * GLA_DV)
ODD_SPLITS = (GQA_HEADS * GQA_HEAD_DIM, GQA_KV_HEADS * GQA_HEAD_DIM, GQA_KV_HEADS * GQA_HEAD_DIM)
MLA_PAD = V7X_LANES
PE_LO, PE_HI = MLA_NOPE, MLA_NOPE + MLA_ROPE


def _dot(a, b):
    return jnp.dot(a, b, preferred_element_type=F32)


def _dot_nt(a, b):
    return lax.dot_general(a, b, (((1,), (1,)), ((), ())), preferred_element_type=F32)


def _dot_tn(a, b):
    return lax.dot_general(a, b, (((0,), (0,)), ((), ())), preferred_element_type=F32)


def _split3(x):
    hi = x.astype(BF16)
    r = x - hi.astype(F32)
    mid = r.astype(BF16)
    lo = (r - mid.astype(F32)).astype(BF16)
    return hi, mid, lo


def _dot_sel(sel, x):
    hi, mid, lo = _split3(x)
    return _dot(sel, hi) + _dot(sel, mid) + _dot(sel, lo)


def _dot_x_sel(x, sel):
    hi, mid, lo = _split3(x)
    return _dot(hi, sel) + _dot(mid, sel) + _dot(lo, sel)


def _params(sem):
    return pltpu.CompilerParams(dimension_semantics=sem, vmem_limit_bytes=VMEM_LIMIT)


def _rot_half(x, half):
    w = x.shape[-1]
    lane = lax.broadcasted_iota(jnp.int32, x.shape, x.ndim - 1)
    first = (lane % (2 * half)) < half
    return jnp.where(first, pltpu.roll(x, w - half, x.ndim - 1), pltpu.roll(x, half, x.ndim - 1))


def _layer_norm(y, g, b):
    mu = jnp.mean(y, axis=-1, keepdims=True)
    yc = y - mu
    var = jnp.mean(yc * yc, axis=-1, keepdims=True)
    return yc * lax.rsqrt(var + EPS) * g + b


def _rms(x, g):
    return x * lax.rsqrt(jnp.mean(x * x, axis=-1, keepdims=True) + EPS) * g


def _log_sigmoid(z):
    return jnp.minimum(z, 0.0) - jnp.log1p(jnp.exp(-jnp.abs(z)))


def _sigmoid(z):
    return 1.0 / (1.0 + jnp.exp(-z))


KEY_BLK = 512
ROW_BLK = 128


def _scores(q, kt, s_scr):
    tq, seq = s_scr.shape
    m_part = None
    for c in range(seq // KEY_BLK):
        cols = slice(c * KEY_BLK, (c + 1) * KEY_BLK)
        s = _dot(q, kt(cols))
        s_scr[:, cols] = s
        for u in range(KEY_BLK // V7X_LANES):
            blk = s[:, u * V7X_LANES:(u + 1) * V7X_LANES]
            m_part = blk if m_part is None else jnp.maximum(m_part, blk)
    return jnp.max(m_part, axis=-1, keepdims=True)


def _exp_rows(m, s_scr, e_scr):
    tq, seq = s_scr.shape
    invs = []
    for r in range(tq // ROW_BLK):
        rows = slice(r * ROW_BLK, (r + 1) * ROW_BLK)
        m_b = jnp.broadcast_to(m[rows], (ROW_BLK, V7X_LANES))
        l_part = None
        for c in range(seq // V7X_LANES):
            cols = slice(c * V7X_LANES, (c + 1) * V7X_LANES)
            e = jnp.exp(s_scr[rows, cols] - m_b)
            l_part = e if l_part is None else l_part + e
            e_scr[rows, cols] = e.astype(BF16)
        invs.append(1.0 / jnp.sum(l_part, axis=-1, keepdims=True))
    return jnp.concatenate(invs, axis=0)


def _attend_heads(n_heads, q_of, kt_of, v_of, s_ref, e_ref):
    m = {0: _scores(q_of(0), kt_of(0), s_ref.at[0])}
    inv = {}
    acc = None
    for h in range(n_heads + 1):
        if h + 1 < n_heads:
            m[h + 1] = _scores(q_of(h + 1), kt_of(h + 1), s_ref.at[(h + 1) % 2])
        if h >= 1:
            o = _dot(e_ref[(h - 1) % 2], v_of(h - 1)) * inv[h - 1]
            acc = o if acc is None else acc + o
        if h < n_heads:
            inv[h] = _exp_rows(m[h], s_ref.at[h % 2], e_ref.at[h % 2])
    return acc


def _even_proj_kernel(x_ref, w_in_ref, qn_ref, w_uq_ref, kvn_ref, w_uk_ref, w_uv_ref,
                      w_gate_ref, b_gate_ref, cq_ref, sq_ref,
                      q_ref, k_ref, v_ref, gq_ref, gk_ref, gv_ref, la_ref, gr_ref):
    h = _dot(x_ref[...].astype(BF16), w_in_ref[...])
    cos = cq_ref[...]
    sin = sq_ref[...]
    lane = lax.broadcasted_iota(jnp.int32, cos.shape, 1)
    pe_lane = (lane >= PE_LO) & (lane < PE_HI)

    def rope(t):
        return t * cos + _rot_half(t, MLA_ROPE // 2) * sin

    c_q = _rms(h[:, 0:256], qn_ref[...])
    q = _dot(c_q.astype(BF16), w_uq_ref[...]) * ((MLA_NOPE + MLA_ROPE) ** -0.5)
    c_kv = _rms(h[:, 256:384], kvn_ref[...]).astype(BF16)
    kn = _dot(c_kv, w_uk_ref[...])
    v_ref[...] = _dot(c_kv, w_uv_ref[...]).astype(v_ref.dtype)
    chunk = h[:, 384:512]
    k_pe = jnp.where(pe_lane, rope(chunk), 0.0)
    for hd in range(MLA_HEADS):
        sl = slice(hd * MLA_PAD, (hd + 1) * MLA_PAD)
        q_ref[:, sl] = rope(q[:, sl]).astype(q_ref.dtype)
        k_ref[:, sl] = (kn[:, sl] + k_pe).astype(k_ref.dtype)
    z = _dot(chunk.astype(BF16), w_gate_ref[...]) + b_gate_ref[...]
    la = _log_sigmoid(z) * (1.0 / GLA_GATE_NORM)
    tm = la.shape[0]
    r = lax.broadcasted_iota(jnp.int32, (tm, tm), 0)
    c = lax.broadcasted_iota(jnp.int32, (tm, tm), 1)
    same = (r // GLA_CHUNK) == (c // GLA_CHUNK)
    dkw = GLA_HEADS * GLA_DK
    la_ref[:, :dkw] = _dot_sel(jnp.where(same & (c <= r), 1.0, 0.0).astype(BF16), la[:, :dkw])
    la_ref[:, dkw:] = _dot_sel(jnp.where(same & (c >= r), 1.0, 0.0).astype(BF16), la[:, dkw:])
    gq_ref[...] = h[:, 512:768] * (GLA_DK ** -0.5)
    gk_ref[...] = h[:, 768:1024]
    gv_ref[...] = h[:, 1024:1536].astype(gv_ref.dtype)
    gr = h[:, 1536:2048]
    gr_ref[...] = gr * _sigmoid(gr)


def _even_proj(x2, w_in_p, qn, w_uq_p, kvn, w_uk_p, w_uv, w_gate, b_gate, cq, sq, seq, tm):
    t, d = x2.shape
    nsb = seq // tm
    row = lambda i: (i, 0)
    fixed = lambda i: (0, 0)
    pos = lambda i: (i % nsb, 0)
    full = lambda a: pl.BlockSpec(a.shape, fixed)
    outs = [(1024, BF16), (1024, BF16), (512, BF16), (256, F32), (256, F32), (512, BF16), (512, F32), (512, F32)]
    return pl.pallas_call(
        _even_proj_kernel,
        grid=(t // tm,),
        in_specs=[pl.BlockSpec((tm, d), row), full(w_in_p), full(qn), full(w_uq_p), full(kvn),
                  full(w_uk_p), full(w_uv), full(w_gate), full(b_gate),
                  pl.BlockSpec((tm, MLA_PAD), pos), pl.BlockSpec((tm, MLA_PAD), pos)],
        out_specs=[pl.BlockSpec((tm, n), row) for n, _ in outs],
        out_shape=[jax.ShapeDtypeStruct((t, n), dt) for n, dt in outs],
        compiler_params=_params(("parallel",)),
        name="even_proj",
    )(x2, w_in_p, qn, w_uq_p, kvn, w_uk_p, w_uv, w_gate, b_gate, cq, sq)


def _mla_attn_kernel(q_ref, k_ref, v_ref, o_ref, kt_ref, vpad_ref, s_ref, e_ref):
    out_w = 2 * MLA_V

    @pl.when(pl.program_id(2) == 0)
    def _():
        r = lax.broadcasted_iota(jnp.int32, (MLA_PAD, MLA_PAD), 0)
        c = lax.broadcasted_iota(jnp.int32, (MLA_PAD, MLA_PAD), 1)
        eye = jnp.where(r == c, 1.0, 0.0).astype(BF16)
        v = v_ref[0]
        lane = lax.broadcasted_iota(jnp.int32, v.shape, 1)
        for a in range(2):
            kt_ref[a] = _dot_nt(eye, k_ref[0, :, a * MLA_PAD:(a + 1) * MLA_PAD]).astype(BF16)
            vpad_ref[a, :, :out_w] = jnp.where((lane // MLA_V) == a, v, jnp.zeros_like(v))
            vpad_ref[a, :, out_w:] = jnp.zeros_like(v)

    acc = _attend_heads(
        2,
        lambda a: q_ref[0, :, a * MLA_PAD:(a + 1) * MLA_PAD],
        lambda a: (lambda cols: kt_ref[a, :, cols]),
        lambda a: vpad_ref[a],
        s_ref, e_ref)
    o_ref[0] = acc[:, :out_w].astype(o_ref.dtype)


def _mla_attn(q, k, v, tq):
    b, s, _ = q.shape
    return pl.pallas_call(
        _mla_attn_kernel,
        grid=(b, MLA_HEADS // 2, s // tq),
        in_specs=[pl.BlockSpec((1, tq, 2 * MLA_PAD), lambda i, j, t: (i, t, j)),
                  pl.BlockSpec((1, s, 2 * MLA_PAD), lambda i, j, t: (i, 0, j)),
                  pl.BlockSpec((1, s, 2 * MLA_V), lambda i, j, t: (i, 0, j))],
        out_specs=pl.BlockSpec((1, tq, 2 * MLA_V), lambda i, j, t: (i, t, j)),
        out_shape=jax.ShapeDtypeStruct((b, s, MLA_HEADS * MLA_V), BF16),
        scratch_shapes=[pltpu.VMEM((2, MLA_PAD, s), BF16), pltpu.VMEM((2, s, 4 * MLA_V), BF16),
                        pltpu.VMEM((2, tq, s), F32), pltpu.VMEM((2, tq, s), BF16)],
        compiler_params=_params(("parallel", "parallel", "arbitrary")),
        name="mla_attn",
    )(q, k, v)


def _gla_kernel(q_ref, k_ref, cum_ref, v_ref, gr_ref, g_ref, o_ref, state_ref, of_ref, ob_ref):
    seq = q_ref.shape[1]
    n_chunks = seq // GLA_CHUNK
    L = GLA_CHUNK
    dkw = GLA_HEADS * GLA_DK
    dvw = GLA_HEADS * GLA_DV

    def iota(shape, dim):
        return lax.broadcasted_iota(jnp.int32, shape, dim)

    row_l = iota((L, dkw), 0)
    col_m = iota((L, dkw), 1) % L
    k_own = (iota((dkw, dkw), 0) // L) == (iota((dkw, dkw), 1) // GLA_DK)
    v_own = (iota((dkw, dvw), 0) // L) == (iota((dkw, dvw), 1) // GLA_DV)
    s_own = (iota((dvw, dkw), 0) // GLA_DV) == (iota((dvw, dkw), 1) // GLA_DK)
    state_ref[...] = jnp.zeros_like(state_ref)

    def body(i, carry):
        for d, fwd in enumerate((True, False)):
            n = i if fwd else n_chunks - 1 - i
            rows = pl.ds(pl.multiple_of(n * L, L), L)
            cum = cum_ref[0, rows, d * dkw:(d + 1) * dkw]
            last = cum[L - 1:L, :] if fwd else cum[0:1, :]
            q = q_ref[0, rows, :]
            k = k_ref[0, rows, :]
            v = v_ref[0, rows, :]
            qe = (q * jnp.exp(cum)).astype(BF16)
            kg = (k * jnp.exp(-cum)).astype(BF16)
            kdec = (k * jnp.exp(last - cum)).astype(BF16)
            k_blk = jnp.where(k_own, jnp.concatenate([kg] * GLA_HEADS, axis=0), jnp.zeros((), BF16))
            att = _dot_nt(qe, k_blk)
            keep = (col_m <= row_l) if fwd else (col_m >= row_l)
            att = jnp.where(keep, att, 0.0).astype(BF16)
            v_blk = jnp.where(v_own, jnp.concatenate([v] * GLA_HEADS, axis=0), jnp.zeros((), BF16))
            st = state_ref[d]
            o = _dot(att, v_blk) + _dot_nt(qe, st.astype(BF16))
            state_ref[d] = st * jnp.exp(last) + jnp.where(s_own, _dot_tn(v, kdec), 0.0)
            if fwd:
                of_ref[rows, :] = o
            else:
                ob_ref[rows, :] = o
        return carry

    lax.fori_loop(0, n_chunks, body, 0, unroll=2)

    def finish(n, carry):
        rows = pl.ds(pl.multiple_of(n * L, L), L)
        tot = of_ref[rows, :] + ob_ref[rows, :]
        for hd in range(GLA_HEADS):
            sl = slice(hd * GLA_DV, (hd + 1) * GLA_DV)
            o_ref[0, rows, sl] = (_rms(tot[:, sl], g_ref[...]) * gr_ref[0, rows, sl]).astype(o_ref.dtype)
        return carry

    lax.fori_loop(0, n_chunks, finish, 0)


def _gla(gq, gk, cum, gv, gr, g_norm):
    b, s, _ = gq.shape
    dkw = GLA_HEADS * GLA_DK
    dvw = GLA_HEADS * GLA_DV
    blk = lambda w: pl.BlockSpec((1, s, w), lambda i: (i, 0, 0))
    return pl.pallas_call(
        _gla_kernel,
        grid=(b,),
        in_specs=[blk(dkw), blk(dkw), blk(2 * dkw), blk(dvw), blk(dvw),
                  pl.BlockSpec((1, GLA_DV), lambda i: (0, 0))],
        out_specs=blk(dvw),
        out_shape=jax.ShapeDtypeStruct((b, s, dvw), BF16),
        scratch_shapes=[pltpu.VMEM((2, dvw, dkw), F32), pltpu.VMEM((s, dvw), F32), pltpu.VMEM((s, dvw), F32)],
        compiler_params=_params(("parallel",)),
        name="gla",
    )(gq, gk, cum, gv, gr, g_norm)


def _odd_proj_kernel(x_ref, w_ref, gq_ref, gk_ref, avg_ref, cos_ref, sin_ref, q_ref, k_ref, v_ref):
    h = _dot(x_ref[...].astype(BF16), w_ref[...])
    cos = cos_ref[...]
    sin = sin_ref[...]
    avg = avg_ref[...]
    nq = GQA_HEADS * GQA_HEAD_DIM
    nk = GQA_KV_HEADS * GQA_HEAD_DIM

    def norm_rope(t, g):
        ms = _dot_x_sel(t * t, avg)
        t = t * lax.rsqrt(ms + EPS) * g
        return t * cos + _rot_half(t, GQA_HEAD_DIM // 2) * sin

    for c in range(nq // nk):
        sl = slice(c * nk, (c + 1) * nk)
        q_ref[:, sl] = (norm_rope(h[:, sl], gq_ref[...]) * (GQA_HEAD_DIM ** -0.5)).astype(q_ref.dtype)
    k_ref[...] = norm_rope(h[:, nq:nq + nk], gk_ref[...]).astype(k_ref.dtype)
    v_ref[...] = h[:, nq + nk:].astype(v_ref.dtype)


def _odd_proj(x2, w, gq, gk, avg, cos, sin, seq, tm):
    t, d = x2.shape
    nsb = seq // tm
    row = lambda i: (i, 0)
    fixed = lambda i: (0, 0)
    pos = lambda i: (i % nsb, 0)
    full = lambda a: pl.BlockSpec(a.shape, fixed)
    outs = [ODD_SPLITS[0], ODD_SPLITS[1], ODD_SPLITS[2]]
    return pl.pallas_call(
        _odd_proj_kernel,
        grid=(t // tm,),
        in_specs=[pl.BlockSpec((tm, d), row), full(w), full(gq), full(gk), full(avg),
                  pl.BlockSpec((tm, cos.shape[1]), pos), pl.BlockSpec((tm, sin.shape[1]), pos)],
        out_specs=[pl.BlockSpec((tm, n), row) for n in outs],
        out_shape=[jax.ShapeDtypeStruct((t, n), BF16) for n in outs],
        compiler_params=_params(("parallel",)),
        name="odd_proj",
    )(x2, w, gq, gk, avg, cos, sin)


def _gqa_attn_kernel(q_ref, k_ref, v_ref, o_ref, krep_ref, vexp_ref, s_ref, e_ref):
    j = pl.program_id(1)
    group = GQA_HEADS // GQA_KV_HEADS
    width = GQA_KV_HEADS * GQA_HEAD_DIM

    @pl.when(pl.program_id(2) == 0)
    def _():
        r = lax.broadcasted_iota(jnp.int32, (width, width), 0)
        c = lax.broadcasted_iota(jnp.int32, (width, width), 1)
        pick = (r // GQA_HEAD_DIM == j) & (r % GQA_HEAD_DIM == c % GQA_HEAD_DIM)
        for g in range(group):
            only_g = pick & (c // GQA_HEAD_DIM == g)
            vexp_ref[g] = _dot(v_ref[0], jnp.where(only_g, 1.0, 0.0).astype(BF16)).astype(BF16)
        pick_t = (c // GQA_HEAD_DIM == j) & (c % GQA_HEAD_DIM == r % GQA_HEAD_DIM)
        krep_ref[...] = _dot_nt(jnp.where(pick_t, 1.0, 0.0).astype(BF16), k_ref[0]).astype(BF16)

    q = q_ref[0]
    lane = lax.broadcasted_iota(jnp.int32, q.shape, 1)
    acc = _attend_heads(
        group,
        lambda g: jnp.where(lane // GQA_HEAD_DIM == g, q, jnp.zeros_like(q)),
        lambda g: (lambda cols: krep_ref[:, cols]),
        lambda g: vexp_ref[g],
        s_ref, e_ref)
    o_ref[0] = acc.astype(o_ref.dtype)


def _gqa_attn(q, k, v, tq):
    b, s, _ = q.shape
    width = GQA_KV_HEADS * GQA_HEAD_DIM
    group = GQA_HEADS // GQA_KV_HEADS
    return pl.pallas_call(
        _gqa_attn_kernel,
        grid=(b, GQA_KV_HEADS, s // tq),
        in_specs=[pl.BlockSpec((1, tq, width), lambda i, j, t: (i, t, j)),
                  pl.BlockSpec((1, s, width), lambda i, j, t: (i, 0, 0)),
                  pl.BlockSpec((1, s, width), lambda i, j, t: (i, 0, 0))],
        out_specs=pl.BlockSpec((1, tq, width), lambda i, j, t: (i, t, j)),
        out_shape=jax.ShapeDtypeStruct((b, s, GQA_HEADS * GQA_HEAD_DIM), BF16),
        scratch_shapes=[pltpu.VMEM((width, s), BF16), pltpu.VMEM((group, s, width), BF16),
                        pltpu.VMEM((2, tq, s), F32), pltpu.VMEM((2, tq, s), BF16)],
        compiler_params=_params(("parallel", "parallel", "arbitrary")),
        name="gqa_attn",
    )(q, k, v)


def _out_ln_kernel(alpha, x_ref, ma_ref, mb_ref, wa_ref, wb_ref, g_ref, b_ref, y_ref, yb_ref, yt_ref):
    y = alpha * x_ref[...] + _dot(ma_ref[...].astype(BF16), wa_ref[...]) \
        + _dot(mb_ref[...].astype(BF16), wb_ref[...])
    y = _layer_norm(y, g_ref[...], b_ref[...])
    y_ref[...] = y
    yb_ref[...] = y.astype(BF16)
    yt_ref[0] = y.T.astype(BF16)


def _out_ln(x2, mix_a, mix_b, col_a, col_b, w_a, w_b, g, b, alpha, seq, tm):
    t, d = x2.shape
    half = w_a.shape[0]
    nsb = seq // tm
    row = lambda i: (i, 0)
    fixed = lambda i: (0, 0)
    return pl.pallas_call(
        functools.partial(_out_ln_kernel, alpha),
        grid=(t // tm,),
        in_specs=[pl.BlockSpec((tm, d), row),
                  pl.BlockSpec((tm, half), lambda i: (i, col_a)),
                  pl.BlockSpec((tm, half), lambda i: (i, col_b)),
                  pl.BlockSpec(w_a.shape, fixed), pl.BlockSpec(w_b.shape, fixed),
                  pl.BlockSpec(g.shape, fixed), pl.BlockSpec(b.shape, fixed)],
        out_specs=[pl.BlockSpec((tm, d), row), pl.BlockSpec((tm, d), row),
                   pl.BlockSpec((1, d, tm), lambda i: (i // nsb, 0, i % nsb))],
        out_shape=[jax.ShapeDtypeStruct((t, d), F32), jax.ShapeDtypeStruct((t, d), BF16),
                   jax.ShapeDtypeStruct((t // seq, d, seq), BF16)],
        compiler_params=_params(("parallel",)),
        name="out_ln",
    )(x2, mix_a, mix_b, w_a, w_b, g, b)


def _route_kernel(cap, x_ref, rw_ref, aff_ref, slot_ref):
    x = x_ref[0]
    seq = x.shape[0]
    xh, xm, _ = _split3(x)
    wh, wm, _ = _split3(rw_ref[...])
    logits = _dot_nt(wh, xh) + (_dot_nt(wh, xm) + _dot_nt(wm, xh))
    e = jnp.exp(logits - jnp.max(logits, axis=0, keepdims=True))
    aff = e / jnp.sum(e, axis=0, keepdims=True)
    aff_ref[0] = aff
    bits = pltpu.bitcast(aff, jnp.int32)

    def pick(i, thr):
        cand = thr | (jnp.int32(1) << (30 - i))
        cnt = jnp.sum((bits >= cand).astype(jnp.int32), axis=1, keepdims=True)
        return jnp.where(cnt >= cap, cand, thr)

    thr = lax.fori_loop(0, 31, pick, jnp.zeros((bits.shape[0], 1), jnp.int32))
    above = bits > thr
    tie = bits == thr
    need = cap - jnp.sum(above.astype(jnp.int32), axis=1, keepdims=True)

    blk = 256 if seq % 256 == 0 else V7X_LANES
    r = lax.broadcasted_iota(jnp.int32, (blk, blk), 0)
    c = lax.broadcasted_iota(jnp.int32, (blk, blk), 1)
    before = jnp.where(r < c, 1.0, 0.0).astype(BF16)

    def prefix(mask):
        m = jnp.where(mask, 1.0, 0.0).astype(BF16)
        run = jnp.zeros((mask.shape[0], 1), F32)
        parts = []
        for t in range(seq // blk):
            mb = m[:, t * blk:(t + 1) * blk]
            parts.append(_dot(mb, before) + run)
            run = run + jnp.sum(mb.astype(F32), axis=1, keepdims=True)
        return jnp.concatenate(parts, axis=1).astype(jnp.int32)

    chosen = above | (tie & (prefix(tie) < need))
    slot_ref[0] = jnp.where(chosen, prefix(chosen), -1)


def _route(x1, rw_t, cap):
    b, s, d = x1.shape
    e = rw_t.shape[0]
    return pl.pallas_call(
        functools.partial(_route_kernel, cap),
        grid=(b,),
        in_specs=[pl.BlockSpec((1, s, d), lambda i: (i, 0, 0)), pl.BlockSpec((e, d), lambda i: (0, 0))],
        out_specs=[pl.BlockSpec((1, e, s), lambda i: (i, 0, 0)), pl.BlockSpec((1, e, s), lambda i: (i, 0, 0))],
        out_shape=[jax.ShapeDtypeStruct((b, e, s), F32), jax.ShapeDtypeStruct((b, e, s), jnp.int32)],
        compiler_params=_params(("parallel",)),
        name="route",
    )(x1, rw_t)


def _moe_kernel(cap, ts, xt_ref, slot_ref, aff_ref, w1_ref, w3_ref, w2_ref, o_ref):
    e_id = pl.program_id(1)
    seq = xt_ref.shape[2]
    ff = w2_ref.shape[2]

    @pl.when(e_id == 0)
    def _():
        o_ref[...] = jnp.zeros_like(o_ref)

    c_row = lax.broadcasted_iota(jnp.int32, (cap, seq), 0)
    pick = jnp.where(slot_ref[0, 0] == c_row, 1.0, 0.0).astype(BF16)
    hd2 = xt_ref.shape[1] // 2
    xg = jnp.concatenate([_dot_nt(xt_ref[0, :hd2, :], pick), _dot_nt(xt_ref[0, hd2:, :], pick)],
                         axis=0).astype(BF16)
    h1 = _dot(w1_ref[0], xg)
    h3 = _dot(w3_ref[0], xg)
    hid = (h1 * _sigmoid(h1) * h3).astype(BF16)
    ye = jnp.concatenate([_dot(w2_ref[0, :hd2, :], hid), _dot(w2_ref[0, hd2:, :], hid)],
                         axis=0).astype(BF16)
    for t in range(seq // ts):
        cols = slice(t * ts, (t + 1) * ts)
        o_ref[0, :, cols] += _dot(ye, pick[:, cols]) * aff_ref[0, 0, :, cols]


def _moe(xt, slot, aff, w1t, w3t, w2t, cap):
    b, d, s = xt.shape
    e = slot.shape[1]
    ff = w2t.shape[2]
    ts = min(512, s)
    return pl.pallas_call(
        functools.partial(_moe_kernel, cap, ts),
        grid=(b, e),
        in_specs=[pl.BlockSpec((1, d, s), lambda i, j: (i, 0, 0)),
                  pl.BlockSpec((1, 1, 1, s), lambda i, j: (i, j, 0, 0)),
                  pl.BlockSpec((1, 1, 1, s), lambda i, j: (i, j, 0, 0)),
                  pl.BlockSpec((1, ff, d), lambda i, j: (j, 0, 0)),
                  pl.BlockSpec((1, ff, d), lambda i, j: (j, 0, 0)),
                  pl.BlockSpec((1, d, ff), lambda i, j: (j, 0, 0))],
        out_specs=pl.BlockSpec((1, d, s), lambda i, j: (i, 0, 0)),
        out_shape=jax.ShapeDtypeStruct((b, d, s), F32),
        compiler_params=_params(("parallel", "arbitrary")),
        name="moe",
    )(xt, slot.reshape(b, e, 1, s), aff.reshape(b, e, 1, s), w1t, w3t, w2t)


def _ple_ln_kernel(alpha, x_ref, xb_ref, f_ref, p_ref, wg_ref, bg_ref, wp_ref, g_ref, b_ref, y_ref):
    gate = _sigmoid(_dot(xb_ref[...], wg_ref[...]) + bg_ref[...])
    ple = gate * _dot(p_ref[...].astype(BF16), wp_ref[...])
    ffn = f_ref[0].T
    y_ref[...] = _layer_norm(alpha * x_ref[...] + ffn + ple, g_ref[...], b_ref[...])


def _ple_ln(x1, x1b, ffn_t, p2, wg, bg, wp, g, b, alpha, tm):
    t, d = x1.shape
    nsb = ffn_t.shape[2] // tm
    row = lambda i: (i, 0)
    fixed = lambda i: (0, 0)
    full = lambda a: pl.BlockSpec(a.shape, fixed)
    return pl.pallas_call(
        functools.partial(_ple_ln_kernel, alpha),
        grid=(t // tm,),
        in_specs=[pl.BlockSpec((tm, d), row), pl.BlockSpec((tm, d), row),
                  pl.BlockSpec((1, d, tm), lambda i: (i // nsb, 0, i % nsb)),
                  pl.BlockSpec((tm, p2.shape[1]), row), full(wg), full(bg), full(wp), full(g), full(b)],
        out_specs=pl.BlockSpec((tm, d), row),
        out_shape=jax.ShapeDtypeStruct((t, d), F32),
        compiler_params=_params(("parallel",)),
        name="ple_ln",
    )(x1, x1b, ffn_t, p2, wg, bg, wp, g, b)


def _rope_tables(seq, rot_dim, lo, width):
    rows = seq // GRID_W
    row = jnp.repeat(jnp.arange(rows, dtype=F32), GRID_W)
    col = jnp.tile(jnp.arange(GRID_W, dtype=F32), rows)
    axis_dim = rot_dim // 2
    inv = ROPE_THETA ** (-jnp.arange(0, axis_dim, 2, dtype=F32) / axis_dim)
    ang = jnp.concatenate([row[:, None] * inv, col[:, None] * inv], axis=-1)
    cos, sin = jnp.cos(ang), jnp.sin(ang)
    cos2 = jnp.concatenate([cos, cos], axis=-1)
    sin2 = jnp.concatenate([-sin, sin], axis=-1)
    if lo == 0:
        reps = width // rot_dim
        return jnp.tile(cos2, (1, reps)), jnp.tile(sin2, (1, reps))
    pad_l = jnp.ones((seq, lo), F32)
    pad_r = jnp.ones((seq, width - lo - rot_dim), F32)
    cos_t = jnp.concatenate([pad_l, cos2, pad_r], axis=-1)
    sin_t = jnp.concatenate([0 * pad_l, sin2, 0 * pad_r], axis=-1)
    return cos_t, sin_t


def _prep_even(w_in, w_uq, w_ukv, gw_f, gb_f, gw_b, gb_b):
    d = w_in.shape[0]
    offs = np.cumsum(EVEN_SPLITS)[:-1].tolist()
    c_q, c_kv, k_pe, gq, gk, gv, g_lr, gr = jnp.split(w_in, offs, axis=-1)
    z32 = jnp.zeros((d, 32), w_in.dtype)
    chunk = jnp.concatenate([g_lr, z32, k_pe, z32], axis=-1)
    w_in_p = jnp.concatenate([c_q, c_kv, chunk, gq, gk, gv, gr], axis=-1).astype(BF16)
    uq = w_uq.reshape(MLA_Q_LORA, MLA_HEADS, MLA_NOPE + MLA_ROPE)
    uq = jnp.pad(uq, ((0, 0), (0, 0), (0, MLA_PAD - MLA_NOPE - MLA_ROPE)))
    w_uq_p = uq.reshape(MLA_Q_LORA, MLA_HEADS * MLA_PAD).astype(BF16)
    ukv = w_ukv.reshape(MLA_KV_LORA, MLA_HEADS, MLA_NOPE + MLA_V)
    uk = jnp.pad(ukv[:, :, :MLA_NOPE], ((0, 0), (0, 0), (0, MLA_PAD - MLA_NOPE)))
    w_uk_p = uk.reshape(MLA_KV_LORA, MLA_HEADS * MLA_PAD).astype(BF16)
    w_uv = ukv[:, :, MLA_NOPE:].reshape(MLA_KV_LORA, MLA_HEADS * MLA_V).astype(BF16)
    dkw = GLA_HEADS * GLA_DK
    w_gate = jnp.zeros((MLA_PAD, 2 * dkw), F32)
    w_gate = w_gate.at[0:GLA_GATE_RANK, 0:dkw].set(gw_f)
    w_gate = w_gate.at[GLA_GATE_RANK:2 * GLA_GATE_RANK, dkw:].set(gw_b).astype(BF16)
    b_gate = jnp.concatenate([gb_f, gb_b])[None, :]
    return w_in_p, w_uq_p, w_uk_p, w_uv, w_gate, b_gate


def kernel(x, p, w_in_even, mla_q_norm, w_uq, mla_kv_norm, w_ukv, gla_gate_w_fwd, gla_gate_b_fwd,
           gla_gate_w_bwd, gla_gate_b_bwd, gla_norm, w_in_odd, gqa_q_norm, gqa_k_norm, w_o, ln1_g,
           ln1_b, router_w, w1, w3, w2, ple_gate_w, ple_gate_b, ple_w, ln2_g, ln2_b):
    b, s, d = x.shape
    depth = w_o.shape[0]
    t = b * s
    alpha = (2.0 * depth) ** 0.25
    cap = EC_CAPACITY_FACTOR * s // N_EXPERTS
    tm = min(256, s)
    tq = min(512, s)
    half = w_o.shape[1] // 2
    cos_a, sin_a = _rope_tables(s, MLA_ROPE, PE_LO, MLA_PAD)
    cos_c, sin_c = _rope_tables(s, GQA_HEAD_DIM, 0, GQA_KV_HEADS * GQA_HEAD_DIM)
    hw = GQA_KV_HEADS * GQA_HEAD_DIM
    head_of = np.arange(hw) // GQA_HEAD_DIM
    avg = jnp.asarray((head_of[:, None] == head_of[None, :]) / GQA_HEAD_DIM, BF16)

    w1t = jnp.swapaxes(w1.astype(BF16), 2, 3)
    w3t = jnp.swapaxes(w3.astype(BF16), 2, 3)
    w2t = jnp.swapaxes(w2.astype(BF16), 2, 3)

    x2 = x.reshape(t, d)
    for i in range(depth):
        j = i // 2
        if i % 2 == 0:
            w_in_p, w_uq_p, w_uk_p, w_uv, w_gate, b_gate = _prep_even(
                w_in_even[j], w_uq[j], w_ukv[j], gla_gate_w_fwd[j], gla_gate_b_fwd[j],
                gla_gate_w_bwd[j], gla_gate_b_bwd[j])
            q, k, v, gq, gk, gv, la, gr = _even_proj(
                x2, w_in_p, mla_q_norm[j][None, :], w_uq_p, mla_kv_norm[j][None, :], w_uk_p, w_uv,
                w_gate, b_gate, cos_a, sin_a, s, tm)
            r3 = lambda a: a.reshape(b, s, a.shape[-1])
            o_mla = _mla_attn(r3(q), r3(k), r3(v), tq).reshape(t, -1)
            o_gla = _gla(r3(gq), r3(gk), r3(la), r3(gv), r3(gr), gla_norm[j][None, :]).reshape(t, -1)
            mix_a, mix_b, col_a, col_b = o_mla, o_gla, 0, 0
        else:
            q, k, v = _odd_proj(x2, w_in_odd[j].astype(BF16),
                                jnp.tile(gqa_q_norm[j], GQA_KV_HEADS)[None, :],
                                jnp.tile(gqa_k_norm[j], GQA_KV_HEADS)[None, :],
                                avg, cos_c, sin_c, s, tm)
            r3 = lambda a: a.reshape(b, s, a.shape[-1])
            o = _gqa_attn(r3(q), r3(k), r3(v), tq).reshape(t, -1)
            mix_a, mix_b, col_a, col_b = o, o, 0, 1
        wo = w_o[i].astype(BF16)
        x1, x1b, x1t = _out_ln(x2, mix_a, mix_b, col_a, col_b, wo[:half], wo[half:],
                               ln1_g[i][None, :], ln1_b[i][None, :], alpha, s, tm)
        aff, slot = _route(x1.reshape(b, s, d), router_w[i].T, cap)
        ffn_t = _moe(x1t, slot, aff, w1t[i], w3t[i], w2t[i], cap)
        x2 = _ple_ln(x1, x1b, ffn_t, p[i].reshape(t, -1), ple_gate_w[i].astype(BF16),
                     ple_gate_b[i][None, :], ple_w[i].astype(BF16), ln2_g[i][None, :], ln2_b[i][None, :],
                     alpha, tm)
    return x2.reshape(b, s, d)
```

```python
import functools
import math

import jax
import jax.numpy as jnp
import numpy as np
from jax import lax
from jax.experimental import pallas as pl
from jax.experimental.pallas import tpu as pltpu

F32 = jnp.float32
BF16 = jnp.bfloat16

V7X_LANES = 128
V7X_VMEM_BYTES = 64 * 1024 * 1024
VMEM_LIMIT = 56 * 1024 * 1024

GRID_W = 64
ROPE_THETA = 10000.0
EPS = 1e-6
MLA_HEADS, MLA_Q_LORA, MLA_KV_LORA = 8, 256, 128
MLA_NOPE, MLA_ROPE, MLA_V = 64, 32, 64
GLA_HEADS, GLA_DK, GLA_DV = 4, 64, 128
GLA_GATE_RANK, GLA_GATE_NORM, GLA_CHUNK = 16, 16.0, 64
GQA_HEADS, GQA_KV_HEADS, GQA_HEAD_DIM = 16, 4, 64
N_EXPERTS, EC_CAPACITY_FACTOR = 16, 2
EVEN_SPLITS = (MLA_Q_LORA, MLA_KV_LORA, MLA_ROPE, GLA_HEADS * GLA_DK, GLA_HEADS * GLA_DK,
               GLA_HEADS * GLA_DV, 2 * GLA_GATE_RANK, GLA_HEADS * GLA_DV)
ODD_SPLITS = (GQA_HEADS * GQA_HEAD_DIM, GQA_KV_HEADS * GQA_HEAD_DIM, GQA_KV_HEADS * GQA_HEAD_DIM)
MLA_PAD = V7X_LANES
PE_LO, PE_HI = MLA_NOPE, MLA_NOPE + MLA_ROPE


def _dot(a, b):
    return jnp.dot(a, b, preferred_element_type=F32)


def _dot_nt(a, b):
    return lax.dot_general(a, b, (((1,), (1,)), ((), ())), preferred_element_type=F32)


def _dot_tn(a, b):
    return lax.dot_general(a, b, (((0,), (0,)), ((), ())), preferred_element_type=F32)


def _split2(x):
    hi = x.astype(BF16)
    lo = (x - hi.astype(F32)).astype(BF16)
    return hi, lo


def _dot_sel(sel, x):
    hi, lo = _split2(x)
    return _dot(sel, hi) + _dot(sel, lo)


def _dot_x_sel(x, sel):
    hi, lo = _split2(x)
    return _dot(hi, sel) + _dot(lo, sel)


def _params(sem):
    return pltpu.CompilerParams(dimension_semantics=sem, vmem_limit_bytes=VMEM_LIMIT)


def _rot_half(x, half):
    w = x.shape[-1]
    lane = lax.broadcasted_iota(jnp.int32, x.shape, x.ndim - 1)
    first = (lane % (2 * half)) < half
    return jnp.where(first, pltpu.roll(x, w - half, x.ndim - 1), pltpu.roll(x, half, x.ndim - 1))


def _layer_norm(y, g, b):
    mu = jnp.mean(y, axis=-1, keepdims=True)
    yc = y - mu
    var = jnp.mean(yc * yc, axis=-1, keepdims=True)
    return yc * lax.rsqrt(var + EPS) * g + b


def _rms(x, g):
    return x * lax.rsqrt(jnp.mean(x * x, axis=-1, keepdims=True) + EPS) * g


def _log_sigmoid(z):
    return jnp.minimum(z, 0.0) - jnp.log1p(jnp.exp(-jnp.abs(z)))


def _sigmoid(z):
    return 1.0 / (1.0 + jnp.exp(-z))


KEY_BLK = 512
ROW_BLK = 128


def _scores(q, kt, s_scr):
    tq, seq = s_scr.shape
    m_part = None
    for c in range(seq // KEY_BLK):
        cols = slice(c * KEY_BLK, (c + 1) * KEY_BLK)
        s = _dot(q, kt(cols))
        s_scr[:, cols] = s
        for u in range(KEY_BLK // V7X_LANES):
            blk = s[:, u * V7X_LANES:(u + 1) * V7X_LANES]
            m_part = blk if m_part is None else jnp.maximum(m_part, blk)
    return jnp.max(m_part, axis=-1, keepdims=True)


def _exp_rows(m, s_scr, e_scr):
    tq, seq = s_scr.shape
    invs = []
    for r in range(tq // ROW_BLK):
        rows = slice(r * ROW_BLK, (r + 1) * ROW_BLK)
        m_b = jnp.broadcast_to(m[rows], (ROW_BLK, V7X_LANES))
        l_part = None
        for c in range(seq // V7X_LANES):
            cols = slice(c * V7X_LANES, (c + 1) * V7X_LANES)
            e = jnp.exp(s_scr[rows, cols] - m_b)
            l_part = e if l_part is None else l_part + e
            e_scr[rows, cols] = e.astype(BF16)
        invs.append(1.0 / jnp.sum(l_part, axis=-1, keepdims=True))
    return jnp.concatenate(invs, axis=0)


def _attend_heads(n_heads, q_of, kt_of, v_of, s_ref, e_ref):
    m = {0: _scores(q_of(0), kt_of(0), s_ref.at[0])}
    inv = {}
    acc = None
    for h in range(n_heads + 1):
        if h + 1 < n_heads:
            m[h + 1] = _scores(q_of(h + 1), kt_of(h + 1), s_ref.at[(h + 1) % 2])
        if h >= 1:
            o = _dot(e_ref[(h - 1) % 2], v_of(h - 1)) * inv[h - 1]
            acc = o if acc is None else acc + o
        if h < n_heads:
            inv[h] = _exp_rows(m[h], s_ref.at[h % 2], e_ref.at[h % 2])
    return acc


def _even_proj_kernel(x_ref, w_in_ref, qn_ref, w_uq_ref, kvn_ref, w_uk_ref, w_uv_ref,
                      w_gate_ref, b_gate_ref, cq_ref, sq_ref,
                      q_ref, k_ref, v_ref, gq_ref, gk_ref, gv_ref, la_ref, gr_ref):
    h = _dot(x_ref[...].astype(BF16), w_in_ref[...])
    cos = cq_ref[...]
    sin = sq_ref[...]
    lane = lax.broadcasted_iota(jnp.int32, cos.shape, 1)
    pe_lane = (lane >= PE_LO) & (lane < PE_HI)

    def rope(t):
        return t * cos + _rot_half(t, MLA_ROPE // 2) * sin

    c_q = _rms(h[:, 0:256], qn_ref[...])
    q = _dot(c_q.astype(BF16), w_uq_ref[...]) * ((MLA_NOPE + MLA_ROPE) ** -0.5)
    c_kv = _rms(h[:, 256:384], kvn_ref[...]).astype(BF16)
    kn = _dot(c_kv, w_uk_ref[...])
    v_ref[...] = _dot(c_kv, w_uv_ref[...]).astype(v_ref.dtype)
    chunk = h[:, 384:512]
    k_pe = jnp.where(pe_lane, rope(chunk), 0.0)
    for hd in range(MLA_HEADS):
        sl = slice(hd * MLA_PAD, (hd + 1) * MLA_PAD)
        q_ref[:, sl] = rope(q[:, sl]).astype(q_ref.dtype)
        k_ref[:, sl] = (kn[:, sl] + k_pe).astype(k_ref.dtype)
    z = _dot(chunk.astype(BF16), w_gate_ref[...]) + b_gate_ref[...]
    la = _log_sigmoid(z) * (1.0 / GLA_GATE_NORM)
    blk = 2 * GLA_CHUNK
    r = lax.broadcasted_iota(jnp.int32, (blk, blk), 0)
    c = lax.broadcasted_iota(jnp.int32, (blk, blk), 1)
    same = (r // GLA_CHUNK) == (c // GLA_CHUNK)
    lower = jnp.where(same & (c <= r), 1.0, 0.0).astype(BF16)
    upper = jnp.where(same & (c >= r), 1.0, 0.0).astype(BF16)
    dkw = GLA_HEADS * GLA_DK
    for t in range(la.shape[0] // blk):
        rows = slice(t * blk, (t + 1) * blk)
        la_ref[rows, :dkw] = _dot_sel(lower, la[rows, :dkw])
        la_ref[rows, dkw:] = _dot_sel(upper, la[rows, dkw:])
    gq_ref[...] = h[:, 512:768] * (GLA_DK ** -0.5)
    gk_ref[...] = h[:, 768:1024]
    gv_ref[...] = h[:, 1024:1536].astype(gv_ref.dtype)
    gr = h[:, 1536:2048]
    gr_ref[...] = gr * _sigmoid(gr)


def _even_proj(x2, w_in_p, qn, w_uq_p, kvn, w_uk_p, w_uv, w_gate, b_gate, cq, sq, seq, tm):
    t, d = x2.shape
    nsb = seq // tm
    row = lambda i: (i, 0)
    fixed = lambda i: (0, 0)
    pos = lambda i: (i % nsb, 0)
    full = lambda a: pl.BlockSpec(a.shape, fixed)
    outs = [(1024, BF16), (1024, BF16), (512, BF16), (256, F32), (256, F32), (512, BF16), (512, F32), (512, F32)]
    return pl.pallas_call(
        _even_proj_kernel,
        grid=(t // tm,),
        in_specs=[pl.BlockSpec((tm, d), row), full(w_in_p), full(qn), full(w_uq_p), full(kvn),
                  full(w_uk_p), full(w_uv), full(w_gate), full(b_gate),
                  pl.BlockSpec((tm, MLA_PAD), pos), pl.BlockSpec((tm, MLA_PAD), pos)],
        out_specs=[pl.BlockSpec((tm, n), row) for n, _ in outs],
        out_shape=[jax.ShapeDtypeStruct((t, n), dt) for n, dt in outs],
        compiler_params=_params(("parallel",)),
        name="even_proj",
    )(x2, w_in_p, qn, w_uq_p, kvn, w_uk_p, w_uv, w_gate, b_gate, cq, sq)


def _mla_attn_kernel(q_ref, k_ref, v_ref, o_ref, kt_ref, vpad_ref, s_ref, e_ref):
    out_w = 2 * MLA_V

    @pl.when(pl.program_id(2) == 0)
    def _():
        r = lax.broadcasted_iota(jnp.int32, (MLA_PAD, MLA_PAD), 0)
        c = lax.broadcasted_iota(jnp.int32, (MLA_PAD, MLA_PAD), 1)
        eye = jnp.where(r == c, 1.0, 0.0).astype(BF16)
        v = v_ref[0]
        lane = lax.broadcasted_iota(jnp.int32, v.shape, 1)
        for a in range(2):
            kt_ref[a] = _dot_nt(eye, k_ref[0, :, a * MLA_PAD:(a + 1) * MLA_PAD]).astype(BF16)
            vpad_ref[a, :, :out_w] = jnp.where((lane // MLA_V) == a, v, jnp.zeros_like(v))
            vpad_ref[a, :, out_w:] = jnp.zeros_like(v)

    acc = _attend_heads(
        2,
        lambda a: q_ref[0, :, a * MLA_PAD:(a + 1) * MLA_PAD],
        lambda a: (lambda cols: kt_ref[a, :, cols]),
        lambda a: vpad_ref[a],
        s_ref, e_ref)
    o_ref[0] = acc[:, :out_w].astype(o_ref.dtype)


def _mla_attn(q, k, v, tq):
    b, s, _ = q.shape
    return pl.pallas_call(
        _mla_attn_kernel,
        grid=(b, MLA_HEADS // 2, s // tq),
        in_specs=[pl.BlockSpec((1, tq, 2 * MLA_PAD), lambda i, j, t: (i, t, j)),
                  pl.BlockSpec((1, s, 2 * MLA_PAD), lambda i, j, t: (i, 0, j)),
                  pl.BlockSpec((1, s, 2 * MLA_V), lambda i, j, t: (i, 0, j))],
        out_specs=pl.BlockSpec((1, tq, 2 * MLA_V), lambda i, j, t: (i, t, j)),
        out_shape=jax.ShapeDtypeStruct((b, s, MLA_HEADS * MLA_V), BF16),
        scratch_shapes=[pltpu.VMEM((2, MLA_PAD, s), BF16), pltpu.VMEM((2, s, 4 * MLA_V), BF16),
                        pltpu.VMEM((2, tq, s), F32), pltpu.VMEM((2, tq, s), BF16)],
        compiler_params=_params(("parallel", "parallel", "arbitrary")),
        name="mla_attn",
    )(q, k, v)


def _gla_kernel(q_ref, k_ref, cum_ref, v_ref, gr_ref, g_ref, o_ref, state_ref, of_ref, ob_ref):
    seq = q_ref.shape[1]
    n_chunks = seq // GLA_CHUNK
    L = GLA_CHUNK
    dkw = GLA_HEADS * GLA_DK
    dvw = GLA_HEADS * GLA_DV

    def iota(shape, dim):
        return lax.broadcasted_iota(jnp.int32, shape, dim)

    row_l = iota((L, dkw), 0)
    col_m = iota((L, dkw), 1) % L
    k_own = (iota((dkw, dkw), 0) // L) == (iota((dkw, dkw), 1) // GLA_DK)
    v_own = (iota((dkw, dvw), 0) // L) == (iota((dkw, dvw), 1) // GLA_DV)
    s_own = (iota((dvw, dkw), 0) // GLA_DV) == (iota((dvw, dkw), 1) // GLA_DK)
    state_ref[...] = jnp.zeros_like(state_ref)

    def body(i, carry):
        for d, fwd in enumerate((True, False)):
            n = i if fwd else n_chunks - 1 - i
            rows = pl.ds(pl.multiple_of(n * L, L), L)
            cum = cum_ref[0, rows, d * dkw:(d + 1) * dkw]
            last = cum[L - 1:L, :] if fwd else cum[0:1, :]
            q = q_ref[0, rows, :]
            k = k_ref[0, rows, :]
            v = v_ref[0, rows, :]
            qe = (q * jnp.exp(cum)).astype(BF16)
            kg = (k * jnp.exp(-cum)).astype(BF16)
            kdec = (k * jnp.exp(last - cum)).astype(BF16)
            k_blk = jnp.where(k_own, jnp.concatenate([kg] * GLA_HEADS, axis=0), jnp.zeros((), BF16))
            att = _dot_nt(qe, k_blk)
            keep = (col_m <= row_l) if fwd else (col_m >= row_l)
            att = jnp.where(keep, att, 0.0).astype(BF16)
            v_blk = jnp.where(v_own, jnp.concatenate([v] * GLA_HEADS, axis=0), jnp.zeros((), BF16))
            st = state_ref[d]
            o = _dot(att, v_blk) + _dot_nt(qe, st.astype(BF16))
            state_ref[d] = st * jnp.exp(last) + jnp.where(s_own, _dot_tn(v, kdec), 0.0)
            if fwd:
                of_ref[rows, :] = o
            else:
                ob_ref[rows, :] = o
        return carry

    lax.fori_loop(0, n_chunks, body, 0, unroll=2)

    def finish(n, carry):
        rows = pl.ds(pl.multiple_of(n * L, L), L)
        tot = of_ref[rows, :] + ob_ref[rows, :]
        for hd in range(GLA_HEADS):
            sl = slice(hd * GLA_DV, (hd + 1) * GLA_DV)
            o_ref[0, rows, sl] = (_rms(tot[:, sl], g_ref[...]) * gr_ref[0, rows, sl]).astype(o_ref.dtype)
        return carry

    lax.fori_loop(0, n_chunks, finish, 0)


def _gla(gq, gk, cum, gv, gr, g_norm):
    b, s, _ = gq.shape
    dkw = GLA_HEADS * GLA_DK
    dvw = GLA_HEADS * GLA_DV
    blk = lambda w: pl.BlockSpec((1, s, w), lambda i: (i, 0, 0))
    return pl.pallas_call(
        _gla_kernel,
        grid=(b,),
        in_specs=[blk(dkw), blk(dkw), blk(2 * dkw), blk(dvw), blk(dvw),
                  pl.BlockSpec((1, GLA_DV), lambda i: (0, 0))],
        out_specs=blk(dvw),
        out_shape=jax.ShapeDtypeStruct((b, s, dvw), BF16),
        scratch_shapes=[pltpu.VMEM((2, dvw, dkw), F32), pltpu.VMEM((s, dvw), F32), pltpu.VMEM((s, dvw), F32)],
        compiler_params=_params(("parallel",)),
        name="gla",
    )(gq, gk, cum, gv, gr, g_norm)


def _odd_proj_kernel(x_ref, w_ref, gq_ref, gk_ref, avg_ref, cos_ref, sin_ref, q_ref, k_ref, v_ref):
    h = _dot(x_ref[...].astype(BF16), w_ref[...])
    cos = cos_ref[...]
    sin = sin_ref[...]
    avg = avg_ref[...]
    nq = GQA_HEADS * GQA_HEAD_DIM
    nk = GQA_KV_HEADS * GQA_HEAD_DIM

    def norm_rope(t, g):
        ms = _dot_x_sel(t * t, avg)
        t = t * lax.rsqrt(ms + EPS) * g
        return t * cos + _rot_half(t, GQA_HEAD_DIM // 2) * sin

    for c in range(nq // nk):
        sl = slice(c * nk, (c + 1) * nk)
        q_ref[:, sl] = (norm_rope(h[:, sl], gq_ref[...]) * (GQA_HEAD_DIM ** -0.5)).astype(q_ref.dtype)
    k_ref[...] = norm_rope(h[:, nq:nq + nk], gk_ref[...]).astype(k_ref.dtype)
    v_ref[...] = h[:, nq + nk:].astype(v_ref.dtype)


def _odd_proj(x2, w, gq, gk, avg, cos, sin, seq, tm):
    t, d = x2.shape
    nsb = seq // tm
    row = lambda i: (i, 0)
    fixed = lambda i: (0, 0)
    pos = lambda i: (i % nsb, 0)
    full = lambda a: pl.BlockSpec(a.shape, fixed)
    outs = [ODD_SPLITS[0], ODD_SPLITS[1], ODD_SPLITS[2]]
    return pl.pallas_call(
        _odd_proj_kernel,
        grid=(t // tm,),
        in_specs=[pl.BlockSpec((tm, d), row), full(w), full(gq), full(gk), full(avg),
                  pl.BlockSpec((tm, cos.shape[1]), pos), pl.BlockSpec((tm, sin.shape[1]), pos)],
        out_specs=[pl.BlockSpec((tm, n), row) for n in outs],
        out_shape=[jax.ShapeDtypeStruct((t, n), BF16) for n in outs],
        compiler_params=_params(("parallel",)),
        name="odd_proj",
    )(x2, w, gq, gk, avg, cos, sin)


def _gqa_attn_kernel(q_ref, k_ref, v_ref, o_ref, krep_ref, vexp_ref, s_ref, e_ref):
    j = pl.program_id(1)
    group = GQA_HEADS // GQA_KV_HEADS
    width = GQA_KV_HEADS * GQA_HEAD_DIM

    @pl.when(pl.program_id(2) == 0)
    def _():
        r = lax.broadcasted_iota(jnp.int32, (width, width), 0)
        c = lax.broadcasted_iota(jnp.int32, (width, width), 1)
        pick = (r // GQA_HEAD_DIM == j) & (r % GQA_HEAD_DIM == c % GQA_HEAD_DIM)
        for g in range(group):
            only_g = pick & (c // GQA_HEAD_DIM == g)
            vexp_ref[g] = _dot(v_ref[0], jnp.where(only_g, 1.0, 0.0).astype(BF16)).astype(BF16)
        pick_t = (c // GQA_HEAD_DIM == j) & (c % GQA_HEAD_DIM == r % GQA_HEAD_DIM)
        krep_ref[...] = _dot_nt(jnp.where(pick_t, 1.0, 0.0).astype(BF16), k_ref[0]).astype(BF16)

    q = q_ref[0]
    lane = lax.broadcasted_iota(jnp.int32, q.shape, 1)
    acc = _attend_heads(
        group,
        lambda g: jnp.where(lane // GQA_HEAD_DIM == g, q, jnp.zeros_like(q)),
        lambda g: (lambda cols: krep_ref[:, cols]),
        lambda g: vexp_ref[g],
        s_ref, e_ref)
    o_ref[0] = acc.astype(o_ref.dtype)


def _gqa_attn(q, k, v, tq):
    b, s, _ = q.shape
    width = GQA_KV_HEADS * GQA_HEAD_DIM
    group = GQA_HEADS // GQA_KV_HEADS
    return pl.pallas_call(
        _gqa_attn_kernel,
        grid=(b, GQA_KV_HEADS, s // tq),
        in_specs=[pl.BlockSpec((1, tq, width), lambda i, j, t: (i, t, j)),
                  pl.BlockSpec((1, s, width), lambda i, j, t: (i, 0, 0)),
                  pl.BlockSpec((1, s, width), lambda i, j, t: (i, 0, 0))],
        out_specs=pl.BlockSpec((1, tq, width), lambda i, j, t: (i, t, j)),
        out_shape=jax.ShapeDtypeStruct((b, s, GQA_HEADS * GQA_HEAD_DIM), BF16),
        scratch_shapes=[pltpu.VMEM((width, s), BF16), pltpu.VMEM((group, s, width), BF16),
                        pltpu.VMEM((2, tq, s), F32), pltpu.VMEM((2, tq, s), BF16)],
        compiler_params=_params(("parallel", "parallel", "arbitrary")),
        name="gqa_attn",
    )(q, k, v)


def _out_ln_kernel(alpha, x_ref, ma_ref, mb_ref, wa_ref, wb_ref, g_ref, b_ref, y_ref, yb_ref, yt_ref):
    y = alpha * x_ref[...] + _dot(ma_ref[...].astype(BF16), wa_ref[...]) \
        + _dot(mb_ref[...].astype(BF16), wb_ref[...])
    y = _layer_norm(y, g_ref[...], b_ref[...])
    y_ref[...] = y
    yb_ref[...] = y.astype(BF16)
    yt_ref[0] = y.T.astype(BF16)


def _out_ln(x2, mix_a, mix_b, col_a, col_b, w_a, w_b, g, b, alpha, seq, tm):
    t, d = x2.shape
    half = w_a.shape[0]
    nsb = seq // tm
    row = lambda i: (i, 0)
    fixed = lambda i: (0, 0)
    return pl.pallas_call(
        functools.partial(_out_ln_kernel, alpha),
        grid=(t // tm,),
        in_specs=[pl.BlockSpec((tm, d), row),
                  pl.BlockSpec((tm, half), lambda i: (i, col_a)),
                  pl.BlockSpec((tm, half), lambda i: (i, col_b)),
                  pl.BlockSpec(w_a.shape, fixed), pl.BlockSpec(w_b.shape, fixed),
                  pl.BlockSpec(g.shape, fixed), pl.BlockSpec(b.shape, fixed)],
        out_specs=[pl.BlockSpec((tm, d), row), pl.BlockSpec((tm, d), row),
                   pl.BlockSpec((1, d, tm), lambda i: (i // nsb, 0, i % nsb))],
        out_shape=[jax.ShapeDtypeStruct((t, d), F32), jax.ShapeDtypeStruct((t, d), BF16),
                   jax.ShapeDtypeStruct((t // seq, d, seq), BF16)],
        compiler_params=_params(("parallel",)),
        name="out_ln",
    )(x2, mix_a, mix_b, w_a, w_b, g, b)


def _route_kernel(cap, x_ref, rw_ref, aff_ref, slot_ref):
    x = x_ref[0]
    seq = x.shape[0]
    xh, xm = _split2(x)
    wh, wm = _split2(rw_ref[...])
    logits = _dot_nt(wh, xh) + (_dot_nt(wh, xm) + _dot_nt(wm, xh))
    e = jnp.exp(logits - jnp.max(logits, axis=0, keepdims=True))
    aff = e / jnp.sum(e, axis=0, keepdims=True)
    aff_ref[0] = aff
    bits = pltpu.bitcast(aff, jnp.int32)

    def pick(i, thr):
        cand = thr | (jnp.int32(1) << (30 - i))
        cnt = jnp.sum((bits >= cand).astype(jnp.int32), axis=1, keepdims=True)
        return jnp.where(cnt >= cap, cand, thr)

    thr = lax.fori_loop(0, 31, pick, jnp.zeros((bits.shape[0], 1), jnp.int32))
    above = bits > thr
    tie = bits == thr
    need = cap - jnp.sum(above.astype(jnp.int32), axis=1, keepdims=True)

    blk = 256 if seq % 256 == 0 else V7X_LANES
    r = lax.broadcasted_iota(jnp.int32, (blk, blk), 0)
    c = lax.broadcasted_iota(jnp.int32, (blk, blk), 1)
    before = jnp.where(r < c, 1.0, 0.0).astype(BF16)

    def prefix(mask):
        m = jnp.where(mask, 1.0, 0.0).astype(BF16)
        run = jnp.zeros((mask.shape[0], 1), F32)
        parts = []
        for t in range(seq // blk):
            mb = m[:, t * blk:(t + 1) * blk]
            parts.append(_dot(mb, before) + run)
            run = run + jnp.sum(mb.astype(F32), axis=1, keepdims=True)
        return jnp.concatenate(parts, axis=1).astype(jnp.int32)

    chosen = above | (tie & (prefix(tie) < need))
    slot_ref[0] = jnp.where(chosen, prefix(chosen), -1)


def _route(x1, rw_t, cap):
    b, s, d = x1.shape
    e = rw_t.shape[0]
    return pl.pallas_call(
        functools.partial(_route_kernel, cap),
        grid=(b,),
        in_specs=[pl.BlockSpec((1, s, d), lambda i: (i, 0, 0)), pl.BlockSpec((e, d), lambda i: (0, 0))],
        out_specs=[pl.BlockSpec((1, e, s), lambda i: (i, 0, 0)), pl.BlockSpec((1, e, s), lambda i: (i, 0, 0))],
        out_shape=[jax.ShapeDtypeStruct((b, e, s), F32), jax.ShapeDtypeStruct((b, e, s), jnp.int32)],
        compiler_params=_params(("parallel",)),
        name="route",
    )(x1, rw_t)


def _moe_kernel(cap, ts, xt_ref, slot_ref, aff_ref, w1_ref, w3_ref, w2_ref, o_ref):
    e_id = pl.program_id(1)
    seq = xt_ref.shape[2]
    ff = w2_ref.shape[2]

    @pl.when(e_id == 0)
    def _():
        o_ref[...] = jnp.zeros_like(o_ref)

    c_row = lax.broadcasted_iota(jnp.int32, (cap, seq), 0)
    pick = jnp.where(slot_ref[0, 0] == c_row, 1.0, 0.0).astype(BF16)
    hd2 = xt_ref.shape[1] // 2
    xg = jnp.concatenate([_dot_nt(xt_ref[0, :hd2, :], pick), _dot_nt(xt_ref[0, hd2:, :], pick)],
                         axis=0).astype(BF16)
    h1 = _dot(w1_ref[0], xg)
    h3 = _dot(w3_ref[0], xg)
    hid = (h1 * _sigmoid(h1) * h3).astype(BF16)
    ye = jnp.concatenate([_dot(w2_ref[0, :hd2, :], hid), _dot(w2_ref[0, hd2:, :], hid)],
                         axis=0).astype(BF16)
    for t in range(seq // ts):
        cols = slice(t * ts, (t + 1) * ts)
        o_ref[0, :, cols] += _dot(ye, pick[:, cols]) * aff_ref[0, 0, :, cols]


def _moe(xt, slot, aff, w1t, w3t, w2t, cap):
    b, d, s = xt.shape
    e = slot.shape[1]
    ff = w2t.shape[2]
    ts = min(512, s)
    return pl.pallas_call(
        functools.partial(_moe_kernel, cap, ts),
        grid=(b, e),
        in_specs=[pl.BlockSpec((1, d, s), lambda i, j: (i, 0, 0)),
                  pl.BlockSpec((1, 1, 1, s), lambda i, j: (i, j, 0, 0)),
                  pl.BlockSpec((1, 1, 1, s), lambda i, j: (i, j, 0, 0)),
                  pl.BlockSpec((1, ff, d), lambda i, j: (j, 0, 0)),
                  pl.BlockSpec((1, ff, d), lambda i, j: (j, 0, 0)),
                  pl.BlockSpec((1, d, ff), lambda i, j: (j, 0, 0))],
        out_specs=pl.BlockSpec((1, d, s), lambda i, j: (i, 0, 0)),
        out_shape=jax.ShapeDtypeStruct((b, d, s), F32),
        compiler_params=_params(("parallel", "arbitrary")),
        name="moe",
    )(xt, slot.reshape(b, e, 1, s), aff.reshape(b, e, 1, s), w1t, w3t, w2t)


def _transpose_cast_kernel(w_ref, o_ref):
    o_ref[0] = w_ref[0].T.astype(o_ref.dtype)


def _transpose_cast(w):
    n, r, c = w.shape
    return pl.pallas_call(
        _transpose_cast_kernel,
        grid=(n,),
        in_specs=[pl.BlockSpec((1, r, c), lambda i: (i, 0, 0))],
        out_specs=pl.BlockSpec((1, c, r), lambda i: (i, 0, 0)),
        out_shape=jax.ShapeDtypeStruct((n, c, r), BF16),
        compiler_params=_params(("parallel",)),
        name="transpose_cast",
    )(w)


def _ple_ln_kernel(alpha, x_ref, xb_ref, f_ref, p_ref, wg_ref, bg_ref, wp_ref, g_ref, b_ref, y_ref):
    gate = _sigmoid(_dot(xb_ref[...], wg_ref[...]) + bg_ref[...])
    ple = gate * _dot(p_ref[...].astype(BF16), wp_ref[...])
    ffn = f_ref[0].T
    y_ref[...] = _layer_norm(alpha * x_ref[...] + ffn + ple, g_ref[...], b_ref[...])


def _ple_ln(x1, x1b, ffn_t, p2, wg, bg, wp, g, b, alpha, tm):
    t, d = x1.shape
    nsb = ffn_t.shape[2] // tm
    row = lambda i: (i, 0)
    fixed = lambda i: (0, 0)
    full = lambda a: pl.BlockSpec(a.shape, fixed)
    return pl.pallas_call(
        functools.partial(_ple_ln_kernel, alpha),
        grid=(t // tm,),
        in_specs=[pl.BlockSpec((tm, d), row), pl.BlockSpec((tm, d), row),
                  pl.BlockSpec((1, d, tm), lambda i: (i // nsb, 0, i % nsb)),
                  pl.BlockSpec((tm, p2.shape[1]), row), full(wg), full(bg), full(wp), full(g), full(b)],
        out_specs=pl.BlockSpec((tm, d), row),
        out_shape=jax.ShapeDtypeStruct((t, d), F32),
        compiler_params=_params(("parallel",)),
        name="ple_ln",
    )(x1, x1b, ffn_t, p2, wg, bg, wp, g, b)


def _rope_tables(seq, rot_dim, lo, width):
    rows = seq // GRID_W
    row = jnp.repeat(jnp.arange(rows, dtype=F32), GRID_W)
    col = jnp.tile(jnp.arange(GRID_W, dtype=F32), rows)
    axis_dim = rot_dim // 2
    inv = ROPE_THETA ** (-jnp.arange(0, axis_dim, 2, dtype=F32) / axis_dim)
    ang = jnp.concatenate([row[:, None] * inv, col[:, None] * inv], axis=-1)
    cos, sin = jnp.cos(ang), jnp.sin(ang)
    cos2 = jnp.concatenate([cos, cos], axis=-1)
    sin2 = jnp.concatenate([-sin, sin], axis=-1)
    if lo == 0:
        reps = width // rot_dim
        return jnp.tile(cos2, (1, reps)), jnp.tile(sin2, (1, reps))
    pad_l = jnp.ones((seq, lo), F32)
    pad_r = jnp.ones((seq, width - lo - rot_dim), F32)
    cos_t = jnp.concatenate([pad_l, cos2, pad_r], axis=-1)
    sin_t = jnp.concatenate([0 * pad_l, sin2, 0 * pad_r], axis=-1)
    return cos_t, sin_t


def _prep_even(w_in, w_uq, w_ukv, gw_f, gb_f, gw_b, gb_b):
    d = w_in.shape[0]
    offs = np.cumsum(EVEN_SPLITS)[:-1].tolist()
    c_q, c_kv, k_pe, gq, gk, gv, g_lr, gr = jnp.split(w_in, offs, axis=-1)
    z32 = jnp.zeros((d, 32), w_in.dtype)
    chunk = jnp.concatenate([g_lr, z32, k_pe, z32], axis=-1)
    w_in_p = jnp.concatenate([c_q, c_kv, chunk, gq, gk, gv, gr], axis=-1).astype(BF16)
    uq = w_uq.reshape(MLA_Q_LORA, MLA_HEADS, MLA_NOPE + MLA_ROPE)
    uq = jnp.pad(uq, ((0, 0), (0, 0), (0, MLA_PAD - MLA_NOPE - MLA_ROPE)))
    w_uq_p = uq.reshape(MLA_Q_LORA, MLA_HEADS * MLA_PAD).astype(BF16)
    ukv = w_ukv.reshape(MLA_KV_LORA, MLA_HEADS, MLA_NOPE + MLA_V)
    uk = jnp.pad(ukv[:, :, :MLA_NOPE], ((0, 0), (0, 0), (0, MLA_PAD - MLA_NOPE)))
    w_uk_p = uk.reshape(MLA_KV_LORA, MLA_HEADS * MLA_PAD).astype(BF16)
    w_uv = ukv[:, :, MLA_NOPE:].reshape(MLA_KV_LORA, MLA_HEADS * MLA_V).astype(BF16)
    dkw = GLA_HEADS * GLA_DK
    w_gate = jnp.zeros((MLA_PAD, 2 * dkw), F32)
    w_gate = w_gate.at[0:GLA_GATE_RANK, 0:dkw].set(gw_f)
    w_gate = w_gate.at[GLA_GATE_RANK:2 * GLA_GATE_RANK, dkw:].set(gw_b).astype(BF16)
    b_gate = jnp.concatenate([gb_f, gb_b])[None, :]
    return w_in_p, w_uq_p, w_uk_p, w_uv, w_gate, b_gate


def kernel(x, p, w_in_even, mla_q_norm, w_uq, mla_kv_norm, w_ukv, gla_gate_w_fwd, gla_gate_b_fwd,
           gla_gate_w_bwd, gla_gate_b_bwd, gla_norm, w_in_odd, gqa_q_norm, gqa_k_norm, w_o, ln1_g,
           ln1_b, router_w, w1, w3, w2, ple_gate_w, ple_gate_b, ple_w, ln2_g, ln2_b):
    b, s, d = x.shape
    depth = w_o.shape[0]
    t = b * s
    alpha = (2.0 * depth) ** 0.25
    cap = EC_CAPACITY_FACTOR * s // N_EXPERTS
    tm = min(512, s)
    tq = min(512, s)
    half = w_o.shape[1] // 2
    cos_a, sin_a = _rope_tables(s, MLA_ROPE, PE_LO, MLA_PAD)
    cos_c, sin_c = _rope_tables(s, GQA_HEAD_DIM, 0, GQA_KV_HEADS * GQA_HEAD_DIM)
    hw = GQA_KV_HEADS * GQA_HEAD_DIM
    head_of = np.arange(hw) // GQA_HEAD_DIM
    avg = jnp.asarray((head_of[:, None] == head_of[None, :]) / GQA_HEAD_DIM, BF16)

    n_e, ff = w1.shape[1], w1.shape[3]
    w1t = _transpose_cast(w1.reshape(depth * n_e, d, ff)).reshape(depth, n_e, ff, d)
    w3t = _transpose_cast(w3.reshape(depth * n_e, d, ff)).reshape(depth, n_e, ff, d)
    w2t = _transpose_cast(w2.reshape(depth * n_e, ff, d)).reshape(depth, n_e, d, ff)

    x2 = x.reshape(t, d)
    for i in range(depth):
        j = i // 2
        if i % 2 == 0:
            w_in_p, w_uq_p, w_uk_p, w_uv, w_gate, b_gate = _prep_even(
                w_in_even[j], w_uq[j], w_ukv[j], gla_gate_w_fwd[j], gla_gate_b_fwd[j],
                gla_gate_w_bwd[j], gla_gate_b_bwd[j])
            q, k, v, gq, gk, gv, la, gr = _even_proj(
                x2, w_in_p, mla_q_norm[j][None, :], w_uq_p, mla_kv_norm[j][None, :], w_uk_p, w_uv,
                w_gate, b_gate, cos_a, sin_a, s, tm)
            r3 = lambda a: a.reshape(b, s, a.shape[-1])
            o_mla = _mla_attn(r3(q), r3(k), r3(v), tq).reshape(t, -1)
            o_gla = _gla(r3(gq), r3(gk), r3(la), r3(gv), r3(gr), gla_norm[j][None, :]).reshape(t, -1)
            mix_a, mix_b, col_a, col_b = o_mla, o_gla, 0, 0
        else:
            q, k, v = _odd_proj(x2, w_in_odd[j].astype(BF16),
                                jnp.tile(gqa_q_norm[j], GQA_KV_HEADS)[None, :],
                                jnp.tile(gqa_k_norm[j], GQA_KV_HEADS)[None, :],
                                avg, cos_c, sin_c, s, tm)
            r3 = lambda a: a.reshape(b, s, a.shape[-1])
            o = _gqa_attn(r3(q), r3(k), r3(v), tq).reshape(t, -1)
            mix_a, mix_b, col_a, col_b = o, o, 0, 1
        wo = w_o[i].astype(BF16)
        x1, x1b, x1t = _out_ln(x2, mix_a, mix_b, col_a, col_b, wo[:half], wo[half:],
                               ln1_g[i][None, :], ln1_b[i][None, :], alpha, s, tm)
        aff, slot = _route(x1.reshape(b, s, d), router_w[i].T, cap)
        ffn_t = _moe(x1t, slot, aff, w1t[i], w3t[i], w2t[i], cap)
        x2 = _ple_ln(x1, x1b, ffn_t, p[i].reshape(t, -1), ple_gate_w[i].astype(BF16),
                     ple_gate_b[i][None, :], ple_w[i].astype(BF16), ln2_g[i][None, :], ln2_b[i][None, :],
                     alpha, tm)
    return x2.reshape(b, s, d)
```

```python
import functools
import math

import jax
import jax.numpy as jnp
import numpy as np
from jax import lax
from jax.experimental import pallas as pl
from jax.experimental.pallas import tpu as pltpu

F32 = jnp.float32
BF16 = jnp.bfloat16

V7X_LANES = 128
V7X_VMEM_BYTES = 64 * 1024 * 1024
VMEM_LIMIT = 56 * 1024 * 1024

GRID_W = 64
ROPE_THETA = 10000.0
EPS = 1e-6
MLA_HEADS, MLA_Q_LORA, MLA_KV_LORA = 8, 256, 128
MLA_NOPE, MLA_ROPE, MLA_V = 64, 32, 64
GLA_HEADS, GLA_DK, GLA_DV = 4, 64, 128
GLA_GATE_RANK, GLA_GATE_NORM, GLA_CHUNK = 16, 16.0, 64
GQA_HEADS, GQA_KV_HEADS, GQA_HEAD_DIM = 16, 4, 64
N_EXPERTS, EC_CAPACITY_FACTOR = 16, 2
EVEN_SPLITS = (MLA_Q_LORA, MLA_KV_LORA, MLA_ROPE, GLA_HEADS * GLA_DK, GLA_HEADS * GLA_DK,
               GLA_HEADS * GLA_DV, 2 * GLA_GATE_RANK, GLA_HEADS * GLA_DV)
ODD_SPLITS = (GQA_HEADS * GQA_HEAD_DIM, GQA_KV_HEADS * GQA_HEAD_DIM, GQA_KV_HEADS * GQA_HEAD_DIM)
MLA_PAD = V7X_LANES
PE_LO, PE_HI = MLA_NOPE, MLA_NOPE + MLA_ROPE


def _dot(a, b):
    return jnp.dot(a, b, preferred_element_type=F32)


def _dot_nt(a, b):
    return lax.dot_general(a, b, (((1,), (1,)), ((), ())), preferred_element_type=F32)


def _dot_tn(a, b):
    return lax.dot_general(a, b, (((0,), (0,)), ((), ())), preferred_element_type=F32)


def _split2(x):
    hi = x.astype(BF16)
    lo = (x - hi.astype(F32)).astype(BF16)
    return hi, lo


def _dot_sel(sel, x):
    hi, lo = _split2(x)
    return _dot(sel, hi) + _dot(sel, lo)


def _dot_x_sel(x, sel):
    hi, lo = _split2(x)
    return _dot(hi, sel) + _dot(lo, sel)


def _params(sem):
    return pltpu.CompilerParams(dimension_semantics=sem, vmem_limit_bytes=VMEM_LIMIT)


def _rot_half(x, half):
    w = x.shape[-1]
    lane = lax.broadcasted_iota(jnp.int32, x.shape, x.ndim - 1)
    first = (lane % (2 * half)) < half
    return jnp.where(first, pltpu.roll(x, w - half, x.ndim - 1), pltpu.roll(x, half, x.ndim - 1))


def _layer_norm(y, g, b):
    mu = jnp.mean(y, axis=-1, keepdims=True)
    yc = y - mu
    var = jnp.mean(yc * yc, axis=-1, keepdims=True)
    return yc * lax.rsqrt(var + EPS) * g + b


def _rms(x, g):
    return x * lax.rsqrt(jnp.mean(x * x, axis=-1, keepdims=True) + EPS) * g


def _log_sigmoid(z):
    return jnp.minimum(z, 0.0) - jnp.log1p(jnp.exp(-jnp.abs(z)))


def _sigmoid(z):
    return 1.0 / (1.0 + jnp.exp(-z))


KEY_BLK = 512
ROW_BLK = 128


def _scores(q, kt, s_scr):
    tq, seq = s_scr.shape
    m_part = None
    for c in range(seq // KEY_BLK):
        cols = slice(c * KEY_BLK, (c + 1) * KEY_BLK)
        s = _dot(q, kt(cols))
        s_scr[:, cols] = s
        for u in range(KEY_BLK // V7X_LANES):
            blk = s[:, u * V7X_LANES:(u + 1) * V7X_LANES]
            m_part = blk if m_part is None else jnp.maximum(m_part, blk)
    return jnp.max(m_part, axis=-1, keepdims=True)


def _exp_rows(m, s_scr, e_scr):
    tq, seq = s_scr.shape
    invs = []
    for r in range(tq // ROW_BLK):
        rows = slice(r * ROW_BLK, (r + 1) * ROW_BLK)
        m_b = jnp.broadcast_to(m[rows], (ROW_BLK, V7X_LANES))
        l_part = None
        for c in range(seq // V7X_LANES):
            cols = slice(c * V7X_LANES, (c + 1) * V7X_LANES)
            e = jnp.exp(s_scr[rows, cols] - m_b)
            l_part = e if l_part is None else l_part + e
            e_scr[rows, cols] = e.astype(BF16)
        invs.append(1.0 / jnp.sum(l_part, axis=-1, keepdims=True))
    return jnp.concatenate(invs, axis=0)


def _attend_heads(n_heads, q_of, kt_of, v_of, s_ref, e_ref):
    m = {0: _scores(q_of(0), kt_of(0), s_ref.at[0])}
    inv = {}
    acc = None
    for h in range(n_heads + 1):
        if h + 1 < n_heads:
            m[h + 1] = _scores(q_of(h + 1), kt_of(h + 1), s_ref.at[(h + 1) % 2])
        if h >= 1:
            o = _dot(e_ref[(h - 1) % 2], v_of(h - 1)) * inv[h - 1]
            acc = o if acc is None else acc + o
        if h < n_heads:
            inv[h] = _exp_rows(m[h], s_ref.at[h % 2], e_ref.at[h % 2])
    return acc


def _even_proj_kernel(x_ref, w_in_ref, qn_ref, w_uq_ref, kvn_ref, w_uk_ref, w_uv_ref,
                      w_gate_ref, b_gate_ref, cq_ref, sq_ref,
                      q_ref, k_ref, v_ref, gq_ref, gk_ref, gv_ref, la_ref, gr_ref):
    h = _dot(x_ref[...].astype(BF16), w_in_ref[...])
    cos = cq_ref[...]
    sin = sq_ref[...]
    lane = lax.broadcasted_iota(jnp.int32, cos.shape, 1)
    pe_lane = (lane >= PE_LO) & (lane < PE_HI)

    def rope(t):
        return t * cos + _rot_half(t, MLA_ROPE // 2) * sin

    c_q = _rms(h[:, 0:256], qn_ref[...])
    q = _dot(c_q.astype(BF16), w_uq_ref[...]) * ((MLA_NOPE + MLA_ROPE) ** -0.5)
    c_kv = _rms(h[:, 256:384], kvn_ref[...]).astype(BF16)
    kn = _dot(c_kv, w_uk_ref[...])
    v_ref[...] = _dot(c_kv, w_uv_ref[...]).astype(v_ref.dtype)
    chunk = h[:, 384:512]
    k_pe = jnp.where(pe_lane, rope(chunk), 0.0)
    for hd in range(MLA_HEADS):
        sl = slice(hd * MLA_PAD, (hd + 1) * MLA_PAD)
        q_ref[:, sl] = rope(q[:, sl]).astype(q_ref.dtype)
        k_ref[:, sl] = (kn[:, sl] + k_pe).astype(k_ref.dtype)
    z = _dot(chunk.astype(BF16), w_gate_ref[...]) + b_gate_ref[...]
    la = _log_sigmoid(z) * (1.0 / GLA_GATE_NORM)
    blk = 2 * GLA_CHUNK
    r = lax.broadcasted_iota(jnp.int32, (blk, blk), 0)
    c = lax.broadcasted_iota(jnp.int32, (blk, blk), 1)
    same = (r // GLA_CHUNK) == (c // GLA_CHUNK)
    lower = jnp.where(same & (c <= r), 1.0, 0.0).astype(BF16)
    upper = jnp.where(same & (c >= r), 1.0, 0.0).astype(BF16)
    dkw = GLA_HEADS * GLA_DK
    for t in range(la.shape[0] // blk):
        rows = slice(t * blk, (t + 1) * blk)
        la_ref[rows, :dkw] = _dot_sel(lower, la[rows, :dkw])
        la_ref[rows, dkw:] = _dot_sel(upper, la[rows, dkw:])
    gq_ref[...] = h[:, 512:768] * (GLA_DK ** -0.5)
    gk_ref[...] = h[:, 768:1024]
    gv_ref[...] = h[:, 1024:1536].astype(gv_ref.dtype)
    gr = h[:, 1536:2048]
    gr_ref[...] = gr * _sigmoid(gr)


def _even_proj(x2, w_in_p, qn, w_uq_p, kvn, w_uk_p, w_uv, w_gate, b_gate, cq, sq, seq, tm):
    t, d = x2.shape
    nsb = seq // tm
    row = lambda i: (i, 0)
    fixed = lambda i: (0, 0)
    pos = lambda i: (i % nsb, 0)
    full = lambda a: pl.BlockSpec(a.shape, fixed)
    outs = [(1024, BF16), (1024, BF16), (512, BF16), (256, F32), (256, F32), (512, BF16), (512, F32), (512, F32)]
    return pl.pallas_call(
        _even_proj_kernel,
        grid=(t // tm,),
        in_specs=[pl.BlockSpec((tm, d), row), full(w_in_p), full(qn), full(w_uq_p), full(kvn),
                  full(w_uk_p), full(w_uv), full(w_gate), full(b_gate),
                  pl.BlockSpec((tm, MLA_PAD), pos), pl.BlockSpec((tm, MLA_PAD), pos)],
        out_specs=[pl.BlockSpec((tm, n), row) for n, _ in outs],
        out_shape=[jax.ShapeDtypeStruct((t, n), dt) for n, dt in outs],
        compiler_params=_params(("parallel",)),
        name="even_proj",
    )(x2, w_in_p, qn, w_uq_p, kvn, w_uk_p, w_uv, w_gate, b_gate, cq, sq)


def _mla_attn_kernel(q_ref, k_ref, v_ref, o_ref, kt_ref, vpad_ref, s_ref, e_ref):
    out_w = 2 * MLA_V

    @pl.when(pl.program_id(2) == 0)
    def _():
        r = lax.broadcasted_iota(jnp.int32, (MLA_PAD, MLA_PAD), 0)
        c = lax.broadcasted_iota(jnp.int32, (MLA_PAD, MLA_PAD), 1)
        eye = jnp.where(r == c, 1.0, 0.0).astype(BF16)
        v = v_ref[0]
        lane = lax.broadcasted_iota(jnp.int32, v.shape, 1)
        for a in range(2):
            kt_ref[a] = _dot_nt(eye, k_ref[0, :, a * MLA_PAD:(a + 1) * MLA_PAD]).astype(BF16)
            vpad_ref[a, :, :out_w] = jnp.where((lane // MLA_V) == a, v, jnp.zeros_like(v))
            vpad_ref[a, :, out_w:] = jnp.zeros_like(v)

    acc = _attend_heads(
        2,
        lambda a: q_ref[0, :, a * MLA_PAD:(a + 1) * MLA_PAD],
        lambda a: (lambda cols: kt_ref[a, :, cols]),
        lambda a: vpad_ref[a],
        s_ref, e_ref)
    o_ref[0] = acc[:, :out_w].astype(o_ref.dtype)


def _mla_attn(q, k, v, tq):
    b, s, _ = q.shape
    return pl.pallas_call(
        _mla_attn_kernel,
        grid=(b, MLA_HEADS // 2, s // tq),
        in_specs=[pl.BlockSpec((1, tq, 2 * MLA_PAD), lambda i, j, t: (i, t, j)),
                  pl.BlockSpec((1, s, 2 * MLA_PAD), lambda i, j, t: (i, 0, j)),
                  pl.BlockSpec((1, s, 2 * MLA_V), lambda i, j, t: (i, 0, j))],
        out_specs=pl.BlockSpec((1, tq, 2 * MLA_V), lambda i, j, t: (i, t, j)),
        out_shape=jax.ShapeDtypeStruct((b, s, MLA_HEADS * MLA_V), BF16),
        scratch_shapes=[pltpu.VMEM((2, MLA_PAD, s), BF16), pltpu.VMEM((2, s, 4 * MLA_V), BF16),
                        pltpu.VMEM((2, tq, s), F32), pltpu.VMEM((2, tq, s), BF16)],
        compiler_params=_params(("parallel", "parallel", "arbitrary")),
        name="mla_attn",
    )(q, k, v)


def _gla_kernel(q_ref, k_ref, cum_ref, v_ref, gr_ref, g_ref, o_ref, state_ref, of_ref, ob_ref):
    seq = q_ref.shape[1]
    n_chunks = seq // GLA_CHUNK
    L = GLA_CHUNK
    dkw = GLA_HEADS * GLA_DK
    dvw = GLA_HEADS * GLA_DV

    def iota(shape, dim):
        return lax.broadcasted_iota(jnp.int32, shape, dim)

    row_l = iota((L, dkw), 0)
    col_m = iota((L, dkw), 1) % L
    k_own = (iota((dkw, dkw), 0) // L) == (iota((dkw, dkw), 1) // GLA_DK)
    v_own = (iota((dkw, dvw), 0) // L) == (iota((dkw, dvw), 1) // GLA_DV)
    s_own = (iota((dvw, dkw), 0) // GLA_DV) == (iota((dvw, dkw), 1) // GLA_DK)
    state_ref[...] = jnp.zeros_like(state_ref)

    def body(i, carry):
        for d, fwd in enumerate((True, False)):
            n = i if fwd else n_chunks - 1 - i
            rows = pl.ds(pl.multiple_of(n * L, L), L)
            cum = cum_ref[0, rows, d * dkw:(d + 1) * dkw]
            last = cum[L - 1:L, :] if fwd else cum[0:1, :]
            q = q_ref[0, rows, :]
            k = k_ref[0, rows, :]
            v = v_ref[0, rows, :]
            qe = (q * jnp.exp(cum)).astype(BF16)
            kg = (k * jnp.exp(-cum)).astype(BF16)
            kdec = (k * jnp.exp(last - cum)).astype(BF16)
            k_blk = jnp.where(k_own, jnp.concatenate([kg] * GLA_HEADS, axis=0), jnp.zeros((), BF16))
            att = _dot_nt(qe, k_blk)
            keep = (col_m <= row_l) if fwd else (col_m >= row_l)
            att = jnp.where(keep, att, 0.0).astype(BF16)
            v_blk = jnp.where(v_own, jnp.concatenate([v] * GLA_HEADS, axis=0), jnp.zeros((), BF16))
            st = state_ref[d]
            o = _dot(att, v_blk) + _dot_nt(qe, st.astype(BF16))
            state_ref[d] = st * jnp.exp(last) + jnp.where(s_own, _dot_tn(v, kdec), 0.0)
            if fwd:
                of_ref[rows, :] = o
            else:
                ob_ref[rows, :] = o
        return carry

    lax.fori_loop(0, n_chunks, body, 0, unroll=2)

    def finish(n, carry):
        rows = pl.ds(pl.multiple_of(n * L, L), L)
        tot = of_ref[rows, :] + ob_ref[rows, :]
        for hd in range(GLA_HEADS):
            sl = slice(hd * GLA_DV, (hd + 1) * GLA_DV)
            o_ref[0, rows, sl] = (_rms(tot[:, sl], g_ref[...]) * gr_ref[0, rows, sl]).astype(o_ref.dtype)
        return carry

    lax.fori_loop(0, n_chunks, finish, 0)


def _gla(gq, gk, cum, gv, gr, g_norm):
    b, s, _ = gq.shape
    dkw = GLA_HEADS * GLA_DK
    dvw = GLA_HEADS * GLA_DV
    blk = lambda w: pl.BlockSpec((1, s, w), lambda i: (i, 0, 0))
    return pl.pallas_call(
        _gla_kernel,
        grid=(b,),
        in_specs=[blk(dkw), blk(dkw), blk(2 * dkw), blk(dvw), blk(dvw),
                  pl.BlockSpec((1, GLA_DV), lambda i: (0, 0))],
        out_specs=blk(dvw),
        out_shape=jax.ShapeDtypeStruct((b, s, dvw), BF16),
        scratch_shapes=[pltpu.VMEM((2, dvw, dkw), F32), pltpu.VMEM((s, dvw), F32), pltpu.VMEM((s, dvw), F32)],
        compiler_params=_params(("parallel",)),
        name="gla",
    )(gq, gk, cum, gv, gr, g_norm)


def _odd_proj_kernel(x_ref, w_ref, gq_ref, gk_ref, avg_ref, cos_ref, sin_ref, q_ref, k_ref, v_ref):
    h = _dot(x_ref[...].astype(BF16), w_ref[...])
    cos = cos_ref[...]
    sin = sin_ref[...]
    avg = avg_ref[...]
    nq = GQA_HEADS * GQA_HEAD_DIM
    nk = GQA_KV_HEADS * GQA_HEAD_DIM

    def norm_rope(t, g):
        ms = _dot_x_sel(t * t, avg)
        t = t * lax.rsqrt(ms + EPS) * g
        return t * cos + _rot_half(t, GQA_HEAD_DIM // 2) * sin

    for c in range(nq // nk):
        sl = slice(c * nk, (c + 1) * nk)
        q_ref[:, sl] = (norm_rope(h[:, sl], gq_ref[...]) * (GQA_HEAD_DIM ** -0.5)).astype(q_ref.dtype)
    k_ref[...] = norm_rope(h[:, nq:nq + nk], gk_ref[...]).astype(k_ref.dtype)
    v_ref[...] = h[:, nq + nk:].astype(v_ref.dtype)


def _odd_proj(x2, w, gq, gk, avg, cos, sin, seq, tm):
    t, d = x2.shape
    nsb = seq // tm
    row = lambda i: (i, 0)
    fixed = lambda i: (0, 0)
    pos = lambda i: (i % nsb, 0)
    full = lambda a: pl.BlockSpec(a.shape, fixed)
    outs = [ODD_SPLITS[0], ODD_SPLITS[1], ODD_SPLITS[2]]
    return pl.pallas_call(
        _odd_proj_kernel,
        grid=(t // tm,),
        in_specs=[pl.BlockSpec((tm, d), row), full(w), full(gq), full(gk), full(avg),
                  pl.BlockSpec((tm, cos.shape[1]), pos), pl.BlockSpec((tm, sin.shape[1]), pos)],
        out_specs=[pl.BlockSpec((tm, n), row) for n in outs],
        out_shape=[jax.ShapeDtypeStruct((t, n), BF16) for n in outs],
        compiler_params=_params(("parallel",)),
        name="odd_proj",
    )(x2, w, gq, gk, avg, cos, sin)


def _gqa_attn_kernel(q_ref, k_ref, v_ref, o_ref, krep_ref, vexp_ref, s_ref, e_ref):
    j = pl.program_id(1)
    group = GQA_HEADS // GQA_KV_HEADS
    width = GQA_KV_HEADS * GQA_HEAD_DIM

    @pl.when(pl.program_id(2) == 0)
    def _():
        r = lax.broadcasted_iota(jnp.int32, (width, width), 0)
        c = lax.broadcasted_iota(jnp.int32, (width, width), 1)
        pick = (r // GQA_HEAD_DIM == j) & (r % GQA_HEAD_DIM == c % GQA_HEAD_DIM)
        for g in range(group):
            only_g = pick & (c // GQA_HEAD_DIM == g)
            vexp_ref[g] = _dot(v_ref[0], jnp.where(only_g, 1.0, 0.0).astype(BF16)).astype(BF16)
        pick_t = (c // GQA_HEAD_DIM == j) & (c % GQA_HEAD_DIM == r % GQA_HEAD_DIM)
        krep_ref[...] = _dot_nt(jnp.where(pick_t, 1.0, 0.0).astype(BF16), k_ref[0]).astype(BF16)

    pair = 2 * GQA_HEAD_DIM
    lane = lax.broadcasted_iota(jnp.int32, (q_ref.shape[1], pair), 1)

    def q_of(g):
        qp = q_ref[0, :, (g // 2) * pair:(g // 2 + 1) * pair]
        return jnp.where(lane // GQA_HEAD_DIM == g % 2, qp, jnp.zeros_like(qp))

    acc = _attend_heads(
        group, q_of,
        lambda g: (lambda cols: krep_ref[:pair, cols]),
        lambda g: vexp_ref[g],
        s_ref, e_ref)
    o_ref[0] = acc.astype(o_ref.dtype)


def _gqa_attn(q, k, v, tq):
    b, s, _ = q.shape
    width = GQA_KV_HEADS * GQA_HEAD_DIM
    group = GQA_HEADS // GQA_KV_HEADS
    return pl.pallas_call(
        _gqa_attn_kernel,
        grid=(b, GQA_KV_HEADS, s // tq),
        in_specs=[pl.BlockSpec((1, tq, width), lambda i, j, t: (i, t, j)),
                  pl.BlockSpec((1, s, width), lambda i, j, t: (i, 0, 0)),
                  pl.BlockSpec((1, s, width), lambda i, j, t: (i, 0, 0))],
        out_specs=pl.BlockSpec((1, tq, width), lambda i, j, t: (i, t, j)),
        out_shape=jax.ShapeDtypeStruct((b, s, GQA_HEADS * GQA_HEAD_DIM), BF16),
        scratch_shapes=[pltpu.VMEM((width, s), BF16), pltpu.VMEM((group, s, width), BF16),
                        pltpu.VMEM((2, tq, s), F32), pltpu.VMEM((2, tq, s), BF16)],
        compiler_params=_params(("parallel", "parallel", "arbitrary")),
        name="gqa_attn",
    )(q, k, v)


def _out_ln_kernel(alpha, x_ref, ma_ref, mb_ref, wa_ref, wb_ref, g_ref, b_ref, y_ref, yb_ref, yt_ref):
    y = alpha * x_ref[...] + _dot(ma_ref[...].astype(BF16), wa_ref[...]) \
        + _dot(mb_ref[...].astype(BF16), wb_ref[...])
    y = _layer_norm(y, g_ref[...], b_ref[...])
    y_ref[...] = y
    yb_ref[...] = y.astype(BF16)
    yt_ref[0] = y.T.astype(BF16)


def _out_ln(x2, mix_a, mix_b, col_a, col_b, w_a, w_b, g, b, alpha, seq, tm):
    t, d = x2.shape
    half = w_a.shape[0]
    nsb = seq // tm
    row = lambda i: (i, 0)
    fixed = lambda i: (0, 0)
    return pl.pallas_call(
        functools.partial(_out_ln_kernel, alpha),
        grid=(t // tm,),
        in_specs=[pl.BlockSpec((tm, d), row),
                  pl.BlockSpec((tm, half), lambda i: (i, col_a)),
                  pl.BlockSpec((tm, half), lambda i: (i, col_b)),
                  pl.BlockSpec(w_a.shape, fixed), pl.BlockSpec(w_b.shape, fixed),
                  pl.BlockSpec(g.shape, fixed), pl.BlockSpec(b.shape, fixed)],
        out_specs=[pl.BlockSpec((tm, d), row), pl.BlockSpec((tm, d), row),
                   pl.BlockSpec((1, d, tm), lambda i: (i // nsb, 0, i % nsb))],
        out_shape=[jax.ShapeDtypeStruct((t, d), F32), jax.ShapeDtypeStruct((t, d), BF16),
                   jax.ShapeDtypeStruct((t // seq, d, seq), BF16)],
        compiler_params=_params(("parallel",)),
        name="out_ln",
    )(x2, mix_a, mix_b, w_a, w_b, g, b)


def _route_kernel(cap, x_ref, rw_ref, aff_ref, slot_ref):
    x = x_ref[0]
    seq = x.shape[0]
    xh, xm = _split2(x)
    wh, wm = _split2(rw_ref[...])
    logits = _dot_nt(wh, xh) + (_dot_nt(wh, xm) + _dot_nt(wm, xh))
    e = jnp.exp(logits - jnp.max(logits, axis=0, keepdims=True))
    aff = e / jnp.sum(e, axis=0, keepdims=True)
    aff_ref[0] = aff
    bits = pltpu.bitcast(aff, jnp.int32)

    def pick(i, thr):
        cand = thr | (jnp.int32(1) << (30 - i))
        cnt = jnp.sum((bits >= cand).astype(jnp.int32), axis=1, keepdims=True)
        return jnp.where(cnt >= cap, cand, thr)

    thr = lax.fori_loop(0, 31, pick, jnp.zeros((bits.shape[0], 1), jnp.int32))
    above = bits > thr
    tie = bits == thr
    need = cap - jnp.sum(above.astype(jnp.int32), axis=1, keepdims=True)

    blk = 256 if seq % 256 == 0 else V7X_LANES
    r = lax.broadcasted_iota(jnp.int32, (blk, blk), 0)
    c = lax.broadcasted_iota(jnp.int32, (blk, blk), 1)
    before = jnp.where(r < c, 1.0, 0.0).astype(BF16)

    def prefix(mask):
        m = jnp.where(mask, 1.0, 0.0).astype(BF16)
        run = jnp.zeros((mask.shape[0], 1), F32)
        parts = []
        for t in range(seq // blk):
            mb = m[:, t * blk:(t + 1) * blk]
            parts.append(_dot(mb, before) + run)
            run = run + jnp.sum(mb.astype(F32), axis=1, keepdims=True)
        return jnp.concatenate(parts, axis=1).astype(jnp.int32)

    chosen = above | (tie & (prefix(tie) < need))
    slot_ref[0] = jnp.where(chosen, prefix(chosen), -1)


def _route(x1, rw_t, cap):
    b, s, d = x1.shape
    e = rw_t.shape[0]
    return pl.pallas_call(
        functools.partial(_route_kernel, cap),
        grid=(b,),
        in_specs=[pl.BlockSpec((1, s, d), lambda i: (i, 0, 0)), pl.BlockSpec((e, d), lambda i: (0, 0))],
        out_specs=[pl.BlockSpec((1, e, s), lambda i: (i, 0, 0)), pl.BlockSpec((1, e, s), lambda i: (i, 0, 0))],
        out_shape=[jax.ShapeDtypeStruct((b, e, s), F32), jax.ShapeDtypeStruct((b, e, s), jnp.int32)],
        compiler_params=_params(("parallel",)),
        name="route",
    )(x1, rw_t)


def _moe_kernel(cap, ts, xt_ref, slot_ref, aff_ref, w1_ref, w3_ref, w2_ref, o_ref):
    e_id = pl.program_id(1)
    seq = xt_ref.shape[2]
    ff = w2_ref.shape[2]

    @pl.when(e_id == 0)
    def _():
        o_ref[...] = jnp.zeros_like(o_ref)

    c_row = lax.broadcasted_iota(jnp.int32, (cap, seq), 0)
    pick = jnp.where(slot_ref[0, 0] == c_row, 1.0, 0.0).astype(BF16)
    hd2 = xt_ref.shape[1] // 2
    xg = jnp.concatenate([_dot_nt(xt_ref[0, :hd2, :], pick), _dot_nt(xt_ref[0, hd2:, :], pick)],
                         axis=0).astype(BF16)
    h1 = _dot(w1_ref[0], xg)
    h3 = _dot(w3_ref[0], xg)
    hid = (h1 * _sigmoid(h1) * h3).astype(BF16)
    ye = jnp.concatenate([_dot(w2_ref[0, :hd2, :], hid), _dot(w2_ref[0, hd2:, :], hid)],
                         axis=0).astype(BF16)
    for t in range(seq // ts):
        cols = slice(t * ts, (t + 1) * ts)
        o_ref[0, :, cols] += _dot(ye, pick[:, cols]) * aff_ref[0, 0, :, cols]


def _moe(xt, slot, aff, w1t, w3t, w2t, cap, layer):
    b, d, s = xt.shape
    e = slot.shape[1]
    ff = w2t.shape[2]
    ts = min(512, s)
    w_blk = lambda i, j: (layer * e + j, 0, 0)
    return pl.pallas_call(
        functools.partial(_moe_kernel, cap, ts),
        grid=(b, e),
        in_specs=[pl.BlockSpec((1, d, s), lambda i, j: (i, 0, 0)),
                  pl.BlockSpec((1, 1, 1, s), lambda i, j: (i, j, 0, 0)),
                  pl.BlockSpec((1, 1, 1, s), lambda i, j: (i, j, 0, 0)),
                  pl.BlockSpec((1, ff, d), w_blk), pl.BlockSpec((1, ff, d), w_blk),
                  pl.BlockSpec((1, d, ff), w_blk)],
        out_specs=pl.BlockSpec((1, d, s), lambda i, j: (i, 0, 0)),
        out_shape=jax.ShapeDtypeStruct((b, d, s), F32),
        compiler_params=_params(("parallel", "arbitrary")),
        name="moe",
    )(xt, slot.reshape(b, e, 1, s), aff.reshape(b, e, 1, s), w1t, w3t, w2t)


def _transpose_cast_kernel(w_ref, o_ref):
    o_ref[0] = w_ref[0].T.astype(o_ref.dtype)


def _transpose_cast(w):
    n, r, c = w.shape
    return pl.pallas_call(
        _transpose_cast_kernel,
        grid=(n,),
        in_specs=[pl.BlockSpec((1, r, c), lambda i: (i, 0, 0))],
        out_specs=pl.BlockSpec((1, c, r), lambda i: (i, 0, 0)),
        out_shape=jax.ShapeDtypeStruct((n, c, r), BF16),
        compiler_params=_params(("parallel",)),
        name="transpose_cast",
    )(w)


def _ple_ln_kernel(alpha, x_ref, xb_ref, f_ref, p_ref, wg_ref, bg_ref, wp_ref, g_ref, b_ref, y_ref):
    gate = _sigmoid(_dot(xb_ref[...], wg_ref[...]) + bg_ref[...])
    ple = gate * _dot(p_ref[...].astype(BF16), wp_ref[...])
    ffn = f_ref[0].T
    y_ref[...] = _layer_norm(alpha * x_ref[...] + ffn + ple, g_ref[...], b_ref[...])


def _ple_ln(x1, x1b, ffn_t, p2, wg, bg, wp, g, b, alpha, tm, layer):
    t, d = x1.shape
    nsb = ffn_t.shape[2] // tm
    row = lambda i: (i, 0)
    p_row = lambda i: (layer * (t // tm) + i, 0)
    fixed = lambda i: (0, 0)
    full = lambda a: pl.BlockSpec(a.shape, fixed)
    return pl.pallas_call(
        functools.partial(_ple_ln_kernel, alpha),
        grid=(t // tm,),
        in_specs=[pl.BlockSpec((tm, d), row), pl.BlockSpec((tm, d), row),
                  pl.BlockSpec((1, d, tm), lambda i: (i // nsb, 0, i % nsb)),
                  pl.BlockSpec((tm, p2.shape[1]), p_row), full(wg), full(bg), full(wp), full(g), full(b)],
        out_specs=pl.BlockSpec((tm, d), row),
        out_shape=jax.ShapeDtypeStruct((t, d), F32),
        compiler_params=_params(("parallel",)),
        name="ple_ln",
    )(x1, x1b, ffn_t, p2, wg, bg, wp, g, b)


def _rope_tables(seq, rot_dim, lo, width):
    rows = seq // GRID_W
    row = jnp.repeat(jnp.arange(rows, dtype=F32), GRID_W)
    col = jnp.tile(jnp.arange(GRID_W, dtype=F32), rows)
    axis_dim = rot_dim // 2
    inv = ROPE_THETA ** (-jnp.arange(0, axis_dim, 2, dtype=F32) / axis_dim)
    ang = jnp.concatenate([row[:, None] * inv, col[:, None] * inv], axis=-1)
    cos, sin = jnp.cos(ang), jnp.sin(ang)
    cos2 = jnp.concatenate([cos, cos], axis=-1)
    sin2 = jnp.concatenate([-sin, sin], axis=-1)
    if lo == 0:
        reps = width // rot_dim
        return jnp.tile(cos2, (1, reps)), jnp.tile(sin2, (1, reps))
    pad_l = jnp.ones((seq, lo), F32)
    pad_r = jnp.ones((seq, width - lo - rot_dim), F32)
    cos_t = jnp.concatenate([pad_l, cos2, pad_r], axis=-1)
    sin_t = jnp.concatenate([0 * pad_l, sin2, 0 * pad_r], axis=-1)
    return cos_t, sin_t


def _prep_even(w_in, w_uq, w_ukv, gw_f, gb_f, gw_b, gb_b):
    d = w_in.shape[0]
    offs = np.cumsum(EVEN_SPLITS)[:-1].tolist()
    c_q, c_kv, k_pe, gq, gk, gv, g_lr, gr = jnp.split(w_in, offs, axis=-1)
    z32 = jnp.zeros((d, 32), w_in.dtype)
    chunk = jnp.concatenate([g_lr, z32, k_pe, z32], axis=-1)
    w_in_p = jnp.concatenate([c_q, c_kv, chunk, gq, gk, gv, gr], axis=-1).astype(BF16)
    uq = w_uq.reshape(MLA_Q_LORA, MLA_HEADS, MLA_NOPE + MLA_ROPE)
    uq = jnp.pad(uq, ((0, 0), (0, 0), (0, MLA_PAD - MLA_NOPE - MLA_ROPE)))
    w_uq_p = uq.reshape(MLA_Q_LORA, MLA_HEADS * MLA_PAD).astype(BF16)
    ukv = w_ukv.reshape(MLA_KV_LORA, MLA_HEADS, MLA_NOPE + MLA_V)
    uk = jnp.pad(ukv[:, :, :MLA_NOPE], ((0, 0), (0, 0), (0, MLA_PAD - MLA_NOPE)))
    w_uk_p = uk.reshape(MLA_KV_LORA, MLA_HEADS * MLA_PAD).astype(BF16)
    w_uv = ukv[:, :, MLA_NOPE:].reshape(MLA_KV_LORA, MLA_HEADS * MLA_V).astype(BF16)
    dkw = GLA_HEADS * GLA_DK
    w_gate = jnp.zeros((MLA_PAD, 2 * dkw), F32)
    w_gate = w_gate.at[0:GLA_GATE_RANK, 0:dkw].set(gw_f)
    w_gate = w_gate.at[GLA_GATE_RANK:2 * GLA_GATE_RANK, dkw:].set(gw_b).astype(BF16)
    b_gate = jnp.concatenate([gb_f, gb_b])[None, :]
    return w_in_p, w_uq_p, w_uk_p, w_uv, w_gate, b_gate


def kernel(x, p, w_in_even, mla_q_norm, w_uq, mla_kv_norm, w_ukv, gla_gate_w_fwd, gla_gate_b_fwd,
           gla_gate_w_bwd, gla_gate_b_bwd, gla_norm, w_in_odd, gqa_q_norm, gqa_k_norm, w_o, ln1_g,
           ln1_b, router_w, w1, w3, w2, ple_gate_w, ple_gate_b, ple_w, ln2_g, ln2_b):
    b, s, d = x.shape
    depth = w_o.shape[0]
    t = b * s
    alpha = (2.0 * depth) ** 0.25
    cap = EC_CAPACITY_FACTOR * s // N_EXPERTS
    tm = min(512, s)
    tq = min(512, s)
    half = w_o.shape[1] // 2
    cos_a, sin_a = _rope_tables(s, MLA_ROPE, PE_LO, MLA_PAD)
    cos_c, sin_c = _rope_tables(s, GQA_HEAD_DIM, 0, GQA_KV_HEADS * GQA_HEAD_DIM)
    hw = GQA_KV_HEADS * GQA_HEAD_DIM
    head_of = np.arange(hw) // GQA_HEAD_DIM
    avg = jnp.asarray((head_of[:, None] == head_of[None, :]) / GQA_HEAD_DIM, BF16)

    n_e, ff = w1.shape[1], w1.shape[3]
    w1t = _transpose_cast(w1.reshape(depth * n_e, d, ff))
    w3t = _transpose_cast(w3.reshape(depth * n_e, d, ff))
    w2t = _transpose_cast(w2.reshape(depth * n_e, ff, d))
    p_all = p.reshape(depth * t, p.shape[-1])

    x2 = x.reshape(t, d)
    for i in range(depth):
        j = i // 2
        if i % 2 == 0:
            w_in_p, w_uq_p, w_uk_p, w_uv, w_gate, b_gate = _prep_even(
                w_in_even[j], w_uq[j], w_ukv[j], gla_gate_w_fwd[j], gla_gate_b_fwd[j],
                gla_gate_w_bwd[j], gla_gate_b_bwd[j])
            q, k, v, gq, gk, gv, la, gr = _even_proj(
                x2, w_in_p, mla_q_norm[j][None, :], w_uq_p, mla_kv_norm[j][None, :], w_uk_p, w_uv,
                w_gate, b_gate, cos_a, sin_a, s, tm)
            r3 = lambda a: a.reshape(b, s, a.shape[-1])
            o_mla = _mla_attn(r3(q), r3(k), r3(v), tq).reshape(t, -1)
            o_gla = _gla(r3(gq), r3(gk), r3(la), r3(gv), r3(gr), gla_norm[j][None, :]).reshape(t, -1)
            mix_a, mix_b, col_a, col_b = o_mla, o_gla, 0, 0
        else:
            q, k, v = _odd_proj(x2, w_in_odd[j].astype(BF16),
                                jnp.tile(gqa_q_norm[j], GQA_KV_HEADS)[None, :],
                                jnp.tile(gqa_k_norm[j], GQA_KV_HEADS)[None, :],
                                avg, cos_c, sin_c, s, tm)
            r3 = lambda a: a.reshape(b, s, a.shape[-1])
            o = _gqa_attn(r3(q), r3(k), r3(v), tq).reshape(t, -1)
            mix_a, mix_b, col_a, col_b = o, o, 0, 1
        wo = w_o[i].astype(BF16)
        x1, x1b, x1t = _out_ln(x2, mix_a, mix_b, col_a, col_b, wo[:half], wo[half:],
                               ln1_g[i][None, :], ln1_b[i][None, :], alpha, s, tm)
        aff, slot = _route(x1.reshape(b, s, d), router_w[i].T, cap)
        ffn_t = _moe(x1t, slot, aff, w1t, w3t, w2t, cap, i)
        x2 = _ple_ln(x1, x1b, ffn_t, p_all, ple_gate_w[i].astype(BF16),
                     ple_gate_b[i][None, :], ple_w[i].astype(BF16), ln2_g[i][None, :], ln2_b[i][None, :],
                     alpha, tm, i)
    return x2.reshape(b, s, d)
```

```python
import functools
import math

import jax
import jax.numpy as jnp
import numpy as np
from jax import lax
from jax.experimental import pallas as pl
from jax.experimental.pallas import tpu as pltpu

F32 = jnp.float32
BF16 = jnp.bfloat16

V7X_LANES = 128
V7X_VMEM_BYTES = 64 * 1024 * 1024
VMEM_LIMIT = 56 * 1024 * 1024

GRID_W = 64
ROPE_THETA = 10000.0
EPS = 1e-6
MLA_HEADS, MLA_Q_LORA, MLA_KV_LORA = 8, 256, 128
MLA_NOPE, MLA_ROPE, MLA_V = 64, 32, 64
GLA_HEADS, GLA_DK, GLA_DV = 4, 64, 128
GLA_GATE_RANK, GLA_GATE_NORM, GLA_CHUNK = 16, 16.0, 64
GQA_HEADS, GQA_KV_HEADS, GQA_HEAD_DIM = 16, 4, 64
N_EXPERTS, EC_CAPACITY_FACTOR = 16, 2
EVEN_SPLITS = (MLA_Q_LORA, MLA_KV_LORA, MLA_ROPE, GLA_HEADS * GLA_DK, GLA_HEADS * GLA_DK,
               GLA_HEADS * GLA_DV, 2 * GLA_GATE_RANK, GLA_HEADS * GLA_DV)
ODD_SPLITS = (GQA_HEADS * GQA_HEAD_DIM, GQA_KV_HEADS * GQA_HEAD_DIM, GQA_KV_HEADS * GQA_HEAD_DIM)
MLA_PAD = V7X_LANES
PE_LO, PE_HI = MLA_NOPE, MLA_NOPE + MLA_ROPE


def _dot(a, b):
    return jnp.dot(a, b, preferred_element_type=F32)


def _dot_nt(a, b):
    return lax.dot_general(a, b, (((1,), (1,)), ((), ())), preferred_element_type=F32)


def _dot_tn(a, b):
    return lax.dot_general(a, b, (((0,), (0,)), ((), ())), preferred_element_type=F32)


def _split2(x):
    hi = x.astype(BF16)
    lo = (x - hi.astype(F32)).astype(BF16)
    return hi, lo


def _dot_sel(sel, x):
    hi, lo = _split2(x)
    return _dot(sel, hi) + _dot(sel, lo)


def _dot_x_sel(x, sel):
    hi, lo = _split2(x)
    return _dot(hi, sel) + _dot(lo, sel)


def _params(sem):
    return pltpu.CompilerParams(dimension_semantics=sem, vmem_limit_bytes=VMEM_LIMIT)


def _rot_half(x, half):
    w = x.shape[-1]
    lane = lax.broadcasted_iota(jnp.int32, x.shape, x.ndim - 1)
    first = (lane % (2 * half)) < half
    return jnp.where(first, pltpu.roll(x, w - half, x.ndim - 1), pltpu.roll(x, half, x.ndim - 1))


def _layer_norm(y, g, b):
    mu = jnp.mean(y, axis=-1, keepdims=True)
    yc = y - mu
    var = jnp.mean(yc * yc, axis=-1, keepdims=True)
    return yc * lax.rsqrt(var + EPS) * g + b


def _rms(x, g):
    return x * lax.rsqrt(jnp.mean(x * x, axis=-1, keepdims=True) + EPS) * g


def _log_sigmoid(z):
    return jnp.minimum(z, 0.0) - jnp.log1p(jnp.exp(-jnp.abs(z)))


def _sigmoid(z):
    return 1.0 / (1.0 + jnp.exp(-z))


LOG2_E = math.log2(math.e)
KEY_BLK = 128


def _scores_t(k, q, s_scr):
    s = _dot_nt(k, q)
    s_scr[...] = s
    return jnp.max(s, axis=0, keepdims=True)


def _exp2_cols(m, s_scr, e_scr):
    seq, tq = s_scr.shape
    for c in range(tq // V7X_LANES):
        cols = slice(c * V7X_LANES, (c + 1) * V7X_LANES)
        m_b = jnp.broadcast_to(m[:, cols], (KEY_BLK, V7X_LANES))
        for r in range(seq // KEY_BLK):
            rows = slice(r * KEY_BLK, (r + 1) * KEY_BLK)
            e_scr[rows, cols] = jnp.exp2(s_scr[rows, cols] - m_b).astype(BF16)


VT_ROWS_PAD = 16


def _values_t(v_t):
    dv, seq = v_t.shape
    extra = jnp.where(lax.broadcasted_iota(jnp.int32, (VT_ROWS_PAD, seq), 0) == 0, 1.0, 0.0)
    return jnp.concatenate([v_t.astype(BF16), extra.astype(BF16)], axis=0)


def _attend_heads(n_heads, k_of, q_of, vt_of, s_ref, e_ref):
    assert s_ref.shape[0] == n_heads and e_ref.shape[0] == n_heads
    m = {0: _scores_t(k_of(0), q_of(0), s_ref.at[0])}
    outs = []
    for h in range(n_heads + 1):
        if h + 1 < n_heads:
            m[h + 1] = _scores_t(k_of(h + 1), q_of(h + 1), s_ref.at[h + 1])
        if h >= 1:
            o = _dot(vt_of(h - 1), e_ref[h - 1])
            dv = o.shape[0] - VT_ROWS_PAD
            outs.append(o[:dv] * (1.0 / o[dv:dv + 1]))
        if h < n_heads:
            _exp2_cols(m[h], s_ref.at[h], e_ref.at[h])
    return jnp.concatenate(outs, axis=0)


def _even_proj_kernel(x_ref, w_in_ref, qn_ref, w_uq_ref, kvn_ref, w_uk_ref, w_uv_ref,
                      w_gate_ref, b_gate_ref, cq_ref, sq_ref,
                      q_ref, k_ref, v_ref, gq_ref, gk_ref, gv_ref, la_ref, gr_ref):
    h = _dot(x_ref[...].astype(BF16), w_in_ref[...])
    cos = cq_ref[...]
    sin = sq_ref[...]
    lane = lax.broadcasted_iota(jnp.int32, cos.shape, 1)
    pe_lane = (lane >= PE_LO) & (lane < PE_HI)

    def rope(t):
        return t * cos + _rot_half(t, MLA_ROPE // 2) * sin

    c_q = _rms(h[:, 0:256], qn_ref[...])
    q = _dot(c_q.astype(BF16), w_uq_ref[...]) * ((MLA_NOPE + MLA_ROPE) ** -0.5 * LOG2_E)
    c_kv = _rms(h[:, 256:384], kvn_ref[...]).astype(BF16)
    kn = _dot(c_kv, w_uk_ref[...])
    v_ref[...] = _dot(c_kv, w_uv_ref[...]).astype(v_ref.dtype)
    chunk = h[:, 384:512]
    k_pe = jnp.where(pe_lane, rope(chunk), 0.0)
    for hd in range(MLA_HEADS):
        sl = slice(hd * MLA_PAD, (hd + 1) * MLA_PAD)
        q_ref[:, sl] = rope(q[:, sl]).astype(q_ref.dtype)
        k_ref[:, sl] = (kn[:, sl] + k_pe).astype(k_ref.dtype)
    z = _dot(chunk.astype(BF16), w_gate_ref[...]) + b_gate_ref[...]
    la = _log_sigmoid(z) * (1.0 / GLA_GATE_NORM)
    blk = 2 * GLA_CHUNK
    r = lax.broadcasted_iota(jnp.int32, (blk, blk), 0)
    c = lax.broadcasted_iota(jnp.int32, (blk, blk), 1)
    same = (r // GLA_CHUNK) == (c // GLA_CHUNK)
    lower = jnp.where(same & (c <= r), 1.0, 0.0).astype(BF16)
    upper = jnp.where(same & (c >= r), 1.0, 0.0).astype(BF16)
    dkw = GLA_HEADS * GLA_DK
    for t in range(la.shape[0] // blk):
        rows = slice(t * blk, (t + 1) * blk)
        la_ref[rows, :dkw] = _dot_sel(lower, la[rows, :dkw])
        la_ref[rows, dkw:] = _dot_sel(upper, la[rows, dkw:])
    gq_ref[...] = h[:, 512:768] * (GLA_DK ** -0.5)
    gk_ref[...] = h[:, 768:1024]
    gv_ref[...] = h[:, 1024:1536].astype(gv_ref.dtype)
    gr = h[:, 1536:2048]
    gr_ref[...] = gr * _sigmoid(gr)


def _even_proj(x2, w_in_p, qn, w_uq_p, kvn, w_uk_p, w_uv, w_gate, b_gate, cq, sq, seq, tm):
    t, d = x2.shape
    nsb = seq // tm
    row = lambda i: (i, 0)
    fixed = lambda i: (0, 0)
    pos = lambda i: (i % nsb, 0)
    full = lambda a: pl.BlockSpec(a.shape, fixed)
    outs = [(1024, BF16), (1024, BF16), (512, BF16), (256, F32), (256, F32), (512, BF16), (512, F32), (512, F32)]
    return pl.pallas_call(
        _even_proj_kernel,
        grid=(t // tm,),
        in_specs=[pl.BlockSpec((tm, d), row), full(w_in_p), full(qn), full(w_uq_p), full(kvn),
                  full(w_uk_p), full(w_uv), full(w_gate), full(b_gate),
                  pl.BlockSpec((tm, MLA_PAD), pos), pl.BlockSpec((tm, MLA_PAD), pos)],
        out_specs=[pl.BlockSpec((tm, n), row) for n, _ in outs],
        out_shape=[jax.ShapeDtypeStruct((t, n), dt) for n, dt in outs],
        compiler_params=_params(("parallel",)),
        name="even_proj",
    )(x2, w_in_p, qn, w_uq_p, kvn, w_uk_p, w_uv, w_gate, b_gate, cq, sq)


def _mla_attn_kernel(q_ref, k_ref, v_ref, o_ref, vt_ref, s_ref, e_ref):
    out_w = 2 * MLA_V

    @pl.when(pl.program_id(2) == 0)
    def _():
        r = lax.broadcasted_iota(jnp.int32, (out_w, out_w), 0)
        c = lax.broadcasted_iota(jnp.int32, (out_w, out_w), 1)
        eye = jnp.where(r == c, 1.0, 0.0).astype(BF16)
        v_t = _dot_nt(eye, v_ref[0])
        for a in range(2):
            vt_ref[a] = _values_t(v_t[a * MLA_V:(a + 1) * MLA_V])

    out_t = _attend_heads(
        2,
        lambda a: k_ref[0, :, a * MLA_PAD:(a + 1) * MLA_PAD],
        lambda a: q_ref[0, :, a * MLA_PAD:(a + 1) * MLA_PAD],
        lambda a: vt_ref[a],
        s_ref, e_ref)
    o_ref[0] = out_t.T.astype(o_ref.dtype)


def _mla_attn(q, k, v, tq):
    b, s, _ = q.shape
    return pl.pallas_call(
        _mla_attn_kernel,
        grid=(b, MLA_HEADS // 2, s // tq),
        in_specs=[pl.BlockSpec((1, tq, 2 * MLA_PAD), lambda i, j, t: (i, t, j)),
                  pl.BlockSpec((1, s, 2 * MLA_PAD), lambda i, j, t: (i, 0, j)),
                  pl.BlockSpec((1, s, 2 * MLA_V), lambda i, j, t: (i, 0, j))],
        out_specs=pl.BlockSpec((1, tq, 2 * MLA_V), lambda i, j, t: (i, t, j)),
        out_shape=jax.ShapeDtypeStruct((b, s, MLA_HEADS * MLA_V), BF16),
        scratch_shapes=[pltpu.VMEM((2, MLA_V + VT_ROWS_PAD, s), BF16),
                        pltpu.VMEM((2, s, tq), F32), pltpu.VMEM((2, s, tq), BF16)],
        compiler_params=_params(("parallel", "parallel", "arbitrary")),
        name="mla_attn",
    )(q, k, v)


def _gla_kernel(q_ref, k_ref, cum_ref, v_ref, gr_ref, g_ref, o_ref, state_ref, of_ref, ob_ref):
    seq = q_ref.shape[1]
    n_chunks = seq // GLA_CHUNK
    L = GLA_CHUNK
    dkw = GLA_HEADS * GLA_DK
    dvw = GLA_HEADS * GLA_DV

    def iota(shape, dim):
        return lax.broadcasted_iota(jnp.int32, shape, dim)

    row_l = iota((L, dkw), 0)
    col_m = iota((L, dkw), 1) % L
    k_own = (iota((dkw, dkw), 0) // L) == (iota((dkw, dkw), 1) // GLA_DK)
    v_own = (iota((dkw, dvw), 0) // L) == (iota((dkw, dvw), 1) // GLA_DV)
    s_own = (iota((dvw, dkw), 0) // GLA_DV) == (iota((dvw, dkw), 1) // GLA_DK)
    state_ref[...] = jnp.zeros_like(state_ref)

    def body(i, carry):
        for d, fwd in enumerate((True, False)):
            n = i if fwd else n_chunks - 1 - i
            rows = pl.ds(pl.multiple_of(n * L, L), L)
            cum = cum_ref[0, rows, d * dkw:(d + 1) * dkw]
            last = cum[L - 1:L, :] if fwd else cum[0:1, :]
            q = q_ref[0, rows, :]
            k = k_ref[0, rows, :]
            v = v_ref[0, rows, :]
            qe = (q * jnp.exp(cum)).astype(BF16)
            kg = (k * jnp.exp(-cum)).astype(BF16)
            kdec = (k * jnp.exp(last - cum)).astype(BF16)
            k_blk = jnp.where(k_own, jnp.concatenate([kg] * GLA_HEADS, axis=0), jnp.zeros((), BF16))
            att = _dot_nt(qe, k_blk)
            keep = (col_m <= row_l) if fwd else (col_m >= row_l)
            att = jnp.where(keep, att, 0.0).astype(BF16)
            v_blk = jnp.where(v_own, jnp.concatenate([v] * GLA_HEADS, axis=0), jnp.zeros((), BF16))
            st = state_ref[d]
            o = _dot(att, v_blk) + _dot_nt(qe, st.astype(BF16))
            state_ref[d] = st * jnp.exp(last) + jnp.where(s_own, _dot_tn(v, kdec), 0.0)
            if fwd:
                of_ref[rows, :] = o
            else:
                ob_ref[rows, :] = o
        return carry

    lax.fori_loop(0, n_chunks, body, 0, unroll=2)

    def finish(n, carry):
        rows = pl.ds(pl.multiple_of(n * L, L), L)
        tot = of_ref[rows, :] + ob_ref[rows, :]
        for hd in range(GLA_HEADS):
            sl = slice(hd * GLA_DV, (hd + 1) * GLA_DV)
            o_ref[0, rows, sl] = (_rms(tot[:, sl], g_ref[...]) * gr_ref[0, rows, sl]).astype(o_ref.dtype)
        return carry

    lax.fori_loop(0, n_chunks, finish, 0)


def _gla(gq, gk, cum, gv, gr, g_norm):
    b, s, _ = gq.shape
    dkw = GLA_HEADS * GLA_DK
    dvw = GLA_HEADS * GLA_DV
    blk = lambda w: pl.BlockSpec((1, s, w), lambda i: (i, 0, 0))
    return pl.pallas_call(
        _gla_kernel,
        grid=(b,),
        in_specs=[blk(dkw), blk(dkw), blk(2 * dkw), blk(dvw), blk(dvw),
                  pl.BlockSpec((1, GLA_DV), lambda i: (0, 0))],
        out_specs=blk(dvw),
        out_shape=jax.ShapeDtypeStruct((b, s, dvw), BF16),
        scratch_shapes=[pltpu.VMEM((2, dvw, dkw), F32), pltpu.VMEM((s, dvw), F32), pltpu.VMEM((s, dvw), F32)],
        compiler_params=_params(("parallel",)),
        name="gla",
    )(gq, gk, cum, gv, gr, g_norm)


def _odd_proj_kernel(x_ref, w_ref, gq_ref, gk_ref, avg_ref, cos_ref, sin_ref, q_ref, k_ref, v_ref):
    h = _dot(x_ref[...].astype(BF16), w_ref[...])
    cos = cos_ref[...]
    sin = sin_ref[...]
    avg = avg_ref[...]
    nq = GQA_HEADS * GQA_HEAD_DIM
    nk = GQA_KV_HEADS * GQA_HEAD_DIM

    def norm_rope(t, g):
        ms = _dot_x_sel(t * t, avg)
        t = t * lax.rsqrt(ms + EPS) * g
        return t * cos + _rot_half(t, GQA_HEAD_DIM // 2) * sin

    for c in range(nq // nk):
        sl = slice(c * nk, (c + 1) * nk)
        q_ref[:, sl] = (norm_rope(h[:, sl], gq_ref[...]) * (GQA_HEAD_DIM ** -0.5 * LOG2_E)).astype(q_ref.dtype)
    k_ref[...] = norm_rope(h[:, nq:nq + nk], gk_ref[...]).astype(k_ref.dtype)
    v_ref[...] = h[:, nq + nk:].astype(v_ref.dtype)


def _odd_proj(x2, w, gq, gk, avg, cos, sin, seq, tm):
    t, d = x2.shape
    nsb = seq // tm
    row = lambda i: (i, 0)
    fixed = lambda i: (0, 0)
    pos = lambda i: (i % nsb, 0)
    full = lambda a: pl.BlockSpec(a.shape, fixed)
    outs = [ODD_SPLITS[0], ODD_SPLITS[1], ODD_SPLITS[2]]
    return pl.pallas_call(
        _odd_proj_kernel,
        grid=(t // tm,),
        in_specs=[pl.BlockSpec((tm, d), row), full(w), full(gq), full(gk), full(avg),
                  pl.BlockSpec((tm, cos.shape[1]), pos), pl.BlockSpec((tm, sin.shape[1]), pos)],
        out_specs=[pl.BlockSpec((tm, n), row) for n in outs],
        out_shape=[jax.ShapeDtypeStruct((t, n), BF16) for n in outs],
        compiler_params=_params(("parallel",)),
        name="odd_proj",
    )(x2, w, gq, gk, avg, cos, sin)


def _gqa_attn_kernel(q_ref, k_ref, v_ref, o_ref, kp_ref, vt_ref, s_ref, e_ref):
    j = pl.program_id(1)
    group = GQA_HEADS // GQA_KV_HEADS
    width = GQA_KV_HEADS * GQA_HEAD_DIM
    pair = 2 * GQA_HEAD_DIM

    @pl.when(pl.program_id(2) == 0)
    def _():
        r = lax.broadcasted_iota(jnp.int32, (width, pair), 0)
        c = lax.broadcasted_iota(jnp.int32, (width, pair), 1)
        twice = (r // GQA_HEAD_DIM == j) & (r % GQA_HEAD_DIM == c % GQA_HEAD_DIM)
        kp_ref[...] = _dot(k_ref[0], jnp.where(twice, 1.0, 0.0).astype(BF16)).astype(BF16)
        rv = lax.broadcasted_iota(jnp.int32, (GQA_HEAD_DIM, width), 0)
        cv = lax.broadcasted_iota(jnp.int32, (GQA_HEAD_DIM, width), 1)
        mine = cv == j * GQA_HEAD_DIM + rv
        vt_ref[...] = _values_t(_dot_nt(jnp.where(mine, 1.0, 0.0).astype(BF16), v_ref[0]))

    lane = lax.broadcasted_iota(jnp.int32, (q_ref.shape[1], pair), 1)

    def q_of(g):
        qp = q_ref[0, :, (g // 2) * pair:(g // 2 + 1) * pair]
        return jnp.where(lane // GQA_HEAD_DIM == g % 2, qp, jnp.zeros_like(qp))

    out_t = _attend_heads(group, lambda g: kp_ref[...], q_of, lambda g: vt_ref[...], s_ref, e_ref)
    o_ref[0] = out_t.T.astype(o_ref.dtype)


def _gqa_attn(q, k, v, tq):
    b, s, _ = q.shape
    width = GQA_KV_HEADS * GQA_HEAD_DIM
    group = GQA_HEADS // GQA_KV_HEADS
    return pl.pallas_call(
        _gqa_attn_kernel,
        grid=(b, GQA_KV_HEADS, s // tq),
        in_specs=[pl.BlockSpec((1, tq, width), lambda i, j, t: (i, t, j)),
                  pl.BlockSpec((1, s, width), lambda i, j, t: (i, 0, 0)),
                  pl.BlockSpec((1, s, width), lambda i, j, t: (i, 0, 0))],
        out_specs=pl.BlockSpec((1, tq, width), lambda i, j, t: (i, t, j)),
        out_shape=jax.ShapeDtypeStruct((b, s, GQA_HEADS * GQA_HEAD_DIM), BF16),
        scratch_shapes=[pltpu.VMEM((s, 2 * GQA_HEAD_DIM), BF16),
                        pltpu.VMEM((GQA_HEAD_DIM + VT_ROWS_PAD, s), BF16),
                        pltpu.VMEM((group, s, tq), F32), pltpu.VMEM((group, s, tq), BF16)],
        compiler_params=_params(("parallel", "parallel", "arbitrary")),
        name="gqa_attn",
    )(q, k, v)


def _out_ln_kernel(alpha, x_ref, ma_ref, mb_ref, wa_ref, wb_ref, g_ref, b_ref, y_ref, yb_ref, yt_ref):
    y = alpha * x_ref[...] + _dot(ma_ref[...].astype(BF16), wa_ref[...]) \
        + _dot(mb_ref[...].astype(BF16), wb_ref[...])
    y = _layer_norm(y, g_ref[...], b_ref[...])
    y_ref[...] = y
    yb_ref[...] = y.astype(BF16)
    yt_ref[0] = y.T.astype(BF16)


def _out_ln(x2, mix_a, mix_b, col_a, col_b, w_a, w_b, g, b, alpha, seq, tm):
    t, d = x2.shape
    half = w_a.shape[0]
    nsb = seq // tm
    row = lambda i: (i, 0)
    fixed = lambda i: (0, 0)
    return pl.pallas_call(
        functools.partial(_out_ln_kernel, alpha),
        grid=(t // tm,),
        in_specs=[pl.BlockSpec((tm, d), row),
                  pl.BlockSpec((tm, half), lambda i: (i, col_a)),
                  pl.BlockSpec((tm, half), lambda i: (i, col_b)),
                  pl.BlockSpec(w_a.shape, fixed), pl.BlockSpec(w_b.shape, fixed),
                  pl.BlockSpec(g.shape, fixed), pl.BlockSpec(b.shape, fixed)],
        out_specs=[pl.BlockSpec((tm, d), row), pl.BlockSpec((tm, d), row),
                   pl.BlockSpec((1, d, tm), lambda i: (i // nsb, 0, i % nsb))],
        out_shape=[jax.ShapeDtypeStruct((t, d), F32), jax.ShapeDtypeStruct((t, d), BF16),
                   jax.ShapeDtypeStruct((t // seq, d, seq), BF16)],
        compiler_params=_params(("parallel",)),
        name="out_ln",
    )(x2, mix_a, mix_b, w_a, w_b, g, b)


def _route_kernel(cap, x_ref, rw_ref, aff_ref, slot_ref):
    x = x_ref[0]
    seq = x.shape[0]
    xh, xm = _split2(x)
    wh, wm = _split2(rw_ref[...])
    logits = _dot_nt(wh, xh) + (_dot_nt(wh, xm) + _dot_nt(wm, xh))
    e = jnp.exp(logits - jnp.max(logits, axis=0, keepdims=True))
    aff = e / jnp.sum(e, axis=0, keepdims=True)
    aff_ref[0] = aff
    bits = pltpu.bitcast(aff, jnp.int32)

    def pick(i, thr):
        cand = thr | (jnp.int32(1) << (30 - i))
        cnt = jnp.sum((bits >= cand).astype(jnp.int32), axis=1, keepdims=True)
        return jnp.where(cnt >= cap, cand, thr)

    thr = lax.fori_loop(0, 31, pick, jnp.zeros((bits.shape[0], 1), jnp.int32))
    above = bits > thr
    tie = bits == thr
    need = cap - jnp.sum(above.astype(jnp.int32), axis=1, keepdims=True)

    blk = 256 if seq % 256 == 0 else V7X_LANES
    r = lax.broadcasted_iota(jnp.int32, (blk, blk), 0)
    c = lax.broadcasted_iota(jnp.int32, (blk, blk), 1)
    before = jnp.where(r < c, 1.0, 0.0).astype(BF16)

    def prefix(mask):
        m = jnp.where(mask, 1.0, 0.0).astype(BF16)
        run = jnp.zeros((mask.shape[0], 1), F32)
        parts = []
        for t in range(seq // blk):
            mb = m[:, t * blk:(t + 1) * blk]
            parts.append(_dot(mb, before) + run)
            run = run + jnp.sum(mb.astype(F32), axis=1, keepdims=True)
        return jnp.concatenate(parts, axis=1).astype(jnp.int32)

    chosen = above | (tie & (prefix(tie) < need))
    slot_ref[0] = jnp.where(chosen, prefix(chosen), -1)


def _route(x1, rw_t, cap):
    b, s, d = x1.shape
    e = rw_t.shape[0]
    return pl.pallas_call(
        functools.partial(_route_kernel, cap),
        grid=(b,),
        in_specs=[pl.BlockSpec((1, s, d), lambda i: (i, 0, 0)), pl.BlockSpec((e, d), lambda i: (0, 0))],
        out_specs=[pl.BlockSpec((1, e, s), lambda i: (i, 0, 0)), pl.BlockSpec((1, e, s), lambda i: (i, 0, 0))],
        out_shape=[jax.ShapeDtypeStruct((b, e, s), F32), jax.ShapeDtypeStruct((b, e, s), jnp.int32)],
        compiler_params=_params(("parallel",)),
        name="route",
    )(x1, rw_t)


def _moe_kernel(cap, ts, xt_ref, slot_ref, aff_ref, w1_ref, w3_ref, w2_ref, o_ref):
    e_id = pl.program_id(1)
    seq = xt_ref.shape[2]
    ff = w2_ref.shape[2]

    @pl.when(e_id == 0)
    def _():
        o_ref[...] = jnp.zeros_like(o_ref)

    c_row = lax.broadcasted_iota(jnp.int32, (cap, seq), 0)
    pick = jnp.where(slot_ref[0, 0] == c_row, 1.0, 0.0).astype(BF16)
    hd2 = xt_ref.shape[1] // 2
    xg = jnp.concatenate([_dot_nt(xt_ref[0, :hd2, :], pick), _dot_nt(xt_ref[0, hd2:, :], pick)],
                         axis=0).astype(BF16)
    h1 = _dot(w1_ref[0], xg)
    h3 = _dot(w3_ref[0], xg)
    hid = (h1 * _sigmoid(h1) * h3).astype(BF16)
    ye = jnp.concatenate([_dot(w2_ref[0, :hd2, :], hid), _dot(w2_ref[0, hd2:, :], hid)],
                         axis=0).astype(BF16)
    for t in range(seq // ts):
        cols = slice(t * ts, (t + 1) * ts)
        o_ref[0, :, cols] += _dot(ye, pick[:, cols]) * aff_ref[0, 0, :, cols]


def _moe(xt, slot, aff, w1t, w3t, w2t, cap, layer):
    b, d, s = xt.shape
    e = slot.shape[1]
    ff = w2t.shape[2]
    ts = min(512, s)
    w_blk = lambda i, j: (layer * e + j, 0, 0)
    return pl.pallas_call(
        functools.partial(_moe_kernel, cap, ts),
        grid=(b, e),
        in_specs=[pl.BlockSpec((1, d, s), lambda i, j: (i, 0, 0)),
                  pl.BlockSpec((1, 1, 1, s), lambda i, j: (i, j, 0, 0)),
                  pl.BlockSpec((1, 1, 1, s), lambda i, j: (i, j, 0, 0)),
                  pl.BlockSpec((1, ff, d), w_blk), pl.BlockSpec((1, ff, d), w_blk),
                  pl.BlockSpec((1, d, ff), w_blk)],
        out_specs=pl.BlockSpec((1, d, s), lambda i, j: (i, 0, 0)),
        out_shape=jax.ShapeDtypeStruct((b, d, s), F32),
        compiler_params=_params(("parallel", "arbitrary")),
        name="moe",
    )(xt, slot.reshape(b, e, 1, s), aff.reshape(b, e, 1, s), w1t, w3t, w2t)


def _transpose_cast_kernel(w_ref, o_ref):
    o_ref[0] = w_ref[0].T.astype(o_ref.dtype)


def _transpose_cast(w):
    n, r, c = w.shape
    return pl.pallas_call(
        _transpose_cast_kernel,
        grid=(n,),
        in_specs=[pl.BlockSpec((1, r, c), lambda i: (i, 0, 0))],
        out_specs=pl.BlockSpec((1, c, r), lambda i: (i, 0, 0)),
        out_shape=jax.ShapeDtypeStruct((n, c, r), BF16),
        compiler_params=_params(("parallel",)),
        name="transpose_cast",
    )(w)


def _ple_ln_kernel(alpha, x_ref, xb_ref, f_ref, p_ref, wg_ref, bg_ref, wp_ref, g_ref, b_ref, y_ref):
    gate = _sigmoid(_dot(xb_ref[...], wg_ref[...]) + bg_ref[...])
    ple = gate * _dot(p_ref[...].astype(BF16), wp_ref[...])
    ffn = f_ref[0].T
    y_ref[...] = _layer_norm(alpha * x_ref[...] + ffn + ple, g_ref[...], b_ref[...])


def _ple_ln(x1, x1b, ffn_t, p2, wg, bg, wp, g, b, alpha, tm, layer):
    t, d = x1.shape
    nsb = ffn_t.shape[2] // tm
    row = lambda i: (i, 0)
    p_row = lambda i: (layer * (t // tm) + i, 0)
    fixed = lambda i: (0, 0)
    full = lambda a: pl.BlockSpec(a.shape, fixed)
    return pl.pallas_call(
        functools.partial(_ple_ln_kernel, alpha),
        grid=(t // tm,),
        in_specs=[pl.BlockSpec((tm, d), row), pl.BlockSpec((tm, d), row),
                  pl.BlockSpec((1, d, tm), lambda i: (i // nsb, 0, i % nsb)),
                  pl.BlockSpec((tm, p2.shape[1]), p_row), full(wg), full(bg), full(wp), full(g), full(b)],
        out_specs=pl.BlockSpec((tm, d), row),
        out_shape=jax.ShapeDtypeStruct((t, d), F32),
        compiler_params=_params(("parallel",)),
        name="ple_ln",
    )(x1, x1b, ffn_t, p2, wg, bg, wp, g, b)


def _rope_tables(seq, rot_dim, lo, width):
    rows = seq // GRID_W
    row = jnp.repeat(jnp.arange(rows, dtype=F32), GRID_W)
    col = jnp.tile(jnp.arange(GRID_W, dtype=F32), rows)
    axis_dim = rot_dim // 2
    inv = ROPE_THETA ** (-jnp.arange(0, axis_dim, 2, dtype=F32) / axis_dim)
    ang = jnp.concatenate([row[:, None] * inv, col[:, None] * inv], axis=-1)
    cos, sin = jnp.cos(ang), jnp.sin(ang)
    cos2 = jnp.concatenate([cos, cos], axis=-1)
    sin2 = jnp.concatenate([-sin, sin], axis=-1)
    if lo == 0:
        reps = width // rot_dim
        return jnp.tile(cos2, (1, reps)), jnp.tile(sin2, (1, reps))
    pad_l = jnp.ones((seq, lo), F32)
    pad_r = jnp.ones((seq, width - lo - rot_dim), F32)
    cos_t = jnp.concatenate([pad_l, cos2, pad_r], axis=-1)
    sin_t = jnp.concatenate([0 * pad_l, sin2, 0 * pad_r], axis=-1)
    return cos_t, sin_t


def _prep_even(w_in, w_uq, w_ukv, gw_f, gb_f, gw_b, gb_b):
    d = w_in.shape[0]
    offs = np.cumsum(EVEN_SPLITS)[:-1].tolist()
    c_q, c_kv, k_pe, gq, gk, gv, g_lr, gr = jnp.split(w_in, offs, axis=-1)
    z32 = jnp.zeros((d, 32), w_in.dtype)
    chunk = jnp.concatenate([g_lr, z32, k_pe, z32], axis=-1)
    w_in_p = jnp.concatenate([c_q, c_kv, chunk, gq, gk, gv, gr], axis=-1).astype(BF16)
    uq = w_uq.reshape(MLA_Q_LORA, MLA_HEADS, MLA_NOPE + MLA_ROPE)
    uq = jnp.pad(uq, ((0, 0), (0, 0), (0, MLA_PAD - MLA_NOPE - MLA_ROPE)))
    w_uq_p = uq.reshape(MLA_Q_LORA, MLA_HEADS * MLA_PAD).astype(BF16)
    ukv = w_ukv.reshape(MLA_KV_LORA, MLA_HEADS, MLA_NOPE + MLA_V)
    uk = jnp.pad(ukv[:, :, :MLA_NOPE], ((0, 0), (0, 0), (0, MLA_PAD - MLA_NOPE)))
    w_uk_p = uk.reshape(MLA_KV_LORA, MLA_HEADS * MLA_PAD).astype(BF16)
    w_uv = ukv[:, :, MLA_NOPE:].reshape(MLA_KV_LORA, MLA_HEADS * MLA_V).astype(BF16)
    dkw = GLA_HEADS * GLA_DK
    w_gate = jnp.zeros((MLA_PAD, 2 * dkw), F32)
    w_gate = w_gate.at[0:GLA_GATE_RANK, 0:dkw].set(gw_f)
    w_gate = w_gate.at[GLA_GATE_RANK:2 * GLA_GATE_RANK, dkw:].set(gw_b).astype(BF16)
    b_gate = jnp.concatenate([gb_f, gb_b])[None, :]
    return w_in_p, w_uq_p, w_uk_p, w_uv, w_gate, b_gate


def kernel(x, p, w_in_even, mla_q_norm, w_uq, mla_kv_norm, w_ukv, gla_gate_w_fwd, gla_gate_b_fwd,
           gla_gate_w_bwd, gla_gate_b_bwd, gla_norm, w_in_odd, gqa_q_norm, gqa_k_norm, w_o, ln1_g,
           ln1_b, router_w, w1, w3, w2, ple_gate_w, ple_gate_b, ple_w, ln2_g, ln2_b):
    b, s, d = x.shape
    depth = w_o.shape[0]
    t = b * s
    alpha = (2.0 * depth) ** 0.25
    cap = EC_CAPACITY_FACTOR * s // N_EXPERTS
    tm = min(512, s)
    tq = min(512, s)
    half = w_o.shape[1] // 2
    cos_a, sin_a = _rope_tables(s, MLA_ROPE, PE_LO, MLA_PAD)
    cos_c, sin_c = _rope_tables(s, GQA_HEAD_DIM, 0, GQA_KV_HEADS * GQA_HEAD_DIM)
    hw = GQA_KV_HEADS * GQA_HEAD_DIM
    head_of = np.arange(hw) // GQA_HEAD_DIM
    avg = jnp.asarray((head_of[:, None] == head_of[None, :]) / GQA_HEAD_DIM, BF16)

    n_e, ff = w1.shape[1], w1.shape[3]
    w1t = _transpose_cast(w1.reshape(depth * n_e, d, ff))
    w3t = _transpose_cast(w3.reshape(depth * n_e, d, ff))
    w2t = _transpose_cast(w2.reshape(depth * n_e, ff, d))
    p_all = p.reshape(depth * t, p.shape[-1])

    x2 = x.reshape(t, d)
    for i in range(depth):
        j = i // 2
        if i % 2 == 0:
            w_in_p, w_uq_p, w_uk_p, w_uv, w_gate, b_gate = _prep_even(
                w_in_even[j], w_uq[j], w_ukv[j], gla_gate_w_fwd[j], gla_gate_b_fwd[j],
                gla_gate_w_bwd[j], gla_gate_b_bwd[j])
            q, k, v, gq, gk, gv, la, gr = _even_proj(
                x2, w_in_p, mla_q_norm[j][None, :], w_uq_p, mla_kv_norm[j][None, :], w_uk_p, w_uv,
                w_gate, b_gate, cos_a, sin_a, s, tm)
            r3 = lambda a: a.reshape(b, s, a.shape[-1])
            o_mla = _mla_attn(r3(q), r3(k), r3(v), tq).reshape(t, -1)
            o_gla = _gla(r3(gq), r3(gk), r3(la), r3(gv), r3(gr), gla_norm[j][None, :]).reshape(t, -1)
            mix_a, mix_b, col_a, col_b = o_mla, o_gla, 0, 0
        else:
            q, k, v = _odd_proj(x2, w_in_odd[j].astype(BF16),
                                jnp.tile(gqa_q_norm[j], GQA_KV_HEADS)[None, :],
                                jnp.tile(gqa_k_norm[j], GQA_KV_HEADS)[None, :],
                                avg, cos_c, sin_c, s, tm)
            r3 = lambda a: a.reshape(b, s, a.shape[-1])
            o = _gqa_attn(r3(q), r3(k), r3(v), tq).reshape(t, -1)
            mix_a, mix_b, col_a, col_b = o, o, 0, 1
        wo = w_o[i].astype(BF16)
        x1, x1b, x1t = _out_ln(x2, mix_a, mix_b, col_a, col_b, wo[:half], wo[half:],
                               ln1_g[i][None, :], ln1_b[i][None, :], alpha, s, tm)
        aff, slot = _route(x1.reshape(b, s, d), router_w[i].T, cap)
        ffn_t = _moe(x1t, slot, aff, w1t, w3t, w2t, cap, i)
        x2 = _ple_ln(x1, x1b, ffn_t, p_all, ple_gate_w[i].astype(BF16),
                     ple_gate_b[i][None, :], ple_w[i].astype(BF16), ln2_g[i][None, :], ln2_b[i][None, :],
                     alpha, tm, i)
    return x2.reshape(b, s, d)
```

```python
import functools
import math

import jax
import jax.numpy as jnp
import numpy as np
from jax import lax
from jax.experimental import pallas as pl
from jax.experimental.pallas import tpu as pltpu

F32 = jnp.float32
BF16 = jnp.bfloat16

V7X_LANES = 128
V7X_VMEM_BYTES = 64 * 1024 * 1024
VMEM_LIMIT = 56 * 1024 * 1024

GRID_W = 64
ROPE_THETA = 10000.0
EPS = 1e-6
MLA_HEADS, MLA_Q_LORA, MLA_KV_LORA = 8, 256, 128
MLA_NOPE, MLA_ROPE, MLA_V = 64, 32, 64
GLA_HEADS, GLA_DK, GLA_DV = 4, 64, 128
GLA_GATE_RANK, GLA_GATE_NORM, GLA_CHUNK = 16, 16.0, 64
GQA_HEADS, GQA_KV_HEADS, GQA_HEAD_DIM = 16, 4, 64
N_EXPERTS, EC_CAPACITY_FACTOR = 16, 2
EVEN_SPLITS = (MLA_Q_LORA, MLA_KV_LORA, MLA_ROPE, GLA_HEADS * GLA_DK, GLA_HEADS * GLA_DK,
               GLA_HEADS * GLA_DV, 2 * GLA_GATE_RANK, GLA_HEADS * GLA_DV)
ODD_SPLITS = (GQA_HEADS * GQA_HEAD_DIM, GQA_KV_HEADS * GQA_HEAD_DIM, GQA_KV_HEADS * GQA_HEAD_DIM)
MLA_PAD = V7X_LANES
PE_LO, PE_HI = MLA_NOPE, MLA_NOPE + MLA_ROPE


def _dot(a, b):
    return jnp.dot(a, b, preferred_element_type=F32)


def _dot_nt(a, b):
    return lax.dot_general(a, b, (((1,), (1,)), ((), ())), preferred_element_type=F32)


def _dot_tn(a, b):
    return lax.dot_general(a, b, (((0,), (0,)), ((), ())), preferred_element_type=F32)


def _split2(x):
    hi = x.astype(BF16)
    lo = (x - hi.astype(F32)).astype(BF16)
    return hi, lo


def _dot_sel(sel, x):
    hi, lo = _split2(x)
    return _dot(sel, hi) + _dot(sel, lo)


def _dot_x_sel(x, sel):
    hi, lo = _split2(x)
    return _dot(hi, sel) + _dot(lo, sel)


def _params(sem):
    return pltpu.CompilerParams(dimension_semantics=sem, vmem_limit_bytes=VMEM_LIMIT)


def _rot_half(x, half):
    w = x.shape[-1]
    lane = lax.broadcasted_iota(jnp.int32, x.shape, x.ndim - 1)
    first = (lane % (2 * half)) < half
    return jnp.where(first, pltpu.roll(x, w - half, x.ndim - 1), pltpu.roll(x, half, x.ndim - 1))


def _layer_norm(y, g, b):
    mu = jnp.mean(y, axis=-1, keepdims=True)
    yc = y - mu
    var = jnp.mean(yc * yc, axis=-1, keepdims=True)
    return yc * lax.rsqrt(var + EPS) * g + b


def _rms(x, g):
    return x * lax.rsqrt(jnp.mean(x * x, axis=-1, keepdims=True) + EPS) * g


def _log_sigmoid(z):
    return jnp.minimum(z, 0.0) - jnp.log1p(jnp.exp(-jnp.abs(z)))


def _sigmoid(z):
    return 1.0 / (1.0 + jnp.exp(-z))


LOG2_E = math.log2(math.e)
KEY_BLK = 128


KEY_CHUNKS = 4


def _exp2_cols(m, s_scr, e_scr):
    seq, tq = s_scr.shape
    for c in range(tq // V7X_LANES):
        cols = slice(c * V7X_LANES, (c + 1) * V7X_LANES)
        m_b = jnp.broadcast_to(m[:, cols], (KEY_BLK, V7X_LANES))
        for r in range(seq // KEY_BLK):
            rows = slice(r * KEY_BLK, (r + 1) * KEY_BLK)
            e_scr[rows, cols] = jnp.exp2(s_scr[rows, cols] - m_b).astype(BF16)


VT_ROWS_PAD = 16


def _values_t(v_t):
    dv, seq = v_t.shape
    extra = jnp.where(lax.broadcasted_iota(jnp.int32, (VT_ROWS_PAD, seq), 0) == 0, 1.0, 0.0)
    return jnp.concatenate([v_t.astype(BF16), extra.astype(BF16)], axis=0)


def _attend_heads(n_heads, k_of, q_of, vt_of, s_ref, e_ref):
    assert s_ref.shape[0] == n_heads and e_ref.shape[0] == n_heads
    seq = s_ref.shape[1]
    ck = seq // KEY_CHUNKS
    m, acc = {}, {}
    for h in range(n_heads + 2):
        m_parts = []
        for c in range(KEY_CHUNKS):
            keys = slice(c * ck, (c + 1) * ck)
            if h < n_heads:
                s = _dot_nt(k_of(h, keys), q_of(h))
                s_ref[h, keys, :] = s
                m_parts.append(jnp.max(s, axis=0, keepdims=True))
            if 1 <= h <= n_heads:
                _exp2_cols(m[h - 1], s_ref.at[h - 1, keys], e_ref.at[h - 1, keys])
            if h >= 2:
                o = _dot(vt_of(h - 2, keys), e_ref[h - 2, keys, :])
                acc[h - 2] = o if c == 0 else acc[h - 2] + o
        if h < n_heads:
            m[h] = functools.reduce(jnp.maximum, m_parts)
    outs = []
    for h in range(n_heads):
        dv = acc[h].shape[0] - VT_ROWS_PAD
        outs.append(acc[h][:dv] * (1.0 / acc[h][dv:dv + 1]))
    return jnp.concatenate(outs, axis=0)


def _even_proj_kernel(x_ref, w_in_ref, qn_ref, w_uq_ref, kvn_ref, w_uk_ref, w_uv_ref,
                      w_gate_ref, b_gate_ref, cq_ref, sq_ref,
                      q_ref, k_ref, v_ref, gq_ref, gk_ref, gv_ref, la_ref, gr_ref):
    h = _dot(x_ref[...].astype(BF16), w_in_ref[...])
    cos = cq_ref[...]
    sin = sq_ref[...]
    lane = lax.broadcasted_iota(jnp.int32, cos.shape, 1)
    pe_lane = (lane >= PE_LO) & (lane < PE_HI)

    def rope(t):
        return t * cos + _rot_half(t, MLA_ROPE // 2) * sin

    c_q = _rms(h[:, 0:256], qn_ref[...])
    q = _dot(c_q.astype(BF16), w_uq_ref[...]) * ((MLA_NOPE + MLA_ROPE) ** -0.5 * LOG2_E)
    c_kv = _rms(h[:, 256:384], kvn_ref[...]).astype(BF16)
    kn = _dot(c_kv, w_uk_ref[...])
    v_ref[...] = _dot(c_kv, w_uv_ref[...]).astype(v_ref.dtype)
    chunk = h[:, 384:512]
    k_pe = jnp.where(pe_lane, rope(chunk), 0.0)
    for hd in range(MLA_HEADS):
        sl = slice(hd * MLA_PAD, (hd + 1) * MLA_PAD)
        q_ref[:, sl] = rope(q[:, sl]).astype(q_ref.dtype)
        k_ref[:, sl] = (kn[:, sl] + k_pe).astype(k_ref.dtype)
    z = _dot(chunk.astype(BF16), w_gate_ref[...]) + b_gate_ref[...]
    la = _log_sigmoid(z) * (1.0 / GLA_GATE_NORM)
    blk = 2 * GLA_CHUNK
    r = lax.broadcasted_iota(jnp.int32, (blk, blk), 0)
    c = lax.broadcasted_iota(jnp.int32, (blk, blk), 1)
    same = (r // GLA_CHUNK) == (c // GLA_CHUNK)
    lower = jnp.where(same & (c <= r), 1.0, 0.0).astype(BF16)
    upper = jnp.where(same & (c >= r), 1.0, 0.0).astype(BF16)
    dkw = GLA_HEADS * GLA_DK
    for t in range(la.shape[0] // blk):
        rows = slice(t * blk, (t + 1) * blk)
        la_ref[rows, :dkw] = _dot_sel(lower, la[rows, :dkw])
        la_ref[rows, dkw:] = _dot_sel(upper, la[rows, dkw:])
    gq_ref[...] = h[:, 512:768] * (GLA_DK ** -0.5)
    gk_ref[...] = h[:, 768:1024]
    gv_ref[...] = h[:, 1024:1536].astype(gv_ref.dtype)
    gr = h[:, 1536:2048]
    gr_ref[...] = gr * _sigmoid(gr)


def _even_proj(x2, w_in_p, qn, w_uq_p, kvn, w_uk_p, w_uv, w_gate, b_gate, cq, sq, seq, tm):
    t, d = x2.shape
    nsb = seq // tm
    row = lambda i: (i, 0)
    fixed = lambda i: (0, 0)
    pos = lambda i: (i % nsb, 0)
    full = lambda a: pl.BlockSpec(a.shape, fixed)
    outs = [(1024, BF16), (1024, BF16), (512, BF16), (256, F32), (256, F32), (512, BF16), (512, F32), (512, F32)]
    return pl.pallas_call(
        _even_proj_kernel,
        grid=(t // tm,),
        in_specs=[pl.BlockSpec((tm, d), row), full(w_in_p), full(qn), full(w_uq_p), full(kvn),
                  full(w_uk_p), full(w_uv), full(w_gate), full(b_gate),
                  pl.BlockSpec((tm, MLA_PAD), pos), pl.BlockSpec((tm, MLA_PAD), pos)],
        out_specs=[pl.BlockSpec((tm, n), row) for n, _ in outs],
        out_shape=[jax.ShapeDtypeStruct((t, n), dt) for n, dt in outs],
        compiler_params=_params(("parallel",)),
        name="even_proj",
    )(x2, w_in_p, qn, w_uq_p, kvn, w_uk_p, w_uv, w_gate, b_gate, cq, sq)


def _mla_attn_kernel(q_ref, k_ref, v_ref, o_ref, vt_ref, s_ref, e_ref):
    out_w = 2 * MLA_V

    @pl.when(pl.program_id(2) == 0)
    def _():
        r = lax.broadcasted_iota(jnp.int32, (out_w, out_w), 0)
        c = lax.broadcasted_iota(jnp.int32, (out_w, out_w), 1)
        eye = jnp.where(r == c, 1.0, 0.0).astype(BF16)
        v_t = _dot_nt(eye, v_ref[0])
        for a in range(2):
            vt_ref[a] = _values_t(v_t[a * MLA_V:(a + 1) * MLA_V])

    out_t = _attend_heads(
        2,
        lambda a, keys: k_ref[0, keys, a * MLA_PAD:(a + 1) * MLA_PAD],
        lambda a: q_ref[0, :, a * MLA_PAD:(a + 1) * MLA_PAD],
        lambda a, keys: vt_ref[a, :, keys],
        s_ref, e_ref)
    o_ref[0] = out_t.T.astype(o_ref.dtype)


def _mla_attn(q, k, v, tq):
    b, s, _ = q.shape
    return pl.pallas_call(
        _mla_attn_kernel,
        grid=(b, MLA_HEADS // 2, s // tq),
        in_specs=[pl.BlockSpec((1, tq, 2 * MLA_PAD), lambda i, j, t: (i, t, j)),
                  pl.BlockSpec((1, s, 2 * MLA_PAD), lambda i, j, t: (i, 0, j)),
                  pl.BlockSpec((1, s, 2 * MLA_V), lambda i, j, t: (i, 0, j))],
        out_specs=pl.BlockSpec((1, tq, 2 * MLA_V), lambda i, j, t: (i, t, j)),
        out_shape=jax.ShapeDtypeStruct((b, s, MLA_HEADS * MLA_V), BF16),
        scratch_shapes=[pltpu.VMEM((2, MLA_V + VT_ROWS_PAD, s), BF16),
                        pltpu.VMEM((2, s, tq), F32), pltpu.VMEM((2, s, tq), BF16)],
        compiler_params=_params(("parallel", "parallel", "arbitrary")),
        name="mla_attn",
    )(q, k, v)


def _gla_kernel(q_ref, k_ref, cum_ref, v_ref, gr_ref, g_ref, o_ref, state_ref, of_ref, ob_ref):
    seq = q_ref.shape[1]
    n_chunks = seq // GLA_CHUNK
    L = GLA_CHUNK
    dkw = GLA_HEADS * GLA_DK
    dvw = GLA_HEADS * GLA_DV

    def iota(shape, dim):
        return lax.broadcasted_iota(jnp.int32, shape, dim)

    row_l = iota((L, dkw), 0)
    col_m = iota((L, dkw), 1) % L
    k_own = (iota((dkw, dkw), 0) // L) == (iota((dkw, dkw), 1) // GLA_DK)
    v_own = (iota((dkw, dvw), 0) // L) == (iota((dkw, dvw), 1) // GLA_DV)
    s_own = (iota((dvw, dkw), 0) // GLA_DV) == (iota((dvw, dkw), 1) // GLA_DK)
    state_ref[...] = jnp.zeros_like(state_ref)

    def body(i, carry):
        for d, fwd in enumerate((True, False)):
            n = i if fwd else n_chunks - 1 - i
            rows = pl.ds(pl.multiple_of(n * L, L), L)
            cum = cum_ref[0, rows, d * dkw:(d + 1) * dkw]
            last = cum[L - 1:L, :] if fwd else cum[0:1, :]
            q = q_ref[0, rows, :]
            k = k_ref[0, rows, :]
            v = v_ref[0, rows, :]
            qe = (q * jnp.exp(cum)).astype(BF16)
            kg = (k * jnp.exp(-cum)).astype(BF16)
            kdec = (k * jnp.exp(last - cum)).astype(BF16)
            k_blk = jnp.where(k_own, jnp.concatenate([kg] * GLA_HEADS, axis=0), jnp.zeros((), BF16))
            att = _dot_nt(qe, k_blk)
            keep = (col_m <= row_l) if fwd else (col_m >= row_l)
            att = jnp.where(keep, att, 0.0).astype(BF16)
            v_blk = jnp.where(v_own, jnp.concatenate([v] * GLA_HEADS, axis=0), jnp.zeros((), BF16))
            st = state_ref[d]
            o = _dot(att, v_blk) + _dot_nt(qe, st.astype(BF16))
            state_ref[d] = st * jnp.exp(last) + jnp.where(s_own, _dot_tn(v, kdec), 0.0)
            if fwd:
                of_ref[rows, :] = o
            else:
                ob_ref[rows, :] = o
        return carry

    lax.fori_loop(0, n_chunks, body, 0, unroll=4)

    def finish(n, carry):
        rows = pl.ds(pl.multiple_of(n * L, L), L)
        tot = of_ref[rows, :] + ob_ref[rows, :]
        for hd in range(GLA_HEADS):
            sl = slice(hd * GLA_DV, (hd + 1) * GLA_DV)
            o_ref[0, rows, sl] = (_rms(tot[:, sl], g_ref[...]) * gr_ref[0, rows, sl]).astype(o_ref.dtype)
        return carry

    lax.fori_loop(0, n_chunks, finish, 0)


def _gla(gq, gk, cum, gv, gr, g_norm):
    b, s, _ = gq.shape
    dkw = GLA_HEADS * GLA_DK
    dvw = GLA_HEADS * GLA_DV
    blk = lambda w: pl.BlockSpec((1, s, w), lambda i: (i, 0, 0))
    return pl.pallas_call(
        _gla_kernel,
        grid=(b,),
        in_specs=[blk(dkw), blk(dkw), blk(2 * dkw), blk(dvw), blk(dvw),
                  pl.BlockSpec((1, GLA_DV), lambda i: (0, 0))],
        out_specs=blk(dvw),
        out_shape=jax.ShapeDtypeStruct((b, s, dvw), BF16),
        scratch_shapes=[pltpu.VMEM((2, dvw, dkw), F32), pltpu.VMEM((s, dvw), F32), pltpu.VMEM((s, dvw), F32)],
        compiler_params=_params(("parallel",)),
        name="gla",
    )(gq, gk, cum, gv, gr, g_norm)


def _odd_proj_kernel(x_ref, w_ref, gq_ref, gk_ref, avg_ref, cos_ref, sin_ref, q_ref, k_ref, v_ref):
    h = _dot(x_ref[...].astype(BF16), w_ref[...])
    cos = cos_ref[...]
    sin = sin_ref[...]
    avg = avg_ref[...]
    nq = GQA_HEADS * GQA_HEAD_DIM
    nk = GQA_KV_HEADS * GQA_HEAD_DIM

    def norm_rope(t, g):
        ms = _dot_x_sel(t * t, avg)
        t = t * lax.rsqrt(ms + EPS) * g
        return t * cos + _rot_half(t, GQA_HEAD_DIM // 2) * sin

    for c in range(nq // nk):
        sl = slice(c * nk, (c + 1) * nk)
        q_ref[:, sl] = (norm_rope(h[:, sl], gq_ref[...]) * (GQA_HEAD_DIM ** -0.5 * LOG2_E)).astype(q_ref.dtype)
    k_ref[...] = norm_rope(h[:, nq:nq + nk], gk_ref[...]).astype(k_ref.dtype)
    v_ref[...] = h[:, nq + nk:].astype(v_ref.dtype)


def _odd_proj(x2, w, gq, gk, avg, cos, sin, seq, tm):
    t, d = x2.shape
    nsb = seq // tm
    row = lambda i: (i, 0)
    fixed = lambda i: (0, 0)
    pos = lambda i: (i % nsb, 0)
    full = lambda a: pl.BlockSpec(a.shape, fixed)
    outs = [ODD_SPLITS[0], ODD_SPLITS[1], ODD_SPLITS[2]]
    return pl.pallas_call(
        _odd_proj_kernel,
        grid=(t // tm,),
        in_specs=[pl.BlockSpec((tm, d), row), full(w), full(gq), full(gk), full(avg),
                  pl.BlockSpec((tm, cos.shape[1]), pos), pl.BlockSpec((tm, sin.shape[1]), pos)],
        out_specs=[pl.BlockSpec((tm, n), row) for n in outs],
        out_shape=[jax.ShapeDtypeStruct((t, n), BF16) for n in outs],
        compiler_params=_params(("parallel",)),
        name="odd_proj",
    )(x2, w, gq, gk, avg, cos, sin)


def _gqa_attn_kernel(q_ref, k_ref, v_ref, o_ref, kp_ref, vt_ref, s_ref, e_ref):
    j = pl.program_id(1)
    group = GQA_HEADS // GQA_KV_HEADS
    width = GQA_KV_HEADS * GQA_HEAD_DIM
    pair = 2 * GQA_HEAD_DIM

    @pl.when(pl.program_id(2) == 0)
    def _():
        r = lax.broadcasted_iota(jnp.int32, (width, pair), 0)
        c = lax.broadcasted_iota(jnp.int32, (width, pair), 1)
        twice = (r // GQA_HEAD_DIM == j) & (r % GQA_HEAD_DIM == c % GQA_HEAD_DIM)
        kp_ref[...] = _dot(k_ref[0], jnp.where(twice, 1.0, 0.0).astype(BF16)).astype(BF16)
        rv = lax.broadcasted_iota(jnp.int32, (GQA_HEAD_DIM, width), 0)
        cv = lax.broadcasted_iota(jnp.int32, (GQA_HEAD_DIM, width), 1)
        mine = cv == j * GQA_HEAD_DIM + rv
        vt_ref[...] = _values_t(_dot_nt(jnp.where(mine, 1.0, 0.0).astype(BF16), v_ref[0]))

    lane = lax.broadcasted_iota(jnp.int32, (q_ref.shape[1], pair), 1)

    def q_of(g):
        qp = q_ref[0, :, (g // 2) * pair:(g // 2 + 1) * pair]
        return jnp.where(lane // GQA_HEAD_DIM == g % 2, qp, jnp.zeros_like(qp))

    out_t = _attend_heads(group, lambda g, keys: kp_ref[keys, :], q_of, lambda g, keys: vt_ref[:, keys],
                          s_ref, e_ref)
    o_ref[0] = out_t.T.astype(o_ref.dtype)


def _gqa_attn(q, k, v, tq):
    b, s, _ = q.shape
    width = GQA_KV_HEADS * GQA_HEAD_DIM
    group = GQA_HEADS // GQA_KV_HEADS
    return pl.pallas_call(
        _gqa_attn_kernel,
        grid=(b, GQA_KV_HEADS, s // tq),
        in_specs=[pl.BlockSpec((1, tq, width), lambda i, j, t: (i, t, j)),
                  pl.BlockSpec((1, s, width), lambda i, j, t: (i, 0, 0)),
                  pl.BlockSpec((1, s, width), lambda i, j, t: (i, 0, 0))],
        out_specs=pl.BlockSpec((1, tq, width), lambda i, j, t: (i, t, j)),
        out_shape=jax.ShapeDtypeStruct((b, s, GQA_HEADS * GQA_HEAD_DIM), BF16),
        scratch_shapes=[pltpu.VMEM((s, 2 * GQA_HEAD_DIM), BF16),
                        pltpu.VMEM((GQA_HEAD_DIM + VT_ROWS_PAD, s), BF16),
                        pltpu.VMEM((group, s, tq), F32), pltpu.VMEM((group, s, tq), BF16)],
        compiler_params=_params(("parallel", "parallel", "arbitrary")),
        name="gqa_attn",
    )(q, k, v)


def _out_ln_kernel(alpha, x_ref, ma_ref, mb_ref, wa_ref, wb_ref, g_ref, b_ref, y_ref, yt_ref):
    y = alpha * x_ref[...] + _dot(ma_ref[...].astype(BF16), wa_ref[...]) \
        + _dot(mb_ref[...].astype(BF16), wb_ref[...])
    y = _layer_norm(y, g_ref[...], b_ref[...])
    y_ref[...] = y
    yt_ref[0] = y.T.astype(BF16)


def _out_ln(x2, mix_a, mix_b, col_a, col_b, w_a, w_b, g, b, alpha, seq, tm):
    t, d = x2.shape
    half = w_a.shape[0]
    nsb = seq // tm
    row = lambda i: (i, 0)
    fixed = lambda i: (0, 0)
    return pl.pallas_call(
        functools.partial(_out_ln_kernel, alpha),
        grid=(t // tm,),
        in_specs=[pl.BlockSpec((tm, d), row),
                  pl.BlockSpec((tm, half), lambda i: (i, col_a)),
                  pl.BlockSpec((tm, half), lambda i: (i, col_b)),
                  pl.BlockSpec(w_a.shape, fixed), pl.BlockSpec(w_b.shape, fixed),
                  pl.BlockSpec(g.shape, fixed), pl.BlockSpec(b.shape, fixed)],
        out_specs=[pl.BlockSpec((tm, d), row),
                   pl.BlockSpec((1, d, tm), lambda i: (i // nsb, 0, i % nsb))],
        out_shape=[jax.ShapeDtypeStruct((t, d), F32), jax.ShapeDtypeStruct((t // seq, d, seq), BF16)],
        compiler_params=_params(("parallel",)),
        name="out_ln",
    )(x2, mix_a, mix_b, w_a, w_b, g, b)


def _route_kernel(cap, x_ref, rw_ref, aff_ref, slot_ref):
    x = x_ref[0]
    seq = x.shape[0]
    xh, xm = _split2(x)
    wh, wm = _split2(rw_ref[...])
    logits = _dot_nt(wh, xh) + (_dot_nt(wh, xm) + _dot_nt(wm, xh))
    e = jnp.exp(logits - jnp.max(logits, axis=0, keepdims=True))
    aff = e / jnp.sum(e, axis=0, keepdims=True)
    aff_ref[0] = aff
    bits = pltpu.bitcast(aff, jnp.int32)

    def enough(cand):
        return jnp.sum((bits >= cand).astype(jnp.int32), axis=1, keepdims=True) >= cap

    top = jnp.full((bits.shape[0], 1), 1 << 30, jnp.int32)
    thr0 = jnp.where(enough(top), top, 0)

    def pick(i, thr):
        lo = 28 - 2 * i
        c1, c2, c3 = thr | (jnp.int32(1) << lo), thr | (jnp.int32(2) << lo), thr | (jnp.int32(3) << lo)
        return jnp.where(enough(c3), c3, jnp.where(enough(c2), c2, jnp.where(enough(c1), c1, thr)))

    thr = lax.fori_loop(0, 15, pick, thr0)
    above = bits > thr
    tie = bits == thr
    need = cap - jnp.sum(above.astype(jnp.int32), axis=1, keepdims=True)

    blk = 256 if seq % 256 == 0 else V7X_LANES
    r = lax.broadcasted_iota(jnp.int32, (blk, blk), 0)
    c = lax.broadcasted_iota(jnp.int32, (blk, blk), 1)
    before = jnp.where(r < c, 1.0, 0.0).astype(BF16)

    def prefix(mask):
        m = jnp.where(mask, 1.0, 0.0).astype(BF16)
        run = jnp.zeros((mask.shape[0], 1), F32)
        parts = []
        for t in range(seq // blk):
            mb = m[:, t * blk:(t + 1) * blk]
            parts.append(_dot(mb, before) + run)
            run = run + jnp.sum(mb.astype(F32), axis=1, keepdims=True)
        return jnp.concatenate(parts, axis=1).astype(jnp.int32)

    chosen = above | (tie & (prefix(tie) < need))
    slot_ref[0] = jnp.where(chosen, prefix(chosen), -1)


def _route(x1, rw_t, cap):
    b, s, d = x1.shape
    e = rw_t.shape[0]
    return pl.pallas_call(
        functools.partial(_route_kernel, cap),
        grid=(b,),
        in_specs=[pl.BlockSpec((1, s, d), lambda i: (i, 0, 0)), pl.BlockSpec((e, d), lambda i: (0, 0))],
        out_specs=[pl.BlockSpec((1, e, s), lambda i: (i, 0, 0)), pl.BlockSpec((1, e, s), lambda i: (i, 0, 0))],
        out_shape=[jax.ShapeDtypeStruct((b, e, s), F32), jax.ShapeDtypeStruct((b, e, s), jnp.int32)],
        compiler_params=_params(("parallel",)),
        name="route",
    )(x1, rw_t)


def _moe_kernel(cap, ts, xt_ref, slot_ref, aff_ref, w1_ref, w3_ref, w2_ref, o_ref):
    e_id = pl.program_id(1)
    seq = xt_ref.shape[2]
    ff = w2_ref.shape[2]

    @pl.when(e_id == 0)
    def _():
        o_ref[...] = jnp.zeros_like(o_ref)

    c_row = lax.broadcasted_iota(jnp.int32, (cap, seq), 0)
    pick = jnp.where(slot_ref[0, 0] == c_row, 1.0, 0.0).astype(BF16)
    hd2 = xt_ref.shape[1] // 2
    xg = jnp.concatenate([_dot_nt(xt_ref[0, :hd2, :], pick), _dot_nt(xt_ref[0, hd2:, :], pick)],
                         axis=0).astype(BF16)
    h1 = _dot(w1_ref[0], xg)
    h3 = _dot(w3_ref[0], xg)
    hid = (h1 * _sigmoid(h1) * h3).astype(BF16)
    ye = jnp.concatenate([_dot(w2_ref[0, :hd2, :], hid), _dot(w2_ref[0, hd2:, :], hid)],
                         axis=0).astype(BF16)
    for t in range(seq // ts):
        cols = slice(t * ts, (t + 1) * ts)
        o_ref[0, :, cols] += _dot(ye, pick[:, cols]) * aff_ref[0, 0, :, cols]


def _moe(xt, slot, aff, w1t, w3t, w2t, cap, layer):
    b, d, s = xt.shape
    e = slot.shape[1]
    ff = w2t.shape[2]
    ts = min(512, s)
    w_blk = lambda i, j: (layer * e + j, 0, 0)
    return pl.pallas_call(
        functools.partial(_moe_kernel, cap, ts),
        grid=(b, e),
        in_specs=[pl.BlockSpec((1, d, s), lambda i, j: (i, 0, 0)),
                  pl.BlockSpec((1, 1, 1, s), lambda i, j: (i, j, 0, 0)),
                  pl.BlockSpec((1, 1, 1, s), lambda i, j: (i, j, 0, 0)),
                  pl.BlockSpec((1, ff, d), w_blk), pl.BlockSpec((1, ff, d), w_blk),
                  pl.BlockSpec((1, d, ff), w_blk)],
        out_specs=pl.BlockSpec((1, d, s), lambda i, j: (i, 0, 0)),
        out_shape=jax.ShapeDtypeStruct((b, d, s), F32),
        compiler_params=_params(("parallel", "arbitrary")),
        name="moe",
    )(xt, slot.reshape(b, e, 1, s), aff.reshape(b, e, 1, s), w1t, w3t, w2t)


def _transpose_cast_kernel(w_ref, o_ref):
    o_ref[0] = w_ref[0].T.astype(o_ref.dtype)


def _transpose_cast(w):
    n, r, c = w.shape
    return pl.pallas_call(
        _transpose_cast_kernel,
        grid=(n,),
        in_specs=[pl.BlockSpec((1, r, c), lambda i: (i, 0, 0))],
        out_specs=pl.BlockSpec((1, c, r), lambda i: (i, 0, 0)),
        out_shape=jax.ShapeDtypeStruct((n, c, r), BF16),
        compiler_params=_params(("parallel",)),
        name="transpose_cast",
    )(w)


def _ple_ln_kernel(alpha, x_ref, f_ref, p_ref, wg_ref, bg_ref, wp_ref, g_ref, b_ref, y_ref):
    x = x_ref[...]
    gate = _sigmoid(_dot(x.astype(BF16), wg_ref[...]) + bg_ref[...])
    ple = gate * _dot(p_ref[...].astype(BF16), wp_ref[...])
    ffn = f_ref[0].T
    y_ref[...] = _layer_norm(alpha * x + ffn + ple, g_ref[...], b_ref[...])


def _ple_ln(x1, ffn_t, p2, wg, bg, wp, g, b, alpha, tm, layer):
    t, d = x1.shape
    nsb = ffn_t.shape[2] // tm
    row = lambda i: (i, 0)
    p_row = lambda i: (layer * (t // tm) + i, 0)
    fixed = lambda i: (0, 0)
    full = lambda a: pl.BlockSpec(a.shape, fixed)
    return pl.pallas_call(
        functools.partial(_ple_ln_kernel, alpha),
        grid=(t // tm,),
        in_specs=[pl.BlockSpec((tm, d), row),
                  pl.BlockSpec((1, d, tm), lambda i: (i // nsb, 0, i % nsb)),
                  pl.BlockSpec((tm, p2.shape[1]), p_row), full(wg), full(bg), full(wp), full(g), full(b)],
        out_specs=pl.BlockSpec((tm, d), row),
        out_shape=jax.ShapeDtypeStruct((t, d), F32),
        compiler_params=_params(("parallel",)),
        name="ple_ln",
    )(x1, ffn_t, p2, wg, bg, wp, g, b)


def _rope_tables(seq, rot_dim, lo, width):
    rows = seq // GRID_W
    row = jnp.repeat(jnp.arange(rows, dtype=F32), GRID_W)
    col = jnp.tile(jnp.arange(GRID_W, dtype=F32), rows)
    axis_dim = rot_dim // 2
    inv = ROPE_THETA ** (-jnp.arange(0, axis_dim, 2, dtype=F32) / axis_dim)
    ang = jnp.concatenate([row[:, None] * inv, col[:, None] * inv], axis=-1)
    cos, sin = jnp.cos(ang), jnp.sin(ang)
    cos2 = jnp.concatenate([cos, cos], axis=-1)
    sin2 = jnp.concatenate([-sin, sin], axis=-1)
    if lo == 0:
        reps = width // rot_dim
        return jnp.tile(cos2, (1, reps)), jnp.tile(sin2, (1, reps))
    pad_l = jnp.ones((seq, lo), F32)
    pad_r = jnp.ones((seq, width - lo - rot_dim), F32)
    cos_t = jnp.concatenate([pad_l, cos2, pad_r], axis=-1)
    sin_t = jnp.concatenate([0 * pad_l, sin2, 0 * pad_r], axis=-1)
    return cos_t, sin_t


def _prep_even(w_in, w_uq, w_ukv, gw_f, gb_f, gw_b, gb_b):
    d = w_in.shape[0]
    offs = np.cumsum(EVEN_SPLITS)[:-1].tolist()
    c_q, c_kv, k_pe, gq, gk, gv, g_lr, gr = jnp.split(w_in, offs, axis=-1)
    z32 = jnp.zeros((d, 32), w_in.dtype)
    chunk = jnp.concatenate([g_lr, z32, k_pe, z32], axis=-1)
    w_in_p = jnp.concatenate([c_q, c_kv, chunk, gq, gk, gv, gr], axis=-1).astype(BF16)
    uq = w_uq.reshape(MLA_Q_LORA, MLA_HEADS, MLA_NOPE + MLA_ROPE)
    uq = jnp.pad(uq, ((0, 0), (0, 0), (0, MLA_PAD - MLA_NOPE - MLA_ROPE)))
    w_uq_p = uq.reshape(MLA_Q_LORA, MLA_HEADS * MLA_PAD).astype(BF16)
    ukv = w_ukv.reshape(MLA_KV_LORA, MLA_HEADS, MLA_NOPE + MLA_V)
    uk = jnp.pad(ukv[:, :, :MLA_NOPE], ((0, 0), (0, 0), (0, MLA_PAD - MLA_NOPE)))
    w_uk_p = uk.reshape(MLA_KV_LORA, MLA_HEADS * MLA_PAD).astype(BF16)
    w_uv = ukv[:, :, MLA_NOPE:].reshape(MLA_KV_LORA, MLA_HEADS * MLA_V).astype(BF16)
    dkw = GLA_HEADS * GLA_DK
    w_gate = jnp.zeros((MLA_PAD, 2 * dkw), F32)
    w_gate = w_gate.at[0:GLA_GATE_RANK, 0:dkw].set(gw_f)
    w_gate = w_gate.at[GLA_GATE_RANK:2 * GLA_GATE_RANK, dkw:].set(gw_b).astype(BF16)
    b_gate = jnp.concatenate([gb_f, gb_b])[None, :]
    return w_in_p, w_uq_p, w_uk_p, w_uv, w_gate, b_gate


def kernel(x, p, w_in_even, mla_q_norm, w_uq, mla_kv_norm, w_ukv, gla_gate_w_fwd, gla_gate_b_fwd,
           gla_gate_w_bwd, gla_gate_b_bwd, gla_norm, w_in_odd, gqa_q_norm, gqa_k_norm, w_o, ln1_g,
           ln1_b, router_w, w1, w3, w2, ple_gate_w, ple_gate_b, ple_w, ln2_g, ln2_b):
    b, s, d = x.shape
    depth = w_o.shape[0]
    t = b * s
    alpha = (2.0 * depth) ** 0.25
    cap = EC_CAPACITY_FACTOR * s // N_EXPERTS
    tm = min(512, s)
    tm_ln = min(1024, s)
    tq = min(512, s)
    half = w_o.shape[1] // 2
    cos_a, sin_a = _rope_tables(s, MLA_ROPE, PE_LO, MLA_PAD)
    cos_c, sin_c = _rope_tables(s, GQA_HEAD_DIM, 0, GQA_KV_HEADS * GQA_HEAD_DIM)
    hw = GQA_KV_HEADS * GQA_HEAD_DIM
    head_of = np.arange(hw) // GQA_HEAD_DIM
    avg = jnp.asarray((head_of[:, None] == head_of[None, :]) / GQA_HEAD_DIM, BF16)

    n_e, ff = w1.shape[1], w1.shape[3]
    w1t = _transpose_cast(w1.reshape(depth * n_e, d, ff))
    w3t = _transpose_cast(w3.reshape(depth * n_e, d, ff))
    w2t = _transpose_cast(w2.reshape(depth * n_e, ff, d))
    p_all = p.reshape(depth * t, p.shape[-1])

    x2 = x.reshape(t, d)
    for i in range(depth):
        j = i // 2
        if i % 2 == 0:
            w_in_p, w_uq_p, w_uk_p, w_uv, w_gate, b_gate = _prep_even(
                w_in_even[j], w_uq[j], w_ukv[j], gla_gate_w_fwd[j], gla_gate_b_fwd[j],
                gla_gate_w_bwd[j], gla_gate_b_bwd[j])
            q, k, v, gq, gk, gv, la, gr = _even_proj(
                x2, w_in_p, mla_q_norm[j][None, :], w_uq_p, mla_kv_norm[j][None, :], w_uk_p, w_uv,
                w_gate, b_gate, cos_a, sin_a, s, tm)
            r3 = lambda a: a.reshape(b, s, a.shape[-1])
            o_mla = _mla_attn(r3(q), r3(k), r3(v), min(2 * tq, s)).reshape(t, -1)
            o_gla = _gla(r3(gq), r3(gk), r3(la), r3(gv), r3(gr), gla_norm[j][None, :]).reshape(t, -1)
            mix_a, mix_b, col_a, col_b = o_mla, o_gla, 0, 0
        else:
            q, k, v = _odd_proj(x2, w_in_odd[j].astype(BF16),
                                jnp.tile(gqa_q_norm[j], GQA_KV_HEADS)[None, :],
                                jnp.tile(gqa_k_norm[j], GQA_KV_HEADS)[None, :],
                                avg, cos_c, sin_c, s, tm)
            r3 = lambda a: a.reshape(b, s, a.shape[-1])
            o = _gqa_attn(r3(q), r3(k), r3(v), tq).reshape(t, -1)
            mix_a, mix_b, col_a, col_b = o, o, 0, 1
        wo = w_o[i].astype(BF16)
        x1, x1t = _out_ln(x2, mix_a, mix_b, col_a, col_b, wo[:half], wo[half:],
                          ln1_g[i][None, :], ln1_b[i][None, :], alpha, s, tm_ln)
        aff, slot = _route(x1.reshape(b, s, d), router_w[i].T, cap)
        ffn_t = _moe(x1t, slot, aff, w1t, w3t, w2t, cap, i)
        x2 = _ple_ln(x1, ffn_t, p_all, ple_gate_w[i].astype(BF16),
                     ple_gate_b[i][None, :], ple_w[i].astype(BF16), ln2_g[i][None, :], ln2_b[i][None, :],
                     alpha, tm_ln, i)
    return x2.reshape(b, s, d)
```

```python
import functools
import math

import jax
import jax.numpy as jnp
import numpy as np
from jax import lax
from jax.experimental import pallas as pl
from jax.experimental.pallas import tpu as pltpu

F32 = jnp.float32
BF16 = jnp.bfloat16

V7X_LANES = 128
V7X_VMEM_BYTES = 64 * 1024 * 1024
VMEM_LIMIT = 56 * 1024 * 1024

GRID_W = 64
ROPE_THETA = 10000.0
EPS = 1e-6
MLA_HEADS, MLA_Q_LORA, MLA_KV_LORA = 8, 256, 128
MLA_NOPE, MLA_ROPE, MLA_V = 64, 32, 64
GLA_HEADS, GLA_DK, GLA_DV = 4, 64, 128
GLA_GATE_RANK, GLA_GATE_NORM, GLA_CHUNK = 16, 16.0, 64
GQA_HEADS, GQA_KV_HEADS, GQA_HEAD_DIM = 16, 4, 64
N_EXPERTS, EC_CAPACITY_FACTOR = 16, 2
EVEN_SPLITS = (MLA_Q_LORA, MLA_KV_LORA, MLA_ROPE, GLA_HEADS * GLA_DK, GLA_HEADS * GLA_DK,
               GLA_HEADS * GLA_DV, 2 * GLA_GATE_RANK, GLA_HEADS * GLA_DV)
ODD_SPLITS = (GQA_HEADS * GQA_HEAD_DIM, GQA_KV_HEADS * GQA_HEAD_DIM, GQA_KV_HEADS * GQA_HEAD_DIM)
MLA_PAD = V7X_LANES
PE_LO, PE_HI = MLA_NOPE, MLA_NOPE + MLA_ROPE


def _dot(a, b):
    return jnp.dot(a, b, preferred_element_type=F32)


def _dot_nt(a, b):
    return lax.dot_general(a, b, (((1,), (1,)), ((), ())), preferred_element_type=F32)


def _dot_tn(a, b):
    return lax.dot_general(a, b, (((0,), (0,)), ((), ())), preferred_element_type=F32)


def _split2(x):
    hi = x.astype(BF16)
    lo = (x - hi.astype(F32)).astype(BF16)
    return hi, lo


def _dot_sel(sel, x):
    hi, lo = _split2(x)
    return _dot(sel, hi) + _dot(sel, lo)


def _dot_x_sel(x, sel):
    hi, lo = _split2(x)
    return _dot(hi, sel) + _dot(lo, sel)


def _params(sem):
    return pltpu.CompilerParams(dimension_semantics=sem, vmem_limit_bytes=VMEM_LIMIT)


def _rot_half(x, half):
    w = x.shape[-1]
    lane = lax.broadcasted_iota(jnp.int32, x.shape, x.ndim - 1)
    first = (lane % (2 * half)) < half
    return jnp.where(first, pltpu.roll(x, w - half, x.ndim - 1), pltpu.roll(x, half, x.ndim - 1))


def _layer_norm(y, g, b):
    mu = jnp.mean(y, axis=-1, keepdims=True)
    yc = y - mu
    var = jnp.mean(yc * yc, axis=-1, keepdims=True)
    return yc * lax.rsqrt(var + EPS) * g + b


def _rms(x, g):
    return x * lax.rsqrt(jnp.mean(x * x, axis=-1, keepdims=True) + EPS) * g


def _log_sigmoid(z):
    return jnp.minimum(z, 0.0) - jnp.log1p(jnp.exp(-jnp.abs(z)))


def _sigmoid(z):
    return 1.0 / (1.0 + jnp.exp(-z))


LOG2_E = math.log2(math.e)
KEY_BLK = 128


KEY_CHUNKS = 4


def _exp2_cols(m, s_scr, e_scr):
    seq, tq = s_scr.shape
    for c in range(tq // V7X_LANES):
        cols = slice(c * V7X_LANES, (c + 1) * V7X_LANES)
        m_b = jnp.broadcast_to(m[:, cols], (KEY_BLK, V7X_LANES))
        for r in range(seq // KEY_BLK):
            rows = slice(r * KEY_BLK, (r + 1) * KEY_BLK)
            e_scr[rows, cols] = jnp.exp2(s_scr[rows, cols] - m_b).astype(BF16)


VT_ROWS_PAD = 16


def _values_t(v_t):
    dv, seq = v_t.shape
    extra = jnp.where(lax.broadcasted_iota(jnp.int32, (VT_ROWS_PAD, seq), 0) == 0, 1.0, 0.0)
    return jnp.concatenate([v_t.astype(BF16), extra.astype(BF16)], axis=0)


def _attend_heads(n_heads, k_of, q_of, vt_of, s_ref, e_ref):
    assert s_ref.shape[0] == n_heads and e_ref.shape[0] == n_heads
    seq = s_ref.shape[1]
    ck = seq // KEY_CHUNKS
    m, acc = {}, {}
    for h in range(n_heads + 2):
        m_parts = []
        for c in range(KEY_CHUNKS):
            keys = slice(c * ck, (c + 1) * ck)
            if h < n_heads:
                s = _dot_nt(k_of(h, keys), q_of(h))
                s_ref[h, keys, :] = s
                m_parts.append(jnp.max(s, axis=0, keepdims=True))
            if 1 <= h <= n_heads:
                _exp2_cols(m[h - 1], s_ref.at[h - 1, keys], e_ref.at[h - 1, keys])
            if h >= 2:
                o = _dot(vt_of(h - 2, keys), e_ref[h - 2, keys, :])
                acc[h - 2] = o if c == 0 else acc[h - 2] + o
        if h < n_heads:
            m[h] = functools.reduce(jnp.maximum, m_parts)
    outs = []
    for h in range(n_heads):
        dv = acc[h].shape[0] - VT_ROWS_PAD
        outs.append(acc[h][:dv] * (1.0 / acc[h][dv:dv + 1]))
    return jnp.concatenate(outs, axis=0)


def _even_proj_kernel(x_ref, w_in_ref, qn_ref, w_uq_ref, kvn_ref, w_uk_ref, w_uv_ref,
                      w_gate_ref, b_gate_ref, cq_ref, sq_ref,
                      q_ref, k_ref, v_ref, gq_ref, gk_ref, gv_ref, la_ref, gr_ref):
    h = _dot(x_ref[...].astype(BF16), w_in_ref[...])
    cos = cq_ref[...]
    sin = sq_ref[...]
    lane = lax.broadcasted_iota(jnp.int32, cos.shape, 1)
    pe_lane = (lane >= PE_LO) & (lane < PE_HI)

    def rope(t):
        return t * cos + _rot_half(t, MLA_ROPE // 2) * sin

    c_q = _rms(h[:, 0:256], qn_ref[...])
    q = _dot(c_q.astype(BF16), w_uq_ref[...]) * ((MLA_NOPE + MLA_ROPE) ** -0.5 * LOG2_E)
    c_kv = _rms(h[:, 256:384], kvn_ref[...]).astype(BF16)
    kn = _dot(c_kv, w_uk_ref[...])
    v_ref[...] = _dot(c_kv, w_uv_ref[...]).astype(v_ref.dtype)
    chunk = h[:, 384:512]
    k_pe = jnp.where(pe_lane, rope(chunk), 0.0)
    for hd in range(MLA_HEADS):
        sl = slice(hd * MLA_PAD, (hd + 1) * MLA_PAD)
        q_ref[:, sl] = rope(q[:, sl]).astype(q_ref.dtype)
        k_ref[:, sl] = (kn[:, sl] + k_pe).astype(k_ref.dtype)
    z = _dot(chunk.astype(BF16), w_gate_ref[...]) + b_gate_ref[...]
    la = _log_sigmoid(z) * (1.0 / GLA_GATE_NORM)
    blk = 2 * GLA_CHUNK
    r = lax.broadcasted_iota(jnp.int32, (blk, blk), 0)
    c = lax.broadcasted_iota(jnp.int32, (blk, blk), 1)
    same = (r // GLA_CHUNK) == (c // GLA_CHUNK)
    lower = jnp.where(same & (c <= r), 1.0, 0.0).astype(BF16)
    upper = jnp.where(same & (c >= r), 1.0, 0.0).astype(BF16)
    dkw = GLA_HEADS * GLA_DK
    for t in range(la.shape[0] // blk):
        rows = slice(t * blk, (t + 1) * blk)
        la_ref[rows, :dkw] = _dot_sel(lower, la[rows, :dkw])
        la_ref[rows, dkw:] = _dot_sel(upper, la[rows, dkw:])
    gq_ref[...] = h[:, 512:768] * (GLA_DK ** -0.5)
    gk_ref[...] = h[:, 768:1024]
    gv_ref[...] = h[:, 1024:1536].astype(gv_ref.dtype)
    gr = h[:, 1536:2048]
    gr_ref[...] = gr * _sigmoid(gr)


def _even_proj(x2, w_in_p, qn, w_uq_p, kvn, w_uk_p, w_uv, w_gate, b_gate, cq, sq, seq, tm):
    t, d = x2.shape
    nsb = seq // tm
    row = lambda i: (i, 0)
    fixed = lambda i: (0, 0)
    pos = lambda i: (i % nsb, 0)
    full = lambda a: pl.BlockSpec(a.shape, fixed)
    outs = [(1024, BF16), (1024, BF16), (512, BF16), (256, F32), (256, F32), (512, BF16), (512, F32), (512, F32)]
    return pl.pallas_call(
        _even_proj_kernel,
        grid=(t // tm,),
        in_specs=[pl.BlockSpec((tm, d), row), full(w_in_p), full(qn), full(w_uq_p), full(kvn),
                  full(w_uk_p), full(w_uv), full(w_gate), full(b_gate),
                  pl.BlockSpec((tm, MLA_PAD), pos), pl.BlockSpec((tm, MLA_PAD), pos)],
        out_specs=[pl.BlockSpec((tm, n), row) for n, _ in outs],
        out_shape=[jax.ShapeDtypeStruct((t, n), dt) for n, dt in outs],
        compiler_params=_params(("parallel",)),
        name="even_proj",
    )(x2, w_in_p, qn, w_uq_p, kvn, w_uk_p, w_uv, w_gate, b_gate, cq, sq)


def _mla_attn_kernel(q_ref, k_ref, v_ref, o_ref, vt_ref, s_ref, e_ref):
    out_w = 2 * MLA_V

    @pl.when(pl.program_id(2) == 0)
    def _():
        r = lax.broadcasted_iota(jnp.int32, (out_w, out_w), 0)
        c = lax.broadcasted_iota(jnp.int32, (out_w, out_w), 1)
        eye = jnp.where(r == c, 1.0, 0.0).astype(BF16)
        v_t = _dot_nt(eye, v_ref[0])
        for a in range(2):
            vt_ref[a] = _values_t(v_t[a * MLA_V:(a + 1) * MLA_V])

    out_t = _attend_heads(
        2,
        lambda a, keys: k_ref[0, keys, a * MLA_PAD:(a + 1) * MLA_PAD],
        lambda a: q_ref[0, :, a * MLA_PAD:(a + 1) * MLA_PAD],
        lambda a, keys: vt_ref[a, :, keys],
        s_ref, e_ref)
    o_ref[0] = out_t.T.astype(o_ref.dtype)


def _mla_attn(q, k, v, tq):
    b, s, _ = q.shape
    return pl.pallas_call(
        _mla_attn_kernel,
        grid=(b, MLA_HEADS // 2, s // tq),
        in_specs=[pl.BlockSpec((1, tq, 2 * MLA_PAD), lambda i, j, t: (i, t, j)),
                  pl.BlockSpec((1, s, 2 * MLA_PAD), lambda i, j, t: (i, 0, j)),
                  pl.BlockSpec((1, s, 2 * MLA_V), lambda i, j, t: (i, 0, j))],
        out_specs=pl.BlockSpec((1, tq, 2 * MLA_V), lambda i, j, t: (i, t, j)),
        out_shape=jax.ShapeDtypeStruct((b, s, MLA_HEADS * MLA_V), BF16),
        scratch_shapes=[pltpu.VMEM((2, MLA_V + VT_ROWS_PAD, s), BF16),
                        pltpu.VMEM((2, s, tq), F32), pltpu.VMEM((2, s, tq), BF16)],
        compiler_params=_params(("parallel", "parallel", "arbitrary")),
        name="mla_attn",
    )(q, k, v)


def _gla_kernel(q_ref, k_ref, cum_ref, v_ref, gr_ref, g_ref, o_ref, state_ref, of_ref, ob_ref):
    seq = q_ref.shape[1]
    n_chunks = seq // GLA_CHUNK
    L = GLA_CHUNK
    dkw = GLA_HEADS * GLA_DK
    dvw = GLA_HEADS * GLA_DV

    def iota(shape, dim):
        return lax.broadcasted_iota(jnp.int32, shape, dim)

    row_l = iota((L, dkw), 0)
    col_m = iota((L, dkw), 1) % L
    k_own = (iota((dkw, dkw), 0) // L) == (iota((dkw, dkw), 1) // GLA_DK)
    v_own = (iota((dkw, dvw), 0) // L) == (iota((dkw, dvw), 1) // GLA_DV)
    s_own = (iota((dvw, dkw), 0) // GLA_DV) == (iota((dvw, dkw), 1) // GLA_DK)
    state_ref[...] = jnp.zeros_like(state_ref)

    def body(i, carry):
        for d, fwd in enumerate((True, False)):
            n = i if fwd else n_chunks - 1 - i
            rows = pl.ds(pl.multiple_of(n * L, L), L)
            cum = cum_ref[0, rows, d * dkw:(d + 1) * dkw]
            last = cum[L - 1:L, :] if fwd else cum[0:1, :]
            q = q_ref[0, rows, :]
            k = k_ref[0, rows, :]
            v = v_ref[0, rows, :]
            qe = (q * jnp.exp(cum)).astype(BF16)
            kg = (k * jnp.exp(-cum)).astype(BF16)
            kdec = (k * jnp.exp(last - cum)).astype(BF16)
            k_blk = jnp.where(k_own, jnp.concatenate([kg] * GLA_HEADS, axis=0), jnp.zeros((), BF16))
            att = _dot_nt(qe, k_blk)
            keep = (col_m <= row_l) if fwd else (col_m >= row_l)
            att = jnp.where(keep, att, 0.0).astype(BF16)
            v_blk = jnp.where(v_own, jnp.concatenate([v] * GLA_HEADS, axis=0), jnp.zeros((), BF16))
            st = state_ref[d]
            o = _dot(att, v_blk) + _dot_nt(qe, st.astype(BF16))
            state_ref[d] = st * jnp.exp(last) + jnp.where(s_own, _dot_tn(v, kdec), 0.0)
            if fwd:
                of_ref[rows, :] = o
            else:
                ob_ref[rows, :] = o
        return carry

    lax.fori_loop(0, n_chunks, body, 0, unroll=4)

    def finish(n, carry):
        rows = pl.ds(pl.multiple_of(n * L, L), L)
        tot = of_ref[rows, :] + ob_ref[rows, :]
        for hd in range(GLA_HEADS):
            sl = slice(hd * GLA_DV, (hd + 1) * GLA_DV)
            o_ref[0, rows, sl] = (_rms(tot[:, sl], g_ref[...]) * gr_ref[0, rows, sl]).astype(o_ref.dtype)
        return carry

    lax.fori_loop(0, n_chunks, finish, 0)


def _gla(gq, gk, cum, gv, gr, g_norm):
    b, s, _ = gq.shape
    dkw = GLA_HEADS * GLA_DK
    dvw = GLA_HEADS * GLA_DV
    blk = lambda w: pl.BlockSpec((1, s, w), lambda i: (i, 0, 0))
    return pl.pallas_call(
        _gla_kernel,
        grid=(b,),
        in_specs=[blk(dkw), blk(dkw), blk(2 * dkw), blk(dvw), blk(dvw),
                  pl.BlockSpec((1, GLA_DV), lambda i: (0, 0))],
        out_specs=blk(dvw),
        out_shape=jax.ShapeDtypeStruct((b, s, dvw), BF16),
        scratch_shapes=[pltpu.VMEM((2, dvw, dkw), F32), pltpu.VMEM((s, dvw), F32), pltpu.VMEM((s, dvw), F32)],
        compiler_params=_params(("parallel",)),
        name="gla",
    )(gq, gk, cum, gv, gr, g_norm)


def _odd_proj_kernel(x_ref, w_ref, gq_ref, gk_ref, avg_ref, cos_ref, sin_ref, q_ref, k_ref, v_ref):
    h = _dot(x_ref[...].astype(BF16), w_ref[...])
    cos = cos_ref[...]
    sin = sin_ref[...]
    avg = avg_ref[...]
    nq = GQA_HEADS * GQA_HEAD_DIM
    nk = GQA_KV_HEADS * GQA_HEAD_DIM

    def norm_rope(t, g):
        ms = _dot_x_sel(t * t, avg)
        t = t * lax.rsqrt(ms + EPS) * g
        return t * cos + _rot_half(t, GQA_HEAD_DIM // 2) * sin

    for c in range(nq // nk):
        sl = slice(c * nk, (c + 1) * nk)
        q_ref[:, sl] = (norm_rope(h[:, sl], gq_ref[...]) * (GQA_HEAD_DIM ** -0.5 * LOG2_E)).astype(q_ref.dtype)
    k_ref[...] = norm_rope(h[:, nq:nq + nk], gk_ref[...]).astype(k_ref.dtype)
    v_ref[...] = h[:, nq + nk:].astype(v_ref.dtype)


def _odd_proj(x2, w, gq, gk, avg, cos, sin, seq, tm):
    t, d = x2.shape
    nsb = seq // tm
    row = lambda i: (i, 0)
    fixed = lambda i: (0, 0)
    pos = lambda i: (i % nsb, 0)
    full = lambda a: pl.BlockSpec(a.shape, fixed)
    outs = [ODD_SPLITS[0], ODD_SPLITS[1], ODD_SPLITS[2]]
    return pl.pallas_call(
        _odd_proj_kernel,
        grid=(t // tm,),
        in_specs=[pl.BlockSpec((tm, d), row), full(w), full(gq), full(gk), full(avg),
                  pl.BlockSpec((tm, cos.shape[1]), pos), pl.BlockSpec((tm, sin.shape[1]), pos)],
        out_specs=[pl.BlockSpec((tm, n), row) for n in outs],
        out_shape=[jax.ShapeDtypeStruct((t, n), BF16) for n in outs],
        compiler_params=_params(("parallel",)),
        name="odd_proj",
    )(x2, w, gq, gk, avg, cos, sin)


def _gqa_attn_kernel(q_ref, k_ref, v_ref, o_ref, kp_ref, vt_ref, s_ref, e_ref):
    j = pl.program_id(1)
    group = GQA_HEADS // GQA_KV_HEADS
    width = GQA_KV_HEADS * GQA_HEAD_DIM
    pair = 2 * GQA_HEAD_DIM

    @pl.when(pl.program_id(2) == 0)
    def _():
        r = lax.broadcasted_iota(jnp.int32, (width, pair), 0)
        c = lax.broadcasted_iota(jnp.int32, (width, pair), 1)
        twice = (r // GQA_HEAD_DIM == j) & (r % GQA_HEAD_DIM == c % GQA_HEAD_DIM)
        kp_ref[...] = _dot(k_ref[0], jnp.where(twice, 1.0, 0.0).astype(BF16)).astype(BF16)
        rv = lax.broadcasted_iota(jnp.int32, (GQA_HEAD_DIM, width), 0)
        cv = lax.broadcasted_iota(jnp.int32, (GQA_HEAD_DIM, width), 1)
        mine = cv == j * GQA_HEAD_DIM + rv
        vt_ref[...] = _values_t(_dot_nt(jnp.where(mine, 1.0, 0.0).astype(BF16), v_ref[0]))

    lane = lax.broadcasted_iota(jnp.int32, (q_ref.shape[1], pair), 1)

    def q_of(g):
        qp = q_ref[0, :, (g // 2) * pair:(g // 2 + 1) * pair]
        return jnp.where(lane // GQA_HEAD_DIM == g % 2, qp, jnp.zeros_like(qp))

    out_t = _attend_heads(group, lambda g, keys: kp_ref[keys, :], q_of, lambda g, keys: vt_ref[:, keys],
                          s_ref, e_ref)
    o_ref[0] = out_t.T.astype(o_ref.dtype)


def _gqa_attn(q, k, v, tq):
    b, s, _ = q.shape
    width = GQA_KV_HEADS * GQA_HEAD_DIM
    group = GQA_HEADS // GQA_KV_HEADS
    return pl.pallas_call(
        _gqa_attn_kernel,
        grid=(b, GQA_KV_HEADS, s // tq),
        in_specs=[pl.BlockSpec((1, tq, width), lambda i, j, t: (i, t, j)),
                  pl.BlockSpec((1, s, width), lambda i, j, t: (i, 0, 0)),
                  pl.BlockSpec((1, s, width), lambda i, j, t: (i, 0, 0))],
        out_specs=pl.BlockSpec((1, tq, width), lambda i, j, t: (i, t, j)),
        out_shape=jax.ShapeDtypeStruct((b, s, GQA_HEADS * GQA_HEAD_DIM), BF16),
        scratch_shapes=[pltpu.VMEM((s, 2 * GQA_HEAD_DIM), BF16),
                        pltpu.VMEM((GQA_HEAD_DIM + VT_ROWS_PAD, s), BF16),
                        pltpu.VMEM((group, s, tq), F32), pltpu.VMEM((group, s, tq), BF16)],
        compiler_params=_params(("parallel", "parallel", "arbitrary")),
        name="gqa_attn",
    )(q, k, v)


def _out_ln_kernel(alpha, x_ref, ma_ref, mb_ref, wa_ref, wb_ref, g_ref, b_ref, y_ref, yt_ref):
    y = alpha * x_ref[...] + _dot(ma_ref[...].astype(BF16), wa_ref[...]) \
        + _dot(mb_ref[...].astype(BF16), wb_ref[...])
    y = _layer_norm(y, g_ref[...], b_ref[...])
    y_ref[...] = y
    yt_ref[0] = y.T.astype(BF16)


def _out_ln(x2, mix_a, mix_b, col_a, col_b, w_a, w_b, g, b, alpha, seq, tm):
    t, d = x2.shape
    half = w_a.shape[0]
    nsb = seq // tm
    row = lambda i: (i, 0)
    fixed = lambda i: (0, 0)
    return pl.pallas_call(
        functools.partial(_out_ln_kernel, alpha),
        grid=(t // tm,),
        in_specs=[pl.BlockSpec((tm, d), row),
                  pl.BlockSpec((tm, half), lambda i: (i, col_a)),
                  pl.BlockSpec((tm, half), lambda i: (i, col_b)),
                  pl.BlockSpec(w_a.shape, fixed), pl.BlockSpec(w_b.shape, fixed),
                  pl.BlockSpec(g.shape, fixed), pl.BlockSpec(b.shape, fixed)],
        out_specs=[pl.BlockSpec((tm, d), row),
                   pl.BlockSpec((1, d, tm), lambda i: (i // nsb, 0, i % nsb))],
        out_shape=[jax.ShapeDtypeStruct((t, d), F32), jax.ShapeDtypeStruct((t // seq, d, seq), BF16)],
        compiler_params=_params(("parallel",)),
        name="out_ln",
    )(x2, mix_a, mix_b, w_a, w_b, g, b)


def _route_kernel(cap, x_ref, rw_ref, aff_ref, slot_ref):
    x = x_ref[0]
    seq = x.shape[0]
    xh, xm = _split2(x)
    wh, wm = _split2(rw_ref[...])
    logits = _dot_nt(wh, xh) + (_dot_nt(wh, xm) + _dot_nt(wm, xh))
    e = jnp.exp(logits - jnp.max(logits, axis=0, keepdims=True))
    aff = e / jnp.sum(e, axis=0, keepdims=True)
    aff_ref[0] = aff
    bits = pltpu.bitcast(aff, jnp.int32)

    def enough(cand):
        return jnp.sum((bits >= cand).astype(jnp.int32), axis=1, keepdims=True) >= cap

    top = jnp.full((bits.shape[0], 1), 1 << 30, jnp.int32)
    thr0 = jnp.where(enough(top), top, 0)

    def pick(i, thr):
        lo = 28 - 2 * i
        c1, c2, c3 = thr | (jnp.int32(1) << lo), thr | (jnp.int32(2) << lo), thr | (jnp.int32(3) << lo)
        return jnp.where(enough(c3), c3, jnp.where(enough(c2), c2, jnp.where(enough(c1), c1, thr)))

    thr = lax.fori_loop(0, 15, pick, thr0)
    above = bits > thr
    tie = bits == thr
    need = cap - jnp.sum(above.astype(jnp.int32), axis=1, keepdims=True)

    blk = 256 if seq % 256 == 0 else V7X_LANES
    r = lax.broadcasted_iota(jnp.int32, (blk, blk), 0)
    c = lax.broadcasted_iota(jnp.int32, (blk, blk), 1)
    before = jnp.where(r < c, 1.0, 0.0).astype(BF16)

    def prefix(mask):
        m = jnp.where(mask, 1.0, 0.0).astype(BF16)
        run = jnp.zeros((mask.shape[0], 1), F32)
        parts = []
        for t in range(seq // blk):
            mb = m[:, t * blk:(t + 1) * blk]
            parts.append(_dot(mb, before) + run)
            run = run + jnp.sum(mb.astype(F32), axis=1, keepdims=True)
        return jnp.concatenate(parts, axis=1).astype(jnp.int32)

    chosen = above | (tie & (prefix(tie) < need))
    slot_ref[0] = jnp.where(chosen, prefix(chosen), -1)


def _route(x1, rw_t, cap):
    b, s, d = x1.shape
    e = rw_t.shape[0]
    return pl.pallas_call(
        functools.partial(_route_kernel, cap),
        grid=(b,),
        in_specs=[pl.BlockSpec((1, s, d), lambda i: (i, 0, 0)), pl.BlockSpec((e, d), lambda i: (0, 0))],
        out_specs=[pl.BlockSpec((1, e, s), lambda i: (i, 0, 0)), pl.BlockSpec((1, e, s), lambda i: (i, 0, 0))],
        out_shape=[jax.ShapeDtypeStruct((b, e, s), F32), jax.ShapeDtypeStruct((b, e, s), jnp.int32)],
        compiler_params=_params(("parallel",)),
        name="route",
    )(x1, rw_t)


def _moe_kernel(cap, ts, xt_ref, slot_ref, aff_ref, w1_ref, w3_ref, w2_ref, o_ref):
    e_id = pl.program_id(1)
    seq = xt_ref.shape[2]
    ff = w2_ref.shape[2]

    @pl.when(e_id == 0)
    def _():
        o_ref[...] = jnp.zeros_like(o_ref)

    c_row = lax.broadcasted_iota(jnp.int32, (cap, seq), 0)
    pick = jnp.where(slot_ref[0, 0] == c_row, 1.0, 0.0).astype(BF16)
    hd2 = xt_ref.shape[1] // 2
    xg = jnp.concatenate([_dot_nt(xt_ref[0, :hd2, :], pick), _dot_nt(xt_ref[0, hd2:, :], pick)],
                         axis=0).astype(BF16)
    h1 = _dot(w1_ref[0], xg)
    h3 = _dot(w3_ref[0], xg)
    hid = (h1 * _sigmoid(h1) * h3).astype(BF16)
    ye = jnp.concatenate([_dot(w2_ref[0, :hd2, :], hid), _dot(w2_ref[0, hd2:, :], hid)],
                         axis=0).astype(BF16)
    for t in range(seq // ts):
        cols = slice(t * ts, (t + 1) * ts)
        o_ref[0, :, cols] += _dot(ye, pick[:, cols]) * aff_ref[0, 0, :, cols]


def _moe(xt, slot, aff, w1t, w3t, w2t, cap, layer):
    b, d, s = xt.shape
    e = slot.shape[1]
    ff = w2t.shape[2]
    ts = min(512, s)
    w_blk = lambda i, j: (layer * e + j, 0, 0)
    return pl.pallas_call(
        functools.partial(_moe_kernel, cap, ts),
        grid=(b, e),
        in_specs=[pl.BlockSpec((1, d, s), lambda i, j: (i, 0, 0)),
                  pl.BlockSpec((1, 1, 1, s), lambda i, j: (i, j, 0, 0)),
                  pl.BlockSpec((1, 1, 1, s), lambda i, j: (i, j, 0, 0)),
                  pl.BlockSpec((1, ff, d), w_blk), pl.BlockSpec((1, ff, d), w_blk),
                  pl.BlockSpec((1, d, ff), w_blk)],
        out_specs=pl.BlockSpec((1, d, s), lambda i, j: (i, 0, 0)),
        out_shape=jax.ShapeDtypeStruct((b, d, s), F32),
        compiler_params=_params(("parallel", "arbitrary")),
        name="moe",
    )(xt, slot.reshape(b, e, 1, s), aff.reshape(b, e, 1, s), w1t, w3t, w2t)


def _transpose_cast_kernel(w_ref, o_ref):
    o_ref[0] = w_ref[0].T.astype(o_ref.dtype)


def _transpose_cast(w):
    n, r, c = w.shape
    return pl.pallas_call(
        _transpose_cast_kernel,
        grid=(n,),
        in_specs=[pl.BlockSpec((1, r, c), lambda i: (i, 0, 0))],
        out_specs=pl.BlockSpec((1, c, r), lambda i: (i, 0, 0)),
        out_shape=jax.ShapeDtypeStruct((n, c, r), BF16),
        compiler_params=_params(("parallel",)),
        name="transpose_cast",
    )(w)


def _ple_ln_kernel(alpha, x_ref, f_ref, p_ref, wg_ref, bg_ref, wp_ref, g_ref, b_ref, y_ref):
    x = x_ref[...]
    gate = _sigmoid(_dot(x.astype(BF16), wg_ref[...]) + bg_ref[...])
    ple = gate * _dot(p_ref[...].astype(BF16), wp_ref[...])
    ffn = f_ref[0].T
    y_ref[...] = _layer_norm(alpha * x + ffn + ple, g_ref[...], b_ref[...])


def _ple_ln(x1, ffn_t, p2, wg, bg, wp, g, b, alpha, tm, layer):
    t, d = x1.shape
    nsb = ffn_t.shape[2] // tm
    row = lambda i: (i, 0)
    p_row = lambda i: (layer * (t // tm) + i, 0)
    fixed = lambda i: (0, 0)
    full = lambda a: pl.BlockSpec(a.shape, fixed)
    return pl.pallas_call(
        functools.partial(_ple_ln_kernel, alpha),
        grid=(t // tm,),
        in_specs=[pl.BlockSpec((tm, d), row),
                  pl.BlockSpec((1, d, tm), lambda i: (i // nsb, 0, i % nsb)),
                  pl.BlockSpec((tm, p2.shape[1]), p_row), full(wg), full(bg), full(wp), full(g), full(b)],
        out_specs=pl.BlockSpec((tm, d), row),
        out_shape=jax.ShapeDtypeStruct((t, d), F32),
        compiler_params=_params(("parallel",)),
        name="ple_ln",
    )(x1, ffn_t, p2, wg, bg, wp, g, b)


def _rope_tables(seq, rot_dim, lo, width):
    rows = seq // GRID_W
    row = jnp.repeat(jnp.arange(rows, dtype=F32), GRID_W)
    col = jnp.tile(jnp.arange(GRID_W, dtype=F32), rows)
    axis_dim = rot_dim // 2
    inv = ROPE_THETA ** (-jnp.arange(0, axis_dim, 2, dtype=F32) / axis_dim)
    ang = jnp.concatenate([row[:, None] * inv, col[:, None] * inv], axis=-1)
    cos, sin = jnp.cos(ang), jnp.sin(ang)
    cos2 = jnp.concatenate([cos, cos], axis=-1)
    sin2 = jnp.concatenate([-sin, sin], axis=-1)
    if lo == 0:
        reps = width // rot_dim
        return jnp.tile(cos2, (1, reps)), jnp.tile(sin2, (1, reps))
    pad_l = jnp.ones((seq, lo), F32)
    pad_r = jnp.ones((seq, width - lo - rot_dim), F32)
    cos_t = jnp.concatenate([pad_l, cos2, pad_r], axis=-1)
    sin_t = jnp.concatenate([0 * pad_l, sin2, 0 * pad_r], axis=-1)
    return cos_t, sin_t


def _prep_even(w_in, w_uq, w_ukv, gw_f, gb_f, gw_b, gb_b):
    d = w_in.shape[0]
    offs = np.cumsum(EVEN_SPLITS)[:-1].tolist()
    c_q, c_kv, k_pe, gq, gk, gv, g_lr, gr = jnp.split(w_in, offs, axis=-1)
    z32 = jnp.zeros((d, 32), w_in.dtype)
    chunk = jnp.concatenate([g_lr, z32, k_pe, z32], axis=-1)
    w_in_p = jnp.concatenate([c_q, c_kv, chunk, gq, gk, gv, gr], axis=-1).astype(BF16)
    uq = w_uq.reshape(MLA_Q_LORA, MLA_HEADS, MLA_NOPE + MLA_ROPE)
    uq = jnp.pad(uq, ((0, 0), (0, 0), (0, MLA_PAD - MLA_NOPE - MLA_ROPE)))
    w_uq_p = uq.reshape(MLA_Q_LORA, MLA_HEADS * MLA_PAD).astype(BF16)
    ukv = w_ukv.reshape(MLA_KV_LORA, MLA_HEADS, MLA_NOPE + MLA_V)
    uk = jnp.pad(ukv[:, :, :MLA_NOPE], ((0, 0), (0, 0), (0, MLA_PAD - MLA_NOPE)))
    w_uk_p = uk.reshape(MLA_KV_LORA, MLA_HEADS * MLA_PAD).astype(BF16)
    w_uv = ukv[:, :, MLA_NOPE:].reshape(MLA_KV_LORA, MLA_HEADS * MLA_V).astype(BF16)
    dkw = GLA_HEADS * GLA_DK
    w_gate = jnp.zeros((MLA_PAD, 2 * dkw), F32)
    w_gate = w_gate.at[0:GLA_GATE_RANK, 0:dkw].set(gw_f)
    w_gate = w_gate.at[GLA_GATE_RANK:2 * GLA_GATE_RANK, dkw:].set(gw_b).astype(BF16)
    b_gate = jnp.concatenate([gb_f, gb_b])[None, :]
    return w_in_p, w_uq_p, w_uk_p, w_uv, w_gate, b_gate


def kernel(x, p, w_in_even, mla_q_norm, w_uq, mla_kv_norm, w_ukv, gla_gate_w_fwd, gla_gate_b_fwd,
           gla_gate_w_bwd, gla_gate_b_bwd, gla_norm, w_in_odd, gqa_q_norm, gqa_k_norm, w_o, ln1_g,
           ln1_b, router_w, w1, w3, w2, ple_gate_w, ple_gate_b, ple_w, ln2_g, ln2_b):
    b, s, d = x.shape
    depth = w_o.shape[0]
    t = b * s
    alpha = (2.0 * depth) ** 0.25
    cap = EC_CAPACITY_FACTOR * s // N_EXPERTS
    tm = min(512, s)
    tm_ln = min(1024, s)
    tq = min(512, s)
    half = w_o.shape[1] // 2
    cos_a, sin_a = _rope_tables(s, MLA_ROPE, PE_LO, MLA_PAD)
    cos_c, sin_c = _rope_tables(s, GQA_HEAD_DIM, 0, GQA_KV_HEADS * GQA_HEAD_DIM)
    hw = GQA_KV_HEADS * GQA_HEAD_DIM
    head_of = np.arange(hw) // GQA_HEAD_DIM
    avg = jnp.asarray((head_of[:, None] == head_of[None, :]) / GQA_HEAD_DIM, BF16)

    n_e, ff = w1.shape[1], w1.shape[3]
    w1t = _transpose_cast(w1.reshape(depth * n_e, d, ff))
    w3t = _transpose_cast(w3.reshape(depth * n_e, d, ff))
    w2t = _transpose_cast(w2.reshape(depth * n_e, ff, d))
    p_all = p.reshape(depth * t, p.shape[-1])

    x2 = x.reshape(t, d)
    for i in range(depth):
        j = i // 2
        if i % 2 == 0:
            w_in_p, w_uq_p, w_uk_p, w_uv, w_gate, b_gate = _prep_even(
                w_in_even[j], w_uq[j], w_ukv[j], gla_gate_w_fwd[j], gla_gate_b_fwd[j],
                gla_gate_w_bwd[j], gla_gate_b_bwd[j])
            q, k, v, gq, gk, gv, la, gr = _even_proj(
                x2, w_in_p, mla_q_norm[j][None, :], w_uq_p, mla_kv_norm[j][None, :], w_uk_p, w_uv,
                w_gate, b_gate, cos_a, sin_a, s, tm_ln)
            r3 = lambda a: a.reshape(b, s, a.shape[-1])
            o_mla = _mla_attn(r3(q), r3(k), r3(v), tq).reshape(t, -1)
            o_gla = _gla(r3(gq), r3(gk), r3(la), r3(gv), r3(gr), gla_norm[j][None, :]).reshape(t, -1)
            mix_a, mix_b, col_a, col_b = o_mla, o_gla, 0, 0
        else:
            q, k, v = _odd_proj(x2, w_in_odd[j].astype(BF16),
                                jnp.tile(gqa_q_norm[j], GQA_KV_HEADS)[None, :],
                                jnp.tile(gqa_k_norm[j], GQA_KV_HEADS)[None, :],
                                avg, cos_c, sin_c, s, tm_ln)
            r3 = lambda a: a.reshape(b, s, a.shape[-1])
            o = _gqa_attn(r3(q), r3(k), r3(v), tq).reshape(t, -1)
            mix_a, mix_b, col_a, col_b = o, o, 0, 1
        wo = w_o[i].astype(BF16)
        x1, x1t = _out_ln(x2, mix_a, mix_b, col_a, col_b, wo[:half], wo[half:],
                          ln1_g[i][None, :], ln1_b[i][None, :], alpha, s, tm_ln)
        aff, slot = _route(x1.reshape(b, s, d), router_w[i].T, cap)
        ffn_t = _moe(x1t, slot, aff, w1t, w3t, w2t, cap, i)
        x2 = _ple_ln(x1, ffn_t, p_all, ple_gate_w[i].astype(BF16),
                     ple_gate_b[i][None, :], ple_w[i].astype(BF16), ln2_g[i][None, :], ln2_b[i][None, :],
                     alpha, tm_ln, i)
    return x2.reshape(b, s, d)
```

```python
import functools
import math

import jax
import jax.numpy as jnp
import numpy as np
from jax import lax
from jax.experimental import pallas as pl
from jax.experimental.pallas import tpu as pltpu

F32 = jnp.float32
BF16 = jnp.bfloat16

V7X_LANES = 128
V7X_VMEM_BYTES = 64 * 1024 * 1024
VMEM_LIMIT = V7X_VMEM_BYTES - 8 * 1024 * 1024

ROW_BLOCK = 1024
QUERY_BLOCK = 512
SCATTER_BLOCK = 512

GRID_W = 64
ROPE_THETA = 10000.0
EPS = 1e-6
MLA_HEADS, MLA_Q_LORA, MLA_KV_LORA = 8, 256, 128
MLA_NOPE, MLA_ROPE, MLA_V = 64, 32, 64
GLA_HEADS, GLA_DK, GLA_DV = 4, 64, 128
GLA_GATE_RANK, GLA_GATE_NORM, GLA_CHUNK = 16, 16.0, 64
GQA_HEADS, GQA_KV_HEADS, GQA_HEAD_DIM = 16, 4, 64
N_EXPERTS, EC_CAPACITY_FACTOR = 16, 2
EVEN_SPLITS = (MLA_Q_LORA, MLA_KV_LORA, MLA_ROPE, GLA_HEADS * GLA_DK, GLA_HEADS * GLA_DK,
               GLA_HEADS * GLA_DV, 2 * GLA_GATE_RANK, GLA_HEADS * GLA_DV)
ODD_SPLITS = (GQA_HEADS * GQA_HEAD_DIM, GQA_KV_HEADS * GQA_HEAD_DIM, GQA_KV_HEADS * GQA_HEAD_DIM)
MLA_PAD = V7X_LANES
PE_LO, PE_HI = MLA_NOPE, MLA_NOPE + MLA_ROPE


def _dot(a, b):
    return jnp.dot(a, b, preferred_element_type=F32)


def _dot_nt(a, b):
    return lax.dot_general(a, b, (((1,), (1,)), ((), ())), preferred_element_type=F32)


def _dot_tn(a, b):
    return lax.dot_general(a, b, (((0,), (0,)), ((), ())), preferred_element_type=F32)


def _split2(x):
    hi = x.astype(BF16)
    lo = (x - hi.astype(F32)).astype(BF16)
    return hi, lo


def _dot_sel(sel, x):
    hi, lo = _split2(x)
    return _dot(sel, hi) + _dot(sel, lo)


def _dot_x_sel(x, sel):
    hi, lo = _split2(x)
    return _dot(hi, sel) + _dot(lo, sel)


def _params(sem):
    return pltpu.CompilerParams(dimension_semantics=sem, vmem_limit_bytes=VMEM_LIMIT)


def _rot_half(x, half):
    w = x.shape[-1]
    lane = lax.broadcasted_iota(jnp.int32, x.shape, x.ndim - 1)
    first = (lane % (2 * half)) < half
    return jnp.where(first, pltpu.roll(x, w - half, x.ndim - 1), pltpu.roll(x, half, x.ndim - 1))


def _layer_norm(y, g, b):
    mu = jnp.mean(y, axis=-1, keepdims=True)
    yc = y - mu
    var = jnp.mean(yc * yc, axis=-1, keepdims=True)
    return yc * lax.rsqrt(var + EPS) * g + b


def _rms(x, g):
    return x * lax.rsqrt(jnp.mean(x * x, axis=-1, keepdims=True) + EPS) * g


def _log_sigmoid(z):
    return jnp.minimum(z, 0.0) - jnp.log1p(jnp.exp(-jnp.abs(z)))


def _sigmoid(z):
    return 1.0 / (1.0 + jnp.exp(-z))


LOG2_E = math.log2(math.e)
KEY_BLK = 128


KEY_CHUNKS = 4


def _exp2_cols(m, s_scr, e_scr):
    seq, tq = s_scr.shape
    for c in range(tq // V7X_LANES):
        cols = slice(c * V7X_LANES, (c + 1) * V7X_LANES)
        m_b = jnp.broadcast_to(m[:, cols], (KEY_BLK, V7X_LANES))
        for r in range(seq // KEY_BLK):
            rows = slice(r * KEY_BLK, (r + 1) * KEY_BLK)
            e_scr[rows, cols] = jnp.exp2(s_scr[rows, cols] - m_b).astype(BF16)


VT_ROWS_PAD = 16


def _values_t(v_t):
    dv, seq = v_t.shape
    extra = jnp.where(lax.broadcasted_iota(jnp.int32, (VT_ROWS_PAD, seq), 0) == 0, 1.0, 0.0)
    return jnp.concatenate([v_t.astype(BF16), extra.astype(BF16)], axis=0)


def _attend_heads(n_heads, k_of, q_of, vt_of, s_ref, e_ref):
    assert s_ref.shape[0] == n_heads and e_ref.shape[0] == n_heads
    seq = s_ref.shape[1]
    ck = seq // KEY_CHUNKS
    m, acc = {}, {}
    for h in range(n_heads + 2):
        m_parts = []
        for c in range(KEY_CHUNKS):
            keys = slice(c * ck, (c + 1) * ck)
            if h < n_heads:
                s = _dot_nt(k_of(h, keys), q_of(h))
                s_ref[h, keys, :] = s
                m_parts.append(jnp.max(s, axis=0, keepdims=True))
            if 1 <= h <= n_heads:
                _exp2_cols(m[h - 1], s_ref.at[h - 1, keys], e_ref.at[h - 1, keys])
            if h >= 2:
                o = _dot(vt_of(h - 2, keys), e_ref[h - 2, keys, :])
                acc[h - 2] = o if c == 0 else acc[h - 2] + o
        if h < n_heads:
            m[h] = functools.reduce(jnp.maximum, m_parts)
    outs = []
    for h in range(n_heads):
        dv = acc[h].shape[0] - VT_ROWS_PAD
        outs.append(acc[h][:dv] * (1.0 / acc[h][dv:dv + 1]))
    return jnp.concatenate(outs, axis=0)


def _even_proj_kernel(x_ref, w_in_ref, qn_ref, w_uq_ref, kvn_ref, w_uk_ref, w_uv_ref,
                      w_gate_ref, b_gate_ref, cq_ref, sq_ref,
                      q_ref, k_ref, v_ref, gq_ref, gk_ref, gv_ref, la_ref, gr_ref):
    h = _dot(x_ref[...].astype(BF16), w_in_ref[...])
    cos = cq_ref[...]
    sin = sq_ref[...]
    lane = lax.broadcasted_iota(jnp.int32, cos.shape, 1)
    pe_lane = (lane >= PE_LO) & (lane < PE_HI)

    def rope(t):
        return t * cos + _rot_half(t, MLA_ROPE // 2) * sin

    c_q = _rms(h[:, 0:256], qn_ref[...])
    q = _dot(c_q.astype(BF16), w_uq_ref[...]) * ((MLA_NOPE + MLA_ROPE) ** -0.5 * LOG2_E)
    c_kv = _rms(h[:, 256:384], kvn_ref[...]).astype(BF16)
    kn = _dot(c_kv, w_uk_ref[...])
    v_ref[...] = _dot(c_kv, w_uv_ref[...]).astype(v_ref.dtype)
    chunk = h[:, 384:512]
    k_pe = jnp.where(pe_lane, rope(chunk), 0.0)
    for hd in range(MLA_HEADS):
        sl = slice(hd * MLA_PAD, (hd + 1) * MLA_PAD)
        q_ref[:, sl] = rope(q[:, sl]).astype(q_ref.dtype)
        k_ref[:, sl] = (kn[:, sl] + k_pe).astype(k_ref.dtype)
    z = _dot(chunk.astype(BF16), w_gate_ref[...]) + b_gate_ref[...]
    la = _log_sigmoid(z) * (1.0 / GLA_GATE_NORM)
    blk = 2 * GLA_CHUNK
    r = lax.broadcasted_iota(jnp.int32, (blk, blk), 0)
    c = lax.broadcasted_iota(jnp.int32, (blk, blk), 1)
    same = (r // GLA_CHUNK) == (c // GLA_CHUNK)
    lower = jnp.where(same & (c <= r), 1.0, 0.0).astype(BF16)
    upper = jnp.where(same & (c >= r), 1.0, 0.0).astype(BF16)
    dkw = GLA_HEADS * GLA_DK
    for t in range(la.shape[0] // blk):
        rows = slice(t * blk, (t + 1) * blk)
        la_ref[rows, :dkw] = _dot_sel(lower, la[rows, :dkw])
        la_ref[rows, dkw:] = _dot_sel(upper, la[rows, dkw:])
    gq_ref[...] = h[:, 512:768] * (GLA_DK ** -0.5)
    gk_ref[...] = h[:, 768:1024]
    gv_ref[...] = h[:, 1024:1536].astype(gv_ref.dtype)
    gr = h[:, 1536:2048]
    gr_ref[...] = gr * _sigmoid(gr)


def _even_proj(x2, w_in_p, qn, w_uq_p, kvn, w_uk_p, w_uv, w_gate, b_gate, cq, sq, seq, tm):
    t, d = x2.shape
    nsb = seq // tm
    row = lambda i: (i, 0)
    fixed = lambda i: (0, 0)
    pos = lambda i: (i % nsb, 0)
    full = lambda a: pl.BlockSpec(a.shape, fixed)
    outs = [(1024, BF16), (1024, BF16), (512, BF16), (256, F32), (256, F32), (512, BF16), (512, F32), (512, F32)]
    return pl.pallas_call(
        _even_proj_kernel,
        grid=(t // tm,),
        in_specs=[pl.BlockSpec((tm, d), row), full(w_in_p), full(qn), full(w_uq_p), full(kvn),
                  full(w_uk_p), full(w_uv), full(w_gate), full(b_gate),
                  pl.BlockSpec((tm, MLA_PAD), pos), pl.BlockSpec((tm, MLA_PAD), pos)],
        out_specs=[pl.BlockSpec((tm, n), row) for n, _ in outs],
        out_shape=[jax.ShapeDtypeStruct((t, n), dt) for n, dt in outs],
        compiler_params=_params(("parallel",)),
        name="even_proj",
    )(x2, w_in_p, qn, w_uq_p, kvn, w_uk_p, w_uv, w_gate, b_gate, cq, sq)


MLA_STEP_HEADS = 4


def _mla_attn_kernel(q_ref, k_ref, v_ref, o_ref, vt_ref, s_ref, e_ref):
    out_w = MLA_STEP_HEADS * MLA_V

    @pl.when(pl.program_id(2) == 0)
    def _():
        r = lax.broadcasted_iota(jnp.int32, (out_w, out_w), 0)
        c = lax.broadcasted_iota(jnp.int32, (out_w, out_w), 1)
        eye = jnp.where(r == c, 1.0, 0.0).astype(BF16)
        v_t = _dot_nt(eye, v_ref[0])
        for a in range(MLA_STEP_HEADS):
            vt_ref[a] = _values_t(v_t[a * MLA_V:(a + 1) * MLA_V])

    out_t = _attend_heads(
        MLA_STEP_HEADS,
        lambda a, keys: k_ref[0, keys, a * MLA_PAD:(a + 1) * MLA_PAD],
        lambda a: q_ref[0, :, a * MLA_PAD:(a + 1) * MLA_PAD],
        lambda a, keys: vt_ref[a, :, keys],
        s_ref, e_ref)
    o_ref[0] = out_t.T.astype(o_ref.dtype)


def _mla_attn(q, k, v, tq):
    b, s, _ = q.shape
    n = MLA_STEP_HEADS
    return pl.pallas_call(
        _mla_attn_kernel,
        grid=(b, MLA_HEADS // n, s // tq),
        in_specs=[pl.BlockSpec((1, tq, n * MLA_PAD), lambda i, j, t: (i, t, j)),
                  pl.BlockSpec((1, s, n * MLA_PAD), lambda i, j, t: (i, 0, j)),
                  pl.BlockSpec((1, s, n * MLA_V), lambda i, j, t: (i, 0, j))],
        out_specs=pl.BlockSpec((1, tq, n * MLA_V), lambda i, j, t: (i, t, j)),
        out_shape=jax.ShapeDtypeStruct((b, s, MLA_HEADS * MLA_V), BF16),
        scratch_shapes=[pltpu.VMEM((n, MLA_V + VT_ROWS_PAD, s), BF16),
                        pltpu.VMEM((n, s, tq), F32), pltpu.VMEM((n, s, tq), BF16)],
        compiler_params=_params(("parallel", "parallel", "arbitrary")),
        name="mla_attn",
    )(q, k, v)


def _gla_kernel(q_ref, k_ref, cum_ref, v_ref, gr_ref, g_ref, o_ref, state_ref, of_ref, ob_ref):
    seq = q_ref.shape[1]
    n_chunks = seq // GLA_CHUNK
    L = GLA_CHUNK
    dkw = GLA_HEADS * GLA_DK
    dvw = GLA_HEADS * GLA_DV

    def iota(shape, dim):
        return lax.broadcasted_iota(jnp.int32, shape, dim)

    row_l = iota((L, dkw), 0)
    col_m = iota((L, dkw), 1) % L
    k_own = (iota((dkw, dkw), 0) // L) == (iota((dkw, dkw), 1) // GLA_DK)
    v_own = (iota((dkw, dvw), 0) // L) == (iota((dkw, dvw), 1) // GLA_DV)
    s_own = (iota((dvw, dkw), 0) // GLA_DV) == (iota((dvw, dkw), 1) // GLA_DK)
    state_ref[...] = jnp.zeros_like(state_ref)

    def body(i, carry):
        for d, fwd in enumerate((True, False)):
            n = i if fwd else n_chunks - 1 - i
            rows = pl.ds(pl.multiple_of(n * L, L), L)
            cum = cum_ref[0, rows, d * dkw:(d + 1) * dkw]
            last = cum[L - 1:L, :] if fwd else cum[0:1, :]
            q = q_ref[0, rows, :]
            k = k_ref[0, rows, :]
            v = v_ref[0, rows, :]
            qe = (q * jnp.exp(cum)).astype(BF16)
            kg = (k * jnp.exp(-cum)).astype(BF16)
            kdec = (k * jnp.exp(last - cum)).astype(BF16)
            k_blk = jnp.where(k_own, jnp.concatenate([kg] * GLA_HEADS, axis=0), jnp.zeros((), BF16))
            att = _dot_nt(qe, k_blk)
            keep = (col_m <= row_l) if fwd else (col_m >= row_l)
            att = jnp.where(keep, att, 0.0).astype(BF16)
            v_blk = jnp.where(v_own, jnp.concatenate([v] * GLA_HEADS, axis=0), jnp.zeros((), BF16))
            st = state_ref[d]
            o = _dot(att, v_blk) + _dot_nt(qe, st.astype(BF16))
            state_ref[d] = st * jnp.exp(last) + jnp.where(s_own, _dot_tn(v, kdec), 0.0)
            if fwd:
                of_ref[rows, :] = o
            else:
                ob_ref[rows, :] = o
        return carry

    lax.fori_loop(0, n_chunks, body, 0, unroll=4)

    def finish(n, carry):
        rows = pl.ds(pl.multiple_of(n * L, L), L)
        tot = of_ref[rows, :] + ob_ref[rows, :]
        for hd in range(GLA_HEADS):
            sl = slice(hd * GLA_DV, (hd + 1) * GLA_DV)
            o_ref[0, rows, sl] = (_rms(tot[:, sl], g_ref[...]) * gr_ref[0, rows, sl]).astype(o_ref.dtype)
        return carry

    lax.fori_loop(0, n_chunks, finish, 0)


def _gla(gq, gk, cum, gv, gr, g_norm):
    b, s, _ = gq.shape
    dkw = GLA_HEADS * GLA_DK
    dvw = GLA_HEADS * GLA_DV
    blk = lambda w: pl.BlockSpec((1, s, w), lambda i: (i, 0, 0))
    return pl.pallas_call(
        _gla_kernel,
        grid=(b,),
        in_specs=[blk(dkw), blk(dkw), blk(2 * dkw), blk(dvw), blk(dvw),
                  pl.BlockSpec((1, GLA_DV), lambda i: (0, 0))],
        out_specs=blk(dvw),
        out_shape=jax.ShapeDtypeStruct((b, s, dvw), BF16),
        scratch_shapes=[pltpu.VMEM((2, dvw, dkw), F32), pltpu.VMEM((s, dvw), F32), pltpu.VMEM((s, dvw), F32)],
        compiler_params=_params(("parallel",)),
        name="gla",
    )(gq, gk, cum, gv, gr, g_norm)


def _odd_proj_kernel(x_ref, w_ref, gq_ref, gk_ref, avg_ref, cos_ref, sin_ref, q_ref, k_ref, v_ref):
    h = _dot(x_ref[...].astype(BF16), w_ref[...])
    cos = cos_ref[...]
    sin = sin_ref[...]
    avg = avg_ref[...]
    nq = GQA_HEADS * GQA_HEAD_DIM
    nk = GQA_KV_HEADS * GQA_HEAD_DIM

    def norm_rope(t, g):
        ms = _dot_x_sel(t * t, avg)
        t = t * lax.rsqrt(ms + EPS) * g
        return t * cos + _rot_half(t, GQA_HEAD_DIM // 2) * sin

    for c in range(nq // nk):
        sl = slice(c * nk, (c + 1) * nk)
        q_ref[:, sl] = (norm_rope(h[:, sl], gq_ref[...]) * (GQA_HEAD_DIM ** -0.5 * LOG2_E)).astype(q_ref.dtype)
    k_ref[...] = norm_rope(h[:, nq:nq + nk], gk_ref[...]).astype(k_ref.dtype)
    v_ref[...] = h[:, nq + nk:].astype(v_ref.dtype)


def _odd_proj(x2, w, gq, gk, avg, cos, sin, seq, tm):
    t, d = x2.shape
    nsb = seq // tm
    row = lambda i: (i, 0)
    fixed = lambda i: (0, 0)
    pos = lambda i: (i % nsb, 0)
    full = lambda a: pl.BlockSpec(a.shape, fixed)
    outs = [ODD_SPLITS[0], ODD_SPLITS[1], ODD_SPLITS[2]]
    return pl.pallas_call(
        _odd_proj_kernel,
        grid=(t // tm,),
        in_specs=[pl.BlockSpec((tm, d), row), full(w), full(gq), full(gk), full(avg),
                  pl.BlockSpec((tm, cos.shape[1]), pos), pl.BlockSpec((tm, sin.shape[1]), pos)],
        out_specs=[pl.BlockSpec((tm, n), row) for n in outs],
        out_shape=[jax.ShapeDtypeStruct((t, n), BF16) for n in outs],
        compiler_params=_params(("parallel",)),
        name="odd_proj",
    )(x2, w, gq, gk, avg, cos, sin)


def _gqa_attn_kernel(q_ref, k_ref, v_ref, o_ref, kp_ref, vt_ref, s_ref, e_ref):
    j = pl.program_id(1)
    group = GQA_HEADS // GQA_KV_HEADS
    width = GQA_KV_HEADS * GQA_HEAD_DIM
    pair = 2 * GQA_HEAD_DIM

    @pl.when(pl.program_id(2) == 0)
    def _():
        r = lax.broadcasted_iota(jnp.int32, (width, pair), 0)
        c = lax.broadcasted_iota(jnp.int32, (width, pair), 1)
        twice = (r // GQA_HEAD_DIM == j) & (r % GQA_HEAD_DIM == c % GQA_HEAD_DIM)
        kp_ref[...] = _dot(k_ref[0], jnp.where(twice, 1.0, 0.0).astype(BF16)).astype(BF16)
        rv = lax.broadcasted_iota(jnp.int32, (GQA_HEAD_DIM, width), 0)
        cv = lax.broadcasted_iota(jnp.int32, (GQA_HEAD_DIM, width), 1)
        mine = cv == j * GQA_HEAD_DIM + rv
        vt_ref[...] = _values_t(_dot_nt(jnp.where(mine, 1.0, 0.0).astype(BF16), v_ref[0]))

    lane = lax.broadcasted_iota(jnp.int32, (q_ref.shape[1], pair), 1)

    def q_of(g):
        qp = q_ref[0, :, (g // 2) * pair:(g // 2 + 1) * pair]
        return jnp.where(lane // GQA_HEAD_DIM == g % 2, qp, jnp.zeros_like(qp))

    out_t = _attend_heads(group, lambda g, keys: kp_ref[keys, :], q_of, lambda g, keys: vt_ref[:, keys],
                          s_ref, e_ref)
    o_ref[0] = out_t.T.astype(o_ref.dtype)


def _gqa_attn(q, k, v, tq):
    b, s, _ = q.shape
    width = GQA_KV_HEADS * GQA_HEAD_DIM
    group = GQA_HEADS // GQA_KV_HEADS
    return pl.pallas_call(
        _gqa_attn_kernel,
        grid=(b, GQA_KV_HEADS, s // tq),
        in_specs=[pl.BlockSpec((1, tq, width), lambda i, j, t: (i, t, j)),
                  pl.BlockSpec((1, s, width), lambda i, j, t: (i, 0, 0)),
                  pl.BlockSpec((1, s, width), lambda i, j, t: (i, 0, 0))],
        out_specs=pl.BlockSpec((1, tq, width), lambda i, j, t: (i, t, j)),
        out_shape=jax.ShapeDtypeStruct((b, s, GQA_HEADS * GQA_HEAD_DIM), BF16),
        scratch_shapes=[pltpu.VMEM((s, 2 * GQA_HEAD_DIM), BF16),
                        pltpu.VMEM((GQA_HEAD_DIM + VT_ROWS_PAD, s), BF16),
                        pltpu.VMEM((group, s, tq), F32), pltpu.VMEM((group, s, tq), BF16)],
        compiler_params=_params(("parallel", "parallel", "arbitrary")),
        name="gqa_attn",
    )(q, k, v)


def _out_ln_kernel(alpha, x_ref, ma_ref, mb_ref, wa_ref, wb_ref, g_ref, b_ref, y_ref, yt_ref):
    y = alpha * x_ref[...] + _dot(ma_ref[...].astype(BF16), wa_ref[...]) \
        + _dot(mb_ref[...].astype(BF16), wb_ref[...])
    y = _layer_norm(y, g_ref[...], b_ref[...])
    y_ref[...] = y
    yt_ref[0] = y.T.astype(BF16)


def _out_ln(x2, mix_a, mix_b, col_a, col_b, w_a, w_b, g, b, alpha, seq, tm):
    t, d = x2.shape
    half = w_a.shape[0]
    nsb = seq // tm
    row = lambda i: (i, 0)
    fixed = lambda i: (0, 0)
    return pl.pallas_call(
        functools.partial(_out_ln_kernel, alpha),
        grid=(t // tm,),
        in_specs=[pl.BlockSpec((tm, d), row),
                  pl.BlockSpec((tm, half), lambda i: (i, col_a)),
                  pl.BlockSpec((tm, half), lambda i: (i, col_b)),
                  pl.BlockSpec(w_a.shape, fixed), pl.BlockSpec(w_b.shape, fixed),
                  pl.BlockSpec(g.shape, fixed), pl.BlockSpec(b.shape, fixed)],
        out_specs=[pl.BlockSpec((tm, d), row),
                   pl.BlockSpec((1, d, tm), lambda i: (i // nsb, 0, i % nsb))],
        out_shape=[jax.ShapeDtypeStruct((t, d), F32), jax.ShapeDtypeStruct((t // seq, d, seq), BF16)],
        compiler_params=_params(("parallel",)),
        name="out_ln",
    )(x2, mix_a, mix_b, w_a, w_b, g, b)


def _route_kernel(cap, x_ref, rw_ref, aff_ref, slot_ref):
    x = x_ref[0]
    seq = x.shape[0]
    xh, xm = _split2(x)
    wh, wm = _split2(rw_ref[...])
    logits = _dot_nt(wh, xh) + (_dot_nt(wh, xm) + _dot_nt(wm, xh))
    e = jnp.exp(logits - jnp.max(logits, axis=0, keepdims=True))
    aff = e / jnp.sum(e, axis=0, keepdims=True)
    aff_ref[0] = aff
    bits = pltpu.bitcast(aff, jnp.int32)

    def enough(cand):
        return jnp.sum((bits >= cand).astype(jnp.int32), axis=1, keepdims=True) >= cap

    top = jnp.full((bits.shape[0], 1), 1 << 30, jnp.int32)
    thr0 = jnp.where(enough(top), top, 0)

    def pick(i, thr):
        lo = 28 - 2 * i
        c1, c2, c3 = thr | (jnp.int32(1) << lo), thr | (jnp.int32(2) << lo), thr | (jnp.int32(3) << lo)
        return jnp.where(enough(c3), c3, jnp.where(enough(c2), c2, jnp.where(enough(c1), c1, thr)))

    thr = lax.fori_loop(0, 15, pick, thr0)
    above = bits > thr
    tie = bits == thr
    need = cap - jnp.sum(above.astype(jnp.int32), axis=1, keepdims=True)

    blk = 256 if seq % 256 == 0 else V7X_LANES
    r = lax.broadcasted_iota(jnp.int32, (blk, blk), 0)
    c = lax.broadcasted_iota(jnp.int32, (blk, blk), 1)
    before = jnp.where(r < c, 1.0, 0.0).astype(BF16)

    def prefix(mask):
        m = jnp.where(mask, 1.0, 0.0).astype(BF16)
        run = jnp.zeros((mask.shape[0], 1), F32)
        parts = []
        for t in range(seq // blk):
            mb = m[:, t * blk:(t + 1) * blk]
            parts.append(_dot(mb, before) + run)
            run = run + jnp.sum(mb.astype(F32), axis=1, keepdims=True)
        return jnp.concatenate(parts, axis=1).astype(jnp.int32)

    chosen = above | (tie & (prefix(tie) < need))
    slot_ref[0] = jnp.where(chosen, prefix(chosen), -1)


def _route(x1, rw_t, cap):
    b, s, d = x1.shape
    e = rw_t.shape[0]
    return pl.pallas_call(
        functools.partial(_route_kernel, cap),
        grid=(b,),
        in_specs=[pl.BlockSpec((1, s, d), lambda i: (i, 0, 0)), pl.BlockSpec((e, d), lambda i: (0, 0))],
        out_specs=[pl.BlockSpec((1, e, s), lambda i: (i, 0, 0)), pl.BlockSpec((1, e, s), lambda i: (i, 0, 0))],
        out_shape=[jax.ShapeDtypeStruct((b, e, s), F32), jax.ShapeDtypeStruct((b, e, s), jnp.int32)],
        compiler_params=_params(("parallel",)),
        name="route",
    )(x1, rw_t)


def _moe_kernel(cap, ts, xt_ref, slot_ref, aff_ref, w1_ref, w3_ref, w2_ref, o_ref):
    e_id = pl.program_id(1)
    seq = xt_ref.shape[2]
    ff = w2_ref.shape[2]

    @pl.when(e_id == 0)
    def _():
        o_ref[...] = jnp.zeros_like(o_ref)

    c_row = lax.broadcasted_iota(jnp.int32, (cap, seq), 0)
    pick = jnp.where(slot_ref[0, 0] == c_row, 1.0, 0.0).astype(BF16)
    hd2 = xt_ref.shape[1] // 2
    xg = jnp.concatenate([_dot_nt(xt_ref[0, :hd2, :], pick), _dot_nt(xt_ref[0, hd2:, :], pick)],
                         axis=0).astype(BF16)
    h1 = _dot(w1_ref[0], xg)
    h3 = _dot(w3_ref[0], xg)
    hid = (h1 * _sigmoid(h1) * h3).astype(BF16)
    ye = jnp.concatenate([_dot(w2_ref[0, :hd2, :], hid), _dot(w2_ref[0, hd2:, :], hid)],
                         axis=0).astype(BF16)
    for t in range(seq // ts):
        cols = slice(t * ts, (t + 1) * ts)
        o_ref[0, :, cols] += _dot(ye, pick[:, cols]) * aff_ref[0, 0, :, cols]


def _moe(xt, slot, aff, w1t, w3t, w2t, cap, layer):
    b, d, s = xt.shape
    e = slot.shape[1]
    ff = w2t.shape[2]
    ts = min(SCATTER_BLOCK, s)
    w_blk = lambda i, j: (layer * e + j, 0, 0)
    return pl.pallas_call(
        functools.partial(_moe_kernel, cap, ts),
        grid=(b, e),
        in_specs=[pl.BlockSpec((1, d, s), lambda i, j: (i, 0, 0)),
                  pl.BlockSpec((1, 1, 1, s), lambda i, j: (i, j, 0, 0)),
                  pl.BlockSpec((1, 1, 1, s), lambda i, j: (i, j, 0, 0)),
                  pl.BlockSpec((1, ff, d), w_blk), pl.BlockSpec((1, ff, d), w_blk),
                  pl.BlockSpec((1, d, ff), w_blk)],
        out_specs=pl.BlockSpec((1, d, s), lambda i, j: (i, 0, 0)),
        out_shape=jax.ShapeDtypeStruct((b, d, s), F32),
        compiler_params=_params(("parallel", "arbitrary")),
        name="moe",
    )(xt, slot.reshape(b, e, 1, s), aff.reshape(b, e, 1, s), w1t, w3t, w2t)


def _transpose_cast_kernel(w_ref, o_ref):
    o_ref[0] = w_ref[0].T.astype(o_ref.dtype)


def _transpose_cast(w):
    n, r, c = w.shape
    return pl.pallas_call(
        _transpose_cast_kernel,
        grid=(n,),
        in_specs=[pl.BlockSpec((1, r, c), lambda i: (i, 0, 0))],
        out_specs=pl.BlockSpec((1, c, r), lambda i: (i, 0, 0)),
        out_shape=jax.ShapeDtypeStruct((n, c, r), BF16),
        compiler_params=_params(("parallel",)),
        name="transpose_cast",
    )(w)


def _ple_ln_kernel(alpha, x_ref, f_ref, p_ref, wg_ref, bg_ref, wp_ref, g_ref, b_ref, y_ref):
    x = x_ref[...]
    gate = _sigmoid(_dot(x.astype(BF16), wg_ref[...]) + bg_ref[...])
    ple = gate * _dot(p_ref[...].astype(BF16), wp_ref[...])
    ffn = f_ref[0].T
    y_ref[...] = _layer_norm(alpha * x + ffn + ple, g_ref[...], b_ref[...])


def _ple_ln(x1, ffn_t, p2, wg, bg, wp, g, b, alpha, tm, layer):
    t, d = x1.shape
    nsb = ffn_t.shape[2] // tm
    row = lambda i: (i, 0)
    p_row = lambda i: (layer * (t // tm) + i, 0)
    fixed = lambda i: (0, 0)
    full = lambda a: pl.BlockSpec(a.shape, fixed)
    return pl.pallas_call(
        functools.partial(_ple_ln_kernel, alpha),
        grid=(t // tm,),
        in_specs=[pl.BlockSpec((tm, d), row),
                  pl.BlockSpec((1, d, tm), lambda i: (i // nsb, 0, i % nsb)),
                  pl.BlockSpec((tm, p2.shape[1]), p_row), full(wg), full(bg), full(wp), full(g), full(b)],
        out_specs=pl.BlockSpec((tm, d), row),
        out_shape=jax.ShapeDtypeStruct((t, d), F32),
        compiler_params=_params(("parallel",)),
        name="ple_ln",
    )(x1, ffn_t, p2, wg, bg, wp, g, b)


def _rope_tables(seq, rot_dim, lo, width):
    rows = seq // GRID_W
    row = jnp.repeat(jnp.arange(rows, dtype=F32), GRID_W)
    col = jnp.tile(jnp.arange(GRID_W, dtype=F32), rows)
    axis_dim = rot_dim // 2
    inv = ROPE_THETA ** (-jnp.arange(0, axis_dim, 2, dtype=F32) / axis_dim)
    ang = jnp.concatenate([row[:, None] * inv, col[:, None] * inv], axis=-1)
    cos, sin = jnp.cos(ang), jnp.sin(ang)
    cos2 = jnp.concatenate([cos, cos], axis=-1)
    sin2 = jnp.concatenate([-sin, sin], axis=-1)
    if lo == 0:
        reps = width // rot_dim
        return jnp.tile(cos2, (1, reps)), jnp.tile(sin2, (1, reps))
    pad_l = jnp.ones((seq, lo), F32)
    pad_r = jnp.ones((seq, width - lo - rot_dim), F32)
    cos_t = jnp.concatenate([pad_l, cos2, pad_r], axis=-1)
    sin_t = jnp.concatenate([0 * pad_l, sin2, 0 * pad_r], axis=-1)
    return cos_t, sin_t


def _prep_even(w_in, w_uq, w_ukv, gw_f, gb_f, gw_b, gb_b):
    d = w_in.shape[0]
    offs = np.cumsum(EVEN_SPLITS)[:-1].tolist()
    c_q, c_kv, k_pe, gq, gk, gv, g_lr, gr = jnp.split(w_in, offs, axis=-1)
    z32 = jnp.zeros((d, 32), w_in.dtype)
    chunk = jnp.concatenate([g_lr, z32, k_pe, z32], axis=-1)
    w_in_p = jnp.concatenate([c_q, c_kv, chunk, gq, gk, gv, gr], axis=-1).astype(BF16)
    uq = w_uq.reshape(MLA_Q_LORA, MLA_HEADS, MLA_NOPE + MLA_ROPE)
    uq = jnp.pad(uq, ((0, 0), (0, 0), (0, MLA_PAD - MLA_NOPE - MLA_ROPE)))
    w_uq_p = uq.reshape(MLA_Q_LORA, MLA_HEADS * MLA_PAD).astype(BF16)
    ukv = w_ukv.reshape(MLA_KV_LORA, MLA_HEADS, MLA_NOPE + MLA_V)
    uk = jnp.pad(ukv[:, :, :MLA_NOPE], ((0, 0), (0, 0), (0, MLA_PAD - MLA_NOPE)))
    w_uk_p = uk.reshape(MLA_KV_LORA, MLA_HEADS * MLA_PAD).astype(BF16)
    w_uv = ukv[:, :, MLA_NOPE:].reshape(MLA_KV_LORA, MLA_HEADS * MLA_V).astype(BF16)
    dkw = GLA_HEADS * GLA_DK
    w_gate = jnp.zeros((MLA_PAD, 2 * dkw), F32)
    w_gate = w_gate.at[0:GLA_GATE_RANK, 0:dkw].set(gw_f)
    w_gate = w_gate.at[GLA_GATE_RANK:2 * GLA_GATE_RANK, dkw:].set(gw_b).astype(BF16)
    b_gate = jnp.concatenate([gb_f, gb_b])[None, :]
    return w_in_p, w_uq_p, w_uk_p, w_uv, w_gate, b_gate


def kernel(x, p, w_in_even, mla_q_norm, w_uq, mla_kv_norm, w_ukv, gla_gate_w_fwd, gla_gate_b_fwd,
           gla_gate_w_bwd, gla_gate_b_bwd, gla_norm, w_in_odd, gqa_q_norm, gqa_k_norm, w_o, ln1_g,
           ln1_b, router_w, w1, w3, w2, ple_gate_w, ple_gate_b, ple_w, ln2_g, ln2_b):
    b, s, d = x.shape
    depth = w_o.shape[0]
    t = b * s
    alpha = (2.0 * depth) ** 0.25
    cap = EC_CAPACITY_FACTOR * s // N_EXPERTS
    tm = min(ROW_BLOCK, s)
    tq = min(QUERY_BLOCK, s)
    half = w_o.shape[1] // 2
    cos_a, sin_a = _rope_tables(s, MLA_ROPE, PE_LO, MLA_PAD)
    cos_c, sin_c = _rope_tables(s, GQA_HEAD_DIM, 0, GQA_KV_HEADS * GQA_HEAD_DIM)
    hw = GQA_KV_HEADS * GQA_HEAD_DIM
    head_of = np.arange(hw) // GQA_HEAD_DIM
    avg = jnp.asarray((head_of[:, None] == head_of[None, :]) / GQA_HEAD_DIM, BF16)

    n_e, ff = w1.shape[1], w1.shape[3]
    w1t = _transpose_cast(w1.reshape(depth * n_e, d, ff))
    w3t = _transpose_cast(w3.reshape(depth * n_e, d, ff))
    w2t = _transpose_cast(w2.reshape(depth * n_e, ff, d))
    p_all = p.reshape(depth * t, p.shape[-1])

    x2 = x.reshape(t, d)
    for i in range(depth):
        j = i // 2
        if i % 2 == 0:
            w_in_p, w_uq_p, w_uk_p, w_uv, w_gate, b_gate = _prep_even(
                w_in_even[j], w_uq[j], w_ukv[j], gla_gate_w_fwd[j], gla_gate_b_fwd[j],
                gla_gate_w_bwd[j], gla_gate_b_bwd[j])
            q, k, v, gq, gk, gv, la, gr = _even_proj(
                x2, w_in_p, mla_q_norm[j][None, :], w_uq_p, mla_kv_norm[j][None, :], w_uk_p, w_uv,
                w_gate, b_gate, cos_a, sin_a, s, tm)
            r3 = lambda a: a.reshape(b, s, a.shape[-1])
            o_mla = _mla_attn(r3(q), r3(k), r3(v), tq).reshape(t, -1)
            o_gla = _gla(r3(gq), r3(gk), r3(la), r3(gv), r3(gr), gla_norm[j][None, :]).reshape(t, -1)
            mix_a, mix_b, col_a, col_b = o_mla, o_gla, 0, 0
        else:
            q, k, v = _odd_proj(x2, w_in_odd[j].astype(BF16),
                                jnp.tile(gqa_q_norm[j], GQA_KV_HEADS)[None, :],
                                jnp.tile(gqa_k_norm[j], GQA_KV_HEADS)[None, :],
                                avg, cos_c, sin_c, s, tm)
            r3 = lambda a: a.reshape(b, s, a.shape[-1])
            o = _gqa_attn(r3(q), r3(k), r3(v), tq).reshape(t, -1)
            mix_a, mix_b, col_a, col_b = o, o, 0, 1
        wo = w_o[i].astype(BF16)
        x1, x1t = _out_ln(x2, mix_a, mix_b, col_a, col_b, wo[:half], wo[half:],
                          ln1_g[i][None, :], ln1_b[i][None, :], alpha, s, tm)
        aff, slot = _route(x1.reshape(b, s, d), router_w[i].T, cap)
        ffn_t = _moe(x1t, slot, aff, w1t, w3t, w2t, cap, i)
        x2 = _ple_ln(x1, ffn_t, p_all, ple_gate_w[i].astype(BF16),
                     ple_gate_b[i][None, :], ple_w[i].astype(BF16), ln2_g[i][None, :], ln2_b[i][None, :],
                     alpha, tm, i)
    return x2.reshape(b, s, d)
```

```python
import functools
import math

import jax
import jax.numpy as jnp
import numpy as np
from jax import lax
from jax.experimental import pallas as pl
from jax.experimental.pallas import tpu as pltpu

F32 = jnp.float32
BF16 = jnp.bfloat16

V7X_LANES = 128
V7X_VMEM_BYTES = 64 * 1024 * 1024
VMEM_LIMIT = V7X_VMEM_BYTES - 8 * 1024 * 1024

ROW_BLOCK = 1024
QUERY_BLOCK = 512
SCATTER_BLOCK = 512

GRID_W = 64
ROPE_THETA = 10000.0
EPS = 1e-6
MLA_HEADS, MLA_Q_LORA, MLA_KV_LORA = 8, 256, 128
MLA_NOPE, MLA_ROPE, MLA_V = 64, 32, 64
GLA_HEADS, GLA_DK, GLA_DV = 4, 64, 128
GLA_GATE_RANK, GLA_GATE_NORM, GLA_CHUNK = 16, 16.0, 64
GQA_HEADS, GQA_KV_HEADS, GQA_HEAD_DIM = 16, 4, 64
N_EXPERTS, EC_CAPACITY_FACTOR = 16, 2
EVEN_SPLITS = (MLA_Q_LORA, MLA_KV_LORA, MLA_ROPE, GLA_HEADS * GLA_DK, GLA_HEADS * GLA_DK,
               GLA_HEADS * GLA_DV, 2 * GLA_GATE_RANK, GLA_HEADS * GLA_DV)
ODD_SPLITS = (GQA_HEADS * GQA_HEAD_DIM, GQA_KV_HEADS * GQA_HEAD_DIM, GQA_KV_HEADS * GQA_HEAD_DIM)
MLA_PAD = V7X_LANES
PE_LO, PE_HI = MLA_NOPE, MLA_NOPE + MLA_ROPE


def _dot(a, b):
    return jnp.dot(a, b, preferred_element_type=F32)


def _dot_nt(a, b):
    return lax.dot_general(a, b, (((1,), (1,)), ((), ())), preferred_element_type=F32)


def _dot_tn(a, b):
    return lax.dot_general(a, b, (((0,), (0,)), ((), ())), preferred_element_type=F32)


def _split2(x):
    hi = x.astype(BF16)
    lo = (x - hi.astype(F32)).astype(BF16)
    return hi, lo


def _dot_sel(sel, x):
    hi, lo = _split2(x)
    return _dot(sel, hi) + _dot(sel, lo)


def _dot_x_sel(x, sel):
    hi, lo = _split2(x)
    return _dot(hi, sel) + _dot(lo, sel)


def _params(sem):
    return pltpu.CompilerParams(dimension_semantics=sem, vmem_limit_bytes=VMEM_LIMIT)


def _rot_half(x, half):
    w = x.shape[-1]
    lane = lax.broadcasted_iota(jnp.int32, x.shape, x.ndim - 1)
    first = (lane % (2 * half)) < half
    return jnp.where(first, pltpu.roll(x, w - half, x.ndim - 1), pltpu.roll(x, half, x.ndim - 1))


def _layer_norm(y, g, b):
    mu = jnp.mean(y, axis=-1, keepdims=True)
    yc = y - mu
    var = jnp.mean(yc * yc, axis=-1, keepdims=True)
    return yc * lax.rsqrt(var + EPS) * g + b


def _rms(x, g):
    return x * lax.rsqrt(jnp.mean(x * x, axis=-1, keepdims=True) + EPS) * g


def _log_sigmoid(z):
    return jnp.minimum(z, 0.0) - jnp.log1p(jnp.exp(-jnp.abs(z)))


def _sigmoid(z):
    return 1.0 / (1.0 + jnp.exp(-z))


LOG2_E = math.log2(math.e)
KEY_BLK = 128


KEY_CHUNKS = 4


def _exp2_cols(m, s_scr, e_scr):
    seq, tq = s_scr.shape
    for c in range(tq // V7X_LANES):
        cols = slice(c * V7X_LANES, (c + 1) * V7X_LANES)
        m_b = jnp.broadcast_to(m[:, cols], (KEY_BLK, V7X_LANES))
        for r in range(seq // KEY_BLK):
            rows = slice(r * KEY_BLK, (r + 1) * KEY_BLK)
            e_scr[rows, cols] = jnp.exp2(s_scr[rows, cols] - m_b).astype(BF16)


VT_ROWS_PAD = 16


def _values_t(v_t):
    dv, seq = v_t.shape
    extra = jnp.where(lax.broadcasted_iota(jnp.int32, (VT_ROWS_PAD, seq), 0) == 0, 1.0, 0.0)
    return jnp.concatenate([v_t.astype(BF16), extra.astype(BF16)], axis=0)


def _attend_heads(n_heads, k_of, q_of, vt_of, s_ref, e_ref):
    assert s_ref.shape[0] == n_heads and e_ref.shape[0] == n_heads
    seq = s_ref.shape[1]
    ck = seq // KEY_CHUNKS
    m, acc = {}, {}
    for h in range(n_heads + 2):
        m_parts = []
        for c in range(KEY_CHUNKS):
            keys = slice(c * ck, (c + 1) * ck)
            if h < n_heads:
                s = _dot_nt(k_of(h, keys), q_of(h))
                s_ref[h, keys, :] = s
                m_parts.append(jnp.max(s, axis=0, keepdims=True))
            if 1 <= h <= n_heads:
                _exp2_cols(m[h - 1], s_ref.at[h - 1, keys], e_ref.at[h - 1, keys])
            if h >= 2:
                o = _dot(vt_of(h - 2, keys), e_ref[h - 2, keys, :])
                acc[h - 2] = o if c == 0 else acc[h - 2] + o
        if h < n_heads:
            m[h] = functools.reduce(jnp.maximum, m_parts)
    outs = []
    for h in range(n_heads):
        dv = acc[h].shape[0] - VT_ROWS_PAD
        outs.append(acc[h][:dv] * (1.0 / acc[h][dv:dv + 1]))
    return jnp.concatenate(outs, axis=0)


def _even_proj_kernel(x_ref, w_in_ref, qn_ref, w_uq_ref, kvn_ref, w_uk_ref, w_uv_ref,
                      w_gate_ref, b_gate_ref, cq_ref, sq_ref,
                      q_ref, k_ref, v_ref, gq_ref, gk_ref, gv_ref, la_ref, gr_ref):
    h = _dot(x_ref[...].astype(BF16), w_in_ref[...])
    cos = cq_ref[...]
    sin = sq_ref[...]
    lane = lax.broadcasted_iota(jnp.int32, cos.shape, 1)
    pe_lane = (lane >= PE_LO) & (lane < PE_HI)

    def rope(t):
        return t * cos + _rot_half(t, MLA_ROPE // 2) * sin

    c_q = _rms(h[:, 0:256], qn_ref[...])
    q = _dot(c_q.astype(BF16), w_uq_ref[...]) * ((MLA_NOPE + MLA_ROPE) ** -0.5 * LOG2_E)
    c_kv = _rms(h[:, 256:384], kvn_ref[...]).astype(BF16)
    kn = _dot(c_kv, w_uk_ref[...])
    v_ref[...] = _dot(c_kv, w_uv_ref[...]).astype(v_ref.dtype)
    chunk = h[:, 384:512]
    k_pe = jnp.where(pe_lane, rope(chunk), 0.0)
    for hd in range(MLA_HEADS):
        sl = slice(hd * MLA_PAD, (hd + 1) * MLA_PAD)
        q_ref[:, sl] = rope(q[:, sl]).astype(q_ref.dtype)
        k_ref[:, sl] = (kn[:, sl] + k_pe).astype(k_ref.dtype)
    z = _dot(chunk.astype(BF16), w_gate_ref[...]) + b_gate_ref[...]
    la = _log_sigmoid(z) * (1.0 / GLA_GATE_NORM)
    blk = 2 * GLA_CHUNK
    r = lax.broadcasted_iota(jnp.int32, (blk, blk), 0)
    c = lax.broadcasted_iota(jnp.int32, (blk, blk), 1)
    same = (r // GLA_CHUNK) == (c // GLA_CHUNK)
    lower = jnp.where(same & (c <= r), 1.0, 0.0).astype(BF16)
    upper = jnp.where(same & (c >= r), 1.0, 0.0).astype(BF16)
    dkw = GLA_HEADS * GLA_DK
    for t in range(la.shape[0] // blk):
        rows = slice(t * blk, (t + 1) * blk)
        la_ref[rows, :dkw] = _dot_sel(lower, la[rows, :dkw])
        la_ref[rows, dkw:] = _dot_sel(upper, la[rows, dkw:])
    gq_ref[...] = h[:, 512:768] * (GLA_DK ** -0.5)
    gk_ref[...] = h[:, 768:1024]
    gv_ref[...] = h[:, 1024:1536].astype(gv_ref.dtype)
    gr = h[:, 1536:2048]
    gr_ref[...] = gr * _sigmoid(gr)


def _even_proj(x2, w_in_p, qn, w_uq_p, kvn, w_uk_p, w_uv, w_gate, b_gate, cq, sq, seq, tm):
    t, d = x2.shape
    nsb = seq // tm
    row = lambda i: (i, 0)
    fixed = lambda i: (0, 0)
    pos = lambda i: (i % nsb, 0)
    full = lambda a: pl.BlockSpec(a.shape, fixed)
    outs = [(1024, BF16), (1024, BF16), (512, BF16), (256, F32), (256, F32), (512, BF16), (512, F32), (512, F32)]
    return pl.pallas_call(
        _even_proj_kernel,
        grid=(t // tm,),
        in_specs=[pl.BlockSpec((tm, d), row), full(w_in_p), full(qn), full(w_uq_p), full(kvn),
                  full(w_uk_p), full(w_uv), full(w_gate), full(b_gate),
                  pl.BlockSpec((tm, MLA_PAD), pos), pl.BlockSpec((tm, MLA_PAD), pos)],
        out_specs=[pl.BlockSpec((tm, n), row) for n, _ in outs],
        out_shape=[jax.ShapeDtypeStruct((t, n), dt) for n, dt in outs],
        compiler_params=_params(("parallel",)),
        name="even_proj",
    )(x2, w_in_p, qn, w_uq_p, kvn, w_uk_p, w_uv, w_gate, b_gate, cq, sq)


MLA_STEP_HEADS = 4


def _mla_attn_kernel(q_ref, k_ref, v_ref, o_ref, vt_ref, s_ref, e_ref):
    out_w = MLA_STEP_HEADS * MLA_V

    @pl.when(pl.program_id(2) == 0)
    def _():
        r = lax.broadcasted_iota(jnp.int32, (out_w, out_w), 0)
        c = lax.broadcasted_iota(jnp.int32, (out_w, out_w), 1)
        eye = jnp.where(r == c, 1.0, 0.0).astype(BF16)
        v_t = _dot_nt(eye, v_ref[0])
        for a in range(MLA_STEP_HEADS):
            vt_ref[a] = _values_t(v_t[a * MLA_V:(a + 1) * MLA_V])

    out_t = _attend_heads(
        MLA_STEP_HEADS,
        lambda a, keys: k_ref[0, keys, a * MLA_PAD:(a + 1) * MLA_PAD],
        lambda a: q_ref[0, :, a * MLA_PAD:(a + 1) * MLA_PAD],
        lambda a, keys: vt_ref[a, :, keys],
        s_ref, e_ref)
    o_ref[0] = out_t.T.astype(o_ref.dtype)


def _mla_attn(q, k, v, tq):
    b, s, _ = q.shape
    n = MLA_STEP_HEADS
    return pl.pallas_call(
        _mla_attn_kernel,
        grid=(b, MLA_HEADS // n, s // tq),
        in_specs=[pl.BlockSpec((1, tq, n * MLA_PAD), lambda i, j, t: (i, t, j)),
                  pl.BlockSpec((1, s, n * MLA_PAD), lambda i, j, t: (i, 0, j)),
                  pl.BlockSpec((1, s, n * MLA_V), lambda i, j, t: (i, 0, j))],
        out_specs=pl.BlockSpec((1, tq, n * MLA_V), lambda i, j, t: (i, t, j)),
        out_shape=jax.ShapeDtypeStruct((b, s, MLA_HEADS * MLA_V), BF16),
        scratch_shapes=[pltpu.VMEM((n, MLA_V + VT_ROWS_PAD, s), BF16),
                        pltpu.VMEM((n, s, tq), F32), pltpu.VMEM((n, s, tq), BF16)],
        compiler_params=_params(("parallel", "parallel", "arbitrary")),
        name="mla_attn",
    )(q, k, v)


def _gla_kernel(q_ref, k_ref, cum_ref, v_ref, gr_ref, g_ref, o_ref, state_ref, of_ref, ob_ref):
    seq = q_ref.shape[1]
    n_chunks = seq // GLA_CHUNK
    L = GLA_CHUNK
    dkw = GLA_HEADS * GLA_DK
    dvw = GLA_HEADS * GLA_DV

    def iota(shape, dim):
        return lax.broadcasted_iota(jnp.int32, shape, dim)

    row_l = iota((L, dkw), 0)
    col_m = iota((L, dkw), 1) % L
    k_own = (iota((dkw, dkw), 0) // L) == (iota((dkw, dkw), 1) // GLA_DK)
    v_own = (iota((dkw, dvw), 0) // L) == (iota((dkw, dvw), 1) // GLA_DV)
    s_own = (iota((dvw, dkw), 0) // GLA_DV) == (iota((dvw, dkw), 1) // GLA_DK)
    state_ref[...] = jnp.zeros_like(state_ref)

    def body(i, carry):
        for d, fwd in enumerate((True, False)):
            n = i if fwd else n_chunks - 1 - i
            rows = pl.ds(pl.multiple_of(n * L, L), L)
            cum = cum_ref[0, rows, d * dkw:(d + 1) * dkw]
            last = cum[L - 1:L, :] if fwd else cum[0:1, :]
            q = q_ref[0, rows, :]
            k = k_ref[0, rows, :]
            v = v_ref[0, rows, :]
            qe = (q * jnp.exp(cum)).astype(BF16)
            kg = (k * jnp.exp(-cum)).astype(BF16)
            kdec = (k * jnp.exp(last - cum)).astype(BF16)
            k_blk = jnp.where(k_own, jnp.concatenate([kg] * GLA_HEADS, axis=0), jnp.zeros((), BF16))
            att = _dot_nt(qe, k_blk)
            keep = (col_m <= row_l) if fwd else (col_m >= row_l)
            att = jnp.where(keep, att, 0.0).astype(BF16)
            v_blk = jnp.where(v_own, jnp.concatenate([v] * GLA_HEADS, axis=0), jnp.zeros((), BF16))
            st = state_ref[d]
            o = _dot(att, v_blk) + _dot_nt(qe, st.astype(BF16))
            state_ref[d] = st * jnp.exp(last) + jnp.where(s_own, _dot_tn(v, kdec), 0.0)
            if fwd:
                of_ref[rows, :] = o
            else:
                ob_ref[rows, :] = o
        return carry

    lax.fori_loop(0, n_chunks, body, 0, unroll=4)

    def finish(n, carry):
        rows = pl.ds(pl.multiple_of(n * L, L), L)
        tot = of_ref[rows, :] + ob_ref[rows, :]
        for hd in range(GLA_HEADS):
            sl = slice(hd * GLA_DV, (hd + 1) * GLA_DV)
            o_ref[0, rows, sl] = (_rms(tot[:, sl], g_ref[...]) * gr_ref[0, rows, sl]).astype(o_ref.dtype)
        return carry

    lax.fori_loop(0, n_chunks, finish, 0)


def _gla(gq, gk, cum, gv, gr, g_norm):
    b, s, _ = gq.shape
    dkw = GLA_HEADS * GLA_DK
    dvw = GLA_HEADS * GLA_DV
    blk = lambda w: pl.BlockSpec((1, s, w), lambda i: (i, 0, 0))
    return pl.pallas_call(
        _gla_kernel,
        grid=(b,),
        in_specs=[blk(dkw), blk(dkw), blk(2 * dkw), blk(dvw), blk(dvw),
                  pl.BlockSpec((1, GLA_DV), lambda i: (0, 0))],
        out_specs=blk(dvw),
        out_shape=jax.ShapeDtypeStruct((b, s, dvw), BF16),
        scratch_shapes=[pltpu.VMEM((2, dvw, dkw), F32), pltpu.VMEM((s, dvw), F32), pltpu.VMEM((s, dvw), F32)],
        compiler_params=_params(("parallel",)),
        name="gla",
    )(gq, gk, cum, gv, gr, g_norm)


def _odd_proj_kernel(x_ref, w_ref, gq_ref, gk_ref, avg_ref, cos_ref, sin_ref, q_ref, k_ref, v_ref):
    h = _dot(x_ref[...].astype(BF16), w_ref[...])
    cos = cos_ref[...]
    sin = sin_ref[...]
    avg = avg_ref[...]
    nq = GQA_HEADS * GQA_HEAD_DIM
    nk = GQA_KV_HEADS * GQA_HEAD_DIM

    def norm_rope(t, g):
        ms = _dot_x_sel(t * t, avg)
        t = t * lax.rsqrt(ms + EPS) * g
        return t * cos + _rot_half(t, GQA_HEAD_DIM // 2) * sin

    for c in range(nq // nk):
        sl = slice(c * nk, (c + 1) * nk)
        q_ref[:, sl] = (norm_rope(h[:, sl], gq_ref[...]) * (GQA_HEAD_DIM ** -0.5 * LOG2_E)).astype(q_ref.dtype)
    k_ref[...] = norm_rope(h[:, nq:nq + nk], gk_ref[...]).astype(k_ref.dtype)
    v_ref[...] = h[:, nq + nk:].astype(v_ref.dtype)


def _odd_proj(x2, w, gq, gk, avg, cos, sin, seq, tm):
    t, d = x2.shape
    nsb = seq // tm
    row = lambda i: (i, 0)
    fixed = lambda i: (0, 0)
    pos = lambda i: (i % nsb, 0)
    full = lambda a: pl.BlockSpec(a.shape, fixed)
    outs = [ODD_SPLITS[0], ODD_SPLITS[1], ODD_SPLITS[2]]
    return pl.pallas_call(
        _odd_proj_kernel,
        grid=(t // tm,),
        in_specs=[pl.BlockSpec((tm, d), row), full(w), full(gq), full(gk), full(avg),
                  pl.BlockSpec((tm, cos.shape[1]), pos), pl.BlockSpec((tm, sin.shape[1]), pos)],
        out_specs=[pl.BlockSpec((tm, n), row) for n in outs],
        out_shape=[jax.ShapeDtypeStruct((t, n), BF16) for n in outs],
        compiler_params=_params(("parallel",)),
        name="odd_proj",
    )(x2, w, gq, gk, avg, cos, sin)


def _gqa_attn_kernel(q_ref, k_ref, v_ref, o_ref, kp_ref, vt_ref, s_ref, e_ref):
    j = pl.program_id(1)
    group = GQA_HEADS // GQA_KV_HEADS
    width = GQA_KV_HEADS * GQA_HEAD_DIM
    pair = 2 * GQA_HEAD_DIM

    @pl.when(pl.program_id(2) == 0)
    def _():
        r = lax.broadcasted_iota(jnp.int32, (width, pair), 0)
        c = lax.broadcasted_iota(jnp.int32, (width, pair), 1)
        twice = (r // GQA_HEAD_DIM == j) & (r % GQA_HEAD_DIM == c % GQA_HEAD_DIM)
        kp_ref[...] = _dot(k_ref[0], jnp.where(twice, 1.0, 0.0).astype(BF16)).astype(BF16)
        rv = lax.broadcasted_iota(jnp.int32, (GQA_HEAD_DIM, width), 0)
        cv = lax.broadcasted_iota(jnp.int32, (GQA_HEAD_DIM, width), 1)
        mine = cv == j * GQA_HEAD_DIM + rv
        vt_ref[...] = _values_t(_dot_nt(jnp.where(mine, 1.0, 0.0).astype(BF16), v_ref[0]))

    lane = lax.broadcasted_iota(jnp.int32, (q_ref.shape[1], pair), 1)

    def q_of(g):
        qp = q_ref[0, :, (g // 2) * pair:(g // 2 + 1) * pair]
        return jnp.where(lane // GQA_HEAD_DIM == g % 2, qp, jnp.zeros_like(qp))

    out_t = _attend_heads(group, lambda g, keys: kp_ref[keys, :], q_of, lambda g, keys: vt_ref[:, keys],
                          s_ref, e_ref)
    o_ref[0] = out_t.T.astype(o_ref.dtype)


def _gqa_attn(q, k, v, tq):
    b, s, _ = q.shape
    width = GQA_KV_HEADS * GQA_HEAD_DIM
    group = GQA_HEADS // GQA_KV_HEADS
    return pl.pallas_call(
        _gqa_attn_kernel,
        grid=(b, GQA_KV_HEADS, s // tq),
        in_specs=[pl.BlockSpec((1, tq, width), lambda i, j, t: (i, t, j)),
                  pl.BlockSpec((1, s, width), lambda i, j, t: (i, 0, 0)),
                  pl.BlockSpec((1, s, width), lambda i, j, t: (i, 0, 0))],
        out_specs=pl.BlockSpec((1, tq, width), lambda i, j, t: (i, t, j)),
        out_shape=jax.ShapeDtypeStruct((b, s, GQA_HEADS * GQA_HEAD_DIM), BF16),
        scratch_shapes=[pltpu.VMEM((s, 2 * GQA_HEAD_DIM), BF16),
                        pltpu.VMEM((GQA_HEAD_DIM + VT_ROWS_PAD, s), BF16),
                        pltpu.VMEM((group, s, tq), F32), pltpu.VMEM((group, s, tq), BF16)],
        compiler_params=_params(("parallel", "parallel", "arbitrary")),
        name="gqa_attn",
    )(q, k, v)


def _out_ln_kernel(alpha, x_ref, ma_ref, mb_ref, wa_ref, wb_ref, g_ref, b_ref, y_ref, yt_ref):
    y = alpha * x_ref[...] + _dot(ma_ref[...].astype(BF16), wa_ref[...]) \
        + _dot(mb_ref[...].astype(BF16), wb_ref[...])
    y = _layer_norm(y, g_ref[...], b_ref[...])
    y_ref[...] = y
    yt_ref[0] = y.T.astype(BF16)


def _out_ln(x2, mix_a, mix_b, col_a, col_b, w_a, w_b, g, b, alpha, seq, tm):
    t, d = x2.shape
    half = w_a.shape[0]
    nsb = seq // tm
    row = lambda i: (i, 0)
    fixed = lambda i: (0, 0)
    return pl.pallas_call(
        functools.partial(_out_ln_kernel, alpha),
        grid=(t // tm,),
        in_specs=[pl.BlockSpec((tm, d), row),
                  pl.BlockSpec((tm, half), lambda i: (i, col_a)),
                  pl.BlockSpec((tm, half), lambda i: (i, col_b)),
                  pl.BlockSpec(w_a.shape, fixed), pl.BlockSpec(w_b.shape, fixed),
                  pl.BlockSpec(g.shape, fixed), pl.BlockSpec(b.shape, fixed)],
        out_specs=[pl.BlockSpec((tm, d), row),
                   pl.BlockSpec((1, d, tm), lambda i: (i // nsb, 0, i % nsb))],
        out_shape=[jax.ShapeDtypeStruct((t, d), F32), jax.ShapeDtypeStruct((t // seq, d, seq), BF16)],
        compiler_params=_params(("parallel",)),
        name="out_ln",
    )(x2, mix_a, mix_b, w_a, w_b, g, b)


def _route_kernel(cap, x_ref, rw_ref, aff_ref, slot_ref):
    x = x_ref[0]
    seq = x.shape[0]
    xh, xm = _split2(x)
    wh, wm = _split2(rw_ref[...])
    logits = _dot_nt(wh, xh) + (_dot_nt(wh, xm) + _dot_nt(wm, xh))
    e = jnp.exp(logits - jnp.max(logits, axis=0, keepdims=True))
    aff = e / jnp.sum(e, axis=0, keepdims=True)
    aff_ref[0] = aff

    def enough(cand):
        return jnp.sum((aff >= pltpu.bitcast(cand, F32)).astype(jnp.int32), axis=1, keepdims=True) >= cap

    top = jnp.full((aff.shape[0], 1), 1 << 30, jnp.int32)
    thr0 = jnp.where(enough(top), top, 0)

    def pick(i, thr):
        lo = 28 - 2 * i
        c1, c2, c3 = thr | (jnp.int32(1) << lo), thr | (jnp.int32(2) << lo), thr | (jnp.int32(3) << lo)
        return jnp.where(enough(c3), c3, jnp.where(enough(c2), c2, jnp.where(enough(c1), c1, thr)))

    thr = pltpu.bitcast(lax.fori_loop(0, 15, pick, thr0), F32)
    above = aff > thr
    tie = aff == thr
    need = cap - jnp.sum(above.astype(jnp.int32), axis=1, keepdims=True)

    blk = 256 if seq % 256 == 0 else V7X_LANES
    r = lax.broadcasted_iota(jnp.int32, (blk, blk), 0)
    c = lax.broadcasted_iota(jnp.int32, (blk, blk), 1)
    before = jnp.where(r < c, 1.0, 0.0).astype(BF16)

    def prefix(mask):
        m = jnp.where(mask, 1.0, 0.0).astype(BF16)
        run = jnp.zeros((mask.shape[0], 1), F32)
        parts = []
        for t in range(seq // blk):
            mb = m[:, t * blk:(t + 1) * blk]
            parts.append(_dot(mb, before) + run)
            run = run + jnp.sum(mb.astype(F32), axis=1, keepdims=True)
        return jnp.concatenate(parts, axis=1).astype(jnp.int32)

    chosen = above | (tie & (prefix(tie) < need))
    slot_ref[0] = jnp.where(chosen, prefix(chosen), -1)


def _route(x1, rw_t, cap):
    b, s, d = x1.shape
    e = rw_t.shape[0]
    return pl.pallas_call(
        functools.partial(_route_kernel, cap),
        grid=(b,),
        in_specs=[pl.BlockSpec((1, s, d), lambda i: (i, 0, 0)), pl.BlockSpec((e, d), lambda i: (0, 0))],
        out_specs=[pl.BlockSpec((1, e, s), lambda i: (i, 0, 0)), pl.BlockSpec((1, e, s), lambda i: (i, 0, 0))],
        out_shape=[jax.ShapeDtypeStruct((b, e, s), F32), jax.ShapeDtypeStruct((b, e, s), jnp.int32)],
        compiler_params=_params(("parallel",)),
        name="route",
    )(x1, rw_t)


def _moe_kernel(cap, ts, xt_ref, slot_ref, aff_ref, w1_ref, w3_ref, w2_ref, o_ref):
    e_id = pl.program_id(1)
    seq = xt_ref.shape[2]
    ff = w2_ref.shape[2]

    @pl.when(e_id == 0)
    def _():
        o_ref[...] = jnp.zeros_like(o_ref)

    c_row = lax.broadcasted_iota(jnp.int32, (cap, seq), 0)
    pick = jnp.where(slot_ref[0, 0] == c_row, 1.0, 0.0).astype(BF16)
    hd2 = xt_ref.shape[1] // 2
    xg = jnp.concatenate([_dot_nt(xt_ref[0, :hd2, :], pick), _dot_nt(xt_ref[0, hd2:, :], pick)],
                         axis=0).astype(BF16)
    h1 = _dot(w1_ref[0], xg)
    h3 = _dot(w3_ref[0], xg)
    hid = (h1 * _sigmoid(h1) * h3).astype(BF16)
    ye = jnp.concatenate([_dot(w2_ref[0, :hd2, :], hid), _dot(w2_ref[0, hd2:, :], hid)],
                         axis=0).astype(BF16)
    for t in range(seq // ts):
        cols = slice(t * ts, (t + 1) * ts)
        o_ref[0, :, cols] += _dot(ye, pick[:, cols]) * aff_ref[0, 0, :, cols]


def _moe(xt, slot, aff, w1t, w3t, w2t, cap, layer):
    b, d, s = xt.shape
    e = slot.shape[1]
    ff = w2t.shape[2]
    ts = min(SCATTER_BLOCK, s)
    w_blk = lambda i, j: (layer * e + j, 0, 0)
    return pl.pallas_call(
        functools.partial(_moe_kernel, cap, ts),
        grid=(b, e),
        in_specs=[pl.BlockSpec((1, d, s), lambda i, j: (i, 0, 0)),
                  pl.BlockSpec((1, 1, 1, s), lambda i, j: (i, j, 0, 0)),
                  pl.BlockSpec((1, 1, 1, s), lambda i, j: (i, j, 0, 0)),
                  pl.BlockSpec((1, ff, d), w_blk), pl.BlockSpec((1, ff, d), w_blk),
                  pl.BlockSpec((1, d, ff), w_blk)],
        out_specs=pl.BlockSpec((1, d, s), lambda i, j: (i, 0, 0)),
        out_shape=jax.ShapeDtypeStruct((b, d, s), F32),
        compiler_params=_params(("parallel", "arbitrary")),
        name="moe",
    )(xt, slot.reshape(b, e, 1, s), aff.reshape(b, e, 1, s), w1t, w3t, w2t)


def _transpose_cast_kernel(w_ref, o_ref):
    o_ref[0] = w_ref[0].T.astype(o_ref.dtype)


def _transpose_cast(w):
    n, r, c = w.shape
    return pl.pallas_call(
        _transpose_cast_kernel,
        grid=(n,),
        in_specs=[pl.BlockSpec((1, r, c), lambda i: (i, 0, 0))],
        out_specs=pl.BlockSpec((1, c, r), lambda i: (i, 0, 0)),
        out_shape=jax.ShapeDtypeStruct((n, c, r), BF16),
        compiler_params=_params(("parallel",)),
        name="transpose_cast",
    )(w)


def _ple_ln_kernel(alpha, x_ref, f_ref, p_ref, wg_ref, bg_ref, wp_ref, g_ref, b_ref, y_ref):
    x = x_ref[...]
    gate = _sigmoid(_dot(x.astype(BF16), wg_ref[...]) + bg_ref[...])
    ple = gate * _dot(p_ref[...].astype(BF16), wp_ref[...])
    ffn = f_ref[0].T
    y_ref[...] = _layer_norm(alpha * x + ffn + ple, g_ref[...], b_ref[...])


def _ple_ln(x1, ffn_t, p2, wg, bg, wp, g, b, alpha, tm, layer):
    t, d = x1.shape
    nsb = ffn_t.shape[2] // tm
    row = lambda i: (i, 0)
    p_row = lambda i: (layer * (t // tm) + i, 0)
    fixed = lambda i: (0, 0)
    full = lambda a: pl.BlockSpec(a.shape, fixed)
    return pl.pallas_call(
        functools.partial(_ple_ln_kernel, alpha),
        grid=(t // tm,),
        in_specs=[pl.BlockSpec((tm, d), row),
                  pl.BlockSpec((1, d, tm), lambda i: (i // nsb, 0, i % nsb)),
                  pl.BlockSpec((tm, p2.shape[1]), p_row), full(wg), full(bg), full(wp), full(g), full(b)],
        out_specs=pl.BlockSpec((tm, d), row),
        out_shape=jax.ShapeDtypeStruct((t, d), F32),
        compiler_params=_params(("parallel",)),
        name="ple_ln",
    )(x1, ffn_t, p2, wg, bg, wp, g, b)


def _rope_tables(seq, rot_dim, lo, width):
    rows = seq // GRID_W
    row = jnp.repeat(jnp.arange(rows, dtype=F32), GRID_W)
    col = jnp.tile(jnp.arange(GRID_W, dtype=F32), rows)
    axis_dim = rot_dim // 2
    inv = ROPE_THETA ** (-jnp.arange(0, axis_dim, 2, dtype=F32) / axis_dim)
    ang = jnp.concatenate([row[:, None] * inv, col[:, None] * inv], axis=-1)
    cos, sin = jnp.cos(ang), jnp.sin(ang)
    cos2 = jnp.concatenate([cos, cos], axis=-1)
    sin2 = jnp.concatenate([-sin, sin], axis=-1)
    if lo == 0:
        reps = width // rot_dim
        return jnp.tile(cos2, (1, reps)), jnp.tile(sin2, (1, reps))
    pad_l = jnp.ones((seq, lo), F32)
    pad_r = jnp.ones((seq, width - lo - rot_dim), F32)
    cos_t = jnp.concatenate([pad_l, cos2, pad_r], axis=-1)
    sin_t = jnp.concatenate([0 * pad_l, sin2, 0 * pad_r], axis=-1)
    return cos_t, sin_t


def _prep_even(w_in, w_uq, w_ukv, gw_f, gb_f, gw_b, gb_b):
    d = w_in.shape[0]
    offs = np.cumsum(EVEN_SPLITS)[:-1].tolist()
    c_q, c_kv, k_pe, gq, gk, gv, g_lr, gr = jnp.split(w_in, offs, axis=-1)
    z32 = jnp.zeros((d, 32), w_in.dtype)
    chunk = jnp.concatenate([g_lr, z32, k_pe, z32], axis=-1)
    w_in_p = jnp.concatenate([c_q, c_kv, chunk, gq, gk, gv, gr], axis=-1).astype(BF16)
    uq = w_uq.reshape(MLA_Q_LORA, MLA_HEADS, MLA_NOPE + MLA_ROPE)
    uq = jnp.pad(uq, ((0, 0), (0, 0), (0, MLA_PAD - MLA_NOPE - MLA_ROPE)))
    w_uq_p = uq.reshape(MLA_Q_LORA, MLA_HEADS * MLA_PAD).astype(BF16)
    ukv = w_ukv.reshape(MLA_KV_LORA, MLA_HEADS, MLA_NOPE + MLA_V)
    uk = jnp.pad(ukv[:, :, :MLA_NOPE], ((0, 0), (0, 0), (0, MLA_PAD - MLA_NOPE)))
    w_uk_p = uk.reshape(MLA_KV_LORA, MLA_HEADS * MLA_PAD).astype(BF16)
    w_uv = ukv[:, :, MLA_NOPE:].reshape(MLA_KV_LORA, MLA_HEADS * MLA_V).astype(BF16)
    dkw = GLA_HEADS * GLA_DK
    w_gate = jnp.zeros((MLA_PAD, 2 * dkw), F32)
    w_gate = w_gate.at[0:GLA_GATE_RANK, 0:dkw].set(gw_f)
    w_gate = w_gate.at[GLA_GATE_RANK:2 * GLA_GATE_RANK, dkw:].set(gw_b).astype(BF16)
    b_gate = jnp.concatenate([gb_f, gb_b])[None, :]
    return w_in_p, w_uq_p, w_uk_p, w_uv, w_gate, b_gate


def kernel(x, p, w_in_even, mla_q_norm, w_uq, mla_kv_norm, w_ukv, gla_gate_w_fwd, gla_gate_b_fwd,
           gla_gate_w_bwd, gla_gate_b_bwd, gla_norm, w_in_odd, gqa_q_norm, gqa_k_norm, w_o, ln1_g,
           ln1_b, router_w, w1, w3, w2, ple_gate_w, ple_gate_b, ple_w, ln2_g, ln2_b):
    b, s, d = x.shape
    depth = w_o.shape[0]
    t = b * s
    alpha = (2.0 * depth) ** 0.25
    cap = EC_CAPACITY_FACTOR * s // N_EXPERTS
    tm = min(ROW_BLOCK, s)
    tq = min(QUERY_BLOCK, s)
    half = w_o.shape[1] // 2
    cos_a, sin_a = _rope_tables(s, MLA_ROPE, PE_LO, MLA_PAD)
    cos_c, sin_c = _rope_tables(s, GQA_HEAD_DIM, 0, GQA_KV_HEADS * GQA_HEAD_DIM)
    hw = GQA_KV_HEADS * GQA_HEAD_DIM
    head_of = np.arange(hw) // GQA_HEAD_DIM
    avg = jnp.asarray((head_of[:, None] == head_of[None, :]) / GQA_HEAD_DIM, BF16)

    n_e, ff = w1.shape[1], w1.shape[3]
    w1t = _transpose_cast(w1.reshape(depth * n_e, d, ff))
    w3t = _transpose_cast(w3.reshape(depth * n_e, d, ff))
    w2t = _transpose_cast(w2.reshape(depth * n_e, ff, d))
    p_all = p.reshape(depth * t, p.shape[-1])

    x2 = x.reshape(t, d)
    for i in range(depth):
        j = i // 2
        if i % 2 == 0:
            w_in_p, w_uq_p, w_uk_p, w_uv, w_gate, b_gate = _prep_even(
                w_in_even[j], w_uq[j], w_ukv[j], gla_gate_w_fwd[j], gla_gate_b_fwd[j],
                gla_gate_w_bwd[j], gla_gate_b_bwd[j])
            q, k, v, gq, gk, gv, la, gr = _even_proj(
                x2, w_in_p, mla_q_norm[j][None, :], w_uq_p, mla_kv_norm[j][None, :], w_uk_p, w_uv,
                w_gate, b_gate, cos_a, sin_a, s, tm)
            r3 = lambda a: a.reshape(b, s, a.shape[-1])
            o_mla = _mla_attn(r3(q), r3(k), r3(v), tq).reshape(t, -1)
            o_gla = _gla(r3(gq), r3(gk), r3(la), r3(gv), r3(gr), gla_norm[j][None, :]).reshape(t, -1)
            mix_a, mix_b, col_a, col_b = o_mla, o_gla, 0, 0
        else:
            q, k, v = _odd_proj(x2, w_in_odd[j].astype(BF16),
                                jnp.tile(gqa_q_norm[j], GQA_KV_HEADS)[None, :],
                                jnp.tile(gqa_k_norm[j], GQA_KV_HEADS)[None, :],
                                avg, cos_c, sin_c, s, tm)
            r3 = lambda a: a.reshape(b, s, a.shape[-1])
            o = _gqa_attn(r3(q), r3(k), r3(v), tq).reshape(t, -1)
            mix_a, mix_b, col_a, col_b = o, o, 0, 1
        wo = w_o[i].astype(BF16)
        x1, x1t = _out_ln(x2, mix_a, mix_b, col_a, col_b, wo[:half], wo[half:],
                          ln1_g[i][None, :], ln1_b[i][None, :], alpha, s, tm)
        aff, slot = _route(x1.reshape(b, s, d), router_w[i].T, cap)
        ffn_t = _moe(x1t, slot, aff, w1t, w3t, w2t, cap, i)
        x2 = _ple_ln(x1, ffn_t, p_all, ple_gate_w[i].astype(BF16),
                     ple_gate_b[i][None, :], ple_w[i].astype(BF16), ln2_g[i][None, :], ln2_b[i][None, :],
                     alpha, tm, i)
    return x2.reshape(b, s, d)
```

```python
import functools
import math

import jax
import jax.numpy as jnp
import numpy as np
from jax import lax
from jax.experimental import pallas as pl
from jax.experimental.pallas import tpu as pltpu

F32 = jnp.float32
BF16 = jnp.bfloat16

V7X_LANES = 128
V7X_BF16_SUBLANES = 16
V7X_VMEM_BYTES = 64 * 1024 * 1024
VMEM_LIMIT = V7X_VMEM_BYTES - 8 * 1024 * 1024

ROW_BLOCK = 1024
QUERY_BLOCK = 512
SCATTER_BLOCK = 512

GRID_W = 64
ROPE_THETA = 10000.0
EPS = 1e-6
MLA_HEADS, MLA_Q_LORA, MLA_KV_LORA = 8, 256, 128
MLA_NOPE, MLA_ROPE, MLA_V = 64, 32, 64
GLA_HEADS, GLA_DK, GLA_DV = 4, 64, 128
GLA_GATE_RANK, GLA_GATE_NORM, GLA_CHUNK = 16, 16.0, 64
GQA_HEADS, GQA_KV_HEADS, GQA_HEAD_DIM = 16, 4, 64
N_EXPERTS, EC_CAPACITY_FACTOR = 16, 2
EVEN_SPLITS = (MLA_Q_LORA, MLA_KV_LORA, MLA_ROPE, GLA_HEADS * GLA_DK, GLA_HEADS * GLA_DK,
               GLA_HEADS * GLA_DV, 2 * GLA_GATE_RANK, GLA_HEADS * GLA_DV)
ODD_SPLITS = (GQA_HEADS * GQA_HEAD_DIM, GQA_KV_HEADS * GQA_HEAD_DIM, GQA_KV_HEADS * GQA_HEAD_DIM)
MLA_PAD = V7X_LANES
PE_LO, PE_HI = MLA_NOPE, MLA_NOPE + MLA_ROPE


def _dot(a, b):
    return jnp.dot(a, b, preferred_element_type=F32)


def _dot_nt(a, b):
    return lax.dot_general(a, b, (((1,), (1,)), ((), ())), preferred_element_type=F32)


def _dot_tn(a, b):
    return lax.dot_general(a, b, (((0,), (0,)), ((), ())), preferred_element_type=F32)


def _split2(x):
    hi = x.astype(BF16)
    lo = (x - hi.astype(F32)).astype(BF16)
    return hi, lo


def _dot_sel(sel, x):
    hi, lo = _split2(x)
    return _dot(sel, hi) + _dot(sel, lo)


def _dot_x_sel(x, sel):
    hi, lo = _split2(x)
    return _dot(hi, sel) + _dot(lo, sel)


def _params(sem):
    return pltpu.CompilerParams(dimension_semantics=sem, vmem_limit_bytes=VMEM_LIMIT)


def _rot_half(x, half):
    w = x.shape[-1]
    lane = lax.broadcasted_iota(jnp.int32, x.shape, x.ndim - 1)
    first = (lane % (2 * half)) < half
    return jnp.where(first, pltpu.roll(x, w - half, x.ndim - 1), pltpu.roll(x, half, x.ndim - 1))


def _layer_norm(y, g, b):
    mu = jnp.mean(y, axis=-1, keepdims=True)
    yc = y - mu
    var = jnp.mean(yc * yc, axis=-1, keepdims=True)
    return yc * lax.rsqrt(var + EPS) * g + b


def _rms(x, g):
    return x * lax.rsqrt(jnp.mean(x * x, axis=-1, keepdims=True) + EPS) * g


def _log_sigmoid(z):
    return jnp.minimum(z, 0.0) - jnp.log1p(jnp.exp(-jnp.abs(z)))


def _sigmoid(z):
    return 1.0 / (1.0 + jnp.exp(-z))


LOG2_E = math.log2(math.e)
KEY_BLK = 128


KEY_CHUNKS = 4


def _exp2_cols(m, s_scr, e_scr):
    seq, tq = s_scr.shape
    for c in range(tq // V7X_LANES):
        cols = slice(c * V7X_LANES, (c + 1) * V7X_LANES)
        m_b = jnp.broadcast_to(m[:, cols], (KEY_BLK, V7X_LANES))
        for r in range(seq // KEY_BLK):
            rows = slice(r * KEY_BLK, (r + 1) * KEY_BLK)
            e_scr[rows, cols] = jnp.exp2(s_scr[rows, cols] - m_b).astype(BF16)


VT_ROWS_PAD = V7X_BF16_SUBLANES


def _values_t(v_t):
    dv, seq = v_t.shape
    extra = jnp.where(lax.broadcasted_iota(jnp.int32, (VT_ROWS_PAD, seq), 0) == 0, 1.0, 0.0)
    return jnp.concatenate([v_t.astype(BF16), extra.astype(BF16)], axis=0)


def _attend_heads(n_heads, k_of, q_of, vt_of, s_ref, e_ref):
    assert s_ref.shape[0] == n_heads and e_ref.shape[0] == n_heads
    seq = s_ref.shape[1]
    ck = seq // KEY_CHUNKS
    m, acc = {}, {}
    for h in range(n_heads + 2):
        m_parts = []
        for c in range(KEY_CHUNKS):
            keys = slice(c * ck, (c + 1) * ck)
            if h < n_heads:
                s = _dot_nt(k_of(h, keys), q_of(h))
                s_ref[h, keys, :] = s
                m_parts.append(jnp.max(s, axis=0, keepdims=True))
            if 1 <= h <= n_heads:
                _exp2_cols(m[h - 1], s_ref.at[h - 1, keys], e_ref.at[h - 1, keys])
            if h >= 2:
                o = _dot(vt_of(h - 2, keys), e_ref[h - 2, keys, :])
                acc[h - 2] = o if c == 0 else acc[h - 2] + o
        if h < n_heads:
            m[h] = functools.reduce(jnp.maximum, m_parts)
    outs = []
    for h in range(n_heads):
        dv = acc[h].shape[0] - VT_ROWS_PAD
        outs.append(acc[h][:dv] * (1.0 / acc[h][dv:dv + 1]))
    return jnp.concatenate(outs, axis=0)


def _even_proj_kernel(x_ref, w_in_ref, qn_ref, w_uq_ref, kvn_ref, w_uk_ref, w_uv_ref,
                      w_gate_ref, b_gate_ref, cq_ref, sq_ref,
                      q_ref, k_ref, v_ref, gq_ref, gk_ref, gv_ref, la_ref, gr_ref):
    h = _dot(x_ref[...].astype(BF16), w_in_ref[...])
    cos = cq_ref[...]
    sin = sq_ref[...]
    lane = lax.broadcasted_iota(jnp.int32, cos.shape, 1)
    pe_lane = (lane >= PE_LO) & (lane < PE_HI)

    def rope(t):
        return t * cos + _rot_half(t, MLA_ROPE // 2) * sin

    c_q = _rms(h[:, 0:256], qn_ref[...])
    q = _dot(c_q.astype(BF16), w_uq_ref[...]) * ((MLA_NOPE + MLA_ROPE) ** -0.5 * LOG2_E)
    c_kv = _rms(h[:, 256:384], kvn_ref[...]).astype(BF16)
    kn = _dot(c_kv, w_uk_ref[...])
    v_ref[...] = _dot(c_kv, w_uv_ref[...]).astype(v_ref.dtype)
    chunk = h[:, 384:512]
    k_pe = jnp.where(pe_lane, rope(chunk), 0.0)
    for hd in range(MLA_HEADS):
        sl = slice(hd * MLA_PAD, (hd + 1) * MLA_PAD)
        q_ref[:, sl] = rope(q[:, sl]).astype(q_ref.dtype)
        k_ref[:, sl] = (kn[:, sl] + k_pe).astype(k_ref.dtype)
    z = _dot(chunk.astype(BF16), w_gate_ref[...]) + b_gate_ref[...]
    la = _log_sigmoid(z) * (1.0 / GLA_GATE_NORM)
    blk = 2 * GLA_CHUNK
    r = lax.broadcasted_iota(jnp.int32, (blk, blk), 0)
    c = lax.broadcasted_iota(jnp.int32, (blk, blk), 1)
    same = (r // GLA_CHUNK) == (c // GLA_CHUNK)
    lower = jnp.where(same & (c <= r), 1.0, 0.0).astype(BF16)
    upper = jnp.where(same & (c >= r), 1.0, 0.0).astype(BF16)
    dkw = GLA_HEADS * GLA_DK
    for t in range(la.shape[0] // blk):
        rows = slice(t * blk, (t + 1) * blk)
        la_ref[rows, :dkw] = _dot_sel(lower, la[rows, :dkw])
        la_ref[rows, dkw:] = _dot_sel(upper, la[rows, dkw:])
    gq_ref[...] = h[:, 512:768] * (GLA_DK ** -0.5)
    gk_ref[...] = h[:, 768:1024]
    gv_ref[...] = h[:, 1024:1536].astype(gv_ref.dtype)
    gr = h[:, 1536:2048]
    gr_ref[...] = gr * _sigmoid(gr)


def _even_proj(x2, w_in_p, qn, w_uq_p, kvn, w_uk_p, w_uv, w_gate, b_gate, cq, sq, seq, tm):
    t, d = x2.shape
    nsb = seq // tm
    row = lambda i: (i, 0)
    fixed = lambda i: (0, 0)
    pos = lambda i: (i % nsb, 0)
    full = lambda a: pl.BlockSpec(a.shape, fixed)
    outs = [(1024, BF16), (1024, BF16), (512, BF16), (256, F32), (256, F32), (512, BF16), (512, F32), (512, F32)]
    return pl.pallas_call(
        _even_proj_kernel,
        grid=(t // tm,),
        in_specs=[pl.BlockSpec((tm, d), row), full(w_in_p), full(qn), full(w_uq_p), full(kvn),
                  full(w_uk_p), full(w_uv), full(w_gate), full(b_gate),
                  pl.BlockSpec((tm, MLA_PAD), pos), pl.BlockSpec((tm, MLA_PAD), pos)],
        out_specs=[pl.BlockSpec((tm, n), row) for n, _ in outs],
        out_shape=[jax.ShapeDtypeStruct((t, n), dt) for n, dt in outs],
        compiler_params=_params(("parallel",)),
        name="even_proj",
    )(x2, w_in_p, qn, w_uq_p, kvn, w_uk_p, w_uv, w_gate, b_gate, cq, sq)


MLA_STEP_HEADS = 4


def _mla_attn_kernel(q_ref, k_ref, v_ref, o_ref, vt_ref, s_ref, e_ref):
    out_w = MLA_STEP_HEADS * MLA_V

    @pl.when(pl.program_id(2) == 0)
    def _():
        r = lax.broadcasted_iota(jnp.int32, (out_w, out_w), 0)
        c = lax.broadcasted_iota(jnp.int32, (out_w, out_w), 1)
        eye = jnp.where(r == c, 1.0, 0.0).astype(BF16)
        v_t = _dot_nt(eye, v_ref[0])
        for a in range(MLA_STEP_HEADS):
            vt_ref[a] = _values_t(v_t[a * MLA_V:(a + 1) * MLA_V])

    out_t = _attend_heads(
        MLA_STEP_HEADS,
        lambda a, keys: k_ref[0, keys, a * MLA_PAD:(a + 1) * MLA_PAD],
        lambda a: q_ref[0, :, a * MLA_PAD:(a + 1) * MLA_PAD],
        lambda a, keys: vt_ref[a, :, keys],
        s_ref, e_ref)
    o_ref[0] = out_t.T.astype(o_ref.dtype)


def _mla_attn(q, k, v, tq):
    b, s, _ = q.shape
    n = MLA_STEP_HEADS
    return pl.pallas_call(
        _mla_attn_kernel,
        grid=(b, MLA_HEADS // n, s // tq),
        in_specs=[pl.BlockSpec((1, tq, n * MLA_PAD), lambda i, j, t: (i, t, j)),
                  pl.BlockSpec((1, s, n * MLA_PAD), lambda i, j, t: (i, 0, j)),
                  pl.BlockSpec((1, s, n * MLA_V), lambda i, j, t: (i, 0, j))],
        out_specs=pl.BlockSpec((1, tq, n * MLA_V), lambda i, j, t: (i, t, j)),
        out_shape=jax.ShapeDtypeStruct((b, s, MLA_HEADS * MLA_V), BF16),
        scratch_shapes=[pltpu.VMEM((n, MLA_V + VT_ROWS_PAD, s), BF16),
                        pltpu.VMEM((n, s, tq), F32), pltpu.VMEM((n, s, tq), BF16)],
        compiler_params=_params(("parallel", "parallel", "arbitrary")),
        name="mla_attn",
    )(q, k, v)


def _gla_kernel(q_ref, k_ref, cum_ref, v_ref, gr_ref, g_ref, o_ref, state_ref, of_ref, ob_ref):
    seq = q_ref.shape[1]
    n_chunks = seq // GLA_CHUNK
    L = GLA_CHUNK
    dkw = GLA_HEADS * GLA_DK
    dvw = GLA_HEADS * GLA_DV

    def iota(shape, dim):
        return lax.broadcasted_iota(jnp.int32, shape, dim)

    row_l = iota((L, dkw), 0)
    col_m = iota((L, dkw), 1) % L
    k_own = (iota((dkw, dkw), 0) // L) == (iota((dkw, dkw), 1) // GLA_DK)
    v_own = (iota((dkw, dvw), 0) // L) == (iota((dkw, dvw), 1) // GLA_DV)
    s_own = (iota((dvw, dkw), 0) // GLA_DV) == (iota((dvw, dkw), 1) // GLA_DK)
    state_ref[...] = jnp.zeros_like(state_ref)

    def body(i, carry):
        for d, fwd in enumerate((True, False)):
            n = i if fwd else n_chunks - 1 - i
            rows = pl.ds(pl.multiple_of(n * L, L), L)
            cum = cum_ref[0, rows, d * dkw:(d + 1) * dkw]
            last = cum[L - 1:L, :] if fwd else cum[0:1, :]
            q = q_ref[0, rows, :]
            k = k_ref[0, rows, :]
            v = v_ref[0, rows, :]
            qe = (q * jnp.exp(cum)).astype(BF16)
            kg = (k * jnp.exp(-cum)).astype(BF16)
            kdec = (k * jnp.exp(last - cum)).astype(BF16)
            k_blk = jnp.where(k_own, jnp.concatenate([kg] * GLA_HEADS, axis=0), jnp.zeros((), BF16))
            att = _dot_nt(qe, k_blk)
            keep = (col_m <= row_l) if fwd else (col_m >= row_l)
            att = jnp.where(keep, att, 0.0).astype(BF16)
            v_blk = jnp.where(v_own, jnp.concatenate([v] * GLA_HEADS, axis=0), jnp.zeros((), BF16))
            st = state_ref[d]
            o = _dot(att, v_blk) + _dot_nt(qe, st.astype(BF16))
            state_ref[d] = st * jnp.exp(last) + jnp.where(s_own, _dot_tn(v, kdec), 0.0)
            if fwd:
                of_ref[rows, :] = o
            else:
                ob_ref[rows, :] = o
        return carry

    lax.fori_loop(0, n_chunks, body, 0, unroll=4)

    def finish(n, carry):
        rows = pl.ds(pl.multiple_of(n * L, L), L)
        tot = of_ref[rows, :] + ob_ref[rows, :]
        for hd in range(GLA_HEADS):
            sl = slice(hd * GLA_DV, (hd + 1) * GLA_DV)
            o_ref[0, rows, sl] = (_rms(tot[:, sl], g_ref[...]) * gr_ref[0, rows, sl]).astype(o_ref.dtype)
        return carry

    lax.fori_loop(0, n_chunks, finish, 0)


def _gla(gq, gk, cum, gv, gr, g_norm):
    b, s, _ = gq.shape
    dkw = GLA_HEADS * GLA_DK
    dvw = GLA_HEADS * GLA_DV
    blk = lambda w: pl.BlockSpec((1, s, w), lambda i: (i, 0, 0))
    return pl.pallas_call(
        _gla_kernel,
        grid=(b,),
        in_specs=[blk(dkw), blk(dkw), blk(2 * dkw), blk(dvw), blk(dvw),
                  pl.BlockSpec((1, GLA_DV), lambda i: (0, 0))],
        out_specs=blk(dvw),
        out_shape=jax.ShapeDtypeStruct((b, s, dvw), BF16),
        scratch_shapes=[pltpu.VMEM((2, dvw, dkw), F32), pltpu.VMEM((s, dvw), F32), pltpu.VMEM((s, dvw), F32)],
        compiler_params=_params(("parallel",)),
        name="gla",
    )(gq, gk, cum, gv, gr, g_norm)


def _odd_proj_kernel(x_ref, w_ref, gq_ref, gk_ref, avg_ref, cos_ref, sin_ref, q_ref, k_ref, v_ref):
    h = _dot(x_ref[...].astype(BF16), w_ref[...])
    cos = cos_ref[...]
    sin = sin_ref[...]
    avg = avg_ref[...]
    nq = GQA_HEADS * GQA_HEAD_DIM
    nk = GQA_KV_HEADS * GQA_HEAD_DIM

    def norm_rope(t, g):
        ms = _dot_x_sel(t * t, avg)
        t = t * lax.rsqrt(ms + EPS) * g
        return t * cos + _rot_half(t, GQA_HEAD_DIM // 2) * sin

    for c in range(nq // nk):
        sl = slice(c * nk, (c + 1) * nk)
        q_ref[:, sl] = (norm_rope(h[:, sl], gq_ref[...]) * (GQA_HEAD_DIM ** -0.5 * LOG2_E)).astype(q_ref.dtype)
    k_ref[...] = norm_rope(h[:, nq:nq + nk], gk_ref[...]).astype(k_ref.dtype)
    v_ref[...] = h[:, nq + nk:].astype(v_ref.dtype)


def _odd_proj(x2, w, gq, gk, avg, cos, sin, seq, tm):
    t, d = x2.shape
    nsb = seq // tm
    row = lambda i: (i, 0)
    fixed = lambda i: (0, 0)
    pos = lambda i: (i % nsb, 0)
    full = lambda a: pl.BlockSpec(a.shape, fixed)
    outs = [ODD_SPLITS[0], ODD_SPLITS[1], ODD_SPLITS[2]]
    return pl.pallas_call(
        _odd_proj_kernel,
        grid=(t // tm,),
        in_specs=[pl.BlockSpec((tm, d), row), full(w), full(gq), full(gk), full(avg),
                  pl.BlockSpec((tm, cos.shape[1]), pos), pl.BlockSpec((tm, sin.shape[1]), pos)],
        out_specs=[pl.BlockSpec((tm, n), row) for n in outs],
        out_shape=[jax.ShapeDtypeStruct((t, n), BF16) for n in outs],
        compiler_params=_params(("parallel",)),
        name="odd_proj",
    )(x2, w, gq, gk, avg, cos, sin)


def _gqa_attn_kernel(q_ref, k_ref, v_ref, o_ref, kp_ref, vt_ref, s_ref, e_ref):
    j = pl.program_id(1)
    group = GQA_HEADS // GQA_KV_HEADS
    width = GQA_KV_HEADS * GQA_HEAD_DIM
    pair = 2 * GQA_HEAD_DIM

    @pl.when(pl.program_id(2) == 0)
    def _():
        r = lax.broadcasted_iota(jnp.int32, (width, pair), 0)
        c = lax.broadcasted_iota(jnp.int32, (width, pair), 1)
        twice = (r // GQA_HEAD_DIM == j) & (r % GQA_HEAD_DIM == c % GQA_HEAD_DIM)
        kp_ref[...] = _dot(k_ref[0], jnp.where(twice, 1.0, 0.0).astype(BF16)).astype(BF16)
        rv = lax.broadcasted_iota(jnp.int32, (GQA_HEAD_DIM, width), 0)
        cv = lax.broadcasted_iota(jnp.int32, (GQA_HEAD_DIM, width), 1)
        mine = cv == j * GQA_HEAD_DIM + rv
        vt_ref[...] = _values_t(_dot_nt(jnp.where(mine, 1.0, 0.0).astype(BF16), v_ref[0]))

    lane = lax.broadcasted_iota(jnp.int32, (q_ref.shape[1], pair), 1)

    def q_of(g):
        qp = q_ref[0, :, (g // 2) * pair:(g // 2 + 1) * pair]
        return jnp.where(lane // GQA_HEAD_DIM == g % 2, qp, jnp.zeros_like(qp))

    out_t = _attend_heads(group, lambda g, keys: kp_ref[keys, :], q_of, lambda g, keys: vt_ref[:, keys],
                          s_ref, e_ref)
    o_ref[0] = out_t.T.astype(o_ref.dtype)


def _gqa_attn(q, k, v, tq):
    b, s, _ = q.shape
    width = GQA_KV_HEADS * GQA_HEAD_DIM
    group = GQA_HEADS // GQA_KV_HEADS
    return pl.pallas_call(
        _gqa_attn_kernel,
        grid=(b, GQA_KV_HEADS, s // tq),
        in_specs=[pl.BlockSpec((1, tq, width), lambda i, j, t: (i, t, j)),
                  pl.BlockSpec((1, s, width), lambda i, j, t: (i, 0, 0)),
                  pl.BlockSpec((1, s, width), lambda i, j, t: (i, 0, 0))],
        out_specs=pl.BlockSpec((1, tq, width), lambda i, j, t: (i, t, j)),
        out_shape=jax.ShapeDtypeStruct((b, s, GQA_HEADS * GQA_HEAD_DIM), BF16),
        scratch_shapes=[pltpu.VMEM((s, 2 * GQA_HEAD_DIM), BF16),
                        pltpu.VMEM((GQA_HEAD_DIM + VT_ROWS_PAD, s), BF16),
                        pltpu.VMEM((group, s, tq), F32), pltpu.VMEM((group, s, tq), BF16)],
        compiler_params=_params(("parallel", "parallel", "arbitrary")),
        name="gqa_attn",
    )(q, k, v)


def _out_ln_kernel(alpha, x_ref, ma_ref, mb_ref, wa_ref, wb_ref, g_ref, b_ref, y_ref, yt_ref):
    y = alpha * x_ref[...] + _dot(ma_ref[...].astype(BF16), wa_ref[...]) \
        + _dot(mb_ref[...].astype(BF16), wb_ref[...])
    y = _layer_norm(y, g_ref[...], b_ref[...])
    y_ref[...] = y
    yt_ref[0] = y.T.astype(BF16)


def _out_ln(x2, mix_a, mix_b, col_a, col_b, w_a, w_b, g, b, alpha, seq, tm):
    t, d = x2.shape
    half = w_a.shape[0]
    nsb = seq // tm
    row = lambda i: (i, 0)
    fixed = lambda i: (0, 0)
    return pl.pallas_call(
        functools.partial(_out_ln_kernel, alpha),
        grid=(t // tm,),
        in_specs=[pl.BlockSpec((tm, d), row),
                  pl.BlockSpec((tm, half), lambda i: (i, col_a)),
                  pl.BlockSpec((tm, half), lambda i: (i, col_b)),
                  pl.BlockSpec(w_a.shape, fixed), pl.BlockSpec(w_b.shape, fixed),
                  pl.BlockSpec(g.shape, fixed), pl.BlockSpec(b.shape, fixed)],
        out_specs=[pl.BlockSpec((tm, d), row),
                   pl.BlockSpec((1, d, tm), lambda i: (i // nsb, 0, i % nsb))],
        out_shape=[jax.ShapeDtypeStruct((t, d), F32), jax.ShapeDtypeStruct((t // seq, d, seq), BF16)],
        compiler_params=_params(("parallel",)),
        name="out_ln",
    )(x2, mix_a, mix_b, w_a, w_b, g, b)


def _route_kernel(cap, x_ref, rw_ref, aff_ref, slot_ref):
    x = x_ref[0]
    seq = x.shape[0]
    xh, xm = _split2(x)
    wh, wm = _split2(rw_ref[...])
    logits = _dot_nt(wh, xh) + (_dot_nt(wh, xm) + _dot_nt(wm, xh))
    e = jnp.exp(logits - jnp.max(logits, axis=0, keepdims=True))
    aff = e / jnp.sum(e, axis=0, keepdims=True)
    aff_ref[0] = aff

    def enough(cand):
        return jnp.sum((aff >= pltpu.bitcast(cand, F32)).astype(jnp.int32), axis=1, keepdims=True) >= cap

    top = jnp.full((aff.shape[0], 1), 1 << 30, jnp.int32)
    thr0 = jnp.where(enough(top), top, 0)

    def pick(i, thr):
        lo = 28 - 2 * i
        c1, c2, c3 = thr | (jnp.int32(1) << lo), thr | (jnp.int32(2) << lo), thr | (jnp.int32(3) << lo)
        return jnp.where(enough(c3), c3, jnp.where(enough(c2), c2, jnp.where(enough(c1), c1, thr)))

    thr = pltpu.bitcast(lax.fori_loop(0, 15, pick, thr0), F32)
    above = aff > thr
    tie = aff == thr
    need = cap - jnp.sum(above.astype(jnp.int32), axis=1, keepdims=True)

    blk = 256 if seq % 256 == 0 else V7X_LANES
    r = lax.broadcasted_iota(jnp.int32, (blk, blk), 0)
    c = lax.broadcasted_iota(jnp.int32, (blk, blk), 1)
    before = jnp.where(r < c, 1.0, 0.0).astype(BF16)

    def prefix(mask):
        m = jnp.where(mask, 1.0, 0.0).astype(BF16)
        run = jnp.zeros((mask.shape[0], 1), F32)
        parts = []
        for t in range(seq // blk):
            mb = m[:, t * blk:(t + 1) * blk]
            parts.append(_dot(mb, before) + run)
            run = run + jnp.sum(mb.astype(F32), axis=1, keepdims=True)
        return jnp.concatenate(parts, axis=1).astype(jnp.int32)

    chosen = above | (tie & (prefix(tie) < need))
    slot_ref[0] = jnp.where(chosen, prefix(chosen), -1)


def _route(x1, rw_t, cap):
    b, s, d = x1.shape
    e = rw_t.shape[0]
    return pl.pallas_call(
        functools.partial(_route_kernel, cap),
        grid=(b,),
        in_specs=[pl.BlockSpec((1, s, d), lambda i: (i, 0, 0)), pl.BlockSpec((e, d), lambda i: (0, 0))],
        out_specs=[pl.BlockSpec((1, e, s), lambda i: (i, 0, 0)), pl.BlockSpec((1, e, s), lambda i: (i, 0, 0))],
        out_shape=[jax.ShapeDtypeStruct((b, e, s), F32), jax.ShapeDtypeStruct((b, e, s), jnp.int32)],
        compiler_params=_params(("parallel",)),
        name="route",
    )(x1, rw_t)


def _moe_kernel(cap, ts, xt_ref, slot_ref, aff_ref, w1_ref, w3_ref, w2_ref, o_ref):
    e_id = pl.program_id(1)
    seq = xt_ref.shape[2]
    ff = w2_ref.shape[2]

    @pl.when(e_id == 0)
    def _():
        o_ref[...] = jnp.zeros_like(o_ref)

    c_row = lax.broadcasted_iota(jnp.int32, (cap, seq), 0)
    pick = jnp.where(slot_ref[0, 0] == c_row, 1.0, 0.0).astype(BF16)
    hd2 = xt_ref.shape[1] // 2
    xg = jnp.concatenate([_dot_nt(xt_ref[0, :hd2, :], pick), _dot_nt(xt_ref[0, hd2:, :], pick)],
                         axis=0).astype(BF16)
    h1 = _dot(w1_ref[0], xg)
    h3 = _dot(w3_ref[0], xg)
    hid = (h1 * _sigmoid(h1) * h3).astype(BF16)
    ye = jnp.concatenate([_dot(w2_ref[0, :hd2, :], hid), _dot(w2_ref[0, hd2:, :], hid)],
                         axis=0).astype(BF16)
    for t in range(seq // ts):
        cols = slice(t * ts, (t + 1) * ts)
        o_ref[0, :, cols] += _dot(ye, pick[:, cols]) * aff_ref[0, 0, :, cols]


def _moe(xt, slot, aff, w1t, w3t, w2t, cap, layer):
    b, d, s = xt.shape
    e = slot.shape[1]
    ff = w2t.shape[2]
    ts = min(SCATTER_BLOCK, s)
    w_blk = lambda i, j: (layer * e + j, 0, 0)
    return pl.pallas_call(
        functools.partial(_moe_kernel, cap, ts),
        grid=(b, e),
        in_specs=[pl.BlockSpec((1, d, s), lambda i, j: (i, 0, 0)),
                  pl.BlockSpec((1, 1, 1, s), lambda i, j: (i, j, 0, 0)),
                  pl.BlockSpec((1, 1, 1, s), lambda i, j: (i, j, 0, 0)),
                  pl.BlockSpec((1, ff, d), w_blk), pl.BlockSpec((1, ff, d), w_blk),
                  pl.BlockSpec((1, d, ff), w_blk)],
        out_specs=pl.BlockSpec((1, d, s), lambda i, j: (i, 0, 0)),
        out_shape=jax.ShapeDtypeStruct((b, d, s), F32),
        compiler_params=_params(("parallel", "arbitrary")),
        name="moe",
    )(xt, slot.reshape(b, e, 1, s), aff.reshape(b, e, 1, s), w1t, w3t, w2t)


def _transpose_cast_kernel(w_ref, o_ref):
    o_ref[0] = w_ref[0].T.astype(o_ref.dtype)


def _transpose_cast(w):
    n, r, c = w.shape
    return pl.pallas_call(
        _transpose_cast_kernel,
        grid=(n,),
        in_specs=[pl.BlockSpec((1, r, c), lambda i: (i, 0, 0))],
        out_specs=pl.BlockSpec((1, c, r), lambda i: (i, 0, 0)),
        out_shape=jax.ShapeDtypeStruct((n, c, r), BF16),
        compiler_params=_params(("parallel",)),
        name="transpose_cast",
    )(w)


def _ple_ln_kernel(alpha, x_ref, f_ref, p_ref, wg_ref, bg_ref, wp_ref, g_ref, b_ref, y_ref):
    x = x_ref[...]
    gate = _sigmoid(_dot(x.astype(BF16), wg_ref[...]) + bg_ref[...])
    ple = gate * _dot(p_ref[...].astype(BF16), wp_ref[...])
    ffn = f_ref[0].T
    y_ref[...] = _layer_norm(alpha * x + ffn + ple, g_ref[...], b_ref[...])


def _ple_ln(x1, ffn_t, p2, wg, bg, wp, g, b, alpha, tm, layer):
    t, d = x1.shape
    nsb = ffn_t.shape[2] // tm
    row = lambda i: (i, 0)
    p_row = lambda i: (layer * (t // tm) + i, 0)
    fixed = lambda i: (0, 0)
    full = lambda a: pl.BlockSpec(a.shape, fixed)
    return pl.pallas_call(
        functools.partial(_ple_ln_kernel, alpha),
        grid=(t // tm,),
        in_specs=[pl.BlockSpec((tm, d), row),
                  pl.BlockSpec((1, d, tm), lambda i: (i // nsb, 0, i % nsb)),
                  pl.BlockSpec((tm, p2.shape[1]), p_row), full(wg), full(bg), full(wp), full(g), full(b)],
        out_specs=pl.BlockSpec((tm, d), row),
        out_shape=jax.ShapeDtypeStruct((t, d), F32),
        compiler_params=_params(("parallel",)),
        name="ple_ln",
    )(x1, ffn_t, p2, wg, bg, wp, g, b)


def _rope_tables(seq, rot_dim, lo, width):
    rows = seq // GRID_W
    row = jnp.repeat(jnp.arange(rows, dtype=F32), GRID_W)
    col = jnp.tile(jnp.arange(GRID_W, dtype=F32), rows)
    axis_dim = rot_dim // 2
    inv = ROPE_THETA ** (-jnp.arange(0, axis_dim, 2, dtype=F32) / axis_dim)
    ang = jnp.concatenate([row[:, None] * inv, col[:, None] * inv], axis=-1)
    cos, sin = jnp.cos(ang), jnp.sin(ang)
    cos2 = jnp.concatenate([cos, cos], axis=-1)
    sin2 = jnp.concatenate([-sin, sin], axis=-1)
    if lo == 0:
        reps = width // rot_dim
        return jnp.tile(cos2, (1, reps)), jnp.tile(sin2, (1, reps))
    pad_l = jnp.ones((seq, lo), F32)
    pad_r = jnp.ones((seq, width - lo - rot_dim), F32)
    cos_t = jnp.concatenate([pad_l, cos2, pad_r], axis=-1)
    sin_t = jnp.concatenate([0 * pad_l, sin2, 0 * pad_r], axis=-1)
    return cos_t, sin_t


def _prep_even(w_in, w_uq, w_ukv, gw_f, gb_f, gw_b, gb_b):
    d = w_in.shape[0]
    offs = np.cumsum(EVEN_SPLITS)[:-1].tolist()
    c_q, c_kv, k_pe, gq, gk, gv, g_lr, gr = jnp.split(w_in, offs, axis=-1)
    z32 = jnp.zeros((d, 32), w_in.dtype)
    chunk = jnp.concatenate([g_lr, z32, k_pe, z32], axis=-1)
    w_in_p = jnp.concatenate([c_q, c_kv, chunk, gq, gk, gv, gr], axis=-1).astype(BF16)
    uq = w_uq.reshape(MLA_Q_LORA, MLA_HEADS, MLA_NOPE + MLA_ROPE)
    uq = jnp.pad(uq, ((0, 0), (0, 0), (0, MLA_PAD - MLA_NOPE - MLA_ROPE)))
    w_uq_p = uq.reshape(MLA_Q_LORA, MLA_HEADS * MLA_PAD).astype(BF16)
    ukv = w_ukv.reshape(MLA_KV_LORA, MLA_HEADS, MLA_NOPE + MLA_V)
    uk = jnp.pad(ukv[:, :, :MLA_NOPE], ((0, 0), (0, 0), (0, MLA_PAD - MLA_NOPE)))
    w_uk_p = uk.reshape(MLA_KV_LORA, MLA_HEADS * MLA_PAD).astype(BF16)
    w_uv = ukv[:, :, MLA_NOPE:].reshape(MLA_KV_LORA, MLA_HEADS * MLA_V).astype(BF16)
    dkw = GLA_HEADS * GLA_DK
    w_gate = jnp.zeros((MLA_PAD, 2 * dkw), F32)
    w_gate = w_gate.at[0:GLA_GATE_RANK, 0:dkw].set(gw_f)
    w_gate = w_gate.at[GLA_GATE_RANK:2 * GLA_GATE_RANK, dkw:].set(gw_b).astype(BF16)
    b_gate = jnp.concatenate([gb_f, gb_b])[None, :]
    return w_in_p, w_uq_p, w_uk_p, w_uv, w_gate, b_gate


def kernel(x, p, w_in_even, mla_q_norm, w_uq, mla_kv_norm, w_ukv, gla_gate_w_fwd, gla_gate_b_fwd,
           gla_gate_w_bwd, gla_gate_b_bwd, gla_norm, w_in_odd, gqa_q_norm, gqa_k_norm, w_o, ln1_g,
           ln1_b, router_w, w1, w3, w2, ple_gate_w, ple_gate_b, ple_w, ln2_g, ln2_b):
    b, s, d = x.shape
    depth = w_o.shape[0]
    t = b * s
    alpha = (2.0 * depth) ** 0.25
    cap = EC_CAPACITY_FACTOR * s // N_EXPERTS
    tm = min(ROW_BLOCK, s)
    tq = min(QUERY_BLOCK, s)
    assert s % tm == 0 and s % tq == 0 and s % GRID_W == 0 and tm % (2 * GLA_CHUNK) == 0
    assert (s // KEY_CHUNKS) % KEY_BLK == 0 and tq % V7X_LANES == 0 and s % min(SCATTER_BLOCK, s) == 0
    assert cap % V7X_BF16_SUBLANES == 0 and w1.shape[1] == N_EXPERTS and router_w.shape[2] == N_EXPERTS
    half = w_o.shape[1] // 2
    cos_a, sin_a = _rope_tables(s, MLA_ROPE, PE_LO, MLA_PAD)
    cos_c, sin_c = _rope_tables(s, GQA_HEAD_DIM, 0, GQA_KV_HEADS * GQA_HEAD_DIM)
    hw = GQA_KV_HEADS * GQA_HEAD_DIM
    head_of = np.arange(hw) // GQA_HEAD_DIM
    avg = jnp.asarray((head_of[:, None] == head_of[None, :]) / GQA_HEAD_DIM, BF16)

    n_e, ff = w1.shape[1], w1.shape[3]
    w1t = _transpose_cast(w1.reshape(depth * n_e, d, ff))
    w3t = _transpose_cast(w3.reshape(depth * n_e, d, ff))
    w2t = _transpose_cast(w2.reshape(depth * n_e, ff, d))
    p_all = p.reshape(depth * t, p.shape[-1])

    x2 = x.reshape(t, d)
    for i in range(depth):
        j = i // 2
        if i % 2 == 0:
            w_in_p, w_uq_p, w_uk_p, w_uv, w_gate, b_gate = _prep_even(
                w_in_even[j], w_uq[j], w_ukv[j], gla_gate_w_fwd[j], gla_gate_b_fwd[j],
                gla_gate_w_bwd[j], gla_gate_b_bwd[j])
            q, k, v, gq, gk, gv, la, gr = _even_proj(
                x2, w_in_p, mla_q_norm[j][None, :], w_uq_p, mla_kv_norm[j][None, :], w_uk_p, w_uv,
                w_gate, b_gate, cos_a, sin_a, s, tm)
            r3 = lambda a: a.reshape(b, s, a.shape[-1])
            o_mla = _mla_attn(r3(q), r3(k), r3(v), tq).reshape(t, -1)
            o_gla = _gla(r3(gq), r3(gk), r3(la), r3(gv), r3(gr), gla_norm[j][None, :]).reshape(t, -1)
            mix_a, mix_b, col_a, col_b = o_mla, o_gla, 0, 0
        else:
            q, k, v = _odd_proj(x2, w_in_odd[j].astype(BF16),
                                jnp.tile(gqa_q_norm[j], GQA_KV_HEADS)[None, :],
                                jnp.tile(gqa_k_norm[j], GQA_KV_HEADS)[None, :],
                                avg, cos_c, sin_c, s, tm)
            r3 = lambda a: a.reshape(b, s, a.shape[-1])
            o = _gqa_attn(r3(q), r3(k), r3(v), tq).reshape(t, -1)
            mix_a, mix_b, col_a, col_b = o, o, 0, 1
        wo = w_o[i].astype(BF16)
        x1, x1t = _out_ln(x2, mix_a, mix_b, col_a, col_b, wo[:half], wo[half:],
                          ln1_g[i][None, :], ln1_b[i][None, :], alpha, s, tm)
        aff, slot = _route(x1.reshape(b, s, d), router_w[i].T, cap)
        ffn_t = _moe(x1t, slot, aff, w1t, w3t, w2t, cap, i)
        x2 = _ple_ln(x1, ffn_t, p_all, ple_gate_w[i].astype(BF16),
                     ple_gate_b[i][None, :], ple_w[i].astype(BF16), ln2_g[i][None, :], ln2_b[i][None, :],
                     alpha, tm, i)
    return x2.reshape(b, s, d)
```

```python
import functools
import math

import jax
import jax.numpy as jnp
import numpy as np
from jax import lax
from jax.experimental import pallas as pl
from jax.experimental.pallas import tpu as pltpu

F32 = jnp.float32
BF16 = jnp.bfloat16

V7X_LANES = 128
V7X_BF16_SUBLANES = 16
V7X_VMEM_BYTES = 64 * 1024 * 1024
VMEM_LIMIT = V7X_VMEM_BYTES - 8 * 1024 * 1024

ROW_BLOCK = 1024
QUERY_BLOCK = 512
SCATTER_BLOCK = 512

GRID_W = 64
ROPE_THETA = 10000.0
EPS = 1e-6
MLA_HEADS, MLA_Q_LORA, MLA_KV_LORA = 8, 256, 128
MLA_NOPE, MLA_ROPE, MLA_V = 64, 32, 64
GLA_HEADS, GLA_DK, GLA_DV = 4, 64, 128
GLA_GATE_RANK, GLA_GATE_NORM, GLA_CHUNK = 16, 16.0, 64
GQA_HEADS, GQA_KV_HEADS, GQA_HEAD_DIM = 16, 4, 64
N_EXPERTS, EC_CAPACITY_FACTOR = 16, 2
EVEN_SPLITS = (MLA_Q_LORA, MLA_KV_LORA, MLA_ROPE, GLA_HEADS * GLA_DK, GLA_HEADS * GLA_DK,
               GLA_HEADS * GLA_DV, 2 * GLA_GATE_RANK, GLA_HEADS * GLA_DV)
ODD_SPLITS = (GQA_HEADS * GQA_HEAD_DIM, GQA_KV_HEADS * GQA_HEAD_DIM, GQA_KV_HEADS * GQA_HEAD_DIM)
MLA_PAD = V7X_LANES
PE_LO, PE_HI = MLA_NOPE, MLA_NOPE + MLA_ROPE


def _dot(a, b):
    return jnp.dot(a, b, preferred_element_type=F32)


def _dot_nt(a, b):
    return lax.dot_general(a, b, (((1,), (1,)), ((), ())), preferred_element_type=F32)


def _dot_tn(a, b):
    return lax.dot_general(a, b, (((0,), (0,)), ((), ())), preferred_element_type=F32)


def _split2(x):
    hi = x.astype(BF16)
    lo = (x - hi.astype(F32)).astype(BF16)
    return hi, lo


def _dot_sel(sel, x):
    hi, lo = _split2(x)
    return _dot(sel, hi) + _dot(sel, lo)


def _dot_x_sel(x, sel):
    hi, lo = _split2(x)
    return _dot(hi, sel) + _dot(lo, sel)


def _params(sem):
    return pltpu.CompilerParams(dimension_semantics=sem, vmem_limit_bytes=VMEM_LIMIT)


def _rot_half(x, half):
    w = x.shape[-1]
    lane = lax.broadcasted_iota(jnp.int32, x.shape, x.ndim - 1)
    first = (lane % (2 * half)) < half
    return jnp.where(first, pltpu.roll(x, w - half, x.ndim - 1), pltpu.roll(x, half, x.ndim - 1))


def _layer_norm(y, g, b):
    mu = jnp.mean(y, axis=-1, keepdims=True)
    yc = y - mu
    var = jnp.mean(yc * yc, axis=-1, keepdims=True)
    return yc * lax.rsqrt(var + EPS) * g + b


def _rms(x, g):
    return x * lax.rsqrt(jnp.mean(x * x, axis=-1, keepdims=True) + EPS) * g


def _log_sigmoid(z):
    return jnp.minimum(z, 0.0) - jnp.log1p(jnp.exp(-jnp.abs(z)))


def _sigmoid(z):
    return 1.0 / (1.0 + jnp.exp(-z))


LOG2_E = math.log2(math.e)
KEY_BLK = 128


KEY_CHUNKS = 4


def _exp2_cols(m, s_scr, e_scr):
    seq, tq = s_scr.shape
    for c in range(tq // V7X_LANES):
        cols = slice(c * V7X_LANES, (c + 1) * V7X_LANES)
        m_b = jnp.broadcast_to(m[:, cols], (KEY_BLK, V7X_LANES))
        for r in range(seq // KEY_BLK):
            rows = slice(r * KEY_BLK, (r + 1) * KEY_BLK)
            e_scr[rows, cols] = jnp.exp2(s_scr[rows, cols] - m_b).astype(BF16)


VT_ROWS_PAD = V7X_BF16_SUBLANES


def _values_t(v_t):
    dv, seq = v_t.shape
    extra = jnp.where(lax.broadcasted_iota(jnp.int32, (VT_ROWS_PAD, seq), 0) == 0, 1.0, 0.0)
    return jnp.concatenate([v_t.astype(BF16), extra.astype(BF16)], axis=0)


SHIFT_SAMPLE = 128
DEN_LO, DEN_HI = 2.0 ** -90, 2.0 ** 90


def _attend_heads_one_pass(n_heads, k_of, q_of, vt_of, e_ref):
    seq = e_ref.shape[1]
    ck = seq // KEY_CHUNKS
    outs, dens = [], []
    for h in range(n_heads):
        q = q_of(h)
        shift = jnp.max(_dot_nt(k_of(h, slice(0, SHIFT_SAMPLE)), q), axis=0, keepdims=True)
        for c in range(KEY_CHUNKS):
            keys = slice(c * ck, (c + 1) * ck)
            e_ref[h, keys, :] = jnp.exp2(_dot_nt(k_of(h, keys), q) - shift).astype(BF16)
        o = _dot(vt_of(h, slice(0, seq)), e_ref[h])
        dv = o.shape[0] - VT_ROWS_PAD
        outs.append(o[:dv] * (1.0 / o[dv:dv + 1]))
        dens.append(o[dv:dv + 1])
    return jnp.concatenate(outs, axis=0), jnp.concatenate(dens, axis=0)


def _attend_heads(n_heads, k_of, q_of, vt_of, s_ref, e_ref):
    assert s_ref.shape[0] == n_heads and e_ref.shape[0] == n_heads
    seq = s_ref.shape[1]
    ck = seq // KEY_CHUNKS
    m, acc = {}, {}
    for h in range(n_heads + 2):
        m_parts = []
        for c in range(KEY_CHUNKS):
            keys = slice(c * ck, (c + 1) * ck)
            if h < n_heads:
                s = _dot_nt(k_of(h, keys), q_of(h))
                s_ref[h, keys, :] = s
                m_parts.append(jnp.max(s, axis=0, keepdims=True))
            if 1 <= h <= n_heads:
                _exp2_cols(m[h - 1], s_ref.at[h - 1, keys], e_ref.at[h - 1, keys])
            if h >= 2:
                o = _dot(vt_of(h - 2, keys), e_ref[h - 2, keys, :])
                acc[h - 2] = o if c == 0 else acc[h - 2] + o
        if h < n_heads:
            m[h] = functools.reduce(jnp.maximum, m_parts)
    outs = []
    for h in range(n_heads):
        dv = acc[h].shape[0] - VT_ROWS_PAD
        outs.append(acc[h][:dv] * (1.0 / acc[h][dv:dv + 1]))
    return jnp.concatenate(outs, axis=0)


def _attend_and_store(n_heads, k_of, q_of, vt_of, s_ref, e_ref, o_ref):
    out_t, den = _attend_heads_one_pass(n_heads, k_of, q_of, vt_of, e_ref)
    o_ref[0] = out_t.T.astype(o_ref.dtype)
    in_range = (den > DEN_LO) & (den < DEN_HI)
    n_bad = jnp.sum(jnp.where(in_range, 0.0, 1.0))

    @pl.when(n_bad > 0.0)
    def _():
        o_ref[0] = _attend_heads(n_heads, k_of, q_of, vt_of, s_ref, e_ref).T.astype(o_ref.dtype)


def _even_proj_kernel(x_ref, w_in_ref, qn_ref, w_uq_ref, kvn_ref, w_uk_ref, w_uv_ref,
                      w_gate_ref, b_gate_ref, cq_ref, sq_ref,
                      q_ref, k_ref, v_ref, gq_ref, gk_ref, gv_ref, la_ref, gr_ref):
    h = _dot(x_ref[...].astype(BF16), w_in_ref[...])
    cos = cq_ref[...]
    sin = sq_ref[...]
    lane = lax.broadcasted_iota(jnp.int32, cos.shape, 1)
    pe_lane = (lane >= PE_LO) & (lane < PE_HI)

    def rope(t):
        return t * cos + _rot_half(t, MLA_ROPE // 2) * sin

    c_q = _rms(h[:, 0:256], qn_ref[...])
    q = _dot(c_q.astype(BF16), w_uq_ref[...]) * ((MLA_NOPE + MLA_ROPE) ** -0.5 * LOG2_E)
    c_kv = _rms(h[:, 256:384], kvn_ref[...]).astype(BF16)
    kn = _dot(c_kv, w_uk_ref[...])
    v_ref[...] = _dot(c_kv, w_uv_ref[...]).astype(v_ref.dtype)
    chunk = h[:, 384:512]
    k_pe = jnp.where(pe_lane, rope(chunk), 0.0)
    for hd in range(MLA_HEADS):
        sl = slice(hd * MLA_PAD, (hd + 1) * MLA_PAD)
        q_ref[:, sl] = rope(q[:, sl]).astype(q_ref.dtype)
        k_ref[:, sl] = (kn[:, sl] + k_pe).astype(k_ref.dtype)
    z = _dot(chunk.astype(BF16), w_gate_ref[...]) + b_gate_ref[...]
    la = _log_sigmoid(z) * (1.0 / GLA_GATE_NORM)
    blk = 2 * GLA_CHUNK
    r = lax.broadcasted_iota(jnp.int32, (blk, blk), 0)
    c = lax.broadcasted_iota(jnp.int32, (blk, blk), 1)
    same = (r // GLA_CHUNK) == (c // GLA_CHUNK)
    lower = jnp.where(same & (c <= r), 1.0, 0.0).astype(BF16)
    upper = jnp.where(same & (c >= r), 1.0, 0.0).astype(BF16)
    dkw = GLA_HEADS * GLA_DK
    for t in range(la.shape[0] // blk):
        rows = slice(t * blk, (t + 1) * blk)
        la_ref[rows, :dkw] = _dot_sel(lower, la[rows, :dkw])
        la_ref[rows, dkw:] = _dot_sel(upper, la[rows, dkw:])
    gq_ref[...] = h[:, 512:768] * (GLA_DK ** -0.5)
    gk_ref[...] = h[:, 768:1024]
    gv_ref[...] = h[:, 1024:1536].astype(gv_ref.dtype)
    gr = h[:, 1536:2048]
    gr_ref[...] = gr * _sigmoid(gr)


def _even_proj(x2, w_in_p, qn, w_uq_p, kvn, w_uk_p, w_uv, w_gate, b_gate, cq, sq, seq, tm):
    t, d = x2.shape
    nsb = seq // tm
    row = lambda i: (i, 0)
    fixed = lambda i: (0, 0)
    pos = lambda i: (i % nsb, 0)
    full = lambda a: pl.BlockSpec(a.shape, fixed)
    outs = [(1024, BF16), (1024, BF16), (512, BF16), (256, F32), (256, F32), (512, BF16), (512, F32), (512, F32)]
    return pl.pallas_call(
        _even_proj_kernel,
        grid=(t // tm,),
        in_specs=[pl.BlockSpec((tm, d), row), full(w_in_p), full(qn), full(w_uq_p), full(kvn),
                  full(w_uk_p), full(w_uv), full(w_gate), full(b_gate),
                  pl.BlockSpec((tm, MLA_PAD), pos), pl.BlockSpec((tm, MLA_PAD), pos)],
        out_specs=[pl.BlockSpec((tm, n), row) for n, _ in outs],
        out_shape=[jax.ShapeDtypeStruct((t, n), dt) for n, dt in outs],
        compiler_params=_params(("parallel",)),
        name="even_proj",
    )(x2, w_in_p, qn, w_uq_p, kvn, w_uk_p, w_uv, w_gate, b_gate, cq, sq)


MLA_STEP_HEADS = 4


def _mla_attn_kernel(q_ref, k_ref, v_ref, o_ref, vt_ref, s_ref, e_ref):
    out_w = MLA_STEP_HEADS * MLA_V

    @pl.when(pl.program_id(2) == 0)
    def _():
        r = lax.broadcasted_iota(jnp.int32, (out_w, out_w), 0)
        c = lax.broadcasted_iota(jnp.int32, (out_w, out_w), 1)
        eye = jnp.where(r == c, 1.0, 0.0).astype(BF16)
        v_t = _dot_nt(eye, v_ref[0])
        for a in range(MLA_STEP_HEADS):
            vt_ref[a] = _values_t(v_t[a * MLA_V:(a + 1) * MLA_V])

    _attend_and_store(
        MLA_STEP_HEADS,
        lambda a, keys: k_ref[0, keys, a * MLA_PAD:(a + 1) * MLA_PAD],
        lambda a: q_ref[0, :, a * MLA_PAD:(a + 1) * MLA_PAD],
        lambda a, keys: vt_ref[a, :, keys],
        s_ref, e_ref, o_ref)


def _mla_attn(q, k, v, tq):
    b, s, _ = q.shape
    n = MLA_STEP_HEADS
    return pl.pallas_call(
        _mla_attn_kernel,
        grid=(b, MLA_HEADS // n, s // tq),
        in_specs=[pl.BlockSpec((1, tq, n * MLA_PAD), lambda i, j, t: (i, t, j)),
                  pl.BlockSpec((1, s, n * MLA_PAD), lambda i, j, t: (i, 0, j)),
                  pl.BlockSpec((1, s, n * MLA_V), lambda i, j, t: (i, 0, j))],
        out_specs=pl.BlockSpec((1, tq, n * MLA_V), lambda i, j, t: (i, t, j)),
        out_shape=jax.ShapeDtypeStruct((b, s, MLA_HEADS * MLA_V), BF16),
        scratch_shapes=[pltpu.VMEM((n, MLA_V + VT_ROWS_PAD, s), BF16),
                        pltpu.VMEM((n, s, tq), F32), pltpu.VMEM((n, s, tq), BF16)],
        compiler_params=_params(("parallel", "parallel", "arbitrary")),
        name="mla_attn",
    )(q, k, v)


def _gla_kernel(q_ref, k_ref, cum_ref, v_ref, gr_ref, g_ref, o_ref, state_ref, of_ref, ob_ref):
    seq = q_ref.shape[1]
    n_chunks = seq // GLA_CHUNK
    L = GLA_CHUNK
    dkw = GLA_HEADS * GLA_DK
    dvw = GLA_HEADS * GLA_DV

    def iota(shape, dim):
        return lax.broadcasted_iota(jnp.int32, shape, dim)

    row_l = iota((L, dkw), 0)
    col_m = iota((L, dkw), 1) % L
    k_own = (iota((dkw, dkw), 0) // L) == (iota((dkw, dkw), 1) // GLA_DK)
    v_own = (iota((dkw, dvw), 0) // L) == (iota((dkw, dvw), 1) // GLA_DV)
    s_own = (iota((dvw, dkw), 0) // GLA_DV) == (iota((dvw, dkw), 1) // GLA_DK)
    state_ref[...] = jnp.zeros_like(state_ref)

    def body(i, carry):
        for d, fwd in enumerate((True, False)):
            n = i if fwd else n_chunks - 1 - i
            rows = pl.ds(pl.multiple_of(n * L, L), L)
            cum = cum_ref[0, rows, d * dkw:(d + 1) * dkw]
            last = cum[L - 1:L, :] if fwd else cum[0:1, :]
            q = q_ref[0, rows, :]
            k = k_ref[0, rows, :]
            v = v_ref[0, rows, :]
            qe = (q * jnp.exp(cum)).astype(BF16)
            kg = (k * jnp.exp(-cum)).astype(BF16)
            kdec = (k * jnp.exp(last - cum)).astype(BF16)
            k_blk = jnp.where(k_own, jnp.concatenate([kg] * GLA_HEADS, axis=0), jnp.zeros((), BF16))
            att = _dot_nt(qe, k_blk)
            keep = (col_m <= row_l) if fwd else (col_m >= row_l)
            att = jnp.where(keep, att, 0.0).astype(BF16)
            v_blk = jnp.where(v_own, jnp.concatenate([v] * GLA_HEADS, axis=0), jnp.zeros((), BF16))
            st = state_ref[d]
            o = _dot(att, v_blk) + _dot_nt(qe, st.astype(BF16))
            state_ref[d] = st * jnp.exp(last) + jnp.where(s_own, _dot_tn(v, kdec), 0.0)
            if fwd:
                of_ref[rows, :] = o
            else:
                ob_ref[rows, :] = o
        return carry

    lax.fori_loop(0, n_chunks, body, 0, unroll=4)

    def finish(n, carry):
        rows = pl.ds(pl.multiple_of(n * L, L), L)
        tot = of_ref[rows, :] + ob_ref[rows, :]
        for hd in range(GLA_HEADS):
            sl = slice(hd * GLA_DV, (hd + 1) * GLA_DV)
            o_ref[0, rows, sl] = (_rms(tot[:, sl], g_ref[...]) * gr_ref[0, rows, sl]).astype(o_ref.dtype)
        return carry

    lax.fori_loop(0, n_chunks, finish, 0)


def _gla(gq, gk, cum, gv, gr, g_norm):
    b, s, _ = gq.shape
    dkw = GLA_HEADS * GLA_DK
    dvw = GLA_HEADS * GLA_DV
    blk = lambda w: pl.BlockSpec((1, s, w), lambda i: (i, 0, 0))
    return pl.pallas_call(
        _gla_kernel,
        grid=(b,),
        in_specs=[blk(dkw), blk(dkw), blk(2 * dkw), blk(dvw), blk(dvw),
                  pl.BlockSpec((1, GLA_DV), lambda i: (0, 0))],
        out_specs=blk(dvw),
        out_shape=jax.ShapeDtypeStruct((b, s, dvw), BF16),
        scratch_shapes=[pltpu.VMEM((2, dvw, dkw), F32), pltpu.VMEM((s, dvw), F32), pltpu.VMEM((s, dvw), F32)],
        compiler_params=_params(("parallel",)),
        name="gla",
    )(gq, gk, cum, gv, gr, g_norm)


def _odd_proj_kernel(x_ref, w_ref, gq_ref, gk_ref, avg_ref, cos_ref, sin_ref, q_ref, k_ref, v_ref):
    h = _dot(x_ref[...].astype(BF16), w_ref[...])
    cos = cos_ref[...]
    sin = sin_ref[...]
    avg = avg_ref[...]
    nq = GQA_HEADS * GQA_HEAD_DIM
    nk = GQA_KV_HEADS * GQA_HEAD_DIM

    def norm_rope(t, g):
        ms = _dot_x_sel(t * t, avg)
        t = t * lax.rsqrt(ms + EPS) * g
        return t * cos + _rot_half(t, GQA_HEAD_DIM // 2) * sin

    for c in range(nq // nk):
        sl = slice(c * nk, (c + 1) * nk)
        q_ref[:, sl] = (norm_rope(h[:, sl], gq_ref[...]) * (GQA_HEAD_DIM ** -0.5 * LOG2_E)).astype(q_ref.dtype)
    k_ref[...] = norm_rope(h[:, nq:nq + nk], gk_ref[...]).astype(k_ref.dtype)
    v_ref[...] = h[:, nq + nk:].astype(v_ref.dtype)


def _odd_proj(x2, w, gq, gk, avg, cos, sin, seq, tm):
    t, d = x2.shape
    nsb = seq // tm
    row = lambda i: (i, 0)
    fixed = lambda i: (0, 0)
    pos = lambda i: (i % nsb, 0)
    full = lambda a: pl.BlockSpec(a.shape, fixed)
    outs = [ODD_SPLITS[0], ODD_SPLITS[1], ODD_SPLITS[2]]
    return pl.pallas_call(
        _odd_proj_kernel,
        grid=(t // tm,),
        in_specs=[pl.BlockSpec((tm, d), row), full(w), full(gq), full(gk), full(avg),
                  pl.BlockSpec((tm, cos.shape[1]), pos), pl.BlockSpec((tm, sin.shape[1]), pos)],
        out_specs=[pl.BlockSpec((tm, n), row) for n in outs],
        out_shape=[jax.ShapeDtypeStruct((t, n), BF16) for n in outs],
        compiler_params=_params(("parallel",)),
        name="odd_proj",
    )(x2, w, gq, gk, avg, cos, sin)


def _gqa_attn_kernel(q_ref, k_ref, v_ref, o_ref, kp_ref, vt_ref, s_ref, e_ref):
    j = pl.program_id(1)
    group = GQA_HEADS // GQA_KV_HEADS
    width = GQA_KV_HEADS * GQA_HEAD_DIM
    pair = 2 * GQA_HEAD_DIM

    @pl.when(pl.program_id(2) == 0)
    def _():
        r = lax.broadcasted_iota(jnp.int32, (width, pair), 0)
        c = lax.broadcasted_iota(jnp.int32, (width, pair), 1)
        twice = (r // GQA_HEAD_DIM == j) & (r % GQA_HEAD_DIM == c % GQA_HEAD_DIM)
        kp_ref[...] = _dot(k_ref[0], jnp.where(twice, 1.0, 0.0).astype(BF16)).astype(BF16)
        rv = lax.broadcasted_iota(jnp.int32, (GQA_HEAD_DIM, width), 0)
        cv = lax.broadcasted_iota(jnp.int32, (GQA_HEAD_DIM, width), 1)
        mine = cv == j * GQA_HEAD_DIM + rv
        vt_ref[...] = _values_t(_dot_nt(jnp.where(mine, 1.0, 0.0).astype(BF16), v_ref[0]))

    lane = lax.broadcasted_iota(jnp.int32, (q_ref.shape[1], pair), 1)

    def q_of(g):
        qp = q_ref[0, :, (g // 2) * pair:(g // 2 + 1) * pair]
        return jnp.where(lane // GQA_HEAD_DIM == g % 2, qp, jnp.zeros_like(qp))

    _attend_and_store(group, lambda g, keys: kp_ref[keys, :], q_of, lambda g, keys: vt_ref[:, keys],
                      s_ref, e_ref, o_ref)


def _gqa_attn(q, k, v, tq):
    b, s, _ = q.shape
    width = GQA_KV_HEADS * GQA_HEAD_DIM
    group = GQA_HEADS // GQA_KV_HEADS
    return pl.pallas_call(
        _gqa_attn_kernel,
        grid=(b, GQA_KV_HEADS, s // tq),
        in_specs=[pl.BlockSpec((1, tq, width), lambda i, j, t: (i, t, j)),
                  pl.BlockSpec((1, s, width), lambda i, j, t: (i, 0, 0)),
                  pl.BlockSpec((1, s, width), lambda i, j, t: (i, 0, 0))],
        out_specs=pl.BlockSpec((1, tq, width), lambda i, j, t: (i, t, j)),
        out_shape=jax.ShapeDtypeStruct((b, s, GQA_HEADS * GQA_HEAD_DIM), BF16),
        scratch_shapes=[pltpu.VMEM((s, 2 * GQA_HEAD_DIM), BF16),
                        pltpu.VMEM((GQA_HEAD_DIM + VT_ROWS_PAD, s), BF16),
                        pltpu.VMEM((group, s, tq), F32), pltpu.VMEM((group, s, tq), BF16)],
        compiler_params=_params(("parallel", "parallel", "arbitrary")),
        name="gqa_attn",
    )(q, k, v)


def _out_ln_kernel(alpha, x_ref, ma_ref, mb_ref, wa_ref, wb_ref, g_ref, b_ref, y_ref, yt_ref):
    y = alpha * x_ref[...] + _dot(ma_ref[...].astype(BF16), wa_ref[...]) \
        + _dot(mb_ref[...].astype(BF16), wb_ref[...])
    y = _layer_norm(y, g_ref[...], b_ref[...])
    y_ref[...] = y
    yt_ref[0] = y.T.astype(BF16)


def _out_ln(x2, mix_a, mix_b, col_a, col_b, w_a, w_b, g, b, alpha, seq, tm):
    t, d = x2.shape
    half = w_a.shape[0]
    nsb = seq // tm
    row = lambda i: (i, 0)
    fixed = lambda i: (0, 0)
    return pl.pallas_call(
        functools.partial(_out_ln_kernel, alpha),
        grid=(t // tm,),
        in_specs=[pl.BlockSpec((tm, d), row),
                  pl.BlockSpec((tm, half), lambda i: (i, col_a)),
                  pl.BlockSpec((tm, half), lambda i: (i, col_b)),
                  pl.BlockSpec(w_a.shape, fixed), pl.BlockSpec(w_b.shape, fixed),
                  pl.BlockSpec(g.shape, fixed), pl.BlockSpec(b.shape, fixed)],
        out_specs=[pl.BlockSpec((tm, d), row),
                   pl.BlockSpec((1, d, tm), lambda i: (i // nsb, 0, i % nsb))],
        out_shape=[jax.ShapeDtypeStruct((t, d), F32), jax.ShapeDtypeStruct((t // seq, d, seq), BF16)],
        compiler_params=_params(("parallel",)),
        name="out_ln",
    )(x2, mix_a, mix_b, w_a, w_b, g, b)


def _route_kernel(cap, x_ref, rw_ref, aff_ref, slot_ref):
    x = x_ref[0]
    seq = x.shape[0]
    xh, xm = _split2(x)
    wh, wm = _split2(rw_ref[...])
    logits = _dot_nt(wh, xh) + (_dot_nt(wh, xm) + _dot_nt(wm, xh))
    e = jnp.exp(logits - jnp.max(logits, axis=0, keepdims=True))
    aff = e / jnp.sum(e, axis=0, keepdims=True)
    aff_ref[0] = aff

    def enough(cand):
        return jnp.sum((aff >= pltpu.bitcast(cand, F32)).astype(jnp.int32), axis=1, keepdims=True) >= cap

    top = jnp.full((aff.shape[0], 1), 1 << 30, jnp.int32)
    thr0 = jnp.where(enough(top), top, 0)

    def pick(i, thr):
        lo = 28 - 2 * i
        c1, c2, c3 = thr | (jnp.int32(1) << lo), thr | (jnp.int32(2) << lo), thr | (jnp.int32(3) << lo)
        return jnp.where(enough(c3), c3, jnp.where(enough(c2), c2, jnp.where(enough(c1), c1, thr)))

    thr = pltpu.bitcast(lax.fori_loop(0, 15, pick, thr0), F32)
    above = aff > thr
    tie = aff == thr
    need = cap - jnp.sum(above.astype(jnp.int32), axis=1, keepdims=True)

    blk = 256 if seq % 256 == 0 else V7X_LANES
    r = lax.broadcasted_iota(jnp.int32, (blk, blk), 0)
    c = lax.broadcasted_iota(jnp.int32, (blk, blk), 1)
    before = jnp.where(r < c, 1.0, 0.0).astype(BF16)

    def prefix(mask):
        m = jnp.where(mask, 1.0, 0.0).astype(BF16)
        run = jnp.zeros((mask.shape[0], 1), F32)
        parts = []
        for t in range(seq // blk):
            mb = m[:, t * blk:(t + 1) * blk]
            parts.append(_dot(mb, before) + run)
            run = run + jnp.sum(mb.astype(F32), axis=1, keepdims=True)
        return jnp.concatenate(parts, axis=1).astype(jnp.int32)

    chosen = above | (tie & (prefix(tie) < need))
    slot_ref[0] = jnp.where(chosen, prefix(chosen), -1)


def _route(x1, rw_t, cap):
    b, s, d = x1.shape
    e = rw_t.shape[0]
    return pl.pallas_call(
        functools.partial(_route_kernel, cap),
        grid=(b,),
        in_specs=[pl.BlockSpec((1, s, d), lambda i: (i, 0, 0)), pl.BlockSpec((e, d), lambda i: (0, 0))],
        out_specs=[pl.BlockSpec((1, e, s), lambda i: (i, 0, 0)), pl.BlockSpec((1, e, s), lambda i: (i, 0, 0))],
        out_shape=[jax.ShapeDtypeStruct((b, e, s), F32), jax.ShapeDtypeStruct((b, e, s), jnp.int32)],
        compiler_params=_params(("parallel",)),
        name="route",
    )(x1, rw_t)


def _moe_kernel(cap, ts, xt_ref, slot_ref, aff_ref, w1_ref, w3_ref, w2_ref, o_ref):
    e_id = pl.program_id(1)
    seq = xt_ref.shape[2]
    ff = w2_ref.shape[2]

    @pl.when(e_id == 0)
    def _():
        o_ref[...] = jnp.zeros_like(o_ref)

    c_row = lax.broadcasted_iota(jnp.int32, (cap, seq), 0)
    pick = jnp.where(slot_ref[0, 0] == c_row, 1.0, 0.0).astype(BF16)
    hd2 = xt_ref.shape[1] // 2
    xg = jnp.concatenate([_dot_nt(xt_ref[0, :hd2, :], pick), _dot_nt(xt_ref[0, hd2:, :], pick)],
                         axis=0).astype(BF16)
    h1 = _dot(w1_ref[0], xg)
    h3 = _dot(w3_ref[0], xg)
    hid = (h1 * _sigmoid(h1) * h3).astype(BF16)
    ye = jnp.concatenate([_dot(w2_ref[0, :hd2, :], hid), _dot(w2_ref[0, hd2:, :], hid)],
                         axis=0).astype(BF16)
    for t in range(seq // ts):
        cols = slice(t * ts, (t + 1) * ts)
        o_ref[0, :, cols] += _dot(ye, pick[:, cols]) * aff_ref[0, 0, :, cols]


def _moe(xt, slot, aff, w1t, w3t, w2t, cap, layer):
    b, d, s = xt.shape
    e = slot.shape[1]
    ff = w2t.shape[2]
    ts = min(SCATTER_BLOCK, s)
    w_blk = lambda i, j: (layer * e + j, 0, 0)
    return pl.pallas_call(
        functools.partial(_moe_kernel, cap, ts),
        grid=(b, e),
        in_specs=[pl.BlockSpec((1, d, s), lambda i, j: (i, 0, 0)),
                  pl.BlockSpec((1, 1, 1, s), lambda i, j: (i, j, 0, 0)),
                  pl.BlockSpec((1, 1, 1, s), lambda i, j: (i, j, 0, 0)),
                  pl.BlockSpec((1, ff, d), w_blk), pl.BlockSpec((1, ff, d), w_blk),
                  pl.BlockSpec((1, d, ff), w_blk)],
        out_specs=pl.BlockSpec((1, d, s), lambda i, j: (i, 0, 0)),
        out_shape=jax.ShapeDtypeStruct((b, d, s), F32),
        compiler_params=_params(("parallel", "arbitrary")),
        name="moe",
    )(xt, slot.reshape(b, e, 1, s), aff.reshape(b, e, 1, s), w1t, w3t, w2t)


def _transpose_cast_kernel(w_ref, o_ref):
    o_ref[0] = w_ref[0].T.astype(o_ref.dtype)


def _transpose_cast(w):
    n, r, c = w.shape
    return pl.pallas_call(
        _transpose_cast_kernel,
        grid=(n,),
        in_specs=[pl.BlockSpec((1, r, c), lambda i: (i, 0, 0))],
        out_specs=pl.BlockSpec((1, c, r), lambda i: (i, 0, 0)),
        out_shape=jax.ShapeDtypeStruct((n, c, r), BF16),
        compiler_params=_params(("parallel",)),
        name="transpose_cast",
    )(w)


def _ple_ln_kernel(alpha, x_ref, f_ref, p_ref, wg_ref, bg_ref, wp_ref, g_ref, b_ref, y_ref):
    x = x_ref[...]
    gate = _sigmoid(_dot(x.astype(BF16), wg_ref[...]) + bg_ref[...])
    ple = gate * _dot(p_ref[...].astype(BF16), wp_ref[...])
    ffn = f_ref[0].T
    y_ref[...] = _layer_norm(alpha * x + ffn + ple, g_ref[...], b_ref[...])


def _ple_ln(x1, ffn_t, p2, wg, bg, wp, g, b, alpha, tm, layer):
    t, d = x1.shape
    nsb = ffn_t.shape[2] // tm
    row = lambda i: (i, 0)
    p_row = lambda i: (layer * (t // tm) + i, 0)
    fixed = lambda i: (0, 0)
    full = lambda a: pl.BlockSpec(a.shape, fixed)
    return pl.pallas_call(
        functools.partial(_ple_ln_kernel, alpha),
        grid=(t // tm,),
        in_specs=[pl.BlockSpec((tm, d), row),
                  pl.BlockSpec((1, d, tm), lambda i: (i // nsb, 0, i % nsb)),
                  pl.BlockSpec((tm, p2.shape[1]), p_row), full(wg), full(bg), full(wp), full(g), full(b)],
        out_specs=pl.BlockSpec((tm, d), row),
        out_shape=jax.ShapeDtypeStruct((t, d), F32),
        compiler_params=_params(("parallel",)),
        name="ple_ln",
    )(x1, ffn_t, p2, wg, bg, wp, g, b)


def _rope_tables(seq, rot_dim, lo, width):
    rows = seq // GRID_W
    row = jnp.repeat(jnp.arange(rows, dtype=F32), GRID_W)
    col = jnp.tile(jnp.arange(GRID_W, dtype=F32), rows)
    axis_dim = rot_dim // 2
    inv = ROPE_THETA ** (-jnp.arange(0, axis_dim, 2, dtype=F32) / axis_dim)
    ang = jnp.concatenate([row[:, None] * inv, col[:, None] * inv], axis=-1)
    cos, sin = jnp.cos(ang), jnp.sin(ang)
    cos2 = jnp.concatenate([cos, cos], axis=-1)
    sin2 = jnp.concatenate([-sin, sin], axis=-1)
    if lo == 0:
        reps = width // rot_dim
        return jnp.tile(cos2, (1, reps)), jnp.tile(sin2, (1, reps))
    pad_l = jnp.ones((seq, lo), F32)
    pad_r = jnp.ones((seq, width - lo - rot_dim), F32)
    cos_t = jnp.concatenate([pad_l, cos2, pad_r], axis=-1)
    sin_t = jnp.concatenate([0 * pad_l, sin2, 0 * pad_r], axis=-1)
    return cos_t, sin_t


def _prep_even(w_in, w_uq, w_ukv, gw_f, gb_f, gw_b, gb_b):
    d = w_in.shape[0]
    offs = np.cumsum(EVEN_SPLITS)[:-1].tolist()
    c_q, c_kv, k_pe, gq, gk, gv, g_lr, gr = jnp.split(w_in, offs, axis=-1)
    z32 = jnp.zeros((d, 32), w_in.dtype)
    chunk = jnp.concatenate([g_lr, z32, k_pe, z32], axis=-1)
    w_in_p = jnp.concatenate([c_q, c_kv, chunk, gq, gk, gv, gr], axis=-1).astype(BF16)
    uq = w_uq.reshape(MLA_Q_LORA, MLA_HEADS, MLA_NOPE + MLA_ROPE)
    uq = jnp.pad(uq, ((0, 0), (0, 0), (0, MLA_PAD - MLA_NOPE - MLA_ROPE)))
    w_uq_p = uq.reshape(MLA_Q_LORA, MLA_HEADS * MLA_PAD).astype(BF16)
    ukv = w_ukv.reshape(MLA_KV_LORA, MLA_HEADS, MLA_NOPE + MLA_V)
    uk = jnp.pad(ukv[:, :, :MLA_NOPE], ((0, 0), (0, 0), (0, MLA_PAD - MLA_NOPE)))
    w_uk_p = uk.reshape(MLA_KV_LORA, MLA_HEADS * MLA_PAD).astype(BF16)
    w_uv = ukv[:, :, MLA_NOPE:].reshape(MLA_KV_LORA, MLA_HEADS * MLA_V).astype(BF16)
    dkw = GLA_HEADS * GLA_DK
    w_gate = jnp.zeros((MLA_PAD, 2 * dkw), F32)
    w_gate = w_gate.at[0:GLA_GATE_RANK, 0:dkw].set(gw_f)
    w_gate = w_gate.at[GLA_GATE_RANK:2 * GLA_GATE_RANK, dkw:].set(gw_b).astype(BF16)
    b_gate = jnp.concatenate([gb_f, gb_b])[None, :]
    return w_in_p, w_uq_p, w_uk_p, w_uv, w_gate, b_gate


def kernel(x, p, w_in_even, mla_q_norm, w_uq, mla_kv_norm, w_ukv, gla_gate_w_fwd, gla_gate_b_fwd,
           gla_gate_w_bwd, gla_gate_b_bwd, gla_norm, w_in_odd, gqa_q_norm, gqa_k_norm, w_o, ln1_g,
           ln1_b, router_w, w1, w3, w2, ple_gate_w, ple_gate_b, ple_w, ln2_g, ln2_b):
    b, s, d = x.shape
    depth = w_o.shape[0]
    t = b * s
    alpha = (2.0 * depth) ** 0.25
    cap = EC_CAPACITY_FACTOR * s // N_EXPERTS
    tm = min(ROW_BLOCK, s)
    tq = min(QUERY_BLOCK, s)
    assert s % tm == 0 and s % tq == 0 and s % GRID_W == 0 and tm % (2 * GLA_CHUNK) == 0
    assert (s // KEY_CHUNKS) % KEY_BLK == 0 and tq % V7X_LANES == 0 and s % min(SCATTER_BLOCK, s) == 0
    assert cap % V7X_BF16_SUBLANES == 0 and w1.shape[1] == N_EXPERTS and router_w.shape[2] == N_EXPERTS
    half = w_o.shape[1] // 2
    cos_a, sin_a = _rope_tables(s, MLA_ROPE, PE_LO, MLA_PAD)
    cos_c, sin_c = _rope_tables(s, GQA_HEAD_DIM, 0, GQA_KV_HEADS * GQA_HEAD_DIM)
    hw = GQA_KV_HEADS * GQA_HEAD_DIM
    head_of = np.arange(hw) // GQA_HEAD_DIM
    avg = jnp.asarray((head_of[:, None] == head_of[None, :]) / GQA_HEAD_DIM, BF16)

    n_e, ff = w1.shape[1], w1.shape[3]
    w1t = _transpose_cast(w1.reshape(depth * n_e, d, ff))
    w3t = _transpose_cast(w3.reshape(depth * n_e, d, ff))
    w2t = _transpose_cast(w2.reshape(depth * n_e, ff, d))
    p_all = p.reshape(depth * t, p.shape[-1])

    x2 = x.reshape(t, d)
    for i in range(depth):
        j = i // 2
        if i % 2 == 0:
            w_in_p, w_uq_p, w_uk_p, w_uv, w_gate, b_gate = _prep_even(
                w_in_even[j], w_uq[j], w_ukv[j], gla_gate_w_fwd[j], gla_gate_b_fwd[j],
                gla_gate_w_bwd[j], gla_gate_b_bwd[j])
            q, k, v, gq, gk, gv, la, gr = _even_proj(
                x2, w_in_p, mla_q_norm[j][None, :], w_uq_p, mla_kv_norm[j][None, :], w_uk_p, w_uv,
                w_gate, b_gate, cos_a, sin_a, s, tm)
            r3 = lambda a: a.reshape(b, s, a.shape[-1])
            o_mla = _mla_attn(r3(q), r3(k), r3(v), tq).reshape(t, -1)
            o_gla = _gla(r3(gq), r3(gk), r3(la), r3(gv), r3(gr), gla_norm[j][None, :]).reshape(t, -1)
            mix_a, mix_b, col_a, col_b = o_mla, o_gla, 0, 0
        else:
            q, k, v = _odd_proj(x2, w_in_odd[j].astype(BF16),
                                jnp.tile(gqa_q_norm[j], GQA_KV_HEADS)[None, :],
                                jnp.tile(gqa_k_norm[j], GQA_KV_HEADS)[None, :],
                                avg, cos_c, sin_c, s, tm)
            r3 = lambda a: a.reshape(b, s, a.shape[-1])
            o = _gqa_attn(r3(q), r3(k), r3(v), tq).reshape(t, -1)
            mix_a, mix_b, col_a, col_b = o, o, 0, 1
        wo = w_o[i].astype(BF16)
        x1, x1t = _out_ln(x2, mix_a, mix_b, col_a, col_b, wo[:half], wo[half:],
                          ln1_g[i][None, :], ln1_b[i][None, :], alpha, s, tm)
        aff, slot = _route(x1.reshape(b, s, d), router_w[i].T, cap)
        ffn_t = _moe(x1t, slot, aff, w1t, w3t, w2t, cap, i)
        x2 = _ple_ln(x1, ffn_t, p_all, ple_gate_w[i].astype(BF16),
                     ple_gate_b[i][None, :], ple_w[i].astype(BF16), ln2_g[i][None, :], ln2_b[i][None, :],
                     alpha, tm, i)
    return x2.reshape(b, s, d)
```

```python
import functools
import math

import jax
import jax.numpy as jnp
import numpy as np
from jax import lax
from jax.experimental import pallas as pl
from jax.experimental.pallas import tpu as pltpu

F32 = jnp.float32
BF16 = jnp.bfloat16

V7X_LANES = 128
V7X_BF16_SUBLANES = 16
V7X_VMEM_BYTES = 64 * 1024 * 1024
VMEM_LIMIT = V7X_VMEM_BYTES - 8 * 1024 * 1024

ROW_BLOCK = 1024
QUERY_BLOCK = 512
SCATTER_BLOCK = 512

GRID_W = 64
ROPE_THETA = 10000.0
EPS = 1e-6
MLA_HEADS, MLA_Q_LORA, MLA_KV_LORA = 8, 256, 128
MLA_NOPE, MLA_ROPE, MLA_V = 64, 32, 64
GLA_HEADS, GLA_DK, GLA_DV = 4, 64, 128
GLA_GATE_RANK, GLA_GATE_NORM, GLA_CHUNK = 16, 16.0, 64
GQA_HEADS, GQA_KV_HEADS, GQA_HEAD_DIM = 16, 4, 64
N_EXPERTS, EC_CAPACITY_FACTOR = 16, 2
EVEN_SPLITS = (MLA_Q_LORA, MLA_KV_LORA, MLA_ROPE, GLA_HEADS * GLA_DK, GLA_HEADS * GLA_DK,
               GLA_HEADS * GLA_DV, 2 * GLA_GATE_RANK, GLA_HEADS * GLA_DV)
ODD_SPLITS = (GQA_HEADS * GQA_HEAD_DIM, GQA_KV_HEADS * GQA_HEAD_DIM, GQA_KV_HEADS * GQA_HEAD_DIM)
MLA_PAD = V7X_LANES
PE_LO, PE_HI = MLA_NOPE, MLA_NOPE + MLA_ROPE


def _dot(a, b):
    return jnp.dot(a, b, preferred_element_type=F32)


def _dot_nt(a, b):
    return lax.dot_general(a, b, (((1,), (1,)), ((), ())), preferred_element_type=F32)


def _dot_tn(a, b):
    return lax.dot_general(a, b, (((0,), (0,)), ((), ())), preferred_element_type=F32)


def _split2(x):
    hi = x.astype(BF16)
    lo = (x - hi.astype(F32)).astype(BF16)
    return hi, lo


def _dot_sel(sel, x):
    hi, lo = _split2(x)
    return _dot(sel, hi) + _dot(sel, lo)


def _dot_x_sel(x, sel):
    hi, lo = _split2(x)
    return _dot(hi, sel) + _dot(lo, sel)


def _params(sem):
    return pltpu.CompilerParams(dimension_semantics=sem, vmem_limit_bytes=VMEM_LIMIT)


def _rot_half(x, half):
    w = x.shape[-1]
    lane = lax.broadcasted_iota(jnp.int32, x.shape, x.ndim - 1)
    first = (lane % (2 * half)) < half
    return jnp.where(first, pltpu.roll(x, w - half, x.ndim - 1), pltpu.roll(x, half, x.ndim - 1))


def _layer_norm(y, g, b):
    mu = jnp.mean(y, axis=-1, keepdims=True)
    yc = y - mu
    var = jnp.mean(yc * yc, axis=-1, keepdims=True)
    return yc * lax.rsqrt(var + EPS) * g + b


def _rms(x, g):
    return x * lax.rsqrt(jnp.mean(x * x, axis=-1, keepdims=True) + EPS) * g


def _log_sigmoid(z):
    return jnp.minimum(z, 0.0) - jnp.log1p(jnp.exp(-jnp.abs(z)))


def _sigmoid(z):
    return 1.0 / (1.0 + jnp.exp(-z))


LOG2_E = math.log2(math.e)
KEY_BLK = 128


KEY_CHUNKS = 8


def _exp2_cols(m, s_scr, e_scr):
    seq, tq = s_scr.shape
    for c in range(tq // V7X_LANES):
        cols = slice(c * V7X_LANES, (c + 1) * V7X_LANES)
        m_b = jnp.broadcast_to(m[:, cols], (KEY_BLK, V7X_LANES))
        for r in range(seq // KEY_BLK):
            rows = slice(r * KEY_BLK, (r + 1) * KEY_BLK)
            e_scr[rows, cols] = jnp.exp2(s_scr[rows, cols] - m_b).astype(BF16)


VT_ROWS_PAD = V7X_BF16_SUBLANES


def _values_t(v_t):
    dv, seq = v_t.shape
    extra = jnp.where(lax.broadcasted_iota(jnp.int32, (VT_ROWS_PAD, seq), 0) == 0, 1.0, 0.0)
    return jnp.concatenate([v_t.astype(BF16), extra.astype(BF16)], axis=0)


SHIFT_SAMPLE = 128
DEN_LO, DEN_HI = 2.0 ** -90, 2.0 ** 90


def _attend_heads_one_pass(n_heads, k_of, q_of, vt_of, e_ref):
    seq = e_ref.shape[1]
    ck = seq // KEY_CHUNKS
    outs, dens = [], []
    for h in range(n_heads):
        q = q_of(h)
        shift = jnp.max(_dot_nt(k_of(h, slice(0, SHIFT_SAMPLE)), q), axis=0, keepdims=True)
        for c in range(KEY_CHUNKS):
            keys = slice(c * ck, (c + 1) * ck)
            e_ref[h, keys, :] = jnp.exp2(_dot_nt(k_of(h, keys), q) - shift).astype(BF16)
        o = _dot(vt_of(h, slice(0, seq)), e_ref[h])
        dv = o.shape[0] - VT_ROWS_PAD
        outs.append(o[:dv] * (1.0 / o[dv:dv + 1]))
        dens.append(o[dv:dv + 1])
    return jnp.concatenate(outs, axis=0), jnp.concatenate(dens, axis=0)


def _attend_heads(n_heads, k_of, q_of, vt_of, s_ref, e_ref):
    assert s_ref.shape[0] == n_heads and e_ref.shape[0] == n_heads
    seq = s_ref.shape[1]
    ck = seq // KEY_CHUNKS
    m, acc = {}, {}
    for h in range(n_heads + 2):
        m_parts = []
        for c in range(KEY_CHUNKS):
            keys = slice(c * ck, (c + 1) * ck)
            if h < n_heads:
                s = _dot_nt(k_of(h, keys), q_of(h))
                s_ref[h, keys, :] = s
                m_parts.append(jnp.max(s, axis=0, keepdims=True))
            if 1 <= h <= n_heads:
                _exp2_cols(m[h - 1], s_ref.at[h - 1, keys], e_ref.at[h - 1, keys])
            if h >= 2:
                o = _dot(vt_of(h - 2, keys), e_ref[h - 2, keys, :])
                acc[h - 2] = o if c == 0 else acc[h - 2] + o
        if h < n_heads:
            m[h] = functools.reduce(jnp.maximum, m_parts)
    outs = []
    for h in range(n_heads):
        dv = acc[h].shape[0] - VT_ROWS_PAD
        outs.append(acc[h][:dv] * (1.0 / acc[h][dv:dv + 1]))
    return jnp.concatenate(outs, axis=0)


def _attend_and_store(n_heads, k_of, q_of, vt_of, s_ref, e_ref, o_ref):
    out_t, den = _attend_heads_one_pass(n_heads, k_of, q_of, vt_of, e_ref)
    o_ref[0] = out_t.T.astype(o_ref.dtype)
    in_range = (den > DEN_LO) & (den < DEN_HI)
    n_bad = jnp.sum(jnp.where(in_range, 0.0, 1.0))

    @pl.when(n_bad > 0.0)
    def _():
        o_ref[0] = _attend_heads(n_heads, k_of, q_of, vt_of, s_ref, e_ref).T.astype(o_ref.dtype)


def _even_proj_kernel(x_ref, w_in_ref, qn_ref, w_uq_ref, kvn_ref, w_uk_ref, w_uv_ref,
                      w_gate_ref, b_gate_ref, cq_ref, sq_ref,
                      q_ref, k_ref, v_ref, gq_ref, gk_ref, gv_ref, la_ref, gr_ref):
    h = _dot(x_ref[...].astype(BF16), w_in_ref[...])
    cos = cq_ref[...]
    sin = sq_ref[...]
    lane = lax.broadcasted_iota(jnp.int32, cos.shape, 1)
    pe_lane = (lane >= PE_LO) & (lane < PE_HI)

    def rope(t):
        return t * cos + _rot_half(t, MLA_ROPE // 2) * sin

    c_q = _rms(h[:, 0:256], qn_ref[...])
    q = _dot(c_q.astype(BF16), w_uq_ref[...]) * ((MLA_NOPE + MLA_ROPE) ** -0.5 * LOG2_E)
    c_kv = _rms(h[:, 256:384], kvn_ref[...]).astype(BF16)
    kn = _dot(c_kv, w_uk_ref[...])
    v_ref[...] = _dot(c_kv, w_uv_ref[...]).astype(v_ref.dtype)
    chunk = h[:, 384:512]
    k_pe = jnp.where(pe_lane, rope(chunk), 0.0)
    for hd in range(MLA_HEADS):
        sl = slice(hd * MLA_PAD, (hd + 1) * MLA_PAD)
        q_ref[:, sl] = rope(q[:, sl]).astype(q_ref.dtype)
        k_ref[:, sl] = (kn[:, sl] + k_pe).astype(k_ref.dtype)
    z = _dot(chunk.astype(BF16), w_gate_ref[...]) + b_gate_ref[...]
    la = _log_sigmoid(z) * (1.0 / GLA_GATE_NORM)
    blk = 2 * GLA_CHUNK
    r = lax.broadcasted_iota(jnp.int32, (blk, blk), 0)
    c = lax.broadcasted_iota(jnp.int32, (blk, blk), 1)
    same = (r // GLA_CHUNK) == (c // GLA_CHUNK)
    lower = jnp.where(same & (c <= r), 1.0, 0.0).astype(BF16)
    upper = jnp.where(same & (c >= r), 1.0, 0.0).astype(BF16)
    dkw = GLA_HEADS * GLA_DK
    for t in range(la.shape[0] // blk):
        rows = slice(t * blk, (t + 1) * blk)
        la_ref[rows, :dkw] = _dot_sel(lower, la[rows, :dkw])
        la_ref[rows, dkw:] = _dot_sel(upper, la[rows, dkw:])
    gq_ref[...] = h[:, 512:768] * (GLA_DK ** -0.5)
    gk_ref[...] = h[:, 768:1024]
    gv_ref[...] = h[:, 1024:1536].astype(gv_ref.dtype)
    gr = h[:, 1536:2048]
    gr_ref[...] = gr * _sigmoid(gr)


def _even_proj(x2, w_in_p, qn, w_uq_p, kvn, w_uk_p, w_uv, w_gate, b_gate, cq, sq, seq, tm):
    t, d = x2.shape
    nsb = seq // tm
    row = lambda i: (i, 0)
    fixed = lambda i: (0, 0)
    pos = lambda i: (i % nsb, 0)
    full = lambda a: pl.BlockSpec(a.shape, fixed)
    outs = [(1024, BF16), (1024, BF16), (512, BF16), (256, F32), (256, F32), (512, BF16), (512, F32), (512, F32)]
    return pl.pallas_call(
        _even_proj_kernel,
        grid=(t // tm,),
        in_specs=[pl.BlockSpec((tm, d), row), full(w_in_p), full(qn), full(w_uq_p), full(kvn),
                  full(w_uk_p), full(w_uv), full(w_gate), full(b_gate),
                  pl.BlockSpec((tm, MLA_PAD), pos), pl.BlockSpec((tm, MLA_PAD), pos)],
        out_specs=[pl.BlockSpec((tm, n), row) for n, _ in outs],
        out_shape=[jax.ShapeDtypeStruct((t, n), dt) for n, dt in outs],
        compiler_params=_params(("parallel",)),
        name="even_proj",
    )(x2, w_in_p, qn, w_uq_p, kvn, w_uk_p, w_uv, w_gate, b_gate, cq, sq)


MLA_STEP_HEADS = 4


def _mla_attn_kernel(q_ref, k_ref, v_ref, o_ref, vt_ref, s_ref, e_ref):
    out_w = MLA_STEP_HEADS * MLA_V

    @pl.when(pl.program_id(2) == 0)
    def _():
        r = lax.broadcasted_iota(jnp.int32, (out_w, out_w), 0)
        c = lax.broadcasted_iota(jnp.int32, (out_w, out_w), 1)
        eye = jnp.where(r == c, 1.0, 0.0).astype(BF16)
        v_t = _dot_nt(eye, v_ref[0])
        for a in range(MLA_STEP_HEADS):
            vt_ref[a] = _values_t(v_t[a * MLA_V:(a + 1) * MLA_V])

    _attend_and_store(
        MLA_STEP_HEADS,
        lambda a, keys: k_ref[0, keys, a * MLA_PAD:(a + 1) * MLA_PAD],
        lambda a: q_ref[0, :, a * MLA_PAD:(a + 1) * MLA_PAD],
        lambda a, keys: vt_ref[a, :, keys],
        s_ref, e_ref, o_ref)


def _mla_attn(q, k, v, tq):
    b, s, _ = q.shape
    n = MLA_STEP_HEADS
    return pl.pallas_call(
        _mla_attn_kernel,
        grid=(b, MLA_HEADS // n, s // tq),
        in_specs=[pl.BlockSpec((1, tq, n * MLA_PAD), lambda i, j, t: (i, t, j)),
                  pl.BlockSpec((1, s, n * MLA_PAD), lambda i, j, t: (i, 0, j)),
                  pl.BlockSpec((1, s, n * MLA_V), lambda i, j, t: (i, 0, j))],
        out_specs=pl.BlockSpec((1, tq, n * MLA_V), lambda i, j, t: (i, t, j)),
        out_shape=jax.ShapeDtypeStruct((b, s, MLA_HEADS * MLA_V), BF16),
        scratch_shapes=[pltpu.VMEM((n, MLA_V + VT_ROWS_PAD, s), BF16),
                        pltpu.VMEM((n, s, tq), F32), pltpu.VMEM((n, s, tq), BF16)],
        compiler_params=_params(("parallel", "parallel", "arbitrary")),
        name="mla_attn",
    )(q, k, v)


def _gla_kernel(q_ref, k_ref, cum_ref, v_ref, gr_ref, g_ref, o_ref, state_ref, of_ref, ob_ref):
    seq = q_ref.shape[1]
    n_chunks = seq // GLA_CHUNK
    L = GLA_CHUNK
    dkw = GLA_HEADS * GLA_DK
    dvw = GLA_HEADS * GLA_DV

    def iota(shape, dim):
        return lax.broadcasted_iota(jnp.int32, shape, dim)

    row_l = iota((L, dkw), 0)
    col_m = iota((L, dkw), 1) % L
    k_own = (iota((dkw, dkw), 0) // L) == (iota((dkw, dkw), 1) // GLA_DK)
    v_own = (iota((dkw, dvw), 0) // L) == (iota((dkw, dvw), 1) // GLA_DV)
    s_own = (iota((dvw, dkw), 0) // GLA_DV) == (iota((dvw, dkw), 1) // GLA_DK)
    state_ref[...] = jnp.zeros_like(state_ref)

    def body(i, carry):
        for d, fwd in enumerate((True, False)):
            n = i if fwd else n_chunks - 1 - i
            rows = pl.ds(pl.multiple_of(n * L, L), L)
            cum = cum_ref[0, rows, d * dkw:(d + 1) * dkw]
            last = cum[L - 1:L, :] if fwd else cum[0:1, :]
            q = q_ref[0, rows, :]
            k = k_ref[0, rows, :]
            v = v_ref[0, rows, :]
            qe = (q * jnp.exp(cum)).astype(BF16)
            kg = (k * jnp.exp(-cum)).astype(BF16)
            kdec = (k * jnp.exp(last - cum)).astype(BF16)
            k_blk = jnp.where(k_own, jnp.concatenate([kg] * GLA_HEADS, axis=0), jnp.zeros((), BF16))
            att = _dot_nt(qe, k_blk)
            keep = (col_m <= row_l) if fwd else (col_m >= row_l)
            att = jnp.where(keep, att, 0.0).astype(BF16)
            v_blk = jnp.where(v_own, jnp.concatenate([v] * GLA_HEADS, axis=0), jnp.zeros((), BF16))
            st = state_ref[d]
            o = _dot(att, v_blk) + _dot_nt(qe, st.astype(BF16))
            state_ref[d] = st * jnp.exp(last) + jnp.where(s_own, _dot_tn(v, kdec), 0.0)
            if fwd:
                of_ref[rows, :] = o
            else:
                ob_ref[rows, :] = o
        return carry

    lax.fori_loop(0, n_chunks, body, 0, unroll=4)

    def finish(n, carry):
        rows = pl.ds(pl.multiple_of(n * L, L), L)
        tot = of_ref[rows, :] + ob_ref[rows, :]
        for hd in range(GLA_HEADS):
            sl = slice(hd * GLA_DV, (hd + 1) * GLA_DV)
            o_ref[0, rows, sl] = (_rms(tot[:, sl], g_ref[...]) * gr_ref[0, rows, sl]).astype(o_ref.dtype)
        return carry

    lax.fori_loop(0, n_chunks, finish, 0)


def _gla(gq, gk, cum, gv, gr, g_norm):
    b, s, _ = gq.shape
    dkw = GLA_HEADS * GLA_DK
    dvw = GLA_HEADS * GLA_DV
    blk = lambda w: pl.BlockSpec((1, s, w), lambda i: (i, 0, 0))
    return pl.pallas_call(
        _gla_kernel,
        grid=(b,),
        in_specs=[blk(dkw), blk(dkw), blk(2 * dkw), blk(dvw), blk(dvw),
                  pl.BlockSpec((1, GLA_DV), lambda i: (0, 0))],
        out_specs=blk(dvw),
        out_shape=jax.ShapeDtypeStruct((b, s, dvw), BF16),
        scratch_shapes=[pltpu.VMEM((2, dvw, dkw), F32), pltpu.VMEM((s, dvw), F32), pltpu.VMEM((s, dvw), F32)],
        compiler_params=_params(("parallel",)),
        name="gla",
    )(gq, gk, cum, gv, gr, g_norm)


def _odd_proj_kernel(x_ref, w_ref, gq_ref, gk_ref, avg_ref, cos_ref, sin_ref, q_ref, k_ref, v_ref):
    h = _dot(x_ref[...].astype(BF16), w_ref[...])
    cos = cos_ref[...]
    sin = sin_ref[...]
    avg = avg_ref[...]
    nq = GQA_HEADS * GQA_HEAD_DIM
    nk = GQA_KV_HEADS * GQA_HEAD_DIM

    def norm_rope(t, g):
        ms = _dot_x_sel(t * t, avg)
        t = t * lax.rsqrt(ms + EPS) * g
        return t * cos + _rot_half(t, GQA_HEAD_DIM // 2) * sin

    for c in range(nq // nk):
        sl = slice(c * nk, (c + 1) * nk)
        q_ref[:, sl] = (norm_rope(h[:, sl], gq_ref[...]) * (GQA_HEAD_DIM ** -0.5 * LOG2_E)).astype(q_ref.dtype)
    k_ref[...] = norm_rope(h[:, nq:nq + nk], gk_ref[...]).astype(k_ref.dtype)
    v_ref[...] = h[:, nq + nk:].astype(v_ref.dtype)


def _odd_proj(x2, w, gq, gk, avg, cos, sin, seq, tm):
    t, d = x2.shape
    nsb = seq // tm
    row = lambda i: (i, 0)
    fixed = lambda i: (0, 0)
    pos = lambda i: (i % nsb, 0)
    full = lambda a: pl.BlockSpec(a.shape, fixed)
    outs = [ODD_SPLITS[0], ODD_SPLITS[1], ODD_SPLITS[2]]
    return pl.pallas_call(
        _odd_proj_kernel,
        grid=(t // tm,),
        in_specs=[pl.BlockSpec((tm, d), row), full(w), full(gq), full(gk), full(avg),
                  pl.BlockSpec((tm, cos.shape[1]), pos), pl.BlockSpec((tm, sin.shape[1]), pos)],
        out_specs=[pl.BlockSpec((tm, n), row) for n in outs],
        out_shape=[jax.ShapeDtypeStruct((t, n), BF16) for n in outs],
        compiler_params=_params(("parallel",)),
        name="odd_proj",
    )(x2, w, gq, gk, avg, cos, sin)


def _gqa_attn_kernel(q_ref, k_ref, v_ref, o_ref, kp_ref, vt_ref, s_ref, e_ref):
    j = pl.program_id(1)
    group = GQA_HEADS // GQA_KV_HEADS
    width = GQA_KV_HEADS * GQA_HEAD_DIM
    pair = 2 * GQA_HEAD_DIM

    @pl.when(pl.program_id(2) == 0)
    def _():
        r = lax.broadcasted_iota(jnp.int32, (width, pair), 0)
        c = lax.broadcasted_iota(jnp.int32, (width, pair), 1)
        twice = (r // GQA_HEAD_DIM == j) & (r % GQA_HEAD_DIM == c % GQA_HEAD_DIM)
        kp_ref[...] = _dot(k_ref[0], jnp.where(twice, 1.0, 0.0).astype(BF16)).astype(BF16)
        rv = lax.broadcasted_iota(jnp.int32, (GQA_HEAD_DIM, width), 0)
        cv = lax.broadcasted_iota(jnp.int32, (GQA_HEAD_DIM, width), 1)
        mine = cv == j * GQA_HEAD_DIM + rv
        vt_ref[...] = _values_t(_dot_nt(jnp.where(mine, 1.0, 0.0).astype(BF16), v_ref[0]))

    lane = lax.broadcasted_iota(jnp.int32, (q_ref.shape[1], pair), 1)

    def q_of(g):
        qp = q_ref[0, :, (g // 2) * pair:(g // 2 + 1) * pair]
        return jnp.where(lane // GQA_HEAD_DIM == g % 2, qp, jnp.zeros_like(qp))

    _attend_and_store(group, lambda g, keys: kp_ref[keys, :], q_of, lambda g, keys: vt_ref[:, keys],
                      s_ref, e_ref, o_ref)


def _gqa_attn(q, k, v, tq):
    b, s, _ = q.shape
    width = GQA_KV_HEADS * GQA_HEAD_DIM
    group = GQA_HEADS // GQA_KV_HEADS
    return pl.pallas_call(
        _gqa_attn_kernel,
        grid=(b, GQA_KV_HEADS, s // tq),
        in_specs=[pl.BlockSpec((1, tq, width), lambda i, j, t: (i, t, j)),
                  pl.BlockSpec((1, s, width), lambda i, j, t: (i, 0, 0)),
                  pl.BlockSpec((1, s, width), lambda i, j, t: (i, 0, 0))],
        out_specs=pl.BlockSpec((1, tq, width), lambda i, j, t: (i, t, j)),
        out_shape=jax.ShapeDtypeStruct((b, s, GQA_HEADS * GQA_HEAD_DIM), BF16),
        scratch_shapes=[pltpu.VMEM((s, 2 * GQA_HEAD_DIM), BF16),
                        pltpu.VMEM((GQA_HEAD_DIM + VT_ROWS_PAD, s), BF16),
                        pltpu.VMEM((group, s, tq), F32), pltpu.VMEM((group, s, tq), BF16)],
        compiler_params=_params(("parallel", "parallel", "arbitrary")),
        name="gqa_attn",
    )(q, k, v)


def _out_ln_kernel(alpha, x_ref, ma_ref, mb_ref, wa_ref, wb_ref, g_ref, b_ref, y_ref, yt_ref):
    y = alpha * x_ref[...] + _dot(ma_ref[...].astype(BF16), wa_ref[...]) \
        + _dot(mb_ref[...].astype(BF16), wb_ref[...])
    y = _layer_norm(y, g_ref[...], b_ref[...])
    y_ref[...] = y
    yt_ref[0] = y.T.astype(BF16)


def _out_ln(x2, mix_a, mix_b, col_a, col_b, w_a, w_b, g, b, alpha, seq, tm):
    t, d = x2.shape
    half = w_a.shape[0]
    nsb = seq // tm
    row = lambda i: (i, 0)
    fixed = lambda i: (0, 0)
    return pl.pallas_call(
        functools.partial(_out_ln_kernel, alpha),
        grid=(t // tm,),
        in_specs=[pl.BlockSpec((tm, d), row),
                  pl.BlockSpec((tm, half), lambda i: (i, col_a)),
                  pl.BlockSpec((tm, half), lambda i: (i, col_b)),
                  pl.BlockSpec(w_a.shape, fixed), pl.BlockSpec(w_b.shape, fixed),
                  pl.BlockSpec(g.shape, fixed), pl.BlockSpec(b.shape, fixed)],
        out_specs=[pl.BlockSpec((tm, d), row),
                   pl.BlockSpec((1, d, tm), lambda i: (i // nsb, 0, i % nsb))],
        out_shape=[jax.ShapeDtypeStruct((t, d), F32), jax.ShapeDtypeStruct((t // seq, d, seq), BF16)],
        compiler_params=_params(("parallel",)),
        name="out_ln",
    )(x2, mix_a, mix_b, w_a, w_b, g, b)


def _route_kernel(cap, x_ref, rw_ref, aff_ref, slot_ref):
    x = x_ref[0]
    seq = x.shape[0]
    xh, xm = _split2(x)
    wh, wm = _split2(rw_ref[...])
    logits = _dot_nt(wh, xh) + (_dot_nt(wh, xm) + _dot_nt(wm, xh))
    e = jnp.exp(logits - jnp.max(logits, axis=0, keepdims=True))
    aff = e / jnp.sum(e, axis=0, keepdims=True)
    aff_ref[0] = aff

    def enough(cand):
        return jnp.sum((aff >= pltpu.bitcast(cand, F32)).astype(jnp.int32), axis=1, keepdims=True) >= cap

    top = jnp.full((aff.shape[0], 1), 1 << 30, jnp.int32)
    thr0 = jnp.where(enough(top), top, 0)

    def pick(i, thr):
        lo = 28 - 2 * i
        c1, c2, c3 = thr | (jnp.int32(1) << lo), thr | (jnp.int32(2) << lo), thr | (jnp.int32(3) << lo)
        return jnp.where(enough(c3), c3, jnp.where(enough(c2), c2, jnp.where(enough(c1), c1, thr)))

    thr = pltpu.bitcast(lax.fori_loop(0, 15, pick, thr0), F32)
    above = aff > thr
    tie = aff == thr
    need = cap - jnp.sum(above.astype(jnp.int32), axis=1, keepdims=True)

    blk = 256 if seq % 256 == 0 else V7X_LANES
    r = lax.broadcasted_iota(jnp.int32, (blk, blk), 0)
    c = lax.broadcasted_iota(jnp.int32, (blk, blk), 1)
    before = jnp.where(r < c, 1.0, 0.0).astype(BF16)

    def prefix(mask):
        m = jnp.where(mask, 1.0, 0.0).astype(BF16)
        run = jnp.zeros((mask.shape[0], 1), F32)
        parts = []
        for t in range(seq // blk):
            mb = m[:, t * blk:(t + 1) * blk]
            parts.append(_dot(mb, before) + run)
            run = run + jnp.sum(mb.astype(F32), axis=1, keepdims=True)
        return jnp.concatenate(parts, axis=1).astype(jnp.int32)

    chosen = above | (tie & (prefix(tie) < need))
    slot_ref[0] = jnp.where(chosen, prefix(chosen), -1)


def _route(x1, rw_t, cap):
    b, s, d = x1.shape
    e = rw_t.shape[0]
    return pl.pallas_call(
        functools.partial(_route_kernel, cap),
        grid=(b,),
        in_specs=[pl.BlockSpec((1, s, d), lambda i: (i, 0, 0)), pl.BlockSpec((e, d), lambda i: (0, 0))],
        out_specs=[pl.BlockSpec((1, e, s), lambda i: (i, 0, 0)), pl.BlockSpec((1, e, s), lambda i: (i, 0, 0))],
        out_shape=[jax.ShapeDtypeStruct((b, e, s), F32), jax.ShapeDtypeStruct((b, e, s), jnp.int32)],
        compiler_params=_params(("parallel",)),
        name="route",
    )(x1, rw_t)


def _moe_kernel(cap, ts, xt_ref, slot_ref, aff_ref, w1_ref, w3_ref, w2_ref, o_ref):
    e_id = pl.program_id(1)
    seq = xt_ref.shape[2]
    ff = w2_ref.shape[2]

    @pl.when(e_id == 0)
    def _():
        o_ref[...] = jnp.zeros_like(o_ref)

    c_row = lax.broadcasted_iota(jnp.int32, (cap, seq), 0)
    pick = jnp.where(slot_ref[0, 0] == c_row, 1.0, 0.0).astype(BF16)
    hd2 = xt_ref.shape[1] // 2
    xg = jnp.concatenate([_dot_nt(xt_ref[0, :hd2, :], pick), _dot_nt(xt_ref[0, hd2:, :], pick)],
                         axis=0).astype(BF16)
    h1 = _dot(w1_ref[0], xg)
    h3 = _dot(w3_ref[0], xg)
    hid = (h1 * _sigmoid(h1) * h3).astype(BF16)
    ye = jnp.concatenate([_dot(w2_ref[0, :hd2, :], hid), _dot(w2_ref[0, hd2:, :], hid)],
                         axis=0).astype(BF16)
    for t in range(seq // ts):
        cols = slice(t * ts, (t + 1) * ts)
        o_ref[0, :, cols] += _dot(ye, pick[:, cols]) * aff_ref[0, 0, :, cols]


def _moe(xt, slot, aff, w1t, w3t, w2t, cap, layer):
    b, d, s = xt.shape
    e = slot.shape[1]
    ff = w2t.shape[2]
    ts = min(SCATTER_BLOCK, s)
    w_blk = lambda i, j: (layer * e + j, 0, 0)
    return pl.pallas_call(
        functools.partial(_moe_kernel, cap, ts),
        grid=(b, e),
        in_specs=[pl.BlockSpec((1, d, s), lambda i, j: (i, 0, 0)),
                  pl.BlockSpec((1, 1, 1, s), lambda i, j: (i, j, 0, 0)),
                  pl.BlockSpec((1, 1, 1, s), lambda i, j: (i, j, 0, 0)),
                  pl.BlockSpec((1, ff, d), w_blk), pl.BlockSpec((1, ff, d), w_blk),
                  pl.BlockSpec((1, d, ff), w_blk)],
        out_specs=pl.BlockSpec((1, d, s), lambda i, j: (i, 0, 0)),
        out_shape=jax.ShapeDtypeStruct((b, d, s), F32),
        compiler_params=_params(("parallel", "arbitrary")),
        name="moe",
    )(xt, slot.reshape(b, e, 1, s), aff.reshape(b, e, 1, s), w1t, w3t, w2t)


def _transpose_cast_kernel(w_ref, o_ref):
    o_ref[0] = w_ref[0].T.astype(o_ref.dtype)


def _transpose_cast(w):
    n, r, c = w.shape
    return pl.pallas_call(
        _transpose_cast_kernel,
        grid=(n,),
        in_specs=[pl.BlockSpec((1, r, c), lambda i: (i, 0, 0))],
        out_specs=pl.BlockSpec((1, c, r), lambda i: (i, 0, 0)),
        out_shape=jax.ShapeDtypeStruct((n, c, r), BF16),
        compiler_params=_params(("parallel",)),
        name="transpose_cast",
    )(w)


def _ple_ln_kernel(alpha, x_ref, f_ref, p_ref, wg_ref, bg_ref, wp_ref, g_ref, b_ref, y_ref):
    x = x_ref[...]
    gate = _sigmoid(_dot(x.astype(BF16), wg_ref[...]) + bg_ref[...])
    ple = gate * _dot(p_ref[...].astype(BF16), wp_ref[...])
    ffn = f_ref[0].T
    y_ref[...] = _layer_norm(alpha * x + ffn + ple, g_ref[...], b_ref[...])


def _ple_ln(x1, ffn_t, p2, wg, bg, wp, g, b, alpha, tm, layer):
    t, d = x1.shape
    nsb = ffn_t.shape[2] // tm
    row = lambda i: (i, 0)
    p_row = lambda i: (layer * (t // tm) + i, 0)
    fixed = lambda i: (0, 0)
    full = lambda a: pl.BlockSpec(a.shape, fixed)
    return pl.pallas_call(
        functools.partial(_ple_ln_kernel, alpha),
        grid=(t // tm,),
        in_specs=[pl.BlockSpec((tm, d), row),
                  pl.BlockSpec((1, d, tm), lambda i: (i // nsb, 0, i % nsb)),
                  pl.BlockSpec((tm, p2.shape[1]), p_row), full(wg), full(bg), full(wp), full(g), full(b)],
        out_specs=pl.BlockSpec((tm, d), row),
        out_shape=jax.ShapeDtypeStruct((t, d), F32),
        compiler_params=_params(("parallel",)),
        name="ple_ln",
    )(x1, ffn_t, p2, wg, bg, wp, g, b)


def _rope_tables(seq, rot_dim, lo, width):
    rows = seq // GRID_W
    row = jnp.repeat(jnp.arange(rows, dtype=F32), GRID_W)
    col = jnp.tile(jnp.arange(GRID_W, dtype=F32), rows)
    axis_dim = rot_dim // 2
    inv = ROPE_THETA ** (-jnp.arange(0, axis_dim, 2, dtype=F32) / axis_dim)
    ang = jnp.concatenate([row[:, None] * inv, col[:, None] * inv], axis=-1)
    cos, sin = jnp.cos(ang), jnp.sin(ang)
    cos2 = jnp.concatenate([cos, cos], axis=-1)
    sin2 = jnp.concatenate([-sin, sin], axis=-1)
    if lo == 0:
        reps = width // rot_dim
        return jnp.tile(cos2, (1, reps)), jnp.tile(sin2, (1, reps))
    pad_l = jnp.ones((seq, lo), F32)
    pad_r = jnp.ones((seq, width - lo - rot_dim), F32)
    cos_t = jnp.concatenate([pad_l, cos2, pad_r], axis=-1)
    sin_t = jnp.concatenate([0 * pad_l, sin2, 0 * pad_r], axis=-1)
    return cos_t, sin_t


def _prep_even(w_in, w_uq, w_ukv, gw_f, gb_f, gw_b, gb_b):
    d = w_in.shape[0]
    offs = np.cumsum(EVEN_SPLITS)[:-1].tolist()
    c_q, c_kv, k_pe, gq, gk, gv, g_lr, gr = jnp.split(w_in, offs, axis=-1)
    z32 = jnp.zeros((d, 32), w_in.dtype)
    chunk = jnp.concatenate([g_lr, z32, k_pe, z32], axis=-1)
    w_in_p = jnp.concatenate([c_q, c_kv, chunk, gq, gk, gv, gr], axis=-1).astype(BF16)
    uq = w_uq.reshape(MLA_Q_LORA, MLA_HEADS, MLA_NOPE + MLA_ROPE)
    uq = jnp.pad(uq, ((0, 0), (0, 0), (0, MLA_PAD - MLA_NOPE - MLA_ROPE)))
    w_uq_p = uq.reshape(MLA_Q_LORA, MLA_HEADS * MLA_PAD).astype(BF16)
    ukv = w_ukv.reshape(MLA_KV_LORA, MLA_HEADS, MLA_NOPE + MLA_V)
    uk = jnp.pad(ukv[:, :, :MLA_NOPE], ((0, 0), (0, 0), (0, MLA_PAD - MLA_NOPE)))
    w_uk_p = uk.reshape(MLA_KV_LORA, MLA_HEADS * MLA_PAD).astype(BF16)
    w_uv = ukv[:, :, MLA_NOPE:].reshape(MLA_KV_LORA, MLA_HEADS * MLA_V).astype(BF16)
    dkw = GLA_HEADS * GLA_DK
    w_gate = jnp.zeros((MLA_PAD, 2 * dkw), F32)
    w_gate = w_gate.at[0:GLA_GATE_RANK, 0:dkw].set(gw_f)
    w_gate = w_gate.at[GLA_GATE_RANK:2 * GLA_GATE_RANK, dkw:].set(gw_b).astype(BF16)
    b_gate = jnp.concatenate([gb_f, gb_b])[None, :]
    return w_in_p, w_uq_p, w_uk_p, w_uv, w_gate, b_gate


def kernel(x, p, w_in_even, mla_q_norm, w_uq, mla_kv_norm, w_ukv, gla_gate_w_fwd, gla_gate_b_fwd,
           gla_gate_w_bwd, gla_gate_b_bwd, gla_norm, w_in_odd, gqa_q_norm, gqa_k_norm, w_o, ln1_g,
           ln1_b, router_w, w1, w3, w2, ple_gate_w, ple_gate_b, ple_w, ln2_g, ln2_b):
    b, s, d = x.shape
    depth = w_o.shape[0]
    t = b * s
    alpha = (2.0 * depth) ** 0.25
    cap = EC_CAPACITY_FACTOR * s // N_EXPERTS
    tm = min(ROW_BLOCK, s)
    tq = min(QUERY_BLOCK, s)
    assert s % tm == 0 and s % tq == 0 and s % GRID_W == 0 and tm % (2 * GLA_CHUNK) == 0
    assert (s // KEY_CHUNKS) % KEY_BLK == 0 and tq % V7X_LANES == 0 and s % min(SCATTER_BLOCK, s) == 0
    assert cap % V7X_BF16_SUBLANES == 0 and w1.shape[1] == N_EXPERTS and router_w.shape[2] == N_EXPERTS
    half = w_o.shape[1] // 2
    cos_a, sin_a = _rope_tables(s, MLA_ROPE, PE_LO, MLA_PAD)
    cos_c, sin_c = _rope_tables(s, GQA_HEAD_DIM, 0, GQA_KV_HEADS * GQA_HEAD_DIM)
    hw = GQA_KV_HEADS * GQA_HEAD_DIM
    head_of = np.arange(hw) // GQA_HEAD_DIM
    avg = jnp.asarray((head_of[:, None] == head_of[None, :]) / GQA_HEAD_DIM, BF16)

    n_e, ff = w1.shape[1], w1.shape[3]
    w1t = _transpose_cast(w1.reshape(depth * n_e, d, ff))
    w3t = _transpose_cast(w3.reshape(depth * n_e, d, ff))
    w2t = _transpose_cast(w2.reshape(depth * n_e, ff, d))
    p_all = p.reshape(depth * t, p.shape[-1])

    x2 = x.reshape(t, d)
    for i in range(depth):
        j = i // 2
        if i % 2 == 0:
            w_in_p, w_uq_p, w_uk_p, w_uv, w_gate, b_gate = _prep_even(
                w_in_even[j], w_uq[j], w_ukv[j], gla_gate_w_fwd[j], gla_gate_b_fwd[j],
                gla_gate_w_bwd[j], gla_gate_b_bwd[j])
            q, k, v, gq, gk, gv, la, gr = _even_proj(
                x2, w_in_p, mla_q_norm[j][None, :], w_uq_p, mla_kv_norm[j][None, :], w_uk_p, w_uv,
                w_gate, b_gate, cos_a, sin_a, s, tm)
            r3 = lambda a: a.reshape(b, s, a.shape[-1])
            o_mla = _mla_attn(r3(q), r3(k), r3(v), tq).reshape(t, -1)
            o_gla = _gla(r3(gq), r3(gk), r3(la), r3(gv), r3(gr), gla_norm[j][None, :]).reshape(t, -1)
            mix_a, mix_b, col_a, col_b = o_mla, o_gla, 0, 0
        else:
            q, k, v = _odd_proj(x2, w_in_odd[j].astype(BF16),
                                jnp.tile(gqa_q_norm[j], GQA_KV_HEADS)[None, :],
                                jnp.tile(gqa_k_norm[j], GQA_KV_HEADS)[None, :],
                                avg, cos_c, sin_c, s, tm)
            r3 = lambda a: a.reshape(b, s, a.shape[-1])
            o = _gqa_attn(r3(q), r3(k), r3(v), tq).reshape(t, -1)
            mix_a, mix_b, col_a, col_b = o, o, 0, 1
        wo = w_o[i].astype(BF16)
        x1, x1t = _out_ln(x2, mix_a, mix_b, col_a, col_b, wo[:half], wo[half:],
                          ln1_g[i][None, :], ln1_b[i][None, :], alpha, s, tm)
        aff, slot = _route(x1.reshape(b, s, d), router_w[i].T, cap)
        ffn_t = _moe(x1t, slot, aff, w1t, w3t, w2t, cap, i)
        x2 = _ple_ln(x1, ffn_t, p_all, ple_gate_w[i].astype(BF16),
                     ple_gate_b[i][None, :], ple_w[i].astype(BF16), ln2_g[i][None, :], ln2_b[i][None, :],
                     alpha, tm, i)
    return x2.reshape(b, s, d)
```

```python
import functools
import math

import jax
import jax.numpy as jnp
import numpy as np
from jax import lax
from jax.experimental import pallas as pl
from jax.experimental.pallas import tpu as pltpu

F32 = jnp.float32
BF16 = jnp.bfloat16

V7X_LANES = 128
V7X_BF16_SUBLANES = 16
V7X_VMEM_BYTES = 64 * 1024 * 1024
VMEM_LIMIT = V7X_VMEM_BYTES - 8 * 1024 * 1024

ROW_BLOCK = 1024
QUERY_BLOCK = 512
SCATTER_BLOCK = 512

GRID_W = 64
ROPE_THETA = 10000.0
EPS = 1e-6
MLA_HEADS, MLA_Q_LORA, MLA_KV_LORA = 8, 256, 128
MLA_NOPE, MLA_ROPE, MLA_V = 64, 32, 64
GLA_HEADS, GLA_DK, GLA_DV = 4, 64, 128
GLA_GATE_RANK, GLA_GATE_NORM, GLA_CHUNK = 16, 16.0, 64
GQA_HEADS, GQA_KV_HEADS, GQA_HEAD_DIM = 16, 4, 64
N_EXPERTS, EC_CAPACITY_FACTOR = 16, 2
EVEN_SPLITS = (MLA_Q_LORA, MLA_KV_LORA, MLA_ROPE, GLA_HEADS * GLA_DK, GLA_HEADS * GLA_DK,
               GLA_HEADS * GLA_DV, 2 * GLA_GATE_RANK, GLA_HEADS * GLA_DV)
ODD_SPLITS = (GQA_HEADS * GQA_HEAD_DIM, GQA_KV_HEADS * GQA_HEAD_DIM, GQA_KV_HEADS * GQA_HEAD_DIM)
MLA_PAD = V7X_LANES
PE_LO, PE_HI = MLA_NOPE, MLA_NOPE + MLA_ROPE


def _dot(a, b):
    return jnp.dot(a, b, preferred_element_type=F32)


def _dot_nt(a, b):
    return lax.dot_general(a, b, (((1,), (1,)), ((), ())), preferred_element_type=F32)


def _dot_tn(a, b):
    return lax.dot_general(a, b, (((0,), (0,)), ((), ())), preferred_element_type=F32)


def _split2(x):
    hi = x.astype(BF16)
    lo = (x - hi.astype(F32)).astype(BF16)
    return hi, lo


def _dot_sel(sel, x):
    hi, lo = _split2(x)
    return _dot(sel, hi) + _dot(sel, lo)


def _dot_x_sel(x, sel):
    hi, lo = _split2(x)
    return _dot(hi, sel) + _dot(lo, sel)


def _params(sem):
    return pltpu.CompilerParams(dimension_semantics=sem, vmem_limit_bytes=VMEM_LIMIT)


def _rot_half(x, half):
    w = x.shape[-1]
    lane = lax.broadcasted_iota(jnp.int32, x.shape, x.ndim - 1)
    first = (lane % (2 * half)) < half
    return jnp.where(first, pltpu.roll(x, w - half, x.ndim - 1), pltpu.roll(x, half, x.ndim - 1))


def _layer_norm(y, g, b):
    mu = jnp.mean(y, axis=-1, keepdims=True)
    yc = y - mu
    var = jnp.mean(yc * yc, axis=-1, keepdims=True)
    return yc * lax.rsqrt(var + EPS) * g + b


def _rms(x, g):
    return x * lax.rsqrt(jnp.mean(x * x, axis=-1, keepdims=True) + EPS) * g


def _log_sigmoid(z):
    return jnp.minimum(z, 0.0) - jnp.log1p(jnp.exp(-jnp.abs(z)))


def _sigmoid(z):
    return 1.0 / (1.0 + jnp.exp(-z))


LOG2_E = math.log2(math.e)
KEY_BLK = 128


KEY_CHUNKS = 4


def _exp2_cols(m, s_scr, e_scr):
    seq, tq = s_scr.shape
    for c in range(tq // V7X_LANES):
        cols = slice(c * V7X_LANES, (c + 1) * V7X_LANES)
        m_b = jnp.broadcast_to(m[:, cols], (KEY_BLK, V7X_LANES))
        for r in range(seq // KEY_BLK):
            rows = slice(r * KEY_BLK, (r + 1) * KEY_BLK)
            e_scr[rows, cols] = jnp.exp2(s_scr[rows, cols] - m_b).astype(BF16)


VT_ROWS_PAD = V7X_BF16_SUBLANES


def _values_t(v_t):
    dv, seq = v_t.shape
    extra = jnp.where(lax.broadcasted_iota(jnp.int32, (VT_ROWS_PAD, seq), 0) == 0, 1.0, 0.0)
    return jnp.concatenate([v_t.astype(BF16), extra.astype(BF16)], axis=0)


SHIFT_SAMPLE = 128
DEN_LO, DEN_HI = 2.0 ** -90, 2.0 ** 90


def _attend_heads_one_pass(n_heads, k_of, q_of, vt_of, e_ref):
    seq = e_ref.shape[1]
    ck = seq // KEY_CHUNKS
    assert ck % SHIFT_SAMPLE == 0
    bounds = sorted({SHIFT_SAMPLE, *range(ck, seq + 1, ck)})
    outs, dens = [], []
    for h in range(n_heads):
        q = q_of(h)
        sample = slice(0, SHIFT_SAMPLE)
        s0 = _dot_nt(k_of(h, sample), q)
        shift = jnp.max(s0, axis=0, keepdims=True)
        e_ref[h, sample, :] = jnp.exp2(s0 - shift).astype(BF16)
        for lo, hi in zip(bounds[:-1], bounds[1:]):
            keys = slice(lo, hi)
            e_ref[h, keys, :] = jnp.exp2(_dot_nt(k_of(h, keys), q) - shift).astype(BF16)
        o = _dot(vt_of(h, slice(0, seq)), e_ref[h])
        dv = o.shape[0] - VT_ROWS_PAD
        outs.append(o[:dv] * (1.0 / o[dv:dv + 1]))
        dens.append(o[dv:dv + 1])
    return jnp.concatenate(outs, axis=0), jnp.concatenate(dens, axis=0)


def _attend_heads(n_heads, k_of, q_of, vt_of, s_ref, e_ref):
    assert s_ref.shape[0] == n_heads and e_ref.shape[0] == n_heads
    seq = s_ref.shape[1]
    ck = seq // KEY_CHUNKS
    m, acc = {}, {}
    for h in range(n_heads + 2):
        m_parts = []
        for c in range(KEY_CHUNKS):
            keys = slice(c * ck, (c + 1) * ck)
            if h < n_heads:
                s = _dot_nt(k_of(h, keys), q_of(h))
                s_ref[h, keys, :] = s
                m_parts.append(jnp.max(s, axis=0, keepdims=True))
            if 1 <= h <= n_heads:
                _exp2_cols(m[h - 1], s_ref.at[h - 1, keys], e_ref.at[h - 1, keys])
            if h >= 2:
                o = _dot(vt_of(h - 2, keys), e_ref[h - 2, keys, :])
                acc[h - 2] = o if c == 0 else acc[h - 2] + o
        if h < n_heads:
            m[h] = functools.reduce(jnp.maximum, m_parts)
    outs = []
    for h in range(n_heads):
        dv = acc[h].shape[0] - VT_ROWS_PAD
        outs.append(acc[h][:dv] * (1.0 / acc[h][dv:dv + 1]))
    return jnp.concatenate(outs, axis=0)


def _attend_and_store(n_heads, k_of, q_of, vt_of, s_ref, e_ref, o_ref):
    out_t, den = _attend_heads_one_pass(n_heads, k_of, q_of, vt_of, e_ref)
    o_ref[0] = out_t.T.astype(o_ref.dtype)
    in_range = (den > DEN_LO) & (den < DEN_HI)
    n_bad = jnp.sum(jnp.where(in_range, 0.0, 1.0))

    @pl.when(n_bad > 0.0)
    def _():
        o_ref[0] = _attend_heads(n_heads, k_of, q_of, vt_of, s_ref, e_ref).T.astype(o_ref.dtype)


def _even_proj_kernel(x_ref, w_in_ref, qn_ref, w_uq_ref, kvn_ref, w_uk_ref, w_uv_ref,
                      w_gate_ref, b_gate_ref, cq_ref, sq_ref,
                      q_ref, k_ref, v_ref, gq_ref, gk_ref, gv_ref, la_ref, gr_ref):
    h = _dot(x_ref[...].astype(BF16), w_in_ref[...])
    cos = cq_ref[...]
    sin = sq_ref[...]
    lane = lax.broadcasted_iota(jnp.int32, cos.shape, 1)
    pe_lane = (lane >= PE_LO) & (lane < PE_HI)

    def rope(t):
        return t * cos + _rot_half(t, MLA_ROPE // 2) * sin

    c_q = _rms(h[:, 0:256], qn_ref[...])
    q = _dot(c_q.astype(BF16), w_uq_ref[...]) * ((MLA_NOPE + MLA_ROPE) ** -0.5 * LOG2_E)
    c_kv = _rms(h[:, 256:384], kvn_ref[...]).astype(BF16)
    kn = _dot(c_kv, w_uk_ref[...])
    v_ref[...] = _dot(c_kv, w_uv_ref[...]).astype(v_ref.dtype)
    chunk = h[:, 384:512]
    k_pe = jnp.where(pe_lane, rope(chunk), 0.0)
    for hd in range(MLA_HEADS):
        sl = slice(hd * MLA_PAD, (hd + 1) * MLA_PAD)
        q_ref[:, sl] = rope(q[:, sl]).astype(q_ref.dtype)
        k_ref[:, sl] = (kn[:, sl] + k_pe).astype(k_ref.dtype)
    z = _dot(chunk.astype(BF16), w_gate_ref[...]) + b_gate_ref[...]
    la = _log_sigmoid(z) * (1.0 / GLA_GATE_NORM)
    blk = 2 * GLA_CHUNK
    r = lax.broadcasted_iota(jnp.int32, (blk, blk), 0)
    c = lax.broadcasted_iota(jnp.int32, (blk, blk), 1)
    same = (r // GLA_CHUNK) == (c // GLA_CHUNK)
    lower = jnp.where(same & (c <= r), 1.0, 0.0).astype(BF16)
    upper = jnp.where(same & (c >= r), 1.0, 0.0).astype(BF16)
    dkw = GLA_HEADS * GLA_DK
    for t in range(la.shape[0] // blk):
        rows = slice(t * blk, (t + 1) * blk)
        la_ref[rows, :dkw] = _dot_sel(lower, la[rows, :dkw])
        la_ref[rows, dkw:] = _dot_sel(upper, la[rows, dkw:])
    gq_ref[...] = h[:, 512:768] * (GLA_DK ** -0.5)
    gk_ref[...] = h[:, 768:1024]
    gv_ref[...] = h[:, 1024:1536].astype(gv_ref.dtype)
    gr = h[:, 1536:2048]
    gr_ref[...] = gr * _sigmoid(gr)


def _even_proj(x2, w_in_p, qn, w_uq_p, kvn, w_uk_p, w_uv, w_gate, b_gate, cq, sq, seq, tm):
    t, d = x2.shape
    nsb = seq // tm
    row = lambda i: (i, 0)
    fixed = lambda i: (0, 0)
    pos = lambda i: (i % nsb, 0)
    full = lambda a: pl.BlockSpec(a.shape, fixed)
    outs = [(1024, BF16), (1024, BF16), (512, BF16), (256, F32), (256, F32), (512, BF16), (512, F32), (512, F32)]
    return pl.pallas_call(
        _even_proj_kernel,
        grid=(t // tm,),
        in_specs=[pl.BlockSpec((tm, d), row), full(w_in_p), full(qn), full(w_uq_p), full(kvn),
                  full(w_uk_p), full(w_uv), full(w_gate), full(b_gate),
                  pl.BlockSpec((tm, MLA_PAD), pos), pl.BlockSpec((tm, MLA_PAD), pos)],
        out_specs=[pl.BlockSpec((tm, n), row) for n, _ in outs],
        out_shape=[jax.ShapeDtypeStruct((t, n), dt) for n, dt in outs],
        compiler_params=_params(("parallel",)),
        name="even_proj",
    )(x2, w_in_p, qn, w_uq_p, kvn, w_uk_p, w_uv, w_gate, b_gate, cq, sq)


MLA_STEP_HEADS = 4


def _mla_attn_kernel(q_ref, k_ref, v_ref, o_ref, vt_ref, s_ref, e_ref):
    out_w = MLA_STEP_HEADS * MLA_V

    @pl.when(pl.program_id(2) == 0)
    def _():
        r = lax.broadcasted_iota(jnp.int32, (out_w, out_w), 0)
        c = lax.broadcasted_iota(jnp.int32, (out_w, out_w), 1)
        eye = jnp.where(r == c, 1.0, 0.0).astype(BF16)
        v_t = _dot_nt(eye, v_ref[0])
        for a in range(MLA_STEP_HEADS):
            vt_ref[a] = _values_t(v_t[a * MLA_V:(a + 1) * MLA_V])

    _attend_and_store(
        MLA_STEP_HEADS,
        lambda a, keys: k_ref[0, keys, a * MLA_PAD:(a + 1) * MLA_PAD],
        lambda a: q_ref[0, :, a * MLA_PAD:(a + 1) * MLA_PAD],
        lambda a, keys: vt_ref[a, :, keys],
        s_ref, e_ref, o_ref)


def _mla_attn(q, k, v, tq):
    b, s, _ = q.shape
    n = MLA_STEP_HEADS
    return pl.pallas_call(
        _mla_attn_kernel,
        grid=(b, MLA_HEADS // n, s // tq),
        in_specs=[pl.BlockSpec((1, tq, n * MLA_PAD), lambda i, j, t: (i, t, j)),
                  pl.BlockSpec((1, s, n * MLA_PAD), lambda i, j, t: (i, 0, j)),
                  pl.BlockSpec((1, s, n * MLA_V), lambda i, j, t: (i, 0, j))],
        out_specs=pl.BlockSpec((1, tq, n * MLA_V), lambda i, j, t: (i, t, j)),
        out_shape=jax.ShapeDtypeStruct((b, s, MLA_HEADS * MLA_V), BF16),
        scratch_shapes=[pltpu.VMEM((n, MLA_V + VT_ROWS_PAD, s), BF16),
                        pltpu.VMEM((n, s, tq), F32), pltpu.VMEM((n, s, tq), BF16)],
        compiler_params=_params(("parallel", "parallel", "arbitrary")),
        name="mla_attn",
    )(q, k, v)


def _gla_kernel(q_ref, k_ref, cum_ref, v_ref, gr_ref, g_ref, o_ref, state_ref, of_ref, ob_ref):
    seq = q_ref.shape[1]
    n_chunks = seq // GLA_CHUNK
    L = GLA_CHUNK
    dkw = GLA_HEADS * GLA_DK
    dvw = GLA_HEADS * GLA_DV

    def iota(shape, dim):
        return lax.broadcasted_iota(jnp.int32, shape, dim)

    row_l = iota((L, dkw), 0)
    col_m = iota((L, dkw), 1) % L
    k_own = (iota((dkw, dkw), 0) // L) == (iota((dkw, dkw), 1) // GLA_DK)
    v_own = (iota((dkw, dvw), 0) // L) == (iota((dkw, dvw), 1) // GLA_DV)
    s_own = (iota((dvw, dkw), 0) // GLA_DV) == (iota((dvw, dkw), 1) // GLA_DK)
    state_ref[...] = jnp.zeros_like(state_ref)

    def body(i, carry):
        for d, fwd in enumerate((True, False)):
            n = i if fwd else n_chunks - 1 - i
            rows = pl.ds(pl.multiple_of(n * L, L), L)
            cum = cum_ref[0, rows, d * dkw:(d + 1) * dkw]
            last = cum[L - 1:L, :] if fwd else cum[0:1, :]
            q = q_ref[0, rows, :]
            k = k_ref[0, rows, :]
            v = v_ref[0, rows, :]
            qe = (q * jnp.exp(cum)).astype(BF16)
            kg = (k * jnp.exp(-cum)).astype(BF16)
            kdec = (k * jnp.exp(last - cum)).astype(BF16)
            k_blk = jnp.where(k_own, jnp.concatenate([kg] * GLA_HEADS, axis=0), jnp.zeros((), BF16))
            att = _dot_nt(qe, k_blk)
            keep = (col_m <= row_l) if fwd else (col_m >= row_l)
            att = jnp.where(keep, att, 0.0).astype(BF16)
            v_blk = jnp.where(v_own, jnp.concatenate([v] * GLA_HEADS, axis=0), jnp.zeros((), BF16))
            st = state_ref[d]
            o = _dot(att, v_blk) + _dot_nt(qe, st.astype(BF16))
            state_ref[d] = st * jnp.exp(last) + jnp.where(s_own, _dot_tn(v, kdec), 0.0)
            if fwd:
                of_ref[rows, :] = o
            else:
                ob_ref[rows, :] = o
        return carry

    lax.fori_loop(0, n_chunks, body, 0, unroll=4)

    def finish(n, carry):
        rows = pl.ds(pl.multiple_of(n * L, L), L)
        tot = of_ref[rows, :] + ob_ref[rows, :]
        for hd in range(GLA_HEADS):
            sl = slice(hd * GLA_DV, (hd + 1) * GLA_DV)
            o_ref[0, rows, sl] = (_rms(tot[:, sl], g_ref[...]) * gr_ref[0, rows, sl]).astype(o_ref.dtype)
        return carry

    lax.fori_loop(0, n_chunks, finish, 0)


def _gla(gq, gk, cum, gv, gr, g_norm):
    b, s, _ = gq.shape
    dkw = GLA_HEADS * GLA_DK
    dvw = GLA_HEADS * GLA_DV
    blk = lambda w: pl.BlockSpec((1, s, w), lambda i: (i, 0, 0))
    return pl.pallas_call(
        _gla_kernel,
        grid=(b,),
        in_specs=[blk(dkw), blk(dkw), blk(2 * dkw), blk(dvw), blk(dvw),
                  pl.BlockSpec((1, GLA_DV), lambda i: (0, 0))],
        out_specs=blk(dvw),
        out_shape=jax.ShapeDtypeStruct((b, s, dvw), BF16),
        scratch_shapes=[pltpu.VMEM((2, dvw, dkw), F32), pltpu.VMEM((s, dvw), F32), pltpu.VMEM((s, dvw), F32)],
        compiler_params=_params(("parallel",)),
        name="gla",
    )(gq, gk, cum, gv, gr, g_norm)


def _odd_proj_kernel(x_ref, w_ref, gq_ref, gk_ref, avg_ref, cos_ref, sin_ref, q_ref, k_ref, v_ref):
    h = _dot(x_ref[...].astype(BF16), w_ref[...])
    cos = cos_ref[...]
    sin = sin_ref[...]
    avg = avg_ref[...]
    nq = GQA_HEADS * GQA_HEAD_DIM
    nk = GQA_KV_HEADS * GQA_HEAD_DIM

    def norm_rope(t, g):
        ms = _dot_x_sel(t * t, avg)
        t = t * lax.rsqrt(ms + EPS) * g
        return t * cos + _rot_half(t, GQA_HEAD_DIM // 2) * sin

    for c in range(nq // nk):
        sl = slice(c * nk, (c + 1) * nk)
        q_ref[:, sl] = (norm_rope(h[:, sl], gq_ref[...]) * (GQA_HEAD_DIM ** -0.5 * LOG2_E)).astype(q_ref.dtype)
    k_ref[...] = norm_rope(h[:, nq:nq + nk], gk_ref[...]).astype(k_ref.dtype)
    v_ref[...] = h[:, nq + nk:].astype(v_ref.dtype)


def _odd_proj(x2, w, gq, gk, avg, cos, sin, seq, tm):
    t, d = x2.shape
    nsb = seq // tm
    row = lambda i: (i, 0)
    fixed = lambda i: (0, 0)
    pos = lambda i: (i % nsb, 0)
    full = lambda a: pl.BlockSpec(a.shape, fixed)
    outs = [ODD_SPLITS[0], ODD_SPLITS[1], ODD_SPLITS[2]]
    return pl.pallas_call(
        _odd_proj_kernel,
        grid=(t // tm,),
        in_specs=[pl.BlockSpec((tm, d), row), full(w), full(gq), full(gk), full(avg),
                  pl.BlockSpec((tm, cos.shape[1]), pos), pl.BlockSpec((tm, sin.shape[1]), pos)],
        out_specs=[pl.BlockSpec((tm, n), row) for n in outs],
        out_shape=[jax.ShapeDtypeStruct((t, n), BF16) for n in outs],
        compiler_params=_params(("parallel",)),
        name="odd_proj",
    )(x2, w, gq, gk, avg, cos, sin)


def _gqa_attn_kernel(q_ref, k_ref, v_ref, o_ref, kp_ref, vt_ref, s_ref, e_ref):
    j = pl.program_id(1)
    group = GQA_HEADS // GQA_KV_HEADS
    width = GQA_KV_HEADS * GQA_HEAD_DIM
    pair = 2 * GQA_HEAD_DIM

    @pl.when(pl.program_id(2) == 0)
    def _():
        r = lax.broadcasted_iota(jnp.int32, (width, pair), 0)
        c = lax.broadcasted_iota(jnp.int32, (width, pair), 1)
        twice = (r // GQA_HEAD_DIM == j) & (r % GQA_HEAD_DIM == c % GQA_HEAD_DIM)
        kp_ref[...] = _dot(k_ref[0], jnp.where(twice, 1.0, 0.0).astype(BF16)).astype(BF16)
        rv = lax.broadcasted_iota(jnp.int32, (GQA_HEAD_DIM, width), 0)
        cv = lax.broadcasted_iota(jnp.int32, (GQA_HEAD_DIM, width), 1)
        mine = cv == j * GQA_HEAD_DIM + rv
        vt_ref[...] = _values_t(_dot_nt(jnp.where(mine, 1.0, 0.0).astype(BF16), v_ref[0]))

    lane = lax.broadcasted_iota(jnp.int32, (q_ref.shape[1], pair), 1)

    def q_of(g):
        qp = q_ref[0, :, (g // 2) * pair:(g // 2 + 1) * pair]
        return jnp.where(lane // GQA_HEAD_DIM == g % 2, qp, jnp.zeros_like(qp))

    _attend_and_store(group, lambda g, keys: kp_ref[keys, :], q_of, lambda g, keys: vt_ref[:, keys],
                      s_ref, e_ref, o_ref)


def _gqa_attn(q, k, v, tq):
    b, s, _ = q.shape
    width = GQA_KV_HEADS * GQA_HEAD_DIM
    group = GQA_HEADS // GQA_KV_HEADS
    return pl.pallas_call(
        _gqa_attn_kernel,
        grid=(b, GQA_KV_HEADS, s // tq),
        in_specs=[pl.BlockSpec((1, tq, width), lambda i, j, t: (i, t, j)),
                  pl.BlockSpec((1, s, width), lambda i, j, t: (i, 0, 0)),
                  pl.BlockSpec((1, s, width), lambda i, j, t: (i, 0, 0))],
        out_specs=pl.BlockSpec((1, tq, width), lambda i, j, t: (i, t, j)),
        out_shape=jax.ShapeDtypeStruct((b, s, GQA_HEADS * GQA_HEAD_DIM), BF16),
        scratch_shapes=[pltpu.VMEM((s, 2 * GQA_HEAD_DIM), BF16),
                        pltpu.VMEM((GQA_HEAD_DIM + VT_ROWS_PAD, s), BF16),
                        pltpu.VMEM((group, s, tq), F32), pltpu.VMEM((group, s, tq), BF16)],
        compiler_params=_params(("parallel", "parallel", "arbitrary")),
        name="gqa_attn",
    )(q, k, v)


def _out_ln_kernel(alpha, x_ref, ma_ref, mb_ref, wa_ref, wb_ref, g_ref, b_ref, y_ref, yt_ref):
    y = alpha * x_ref[...] + _dot(ma_ref[...].astype(BF16), wa_ref[...]) \
        + _dot(mb_ref[...].astype(BF16), wb_ref[...])
    y = _layer_norm(y, g_ref[...], b_ref[...])
    y_ref[...] = y
    yt_ref[0] = y.T.astype(BF16)


def _out_ln(x2, mix_a, mix_b, col_a, col_b, w_a, w_b, g, b, alpha, seq, tm):
    t, d = x2.shape
    half = w_a.shape[0]
    nsb = seq // tm
    row = lambda i: (i, 0)
    fixed = lambda i: (0, 0)
    return pl.pallas_call(
        functools.partial(_out_ln_kernel, alpha),
        grid=(t // tm,),
        in_specs=[pl.BlockSpec((tm, d), row),
                  pl.BlockSpec((tm, half), lambda i: (i, col_a)),
                  pl.BlockSpec((tm, half), lambda i: (i, col_b)),
                  pl.BlockSpec(w_a.shape, fixed), pl.BlockSpec(w_b.shape, fixed),
                  pl.BlockSpec(g.shape, fixed), pl.BlockSpec(b.shape, fixed)],
        out_specs=[pl.BlockSpec((tm, d), row),
                   pl.BlockSpec((1, d, tm), lambda i: (i // nsb, 0, i % nsb))],
        out_shape=[jax.ShapeDtypeStruct((t, d), F32), jax.ShapeDtypeStruct((t // seq, d, seq), BF16)],
        compiler_params=_params(("parallel",)),
        name="out_ln",
    )(x2, mix_a, mix_b, w_a, w_b, g, b)


def _route_kernel(cap, x_ref, rw_ref, aff_ref, slot_ref):
    x = x_ref[0]
    seq = x.shape[0]
    xh, xm = _split2(x)
    wh, wm = _split2(rw_ref[...])
    logits = _dot_nt(wh, xh) + (_dot_nt(wh, xm) + _dot_nt(wm, xh))
    e = jnp.exp(logits - jnp.max(logits, axis=0, keepdims=True))
    aff = e / jnp.sum(e, axis=0, keepdims=True)
    aff_ref[0] = aff

    def enough(cand):
        return jnp.sum((aff >= pltpu.bitcast(cand, F32)).astype(jnp.int32), axis=1, keepdims=True) >= cap

    top = jnp.full((aff.shape[0], 1), 1 << 30, jnp.int32)
    thr0 = jnp.where(enough(top), top, 0)

    def pick(i, thr):
        lo = 28 - 2 * i
        c1, c2, c3 = thr | (jnp.int32(1) << lo), thr | (jnp.int32(2) << lo), thr | (jnp.int32(3) << lo)
        return jnp.where(enough(c3), c3, jnp.where(enough(c2), c2, jnp.where(enough(c1), c1, thr)))

    thr = pltpu.bitcast(lax.fori_loop(0, 15, pick, thr0), F32)
    above = aff > thr
    tie = aff == thr
    need = cap - jnp.sum(above.astype(jnp.int32), axis=1, keepdims=True)

    blk = 256 if seq % 256 == 0 else V7X_LANES
    r = lax.broadcasted_iota(jnp.int32, (blk, blk), 0)
    c = lax.broadcasted_iota(jnp.int32, (blk, blk), 1)
    before = jnp.where(r < c, 1.0, 0.0).astype(BF16)

    def prefix(mask):
        m = jnp.where(mask, 1.0, 0.0).astype(BF16)
        run = jnp.zeros((mask.shape[0], 1), F32)
        parts = []
        for t in range(seq // blk):
            mb = m[:, t * blk:(t + 1) * blk]
            parts.append(_dot(mb, before) + run)
            run = run + jnp.sum(mb.astype(F32), axis=1, keepdims=True)
        return jnp.concatenate(parts, axis=1).astype(jnp.int32)

    chosen = above | (tie & (prefix(tie) < need))
    slot_ref[0] = jnp.where(chosen, prefix(chosen), -1)


def _route(x1, rw_t, cap):
    b, s, d = x1.shape
    e = rw_t.shape[0]
    return pl.pallas_call(
        functools.partial(_route_kernel, cap),
        grid=(b,),
        in_specs=[pl.BlockSpec((1, s, d), lambda i: (i, 0, 0)), pl.BlockSpec((e, d), lambda i: (0, 0))],
        out_specs=[pl.BlockSpec((1, e, s), lambda i: (i, 0, 0)), pl.BlockSpec((1, e, s), lambda i: (i, 0, 0))],
        out_shape=[jax.ShapeDtypeStruct((b, e, s), F32), jax.ShapeDtypeStruct((b, e, s), jnp.int32)],
        compiler_params=_params(("parallel",)),
        name="route",
    )(x1, rw_t)


def _moe_kernel(cap, ts, xt_ref, slot_ref, aff_ref, w1_ref, w3_ref, w2_ref, o_ref):
    e_id = pl.program_id(1)
    seq = xt_ref.shape[2]
    ff = w2_ref.shape[2]

    @pl.when(e_id == 0)
    def _():
        o_ref[...] = jnp.zeros_like(o_ref)

    c_row = lax.broadcasted_iota(jnp.int32, (cap, seq), 0)
    pick = jnp.where(slot_ref[0, 0] == c_row, 1.0, 0.0).astype(BF16)
    hd2 = xt_ref.shape[1] // 2
    xg = jnp.concatenate([_dot_nt(xt_ref[0, :hd2, :], pick), _dot_nt(xt_ref[0, hd2:, :], pick)],
                         axis=0).astype(BF16)
    h1 = _dot(w1_ref[0], xg)
    h3 = _dot(w3_ref[0], xg)
    hid = (h1 * _sigmoid(h1) * h3).astype(BF16)
    ye = jnp.concatenate([_dot(w2_ref[0, :hd2, :], hid), _dot(w2_ref[0, hd2:, :], hid)],
                         axis=0).astype(BF16)
    for t in range(seq // ts):
        cols = slice(t * ts, (t + 1) * ts)
        o_ref[0, :, cols] += _dot(ye, pick[:, cols]) * aff_ref[0, 0, :, cols]


def _moe(xt, slot, aff, w1t, w3t, w2t, cap, layer):
    b, d, s = xt.shape
    e = slot.shape[1]
    ff = w2t.shape[2]
    ts = min(SCATTER_BLOCK, s)
    w_blk = lambda i, j: (layer * e + j, 0, 0)
    return pl.pallas_call(
        functools.partial(_moe_kernel, cap, ts),
        grid=(b, e),
        in_specs=[pl.BlockSpec((1, d, s), lambda i, j: (i, 0, 0)),
                  pl.BlockSpec((1, 1, 1, s), lambda i, j: (i, j, 0, 0)),
                  pl.BlockSpec((1, 1, 1, s), lambda i, j: (i, j, 0, 0)),
                  pl.BlockSpec((1, ff, d), w_blk), pl.BlockSpec((1, ff, d), w_blk),
                  pl.BlockSpec((1, d, ff), w_blk)],
        out_specs=pl.BlockSpec((1, d, s), lambda i, j: (i, 0, 0)),
        out_shape=jax.ShapeDtypeStruct((b, d, s), F32),
        compiler_params=_params(("parallel", "arbitrary")),
        name="moe",
    )(xt, slot.reshape(b, e, 1, s), aff.reshape(b, e, 1, s), w1t, w3t, w2t)


def _transpose_cast_kernel(w_ref, o_ref):
    o_ref[0] = w_ref[0].T.astype(o_ref.dtype)


def _transpose_cast(w):
    n, r, c = w.shape
    return pl.pallas_call(
        _transpose_cast_kernel,
        grid=(n,),
        in_specs=[pl.BlockSpec((1, r, c), lambda i: (i, 0, 0))],
        out_specs=pl.BlockSpec((1, c, r), lambda i: (i, 0, 0)),
        out_shape=jax.ShapeDtypeStruct((n, c, r), BF16),
        compiler_params=_params(("parallel",)),
        name="transpose_cast",
    )(w)


def _ple_ln_kernel(alpha, x_ref, f_ref, p_ref, wg_ref, bg_ref, wp_ref, g_ref, b_ref, y_ref):
    x = x_ref[...]
    gate = _sigmoid(_dot(x.astype(BF16), wg_ref[...]) + bg_ref[...])
    ple = gate * _dot(p_ref[...].astype(BF16), wp_ref[...])
    ffn = f_ref[0].T
    y_ref[...] = _layer_norm(alpha * x + ffn + ple, g_ref[...], b_ref[...])


def _ple_ln(x1, ffn_t, p2, wg, bg, wp, g, b, alpha, tm, layer):
    t, d = x1.shape
    nsb = ffn_t.shape[2] // tm
    row = lambda i: (i, 0)
    p_row = lambda i: (layer * (t // tm) + i, 0)
    fixed = lambda i: (0, 0)
    full = lambda a: pl.BlockSpec(a.shape, fixed)
    return pl.pallas_call(
        functools.partial(_ple_ln_kernel, alpha),
        grid=(t // tm,),
        in_specs=[pl.BlockSpec((tm, d), row),
                  pl.BlockSpec((1, d, tm), lambda i: (i // nsb, 0, i % nsb)),
                  pl.BlockSpec((tm, p2.shape[1]), p_row), full(wg), full(bg), full(wp), full(g), full(b)],
        out_specs=pl.BlockSpec((tm, d), row),
        out_shape=jax.ShapeDtypeStruct((t, d), F32),
        compiler_params=_params(("parallel",)),
        name="ple_ln",
    )(x1, ffn_t, p2, wg, bg, wp, g, b)


def _rope_tables(seq, rot_dim, lo, width):
    rows = seq // GRID_W
    row = jnp.repeat(jnp.arange(rows, dtype=F32), GRID_W)
    col = jnp.tile(jnp.arange(GRID_W, dtype=F32), rows)
    axis_dim = rot_dim // 2
    inv = ROPE_THETA ** (-jnp.arange(0, axis_dim, 2, dtype=F32) / axis_dim)
    ang = jnp.concatenate([row[:, None] * inv, col[:, None] * inv], axis=-1)
    cos, sin = jnp.cos(ang), jnp.sin(ang)
    cos2 = jnp.concatenate([cos, cos], axis=-1)
    sin2 = jnp.concatenate([-sin, sin], axis=-1)
    if lo == 0:
        reps = width // rot_dim
        return jnp.tile(cos2, (1, reps)), jnp.tile(sin2, (1, reps))
    pad_l = jnp.ones((seq, lo), F32)
    pad_r = jnp.ones((seq, width - lo - rot_dim), F32)
    cos_t = jnp.concatenate([pad_l, cos2, pad_r], axis=-1)
    sin_t = jnp.concatenate([0 * pad_l, sin2, 0 * pad_r], axis=-1)
    return cos_t, sin_t


def _prep_even(w_in, w_uq, w_ukv, gw_f, gb_f, gw_b, gb_b):
    d = w_in.shape[0]
    offs = np.cumsum(EVEN_SPLITS)[:-1].tolist()
    c_q, c_kv, k_pe, gq, gk, gv, g_lr, gr = jnp.split(w_in, offs, axis=-1)
    z32 = jnp.zeros((d, 32), w_in.dtype)
    chunk = jnp.concatenate([g_lr, z32, k_pe, z32], axis=-1)
    w_in_p = jnp.concatenate([c_q, c_kv, chunk, gq, gk, gv, gr], axis=-1).astype(BF16)
    uq = w_uq.reshape(MLA_Q_LORA, MLA_HEADS, MLA_NOPE + MLA_ROPE)
    uq = jnp.pad(uq, ((0, 0), (0, 0), (0, MLA_PAD - MLA_NOPE - MLA_ROPE)))
    w_uq_p = uq.reshape(MLA_Q_LORA, MLA_HEADS * MLA_PAD).astype(BF16)
    ukv = w_ukv.reshape(MLA_KV_LORA, MLA_HEADS, MLA_NOPE + MLA_V)
    uk = jnp.pad(ukv[:, :, :MLA_NOPE], ((0, 0), (0, 0), (0, MLA_PAD - MLA_NOPE)))
    w_uk_p = uk.reshape(MLA_KV_LORA, MLA_HEADS * MLA_PAD).astype(BF16)
    w_uv = ukv[:, :, MLA_NOPE:].reshape(MLA_KV_LORA, MLA_HEADS * MLA_V).astype(BF16)
    dkw = GLA_HEADS * GLA_DK
    w_gate = jnp.zeros((MLA_PAD, 2 * dkw), F32)
    w_gate = w_gate.at[0:GLA_GATE_RANK, 0:dkw].set(gw_f)
    w_gate = w_gate.at[GLA_GATE_RANK:2 * GLA_GATE_RANK, dkw:].set(gw_b).astype(BF16)
    b_gate = jnp.concatenate([gb_f, gb_b])[None, :]
    return w_in_p, w_uq_p, w_uk_p, w_uv, w_gate, b_gate


def kernel(x, p, w_in_even, mla_q_norm, w_uq, mla_kv_norm, w_ukv, gla_gate_w_fwd, gla_gate_b_fwd,
           gla_gate_w_bwd, gla_gate_b_bwd, gla_norm, w_in_odd, gqa_q_norm, gqa_k_norm, w_o, ln1_g,
           ln1_b, router_w, w1, w3, w2, ple_gate_w, ple_gate_b, ple_w, ln2_g, ln2_b):
    b, s, d = x.shape
    depth = w_o.shape[0]
    t = b * s
    alpha = (2.0 * depth) ** 0.25
    cap = EC_CAPACITY_FACTOR * s // N_EXPERTS
    tm = min(ROW_BLOCK, s)
    tq = min(QUERY_BLOCK, s)
    assert s % tm == 0 and s % tq == 0 and s % GRID_W == 0 and tm % (2 * GLA_CHUNK) == 0
    assert (s // KEY_CHUNKS) % KEY_BLK == 0 and tq % V7X_LANES == 0 and s % min(SCATTER_BLOCK, s) == 0
    assert cap % V7X_BF16_SUBLANES == 0 and w1.shape[1] == N_EXPERTS and router_w.shape[2] == N_EXPERTS
    half = w_o.shape[1] // 2
    cos_a, sin_a = _rope_tables(s, MLA_ROPE, PE_LO, MLA_PAD)
    cos_c, sin_c = _rope_tables(s, GQA_HEAD_DIM, 0, GQA_KV_HEADS * GQA_HEAD_DIM)
    hw = GQA_KV_HEADS * GQA_HEAD_DIM
    head_of = np.arange(hw) // GQA_HEAD_DIM
    avg = jnp.asarray((head_of[:, None] == head_of[None, :]) / GQA_HEAD_DIM, BF16)

    n_e, ff = w1.shape[1], w1.shape[3]
    w1t = _transpose_cast(w1.reshape(depth * n_e, d, ff))
    w3t = _transpose_cast(w3.reshape(depth * n_e, d, ff))
    w2t = _transpose_cast(w2.reshape(depth * n_e, ff, d))
    p_all = p.reshape(depth * t, p.shape[-1])

    x2 = x.reshape(t, d)
    for i in range(depth):
        j = i // 2
        if i % 2 == 0:
            w_in_p, w_uq_p, w_uk_p, w_uv, w_gate, b_gate = _prep_even(
                w_in_even[j], w_uq[j], w_ukv[j], gla_gate_w_fwd[j], gla_gate_b_fwd[j],
                gla_gate_w_bwd[j], gla_gate_b_bwd[j])
            q, k, v, gq, gk, gv, la, gr = _even_proj(
                x2, w_in_p, mla_q_norm[j][None, :], w_uq_p, mla_kv_norm[j][None, :], w_uk_p, w_uv,
                w_gate, b_gate, cos_a, sin_a, s, tm)
            r3 = lambda a: a.reshape(b, s, a.shape[-1])
            o_mla = _mla_attn(r3(q), r3(k), r3(v), tq).reshape(t, -1)
            o_gla = _gla(r3(gq), r3(gk), r3(la), r3(gv), r3(gr), gla_norm[j][None, :]).reshape(t, -1)
            mix_a, mix_b, col_a, col_b = o_mla, o_gla, 0, 0
        else:
            q, k, v = _odd_proj(x2, w_in_odd[j].astype(BF16),
                                jnp.tile(gqa_q_norm[j], GQA_KV_HEADS)[None, :],
                                jnp.tile(gqa_k_norm[j], GQA_KV_HEADS)[None, :],
                                avg, cos_c, sin_c, s, tm)
            r3 = lambda a: a.reshape(b, s, a.shape[-1])
            o = _gqa_attn(r3(q), r3(k), r3(v), tq).reshape(t, -1)
            mix_a, mix_b, col_a, col_b = o, o, 0, 1
        wo = w_o[i].astype(BF16)
        x1, x1t = _out_ln(x2, mix_a, mix_b, col_a, col_b, wo[:half], wo[half:],
                          ln1_g[i][None, :], ln1_b[i][None, :], alpha, s, tm)
        aff, slot = _route(x1.reshape(b, s, d), router_w[i].T, cap)
        ffn_t = _moe(x1t, slot, aff, w1t, w3t, w2t, cap, i)
        x2 = _ple_ln(x1, ffn_t, p_all, ple_gate_w[i].astype(BF16),
                     ple_gate_b[i][None, :], ple_w[i].astype(BF16), ln2_g[i][None, :], ln2_b[i][None, :],
                     alpha, tm, i)
    return x2.reshape(b, s, d)
```

```python
import functools
import math

import jax
import jax.numpy as jnp
import numpy as np
from jax import lax
from jax.experimental import pallas as pl
from jax.experimental.pallas import tpu as pltpu

F32 = jnp.float32
BF16 = jnp.bfloat16

V7X_LANES = 128
V7X_BF16_SUBLANES = 16
V7X_VMEM_BYTES = 64 * 1024 * 1024
VMEM_LIMIT = V7X_VMEM_BYTES - 8 * 1024 * 1024

ROW_BLOCK = 1024
QUERY_BLOCK = 1024
SCATTER_BLOCK = 512

GRID_W = 64
ROPE_THETA = 10000.0
EPS = 1e-6
MLA_HEADS, MLA_Q_LORA, MLA_KV_LORA = 8, 256, 128
MLA_NOPE, MLA_ROPE, MLA_V = 64, 32, 64
GLA_HEADS, GLA_DK, GLA_DV = 4, 64, 128
GLA_GATE_RANK, GLA_GATE_NORM, GLA_CHUNK = 16, 16.0, 64
GQA_HEADS, GQA_KV_HEADS, GQA_HEAD_DIM = 16, 4, 64
N_EXPERTS, EC_CAPACITY_FACTOR = 16, 2
EVEN_SPLITS = (MLA_Q_LORA, MLA_KV_LORA, MLA_ROPE, GLA_HEADS * GLA_DK, GLA_HEADS * GLA_DK,
               GLA_HEADS * GLA_DV, 2 * GLA_GATE_RANK, GLA_HEADS * GLA_DV)
ODD_SPLITS = (GQA_HEADS * GQA_HEAD_DIM, GQA_KV_HEADS * GQA_HEAD_DIM, GQA_KV_HEADS * GQA_HEAD_DIM)
MLA_PAD = V7X_LANES
PE_LO, PE_HI = MLA_NOPE, MLA_NOPE + MLA_ROPE


def _dot(a, b):
    return jnp.dot(a, b, preferred_element_type=F32)


def _dot_nt(a, b):
    return lax.dot_general(a, b, (((1,), (1,)), ((), ())), preferred_element_type=F32)


def _dot_tn(a, b):
    return lax.dot_general(a, b, (((0,), (0,)), ((), ())), preferred_element_type=F32)


def _split2(x):
    hi = x.astype(BF16)
    lo = (x - hi.astype(F32)).astype(BF16)
    return hi, lo


def _dot_sel(sel, x):
    hi, lo = _split2(x)
    return _dot(sel, hi) + _dot(sel, lo)


def _dot_x_sel(x, sel):
    hi, lo = _split2(x)
    return _dot(hi, sel) + _dot(lo, sel)


def _params(sem):
    return pltpu.CompilerParams(dimension_semantics=sem, vmem_limit_bytes=VMEM_LIMIT)


def _rot_half(x, half):
    w = x.shape[-1]
    lane = lax.broadcasted_iota(jnp.int32, x.shape, x.ndim - 1)
    first = (lane % (2 * half)) < half
    return jnp.where(first, pltpu.roll(x, w - half, x.ndim - 1), pltpu.roll(x, half, x.ndim - 1))


def _layer_norm(y, g, b):
    mu = jnp.mean(y, axis=-1, keepdims=True)
    yc = y - mu
    var = jnp.mean(yc * yc, axis=-1, keepdims=True)
    return yc * lax.rsqrt(var + EPS) * g + b


def _rms(x, g):
    return x * lax.rsqrt(jnp.mean(x * x, axis=-1, keepdims=True) + EPS) * g


def _log_sigmoid(z):
    return jnp.minimum(z, 0.0) - jnp.log1p(jnp.exp(-jnp.abs(z)))


def _sigmoid(z):
    return 1.0 / (1.0 + jnp.exp(-z))


LOG2_E = math.log2(math.e)
KEY_BLK = 128


KEY_CHUNKS = 4


def _exp2_cols(m, s_scr, e_scr):
    seq, tq = s_scr.shape
    for c in range(tq // V7X_LANES):
        cols = slice(c * V7X_LANES, (c + 1) * V7X_LANES)
        m_b = jnp.broadcast_to(m[:, cols], (KEY_BLK, V7X_LANES))
        for r in range(seq // KEY_BLK):
            rows = slice(r * KEY_BLK, (r + 1) * KEY_BLK)
            e_scr[rows, cols] = jnp.exp2(s_scr[rows, cols] - m_b).astype(BF16)


VT_ROWS_PAD = V7X_BF16_SUBLANES


def _values_t(v_t):
    dv, seq = v_t.shape
    extra = jnp.where(lax.broadcasted_iota(jnp.int32, (VT_ROWS_PAD, seq), 0) == 0, 1.0, 0.0)
    return jnp.concatenate([v_t.astype(BF16), extra.astype(BF16)], axis=0)


SHIFT_SAMPLE = 128
DEN_LO, DEN_HI = 2.0 ** -90, 2.0 ** 90


def _attend_heads_one_pass(n_heads, k_of, q_of, vt_of, e_ref):
    seq = e_ref.shape[1]
    ck = seq // KEY_CHUNKS
    assert ck % SHIFT_SAMPLE == 0
    bounds = sorted({SHIFT_SAMPLE, *range(ck, seq + 1, ck)})
    outs, dens = [], []
    for h in range(n_heads):
        q = q_of(h)
        sample = slice(0, SHIFT_SAMPLE)
        s0 = _dot_nt(k_of(h, sample), q)
        shift = jnp.max(s0, axis=0, keepdims=True)
        e_ref[h, sample, :] = jnp.exp2(s0 - shift).astype(BF16)
        for lo, hi in zip(bounds[:-1], bounds[1:]):
            keys = slice(lo, hi)
            e_ref[h, keys, :] = jnp.exp2(_dot_nt(k_of(h, keys), q) - shift).astype(BF16)
        o = _dot(vt_of(h, slice(0, seq)), e_ref[h])
        dv = o.shape[0] - VT_ROWS_PAD
        outs.append(o[:dv] * (1.0 / o[dv:dv + 1]))
        dens.append(o[dv:dv + 1])
    return jnp.concatenate(outs, axis=0), jnp.concatenate(dens, axis=0)


def _attend_heads(n_heads, k_of, q_of, vt_of, s_ref, e_ref):
    seq = s_ref.shape[1]
    all_keys = slice(0, seq)
    outs = []
    for h in range(n_heads):
        s_scr = s_ref.at[h % 2]
        s = _dot_nt(k_of(h, all_keys), q_of(h))
        s_scr[...] = s
        _exp2_cols(jnp.max(s, axis=0, keepdims=True), s_scr, e_ref.at[h])
        o = _dot(vt_of(h, all_keys), e_ref[h])
        dv = o.shape[0] - VT_ROWS_PAD
        outs.append(o[:dv] * (1.0 / o[dv:dv + 1]))
    return jnp.concatenate(outs, axis=0)


def _attend_and_store(n_heads, k_of, q_of, vt_of, s_ref, e_ref, o_ref):
    out_t, den = _attend_heads_one_pass(n_heads, k_of, q_of, vt_of, e_ref)
    o_ref[0] = out_t.T.astype(o_ref.dtype)
    in_range = (den > DEN_LO) & (den < DEN_HI)
    n_bad = jnp.sum(jnp.where(in_range, 0.0, 1.0))

    @pl.when(n_bad > 0.0)
    def _():
        o_ref[0] = _attend_heads(n_heads, k_of, q_of, vt_of, s_ref, e_ref).T.astype(o_ref.dtype)


def _even_proj_kernel(x_ref, w_in_ref, qn_ref, w_uq_ref, kvn_ref, w_uk_ref, w_uv_ref,
                      w_gate_ref, b_gate_ref, cq_ref, sq_ref,
                      q_ref, k_ref, v_ref, gq_ref, gk_ref, gv_ref, la_ref, gr_ref):
    h = _dot(x_ref[...].astype(BF16), w_in_ref[...])
    cos = cq_ref[...]
    sin = sq_ref[...]
    lane = lax.broadcasted_iota(jnp.int32, cos.shape, 1)
    pe_lane = (lane >= PE_LO) & (lane < PE_HI)

    def rope(t):
        return t * cos + _rot_half(t, MLA_ROPE // 2) * sin

    c_q = _rms(h[:, 0:256], qn_ref[...])
    q = _dot(c_q.astype(BF16), w_uq_ref[...]) * ((MLA_NOPE + MLA_ROPE) ** -0.5 * LOG2_E)
    c_kv = _rms(h[:, 256:384], kvn_ref[...]).astype(BF16)
    kn = _dot(c_kv, w_uk_ref[...])
    v_ref[...] = _dot(c_kv, w_uv_ref[...]).astype(v_ref.dtype)
    chunk = h[:, 384:512]
    k_pe = jnp.where(pe_lane, rope(chunk), 0.0)
    for hd in range(MLA_HEADS):
        sl = slice(hd * MLA_PAD, (hd + 1) * MLA_PAD)
        q_ref[:, sl] = rope(q[:, sl]).astype(q_ref.dtype)
        k_ref[:, sl] = (kn[:, sl] + k_pe).astype(k_ref.dtype)
    z = _dot(chunk.astype(BF16), w_gate_ref[...]) + b_gate_ref[...]
    la = _log_sigmoid(z) * (1.0 / GLA_GATE_NORM)
    blk = 2 * GLA_CHUNK
    r = lax.broadcasted_iota(jnp.int32, (blk, blk), 0)
    c = lax.broadcasted_iota(jnp.int32, (blk, blk), 1)
    same = (r // GLA_CHUNK) == (c // GLA_CHUNK)
    lower = jnp.where(same & (c <= r), 1.0, 0.0).astype(BF16)
    upper = jnp.where(same & (c >= r), 1.0, 0.0).astype(BF16)
    dkw = GLA_HEADS * GLA_DK
    for t in range(la.shape[0] // blk):
        rows = slice(t * blk, (t + 1) * blk)
        la_ref[rows, :dkw] = _dot_sel(lower, la[rows, :dkw])
        la_ref[rows, dkw:] = _dot_sel(upper, la[rows, dkw:])
    gq_ref[...] = h[:, 512:768] * (GLA_DK ** -0.5)
    gk_ref[...] = h[:, 768:1024]
    gv_ref[...] = h[:, 1024:1536].astype(gv_ref.dtype)
    gr = h[:, 1536:2048]
    gr_ref[...] = gr * _sigmoid(gr)


def _even_proj(x2, w_in_p, qn, w_uq_p, kvn, w_uk_p, w_uv, w_gate, b_gate, cq, sq, seq, tm):
    t, d = x2.shape
    nsb = seq // tm
    row = lambda i: (i, 0)
    fixed = lambda i: (0, 0)
    pos = lambda i: (i % nsb, 0)
    full = lambda a: pl.BlockSpec(a.shape, fixed)
    outs = [(1024, BF16), (1024, BF16), (512, BF16), (256, F32), (256, F32), (512, BF16), (512, F32), (512, F32)]
    return pl.pallas_call(
        _even_proj_kernel,
        grid=(t // tm,),
        in_specs=[pl.BlockSpec((tm, d), row), full(w_in_p), full(qn), full(w_uq_p), full(kvn),
                  full(w_uk_p), full(w_uv), full(w_gate), full(b_gate),
                  pl.BlockSpec((tm, MLA_PAD), pos), pl.BlockSpec((tm, MLA_PAD), pos)],
        out_specs=[pl.BlockSpec((tm, n), row) for n, _ in outs],
        out_shape=[jax.ShapeDtypeStruct((t, n), dt) for n, dt in outs],
        compiler_params=_params(("parallel",)),
        name="even_proj",
    )(x2, w_in_p, qn, w_uq_p, kvn, w_uk_p, w_uv, w_gate, b_gate, cq, sq)


MLA_STEP_HEADS = 4


def _mla_attn_kernel(q_ref, k_ref, v_ref, o_ref, vt_ref, s_ref, e_ref):
    out_w = MLA_STEP_HEADS * MLA_V

    @pl.when(pl.program_id(2) == 0)
    def _():
        r = lax.broadcasted_iota(jnp.int32, (out_w, out_w), 0)
        c = lax.broadcasted_iota(jnp.int32, (out_w, out_w), 1)
        eye = jnp.where(r == c, 1.0, 0.0).astype(BF16)
        v_t = _dot_nt(eye, v_ref[0])
        for a in range(MLA_STEP_HEADS):
            vt_ref[a] = _values_t(v_t[a * MLA_V:(a + 1) * MLA_V])

    _attend_and_store(
        MLA_STEP_HEADS,
        lambda a, keys: k_ref[0, keys, a * MLA_PAD:(a + 1) * MLA_PAD],
        lambda a: q_ref[0, :, a * MLA_PAD:(a + 1) * MLA_PAD],
        lambda a, keys: vt_ref[a, :, keys],
        s_ref, e_ref, o_ref)


def _mla_attn(q, k, v, tq):
    b, s, _ = q.shape
    n = MLA_STEP_HEADS
    return pl.pallas_call(
        _mla_attn_kernel,
        grid=(b, MLA_HEADS // n, s // tq),
        in_specs=[pl.BlockSpec((1, tq, n * MLA_PAD), lambda i, j, t: (i, t, j)),
                  pl.BlockSpec((1, s, n * MLA_PAD), lambda i, j, t: (i, 0, j)),
                  pl.BlockSpec((1, s, n * MLA_V), lambda i, j, t: (i, 0, j))],
        out_specs=pl.BlockSpec((1, tq, n * MLA_V), lambda i, j, t: (i, t, j)),
        out_shape=jax.ShapeDtypeStruct((b, s, MLA_HEADS * MLA_V), BF16),
        scratch_shapes=[pltpu.VMEM((n, MLA_V + VT_ROWS_PAD, s), BF16),
                        pltpu.VMEM((2, s, tq), F32), pltpu.VMEM((n, s, tq), BF16)],
        compiler_params=_params(("parallel", "parallel", "arbitrary")),
        name="mla_attn",
    )(q, k, v)


def _gla_kernel(q_ref, k_ref, cum_ref, v_ref, gr_ref, g_ref, o_ref, state_ref, of_ref, ob_ref):
    seq = q_ref.shape[1]
    n_chunks = seq // GLA_CHUNK
    L = GLA_CHUNK
    dkw = GLA_HEADS * GLA_DK
    dvw = GLA_HEADS * GLA_DV

    def iota(shape, dim):
        return lax.broadcasted_iota(jnp.int32, shape, dim)

    row_l = iota((L, dkw), 0)
    col_m = iota((L, dkw), 1) % L
    k_own = (iota((dkw, dkw), 0) // L) == (iota((dkw, dkw), 1) // GLA_DK)
    v_own = (iota((dkw, dvw), 0) // L) == (iota((dkw, dvw), 1) // GLA_DV)
    s_own = (iota((dvw, dkw), 0) // GLA_DV) == (iota((dvw, dkw), 1) // GLA_DK)
    state_ref[...] = jnp.zeros_like(state_ref)

    def body(i, carry):
        for d, fwd in enumerate((True, False)):
            n = i if fwd else n_chunks - 1 - i
            rows = pl.ds(pl.multiple_of(n * L, L), L)
            cum = cum_ref[0, rows, d * dkw:(d + 1) * dkw]
            last = cum[L - 1:L, :] if fwd else cum[0:1, :]
            q = q_ref[0, rows, :]
            k = k_ref[0, rows, :]
            v = v_ref[0, rows, :]
            qe = (q * jnp.exp(cum)).astype(BF16)
            kg = (k * jnp.exp(-cum)).astype(BF16)
            kdec = (k * jnp.exp(last - cum)).astype(BF16)
            k_blk = jnp.where(k_own, jnp.concatenate([kg] * GLA_HEADS, axis=0), jnp.zeros((), BF16))
            att = _dot_nt(qe, k_blk)
            keep = (col_m <= row_l) if fwd else (col_m >= row_l)
            att = jnp.where(keep, att, 0.0).astype(BF16)
            v_blk = jnp.where(v_own, jnp.concatenate([v] * GLA_HEADS, axis=0), jnp.zeros((), BF16))
            st = state_ref[d]
            o = _dot(att, v_blk) + _dot_nt(qe, st.astype(BF16))
            state_ref[d] = st * jnp.exp(last) + jnp.where(s_own, _dot_tn(v, kdec), 0.0)
            if fwd:
                of_ref[rows, :] = o
            else:
                ob_ref[rows, :] = o
        return carry

    lax.fori_loop(0, n_chunks, body, 0, unroll=4)

    def finish(n, carry):
        rows = pl.ds(pl.multiple_of(n * L, L), L)
        tot = of_ref[rows, :] + ob_ref[rows, :]
        for hd in range(GLA_HEADS):
            sl = slice(hd * GLA_DV, (hd + 1) * GLA_DV)
            o_ref[0, rows, sl] = (_rms(tot[:, sl], g_ref[...]) * gr_ref[0, rows, sl]).astype(o_ref.dtype)
        return carry

    lax.fori_loop(0, n_chunks, finish, 0)


def _gla(gq, gk, cum, gv, gr, g_norm):
    b, s, _ = gq.shape
    dkw = GLA_HEADS * GLA_DK
    dvw = GLA_HEADS * GLA_DV
    blk = lambda w: pl.BlockSpec((1, s, w), lambda i: (i, 0, 0))
    return pl.pallas_call(
        _gla_kernel,
        grid=(b,),
        in_specs=[blk(dkw), blk(dkw), blk(2 * dkw), blk(dvw), blk(dvw),
                  pl.BlockSpec((1, GLA_DV), lambda i: (0, 0))],
        out_specs=blk(dvw),
        out_shape=jax.ShapeDtypeStruct((b, s, dvw), BF16),
        scratch_shapes=[pltpu.VMEM((2, dvw, dkw), F32), pltpu.VMEM((s, dvw), F32), pltpu.VMEM((s, dvw), F32)],
        compiler_params=_params(("parallel",)),
        name="gla",
    )(gq, gk, cum, gv, gr, g_norm)


def _odd_proj_kernel(x_ref, w_ref, gq_ref, gk_ref, avg_ref, cos_ref, sin_ref, q_ref, k_ref, v_ref):
    h = _dot(x_ref[...].astype(BF16), w_ref[...])
    cos = cos_ref[...]
    sin = sin_ref[...]
    avg = avg_ref[...]
    nq = GQA_HEADS * GQA_HEAD_DIM
    nk = GQA_KV_HEADS * GQA_HEAD_DIM

    def norm_rope(t, g):
        ms = _dot_x_sel(t * t, avg)
        t = t * lax.rsqrt(ms + EPS) * g
        return t * cos + _rot_half(t, GQA_HEAD_DIM // 2) * sin

    for c in range(nq // nk):
        sl = slice(c * nk, (c + 1) * nk)
        q_ref[:, sl] = (norm_rope(h[:, sl], gq_ref[...]) * (GQA_HEAD_DIM ** -0.5 * LOG2_E)).astype(q_ref.dtype)
    k_ref[...] = norm_rope(h[:, nq:nq + nk], gk_ref[...]).astype(k_ref.dtype)
    v_ref[...] = h[:, nq + nk:].astype(v_ref.dtype)


def _odd_proj(x2, w, gq, gk, avg, cos, sin, seq, tm):
    t, d = x2.shape
    nsb = seq // tm
    row = lambda i: (i, 0)
    fixed = lambda i: (0, 0)
    pos = lambda i: (i % nsb, 0)
    full = lambda a: pl.BlockSpec(a.shape, fixed)
    outs = [ODD_SPLITS[0], ODD_SPLITS[1], ODD_SPLITS[2]]
    return pl.pallas_call(
        _odd_proj_kernel,
        grid=(t // tm,),
        in_specs=[pl.BlockSpec((tm, d), row), full(w), full(gq), full(gk), full(avg),
                  pl.BlockSpec((tm, cos.shape[1]), pos), pl.BlockSpec((tm, sin.shape[1]), pos)],
        out_specs=[pl.BlockSpec((tm, n), row) for n in outs],
        out_shape=[jax.ShapeDtypeStruct((t, n), BF16) for n in outs],
        compiler_params=_params(("parallel",)),
        name="odd_proj",
    )(x2, w, gq, gk, avg, cos, sin)


def _gqa_attn_kernel(q_ref, k_ref, v_ref, o_ref, kp_ref, vt_ref, s_ref, e_ref):
    j = pl.program_id(1)
    group = GQA_HEADS // GQA_KV_HEADS
    width = GQA_KV_HEADS * GQA_HEAD_DIM
    pair = 2 * GQA_HEAD_DIM

    @pl.when(pl.program_id(2) == 0)
    def _():
        r = lax.broadcasted_iota(jnp.int32, (width, pair), 0)
        c = lax.broadcasted_iota(jnp.int32, (width, pair), 1)
        twice = (r // GQA_HEAD_DIM == j) & (r % GQA_HEAD_DIM == c % GQA_HEAD_DIM)
        kp_ref[...] = _dot(k_ref[0], jnp.where(twice, 1.0, 0.0).astype(BF16)).astype(BF16)
        rv = lax.broadcasted_iota(jnp.int32, (GQA_HEAD_DIM, width), 0)
        cv = lax.broadcasted_iota(jnp.int32, (GQA_HEAD_DIM, width), 1)
        mine = cv == j * GQA_HEAD_DIM + rv
        vt_ref[...] = _values_t(_dot_nt(jnp.where(mine, 1.0, 0.0).astype(BF16), v_ref[0]))

    lane = lax.broadcasted_iota(jnp.int32, (q_ref.shape[1], pair), 1)

    def q_of(g):
        qp = q_ref[0, :, (g // 2) * pair:(g // 2 + 1) * pair]
        return jnp.where(lane // GQA_HEAD_DIM == g % 2, qp, jnp.zeros_like(qp))

    _attend_and_store(group, lambda g, keys: kp_ref[keys, :], q_of, lambda g, keys: vt_ref[:, keys],
                      s_ref, e_ref, o_ref)


def _gqa_attn(q, k, v, tq):
    b, s, _ = q.shape
    width = GQA_KV_HEADS * GQA_HEAD_DIM
    group = GQA_HEADS // GQA_KV_HEADS
    return pl.pallas_call(
        _gqa_attn_kernel,
        grid=(b, GQA_KV_HEADS, s // tq),
        in_specs=[pl.BlockSpec((1, tq, width), lambda i, j, t: (i, t, j)),
                  pl.BlockSpec((1, s, width), lambda i, j, t: (i, 0, 0)),
                  pl.BlockSpec((1, s, width), lambda i, j, t: (i, 0, 0))],
        out_specs=pl.BlockSpec((1, tq, width), lambda i, j, t: (i, t, j)),
        out_shape=jax.ShapeDtypeStruct((b, s, GQA_HEADS * GQA_HEAD_DIM), BF16),
        scratch_shapes=[pltpu.VMEM((s, 2 * GQA_HEAD_DIM), BF16),
                        pltpu.VMEM((GQA_HEAD_DIM + VT_ROWS_PAD, s), BF16),
                        pltpu.VMEM((2, s, tq), F32), pltpu.VMEM((group, s, tq), BF16)],
        compiler_params=_params(("parallel", "parallel", "arbitrary")),
        name="gqa_attn",
    )(q, k, v)


def _out_ln_kernel(alpha, x_ref, ma_ref, mb_ref, wa_ref, wb_ref, g_ref, b_ref, y_ref, yt_ref):
    y = alpha * x_ref[...] + _dot(ma_ref[...].astype(BF16), wa_ref[...]) \
        + _dot(mb_ref[...].astype(BF16), wb_ref[...])
    y = _layer_norm(y, g_ref[...], b_ref[...])
    y_ref[...] = y
    yt_ref[0] = y.T.astype(BF16)


def _out_ln(x2, mix_a, mix_b, col_a, col_b, w_a, w_b, g, b, alpha, seq, tm):
    t, d = x2.shape
    half = w_a.shape[0]
    nsb = seq // tm
    row = lambda i: (i, 0)
    fixed = lambda i: (0, 0)
    return pl.pallas_call(
        functools.partial(_out_ln_kernel, alpha),
        grid=(t // tm,),
        in_specs=[pl.BlockSpec((tm, d), row),
                  pl.BlockSpec((tm, half), lambda i: (i, col_a)),
                  pl.BlockSpec((tm, half), lambda i: (i, col_b)),
                  pl.BlockSpec(w_a.shape, fixed), pl.BlockSpec(w_b.shape, fixed),
                  pl.BlockSpec(g.shape, fixed), pl.BlockSpec(b.shape, fixed)],
        out_specs=[pl.BlockSpec((tm, d), row),
                   pl.BlockSpec((1, d, tm), lambda i: (i // nsb, 0, i % nsb))],
        out_shape=[jax.ShapeDtypeStruct((t, d), F32), jax.ShapeDtypeStruct((t // seq, d, seq), BF16)],
        compiler_params=_params(("parallel",)),
        name="out_ln",
    )(x2, mix_a, mix_b, w_a, w_b, g, b)


def _route_kernel(cap, x_ref, rw_ref, aff_ref, slot_ref):
    x = x_ref[0]
    seq = x.shape[0]
    xh, xm = _split2(x)
    wh, wm = _split2(rw_ref[...])
    logits = _dot_nt(wh, xh) + (_dot_nt(wh, xm) + _dot_nt(wm, xh))
    e = jnp.exp(logits - jnp.max(logits, axis=0, keepdims=True))
    aff = e / jnp.sum(e, axis=0, keepdims=True)
    aff_ref[0] = aff

    def enough(cand):
        return jnp.sum((aff >= pltpu.bitcast(cand, F32)).astype(jnp.int32), axis=1, keepdims=True) >= cap

    top = jnp.full((aff.shape[0], 1), 1 << 30, jnp.int32)
    thr0 = jnp.where(enough(top), top, 0)

    def pick(i, thr):
        lo = 28 - 2 * i
        c1, c2, c3 = thr | (jnp.int32(1) << lo), thr | (jnp.int32(2) << lo), thr | (jnp.int32(3) << lo)
        return jnp.where(enough(c3), c3, jnp.where(enough(c2), c2, jnp.where(enough(c1), c1, thr)))

    thr = pltpu.bitcast(lax.fori_loop(0, 15, pick, thr0), F32)
    above = aff > thr
    tie = aff == thr
    need = cap - jnp.sum(above.astype(jnp.int32), axis=1, keepdims=True)

    blk = 256 if seq % 256 == 0 else V7X_LANES
    r = lax.broadcasted_iota(jnp.int32, (blk, blk), 0)
    c = lax.broadcasted_iota(jnp.int32, (blk, blk), 1)
    before = jnp.where(r < c, 1.0, 0.0).astype(BF16)

    def prefix(mask):
        m = jnp.where(mask, 1.0, 0.0).astype(BF16)
        run = jnp.zeros((mask.shape[0], 1), F32)
        parts = []
        for t in range(seq // blk):
            mb = m[:, t * blk:(t + 1) * blk]
            parts.append(_dot(mb, before) + run)
            run = run + jnp.sum(mb.astype(F32), axis=1, keepdims=True)
        return jnp.concatenate(parts, axis=1).astype(jnp.int32)

    chosen = above | (tie & (prefix(tie) < need))
    slot_ref[0] = jnp.where(chosen, prefix(chosen), -1)


def _route(x1, rw_t, cap):
    b, s, d = x1.shape
    e = rw_t.shape[0]
    return pl.pallas_call(
        functools.partial(_route_kernel, cap),
        grid=(b,),
        in_specs=[pl.BlockSpec((1, s, d), lambda i: (i, 0, 0)), pl.BlockSpec((e, d), lambda i: (0, 0))],
        out_specs=[pl.BlockSpec((1, e, s), lambda i: (i, 0, 0)), pl.BlockSpec((1, e, s), lambda i: (i, 0, 0))],
        out_shape=[jax.ShapeDtypeStruct((b, e, s), F32), jax.ShapeDtypeStruct((b, e, s), jnp.int32)],
        compiler_params=_params(("parallel",)),
        name="route",
    )(x1, rw_t)


def _moe_kernel(cap, ts, xt_ref, slot_ref, aff_ref, w1_ref, w3_ref, w2_ref, o_ref):
    e_id = pl.program_id(1)
    seq = xt_ref.shape[2]
    ff = w2_ref.shape[2]

    @pl.when(e_id == 0)
    def _():
        o_ref[...] = jnp.zeros_like(o_ref)

    c_row = lax.broadcasted_iota(jnp.int32, (cap, seq), 0)
    pick = jnp.where(slot_ref[0, 0] == c_row, 1.0, 0.0).astype(BF16)
    hd2 = xt_ref.shape[1] // 2
    xg = jnp.concatenate([_dot_nt(xt_ref[0, :hd2, :], pick), _dot_nt(xt_ref[0, hd2:, :], pick)],
                         axis=0).astype(BF16)
    h1 = _dot(w1_ref[0], xg)
    h3 = _dot(w3_ref[0], xg)
    hid = (h1 * _sigmoid(h1) * h3).astype(BF16)
    ye = jnp.concatenate([_dot(w2_ref[0, :hd2, :], hid), _dot(w2_ref[0, hd2:, :], hid)],
                         axis=0).astype(BF16)
    for t in range(seq // ts):
        cols = slice(t * ts, (t + 1) * ts)
        o_ref[0, :, cols] += _dot(ye, pick[:, cols]) * aff_ref[0, 0, :, cols]


def _moe(xt, slot, aff, w1t, w3t, w2t, cap, layer):
    b, d, s = xt.shape
    e = slot.shape[1]
    ff = w2t.shape[2]
    ts = min(SCATTER_BLOCK, s)
    w_blk = lambda i, j: (layer * e + j, 0, 0)
    return pl.pallas_call(
        functools.partial(_moe_kernel, cap, ts),
        grid=(b, e),
        in_specs=[pl.BlockSpec((1, d, s), lambda i, j: (i, 0, 0)),
                  pl.BlockSpec((1, 1, 1, s), lambda i, j: (i, j, 0, 0)),
                  pl.BlockSpec((1, 1, 1, s), lambda i, j: (i, j, 0, 0)),
                  pl.BlockSpec((1, ff, d), w_blk), pl.BlockSpec((1, ff, d), w_blk),
                  pl.BlockSpec((1, d, ff), w_blk)],
        out_specs=pl.BlockSpec((1, d, s), lambda i, j: (i, 0, 0)),
        out_shape=jax.ShapeDtypeStruct((b, d, s), F32),
        compiler_params=_params(("parallel", "arbitrary")),
        name="moe",
    )(xt, slot.reshape(b, e, 1, s), aff.reshape(b, e, 1, s), w1t, w3t, w2t)


def _transpose_cast_kernel(w_ref, o_ref):
    o_ref[0] = w_ref[0].T.astype(o_ref.dtype)


def _transpose_cast(w):
    n, r, c = w.shape
    return pl.pallas_call(
        _transpose_cast_kernel,
        grid=(n,),
        in_specs=[pl.BlockSpec((1, r, c), lambda i: (i, 0, 0))],
        out_specs=pl.BlockSpec((1, c, r), lambda i: (i, 0, 0)),
        out_shape=jax.ShapeDtypeStruct((n, c, r), BF16),
        compiler_params=_params(("parallel",)),
        name="transpose_cast",
    )(w)


def _ple_ln_kernel(alpha, x_ref, f_ref, p_ref, wg_ref, bg_ref, wp_ref, g_ref, b_ref, y_ref):
    x = x_ref[...]
    gate = _sigmoid(_dot(x.astype(BF16), wg_ref[...]) + bg_ref[...])
    ple = gate * _dot(p_ref[...].astype(BF16), wp_ref[...])
    ffn = f_ref[0].T
    y_ref[...] = _layer_norm(alpha * x + ffn + ple, g_ref[...], b_ref[...])


def _ple_ln(x1, ffn_t, p2, wg, bg, wp, g, b, alpha, tm, layer):
    t, d = x1.shape
    nsb = ffn_t.shape[2] // tm
    row = lambda i: (i, 0)
    p_row = lambda i: (layer * (t // tm) + i, 0)
    fixed = lambda i: (0, 0)
    full = lambda a: pl.BlockSpec(a.shape, fixed)
    return pl.pallas_call(
        functools.partial(_ple_ln_kernel, alpha),
        grid=(t // tm,),
        in_specs=[pl.BlockSpec((tm, d), row),
                  pl.BlockSpec((1, d, tm), lambda i: (i // nsb, 0, i % nsb)),
                  pl.BlockSpec((tm, p2.shape[1]), p_row), full(wg), full(bg), full(wp), full(g), full(b)],
        out_specs=pl.BlockSpec((tm, d), row),
        out_shape=jax.ShapeDtypeStruct((t, d), F32),
        compiler_params=_params(("parallel",)),
        name="ple_ln",
    )(x1, ffn_t, p2, wg, bg, wp, g, b)


def _rope_tables(seq, rot_dim, lo, width):
    rows = seq // GRID_W
    row = jnp.repeat(jnp.arange(rows, dtype=F32), GRID_W)
    col = jnp.tile(jnp.arange(GRID_W, dtype=F32), rows)
    axis_dim = rot_dim // 2
    inv = ROPE_THETA ** (-jnp.arange(0, axis_dim, 2, dtype=F32) / axis_dim)
    ang = jnp.concatenate([row[:, None] * inv, col[:, None] * inv], axis=-1)
    cos, sin = jnp.cos(ang), jnp.sin(ang)
    cos2 = jnp.concatenate([cos, cos], axis=-1)
    sin2 = jnp.concatenate([-sin, sin], axis=-1)
    if lo == 0:
        reps = width // rot_dim
        return jnp.tile(cos2, (1, reps)), jnp.tile(sin2, (1, reps))
    pad_l = jnp.ones((seq, lo), F32)
    pad_r = jnp.ones((seq, width - lo - rot_dim), F32)
    cos_t = jnp.concatenate([pad_l, cos2, pad_r], axis=-1)
    sin_t = jnp.concatenate([0 * pad_l, sin2, 0 * pad_r], axis=-1)
    return cos_t, sin_t


def _prep_even(w_in, w_uq, w_ukv, gw_f, gb_f, gw_b, gb_b):
    d = w_in.shape[0]
    offs = np.cumsum(EVEN_SPLITS)[:-1].tolist()
    c_q, c_kv, k_pe, gq, gk, gv, g_lr, gr = jnp.split(w_in, offs, axis=-1)
    z32 = jnp.zeros((d, 32), w_in.dtype)
    chunk = jnp.concatenate([g_lr, z32, k_pe, z32], axis=-1)
    w_in_p = jnp.concatenate([c_q, c_kv, chunk, gq, gk, gv, gr], axis=-1).astype(BF16)
    uq = w_uq.reshape(MLA_Q_LORA, MLA_HEADS, MLA_NOPE + MLA_ROPE)
    uq = jnp.pad(uq, ((0, 0), (0, 0), (0, MLA_PAD - MLA_NOPE - MLA_ROPE)))
    w_uq_p = uq.reshape(MLA_Q_LORA, MLA_HEADS * MLA_PAD).astype(BF16)
    ukv = w_ukv.reshape(MLA_KV_LORA, MLA_HEADS, MLA_NOPE + MLA_V)
    uk = jnp.pad(ukv[:, :, :MLA_NOPE], ((0, 0), (0, 0), (0, MLA_PAD - MLA_NOPE)))
    w_uk_p = uk.reshape(MLA_KV_LORA, MLA_HEADS * MLA_PAD).astype(BF16)
    w_uv = ukv[:, :, MLA_NOPE:].reshape(MLA_KV_LORA, MLA_HEADS * MLA_V).astype(BF16)
    dkw = GLA_HEADS * GLA_DK
    w_gate = jnp.zeros((MLA_PAD, 2 * dkw), F32)
    w_gate = w_gate.at[0:GLA_GATE_RANK, 0:dkw].set(gw_f)
    w_gate = w_gate.at[GLA_GATE_RANK:2 * GLA_GATE_RANK, dkw:].set(gw_b).astype(BF16)
    b_gate = jnp.concatenate([gb_f, gb_b])[None, :]
    return w_in_p, w_uq_p, w_uk_p, w_uv, w_gate, b_gate


def kernel(x, p, w_in_even, mla_q_norm, w_uq, mla_kv_norm, w_ukv, gla_gate_w_fwd, gla_gate_b_fwd,
           gla_gate_w_bwd, gla_gate_b_bwd, gla_norm, w_in_odd, gqa_q_norm, gqa_k_norm, w_o, ln1_g,
           ln1_b, router_w, w1, w3, w2, ple_gate_w, ple_gate_b, ple_w, ln2_g, ln2_b):
    b, s, d = x.shape
    depth = w_o.shape[0]
    t = b * s
    alpha = (2.0 * depth) ** 0.25
    cap = EC_CAPACITY_FACTOR * s // N_EXPERTS
    tm = min(ROW_BLOCK, s)
    tq = min(QUERY_BLOCK, s)
    assert s % tm == 0 and s % tq == 0 and s % GRID_W == 0 and tm % (2 * GLA_CHUNK) == 0
    assert (s // KEY_CHUNKS) % KEY_BLK == 0 and tq % V7X_LANES == 0 and s % min(SCATTER_BLOCK, s) == 0
    assert cap % V7X_BF16_SUBLANES == 0 and w1.shape[1] == N_EXPERTS and router_w.shape[2] == N_EXPERTS
    half = w_o.shape[1] // 2
    cos_a, sin_a = _rope_tables(s, MLA_ROPE, PE_LO, MLA_PAD)
    cos_c, sin_c = _rope_tables(s, GQA_HEAD_DIM, 0, GQA_KV_HEADS * GQA_HEAD_DIM)
    hw = GQA_KV_HEADS * GQA_HEAD_DIM
    head_of = np.arange(hw) // GQA_HEAD_DIM
    avg = jnp.asarray((head_of[:, None] == head_of[None, :]) / GQA_HEAD_DIM, BF16)

    n_e, ff = w1.shape[1], w1.shape[3]
    w1t = _transpose_cast(w1.reshape(depth * n_e, d, ff))
    w3t = _transpose_cast(w3.reshape(depth * n_e, d, ff))
    w2t = _transpose_cast(w2.reshape(depth * n_e, ff, d))
    p_all = p.reshape(depth * t, p.shape[-1])

    x2 = x.reshape(t, d)
    for i in range(depth):
        j = i // 2
        if i % 2 == 0:
            w_in_p, w_uq_p, w_uk_p, w_uv, w_gate, b_gate = _prep_even(
                w_in_even[j], w_uq[j], w_ukv[j], gla_gate_w_fwd[j], gla_gate_b_fwd[j],
                gla_gate_w_bwd[j], gla_gate_b_bwd[j])
            q, k, v, gq, gk, gv, la, gr = _even_proj(
                x2, w_in_p, mla_q_norm[j][None, :], w_uq_p, mla_kv_norm[j][None, :], w_uk_p, w_uv,
                w_gate, b_gate, cos_a, sin_a, s, tm)
            r3 = lambda a: a.reshape(b, s, a.shape[-1])
            o_mla = _mla_attn(r3(q), r3(k), r3(v), tq).reshape(t, -1)
            o_gla = _gla(r3(gq), r3(gk), r3(la), r3(gv), r3(gr), gla_norm[j][None, :]).reshape(t, -1)
            mix_a, mix_b, col_a, col_b = o_mla, o_gla, 0, 0
        else:
            q, k, v = _odd_proj(x2, w_in_odd[j].astype(BF16),
                                jnp.tile(gqa_q_norm[j], GQA_KV_HEADS)[None, :],
                                jnp.tile(gqa_k_norm[j], GQA_KV_HEADS)[None, :],
                                avg, cos_c, sin_c, s, tm)
            r3 = lambda a: a.reshape(b, s, a.shape[-1])
            o = _gqa_attn(r3(q), r3(k), r3(v), tq).reshape(t, -1)
            mix_a, mix_b, col_a, col_b = o, o, 0, 1
        wo = w_o[i].astype(BF16)
        x1, x1t = _out_ln(x2, mix_a, mix_b, col_a, col_b, wo[:half], wo[half:],
                          ln1_g[i][None, :], ln1_b[i][None, :], alpha, s, tm)
        aff, slot = _route(x1.reshape(b, s, d), router_w[i].T, cap)
        ffn_t = _moe(x1t, slot, aff, w1t, w3t, w2t, cap, i)
        x2 = _ple_ln(x1, ffn_t, p_all, ple_gate_w[i].astype(BF16),
                     ple_gate_b[i][None, :], ple_w[i].astype(BF16), ln2_g[i][None, :], ln2_b[i][None, :],
                     alpha, tm, i)
    return x2.reshape(b, s, d)
```

```python
import functools
import math

import jax
import jax.numpy as jnp
import numpy as np
from jax import lax
from jax.experimental import pallas as pl
from jax.experimental.pallas import tpu as pltpu

F32 = jnp.float32
BF16 = jnp.bfloat16

V7X_LANES = 128
V7X_BF16_SUBLANES = 16
V7X_VMEM_BYTES = 64 * 1024 * 1024
VMEM_LIMIT = V7X_VMEM_BYTES - 8 * 1024 * 1024

ROW_BLOCK = 1024
QUERY_BLOCK = 1024
SCATTER_BLOCK = 512

GRID_W = 64
ROPE_THETA = 10000.0
EPS = 1e-6
MLA_HEADS, MLA_Q_LORA, MLA_KV_LORA = 8, 256, 128
MLA_NOPE, MLA_ROPE, MLA_V = 64, 32, 64
GLA_HEADS, GLA_DK, GLA_DV = 4, 64, 128
GLA_GATE_RANK, GLA_GATE_NORM, GLA_CHUNK = 16, 16.0, 64
GQA_HEADS, GQA_KV_HEADS, GQA_HEAD_DIM = 16, 4, 64
N_EXPERTS, EC_CAPACITY_FACTOR = 16, 2
EVEN_SPLITS = (MLA_Q_LORA, MLA_KV_LORA, MLA_ROPE, GLA_HEADS * GLA_DK, GLA_HEADS * GLA_DK,
               GLA_HEADS * GLA_DV, 2 * GLA_GATE_RANK, GLA_HEADS * GLA_DV)
ODD_SPLITS = (GQA_HEADS * GQA_HEAD_DIM, GQA_KV_HEADS * GQA_HEAD_DIM, GQA_KV_HEADS * GQA_HEAD_DIM)
MLA_PAD = V7X_LANES
PE_LO, PE_HI = MLA_NOPE, MLA_NOPE + MLA_ROPE


def _dot(a, b):
    return jnp.dot(a, b, preferred_element_type=F32)


def _dot_nt(a, b):
    return lax.dot_general(a, b, (((1,), (1,)), ((), ())), preferred_element_type=F32)


def _dot_tn(a, b):
    return lax.dot_general(a, b, (((0,), (0,)), ((), ())), preferred_element_type=F32)


def _split2(x):
    hi = x.astype(BF16)
    lo = (x - hi.astype(F32)).astype(BF16)
    return hi, lo


def _dot_sel(sel, x):
    hi, lo = _split2(x)
    return _dot(sel, hi) + _dot(sel, lo)


def _dot_x_sel(x, sel):
    hi, lo = _split2(x)
    return _dot(hi, sel) + _dot(lo, sel)


def _params(sem):
    return pltpu.CompilerParams(dimension_semantics=sem, vmem_limit_bytes=VMEM_LIMIT)


def _rot_half(x, half):
    w = x.shape[-1]
    lane = lax.broadcasted_iota(jnp.int32, x.shape, x.ndim - 1)
    first = (lane % (2 * half)) < half
    return jnp.where(first, pltpu.roll(x, w - half, x.ndim - 1), pltpu.roll(x, half, x.ndim - 1))


def _layer_norm(y, g, b):
    mu = jnp.mean(y, axis=-1, keepdims=True)
    yc = y - mu
    var = jnp.mean(yc * yc, axis=-1, keepdims=True)
    return yc * lax.rsqrt(var + EPS) * g + b


def _rms(x, g):
    return x * lax.rsqrt(jnp.mean(x * x, axis=-1, keepdims=True) + EPS) * g


def _log_sigmoid(z):
    return jnp.minimum(z, 0.0) - jnp.log1p(jnp.exp(-jnp.abs(z)))


def _sigmoid(z):
    return 1.0 / (1.0 + jnp.exp(-z))


LOG2_E = math.log2(math.e)
KEY_BLK = 128


KEY_CHUNKS = 4


def _exp2_cols(m, s_scr, e_scr):
    seq, tq = s_scr.shape
    for c in range(tq // V7X_LANES):
        cols = slice(c * V7X_LANES, (c + 1) * V7X_LANES)
        m_b = jnp.broadcast_to(m[:, cols], (KEY_BLK, V7X_LANES))
        for r in range(seq // KEY_BLK):
            rows = slice(r * KEY_BLK, (r + 1) * KEY_BLK)
            e_scr[rows, cols] = jnp.exp2(s_scr[rows, cols] - m_b).astype(BF16)


VT_ROWS_PAD = V7X_BF16_SUBLANES


def _values_t(v_t):
    dv, seq = v_t.shape
    extra = jnp.where(lax.broadcasted_iota(jnp.int32, (VT_ROWS_PAD, seq), 0) == 0, 1.0, 0.0)
    return jnp.concatenate([v_t.astype(BF16), extra.astype(BF16)], axis=0)


SHIFT_SAMPLE = 128
DEN_LO, DEN_HI = 2.0 ** -90, 2.0 ** 90


def _attend_heads_one_pass(n_heads, k_of, q_of, vt_of, e_ref):
    seq = e_ref.shape[1]
    ck = seq // KEY_CHUNKS
    assert ck % SHIFT_SAMPLE == 0
    bounds = sorted({SHIFT_SAMPLE, *range(ck, seq + 1, ck)})
    outs, dens = [], []
    for h in range(n_heads):
        q = q_of(h)
        sample = slice(0, SHIFT_SAMPLE)
        s0 = _dot_nt(k_of(h, sample), q)
        shift = jnp.max(s0, axis=0, keepdims=True)
        e_ref[h, sample, :] = jnp.exp2(s0 - shift).astype(BF16)
        for lo, hi in zip(bounds[:-1], bounds[1:]):
            keys = slice(lo, hi)
            e_ref[h, keys, :] = jnp.exp2(_dot_nt(k_of(h, keys), q) - shift).astype(BF16)
        o = _dot(vt_of(h, slice(0, seq)), e_ref[h])
        dv = o.shape[0] - VT_ROWS_PAD
        outs.append(o[:dv] * (1.0 / o[dv:dv + 1]))
        dens.append(o[dv:dv + 1])
    return jnp.concatenate(outs, axis=0), jnp.concatenate(dens, axis=0)


def _attend_heads(n_heads, k_of, q_of, vt_of, s_ref, e_ref):
    seq = s_ref.shape[1]
    all_keys = slice(0, seq)
    outs = []
    for h in range(n_heads):
        s_scr = s_ref.at[h % 2]
        s = _dot_nt(k_of(h, all_keys), q_of(h))
        s_scr[...] = s
        _exp2_cols(jnp.max(s, axis=0, keepdims=True), s_scr, e_ref.at[h])
        o = _dot(vt_of(h, all_keys), e_ref[h])
        dv = o.shape[0] - VT_ROWS_PAD
        outs.append(o[:dv] * (1.0 / o[dv:dv + 1]))
    return jnp.concatenate(outs, axis=0)


def _attend_and_store(n_heads, k_of, q_of, vt_of, s_ref, e_ref, o_ref):
    out_t, den = _attend_heads_one_pass(n_heads, k_of, q_of, vt_of, e_ref)
    o_ref[0] = out_t.T.astype(o_ref.dtype)
    in_range = (den > DEN_LO) & (den < DEN_HI)
    n_bad = jnp.sum(jnp.where(in_range, 0.0, 1.0))

    @pl.when(n_bad > 0.0)
    def _():
        o_ref[0] = _attend_heads(n_heads, k_of, q_of, vt_of, s_ref, e_ref).T.astype(o_ref.dtype)


def _even_proj_kernel(x_ref, w_in_ref, qn_ref, w_uq_ref, kvn_ref, w_uk_ref, w_uv_ref,
                      w_gate_ref, b_gate_ref, cq_ref, sq_ref,
                      q_ref, k_ref, v_ref, gq_ref, gk_ref, gv_ref, la_ref, gr_ref):
    xb = x_ref[...].astype(BF16)
    h = _dot(xb, w_in_ref[:, 0:512])
    cos = cq_ref[...]
    sin = sq_ref[...]
    lane = lax.broadcasted_iota(jnp.int32, cos.shape, 1)
    pe_lane = (lane >= PE_LO) & (lane < PE_HI)

    def rope(t):
        return t * cos + _rot_half(t, MLA_ROPE // 2) * sin

    c_q = _rms(h[:, 0:256], qn_ref[...])
    q = _dot(c_q.astype(BF16), w_uq_ref[...]) * ((MLA_NOPE + MLA_ROPE) ** -0.5 * LOG2_E)
    c_kv = _rms(h[:, 256:384], kvn_ref[...]).astype(BF16)
    kn = _dot(c_kv, w_uk_ref[...])
    v_ref[...] = _dot(c_kv, w_uv_ref[...]).astype(v_ref.dtype)
    chunk = h[:, 384:512]
    k_pe = jnp.where(pe_lane, rope(chunk), 0.0)
    for hd in range(MLA_HEADS):
        sl = slice(hd * MLA_PAD, (hd + 1) * MLA_PAD)
        q_ref[:, sl] = rope(q[:, sl]).astype(q_ref.dtype)
        k_ref[:, sl] = (kn[:, sl] + k_pe).astype(k_ref.dtype)
    z = _dot(chunk.astype(BF16), w_gate_ref[...]) + b_gate_ref[...]
    la = _log_sigmoid(z) * (1.0 / GLA_GATE_NORM)
    blk = 2 * GLA_CHUNK
    r = lax.broadcasted_iota(jnp.int32, (blk, blk), 0)
    c = lax.broadcasted_iota(jnp.int32, (blk, blk), 1)
    same = (r // GLA_CHUNK) == (c // GLA_CHUNK)
    lower = jnp.where(same & (c <= r), 1.0, 0.0).astype(BF16)
    upper = jnp.where(same & (c >= r), 1.0, 0.0).astype(BF16)
    dkw = GLA_HEADS * GLA_DK
    for t in range(la.shape[0] // blk):
        rows = slice(t * blk, (t + 1) * blk)
        la_ref[rows, :dkw] = _dot_sel(lower, la[rows, :dkw])
        la_ref[rows, dkw:] = _dot_sel(upper, la[rows, dkw:])
    h_qk = _dot(xb, w_in_ref[:, 512:1024])
    gq_ref[...] = h_qk[:, 0:256] * (GLA_DK ** -0.5)
    gk_ref[...] = h_qk[:, 256:512]
    gv_ref[...] = _dot(xb, w_in_ref[:, 1024:1536]).astype(gv_ref.dtype)
    gr = _dot(xb, w_in_ref[:, 1536:2048])
    gr_ref[...] = gr * _sigmoid(gr)


def _even_proj(x2, w_in_p, qn, w_uq_p, kvn, w_uk_p, w_uv, w_gate, b_gate, cq, sq, seq, tm):
    t, d = x2.shape
    nsb = seq // tm
    row = lambda i: (i, 0)
    fixed = lambda i: (0, 0)
    pos = lambda i: (i % nsb, 0)
    full = lambda a: pl.BlockSpec(a.shape, fixed)
    outs = [(1024, BF16), (1024, BF16), (512, BF16), (256, F32), (256, F32), (512, BF16), (512, F32), (512, F32)]
    return pl.pallas_call(
        _even_proj_kernel,
        grid=(t // tm,),
        in_specs=[pl.BlockSpec((tm, d), row), full(w_in_p), full(qn), full(w_uq_p), full(kvn),
                  full(w_uk_p), full(w_uv), full(w_gate), full(b_gate),
                  pl.BlockSpec((tm, MLA_PAD), pos), pl.BlockSpec((tm, MLA_PAD), pos)],
        out_specs=[pl.BlockSpec((tm, n), row) for n, _ in outs],
        out_shape=[jax.ShapeDtypeStruct((t, n), dt) for n, dt in outs],
        compiler_params=_params(("parallel",)),
        name="even_proj",
    )(x2, w_in_p, qn, w_uq_p, kvn, w_uk_p, w_uv, w_gate, b_gate, cq, sq)


MLA_STEP_HEADS = 4


def _mla_attn_kernel(q_ref, k_ref, v_ref, o_ref, vt_ref, s_ref, e_ref):
    out_w = MLA_STEP_HEADS * MLA_V

    @pl.when(pl.program_id(2) == 0)
    def _():
        r = lax.broadcasted_iota(jnp.int32, (out_w, out_w), 0)
        c = lax.broadcasted_iota(jnp.int32, (out_w, out_w), 1)
        eye = jnp.where(r == c, 1.0, 0.0).astype(BF16)
        v_t = _dot_nt(eye, v_ref[0])
        for a in range(MLA_STEP_HEADS):
            vt_ref[a] = _values_t(v_t[a * MLA_V:(a + 1) * MLA_V])

    _attend_and_store(
        MLA_STEP_HEADS,
        lambda a, keys: k_ref[0, keys, a * MLA_PAD:(a + 1) * MLA_PAD],
        lambda a: q_ref[0, :, a * MLA_PAD:(a + 1) * MLA_PAD],
        lambda a, keys: vt_ref[a, :, keys],
        s_ref, e_ref, o_ref)


def _mla_attn(q, k, v, tq):
    b, s, _ = q.shape
    n = MLA_STEP_HEADS
    return pl.pallas_call(
        _mla_attn_kernel,
        grid=(b, MLA_HEADS // n, s // tq),
        in_specs=[pl.BlockSpec((1, tq, n * MLA_PAD), lambda i, j, t: (i, t, j)),
                  pl.BlockSpec((1, s, n * MLA_PAD), lambda i, j, t: (i, 0, j)),
                  pl.BlockSpec((1, s, n * MLA_V), lambda i, j, t: (i, 0, j))],
        out_specs=pl.BlockSpec((1, tq, n * MLA_V), lambda i, j, t: (i, t, j)),
        out_shape=jax.ShapeDtypeStruct((b, s, MLA_HEADS * MLA_V), BF16),
        scratch_shapes=[pltpu.VMEM((n, MLA_V + VT_ROWS_PAD, s), BF16),
                        pltpu.VMEM((2, s, tq), F32), pltpu.VMEM((n, s, tq), BF16)],
        compiler_params=_params(("parallel", "parallel", "arbitrary")),
        name="mla_attn",
    )(q, k, v)


def _gla_kernel(q_ref, k_ref, cum_ref, v_ref, gr_ref, g_ref, o_ref, state_ref, of_ref, ob_ref):
    seq = q_ref.shape[1]
    n_chunks = seq // GLA_CHUNK
    L = GLA_CHUNK
    dkw = GLA_HEADS * GLA_DK
    dvw = GLA_HEADS * GLA_DV

    def iota(shape, dim):
        return lax.broadcasted_iota(jnp.int32, shape, dim)

    row_l = iota((L, dkw), 0)
    col_m = iota((L, dkw), 1) % L
    k_own = (iota((dkw, dkw), 0) // L) == (iota((dkw, dkw), 1) // GLA_DK)
    v_own = (iota((dkw, dvw), 0) // L) == (iota((dkw, dvw), 1) // GLA_DV)
    s_own = (iota((dvw, dkw), 0) // GLA_DV) == (iota((dvw, dkw), 1) // GLA_DK)
    state_ref[...] = jnp.zeros_like(state_ref)

    def body(i, carry):
        for d, fwd in enumerate((True, False)):
            n = i if fwd else n_chunks - 1 - i
            rows = pl.ds(pl.multiple_of(n * L, L), L)
            cum = cum_ref[0, rows, d * dkw:(d + 1) * dkw]
            last = cum[L - 1:L, :] if fwd else cum[0:1, :]
            q = q_ref[0, rows, :]
            k = k_ref[0, rows, :]
            v = v_ref[0, rows, :]
            qe = (q * jnp.exp(cum)).astype(BF16)
            kg = (k * jnp.exp(-cum)).astype(BF16)
            kdec = (k * jnp.exp(last - cum)).astype(BF16)
            k_blk = jnp.where(k_own, jnp.concatenate([kg] * GLA_HEADS, axis=0), jnp.zeros((), BF16))
            att = _dot_nt(qe, k_blk)
            keep = (col_m <= row_l) if fwd else (col_m >= row_l)
            att = jnp.where(keep, att, 0.0).astype(BF16)
            v_blk = jnp.where(v_own, jnp.concatenate([v] * GLA_HEADS, axis=0), jnp.zeros((), BF16))
            st = state_ref[d]
            o = _dot(att, v_blk) + _dot_nt(qe, st.astype(BF16))
            state_ref[d] = st * jnp.exp(last) + jnp.where(s_own, _dot_tn(v, kdec), 0.0)
            if fwd:
                of_ref[rows, :] = o
            else:
                ob_ref[rows, :] = o
        return carry

    lax.fori_loop(0, n_chunks, body, 0, unroll=4)

    def finish(n, carry):
        rows = pl.ds(pl.multiple_of(n * L, L), L)
        tot = of_ref[rows, :] + ob_ref[rows, :]
        for hd in range(GLA_HEADS):
            sl = slice(hd * GLA_DV, (hd + 1) * GLA_DV)
            o_ref[0, rows, sl] = (_rms(tot[:, sl], g_ref[...]) * gr_ref[0, rows, sl]).astype(o_ref.dtype)
        return carry

    lax.fori_loop(0, n_chunks, finish, 0)


def _gla(gq, gk, cum, gv, gr, g_norm):
    b, s, _ = gq.shape
    dkw = GLA_HEADS * GLA_DK
    dvw = GLA_HEADS * GLA_DV
    blk = lambda w: pl.BlockSpec((1, s, w), lambda i: (i, 0, 0))
    return pl.pallas_call(
        _gla_kernel,
        grid=(b,),
        in_specs=[blk(dkw), blk(dkw), blk(2 * dkw), blk(dvw), blk(dvw),
                  pl.BlockSpec((1, GLA_DV), lambda i: (0, 0))],
        out_specs=blk(dvw),
        out_shape=jax.ShapeDtypeStruct((b, s, dvw), BF16),
        scratch_shapes=[pltpu.VMEM((2, dvw, dkw), F32), pltpu.VMEM((s, dvw), F32), pltpu.VMEM((s, dvw), F32)],
        compiler_params=_params(("parallel",)),
        name="gla",
    )(gq, gk, cum, gv, gr, g_norm)


def _odd_proj_kernel(x_ref, w_ref, gq_ref, gk_ref, avg_ref, cos_ref, sin_ref, q_ref, k_ref, v_ref):
    xb = x_ref[...].astype(BF16)
    cos = cos_ref[...]
    sin = sin_ref[...]
    avg = avg_ref[...]
    nq = GQA_HEADS * GQA_HEAD_DIM
    nk = GQA_KV_HEADS * GQA_HEAD_DIM

    def norm_rope(t, g):
        ms = _dot_x_sel(t * t, avg)
        t = t * lax.rsqrt(ms + EPS) * g
        return t * cos + _rot_half(t, GQA_HEAD_DIM // 2) * sin

    grp = 2 * nk
    for c in range(nq // grp):
        h = _dot(xb, w_ref[:, c * grp:(c + 1) * grp])
        for u in range(2):
            sl = slice(c * grp + u * nk, c * grp + (u + 1) * nk)
            q_ref[:, sl] = (norm_rope(h[:, u * nk:(u + 1) * nk], gq_ref[...])
                            * (GQA_HEAD_DIM ** -0.5 * LOG2_E)).astype(q_ref.dtype)
    h = _dot(xb, w_ref[:, nq:nq + grp])
    k_ref[...] = norm_rope(h[:, :nk], gk_ref[...]).astype(k_ref.dtype)
    v_ref[...] = h[:, nk:].astype(v_ref.dtype)


def _odd_proj(x2, w, gq, gk, avg, cos, sin, seq, tm):
    t, d = x2.shape
    nsb = seq // tm
    row = lambda i: (i, 0)
    fixed = lambda i: (0, 0)
    pos = lambda i: (i % nsb, 0)
    full = lambda a: pl.BlockSpec(a.shape, fixed)
    outs = [ODD_SPLITS[0], ODD_SPLITS[1], ODD_SPLITS[2]]
    return pl.pallas_call(
        _odd_proj_kernel,
        grid=(t // tm,),
        in_specs=[pl.BlockSpec((tm, d), row), full(w), full(gq), full(gk), full(avg),
                  pl.BlockSpec((tm, cos.shape[1]), pos), pl.BlockSpec((tm, sin.shape[1]), pos)],
        out_specs=[pl.BlockSpec((tm, n), row) for n in outs],
        out_shape=[jax.ShapeDtypeStruct((t, n), BF16) for n in outs],
        compiler_params=_params(("parallel",)),
        name="odd_proj",
    )(x2, w, gq, gk, avg, cos, sin)


def _gqa_attn_kernel(q_ref, k_ref, v_ref, o_ref, kp_ref, vt_ref, s_ref, e_ref):
    j = pl.program_id(1)
    group = GQA_HEADS // GQA_KV_HEADS
    width = GQA_KV_HEADS * GQA_HEAD_DIM
    pair = 2 * GQA_HEAD_DIM

    @pl.when(pl.program_id(2) == 0)
    def _():
        r = lax.broadcasted_iota(jnp.int32, (width, pair), 0)
        c = lax.broadcasted_iota(jnp.int32, (width, pair), 1)
        twice = (r // GQA_HEAD_DIM == j) & (r % GQA_HEAD_DIM == c % GQA_HEAD_DIM)
        kp_ref[...] = _dot(k_ref[0], jnp.where(twice, 1.0, 0.0).astype(BF16)).astype(BF16)
        rv = lax.broadcasted_iota(jnp.int32, (GQA_HEAD_DIM, width), 0)
        cv = lax.broadcasted_iota(jnp.int32, (GQA_HEAD_DIM, width), 1)
        mine = cv == j * GQA_HEAD_DIM + rv
        vt_ref[...] = _values_t(_dot_nt(jnp.where(mine, 1.0, 0.0).astype(BF16), v_ref[0]))

    lane = lax.broadcasted_iota(jnp.int32, (q_ref.shape[1], pair), 1)

    def q_of(g):
        qp = q_ref[0, :, (g // 2) * pair:(g // 2 + 1) * pair]
        return jnp.where(lane // GQA_HEAD_DIM == g % 2, qp, jnp.zeros_like(qp))

    _attend_and_store(group, lambda g, keys: kp_ref[keys, :], q_of, lambda g, keys: vt_ref[:, keys],
                      s_ref, e_ref, o_ref)


def _gqa_attn(q, k, v, tq):
    b, s, _ = q.shape
    width = GQA_KV_HEADS * GQA_HEAD_DIM
    group = GQA_HEADS // GQA_KV_HEADS
    return pl.pallas_call(
        _gqa_attn_kernel,
        grid=(b, GQA_KV_HEADS, s // tq),
        in_specs=[pl.BlockSpec((1, tq, width), lambda i, j, t: (i, t, j)),
                  pl.BlockSpec((1, s, width), lambda i, j, t: (i, 0, 0)),
                  pl.BlockSpec((1, s, width), lambda i, j, t: (i, 0, 0))],
        out_specs=pl.BlockSpec((1, tq, width), lambda i, j, t: (i, t, j)),
        out_shape=jax.ShapeDtypeStruct((b, s, GQA_HEADS * GQA_HEAD_DIM), BF16),
        scratch_shapes=[pltpu.VMEM((s, 2 * GQA_HEAD_DIM), BF16),
                        pltpu.VMEM((GQA_HEAD_DIM + VT_ROWS_PAD, s), BF16),
                        pltpu.VMEM((2, s, tq), F32), pltpu.VMEM((group, s, tq), BF16)],
        compiler_params=_params(("parallel", "parallel", "arbitrary")),
        name="gqa_attn",
    )(q, k, v)


def _out_ln_kernel(alpha, x_ref, ma_ref, mb_ref, wa_ref, wb_ref, g_ref, b_ref, y_ref, yt_ref):
    y = alpha * x_ref[...] + _dot(ma_ref[...].astype(BF16), wa_ref[...]) \
        + _dot(mb_ref[...].astype(BF16), wb_ref[...])
    y = _layer_norm(y, g_ref[...], b_ref[...])
    y_ref[...] = y
    yt_ref[0] = y.T.astype(BF16)


def _out_ln(x2, mix_a, mix_b, col_a, col_b, w_a, w_b, g, b, alpha, seq, tm):
    t, d = x2.shape
    half = w_a.shape[0]
    nsb = seq // tm
    row = lambda i: (i, 0)
    fixed = lambda i: (0, 0)
    return pl.pallas_call(
        functools.partial(_out_ln_kernel, alpha),
        grid=(t // tm,),
        in_specs=[pl.BlockSpec((tm, d), row),
                  pl.BlockSpec((tm, half), lambda i: (i, col_a)),
                  pl.BlockSpec((tm, half), lambda i: (i, col_b)),
                  pl.BlockSpec(w_a.shape, fixed), pl.BlockSpec(w_b.shape, fixed),
                  pl.BlockSpec(g.shape, fixed), pl.BlockSpec(b.shape, fixed)],
        out_specs=[pl.BlockSpec((tm, d), row),
                   pl.BlockSpec((1, d, tm), lambda i: (i // nsb, 0, i % nsb))],
        out_shape=[jax.ShapeDtypeStruct((t, d), F32), jax.ShapeDtypeStruct((t // seq, d, seq), BF16)],
        compiler_params=_params(("parallel",)),
        name="out_ln",
    )(x2, mix_a, mix_b, w_a, w_b, g, b)


def _route_kernel(cap, x_ref, rw_ref, aff_ref, slot_ref):
    x = x_ref[0]
    seq = x.shape[0]
    xh, xm = _split2(x)
    wh, wm = _split2(rw_ref[...])
    logits = _dot_nt(wh, xh) + (_dot_nt(wh, xm) + _dot_nt(wm, xh))
    e = jnp.exp(logits - jnp.max(logits, axis=0, keepdims=True))
    aff = e / jnp.sum(e, axis=0, keepdims=True)
    aff_ref[0] = aff

    def enough(cand):
        return jnp.sum((aff >= pltpu.bitcast(cand, F32)).astype(jnp.int32), axis=1, keepdims=True) >= cap

    top = jnp.full((aff.shape[0], 1), 1 << 30, jnp.int32)
    thr0 = jnp.where(enough(top), top, 0)

    def pick(i, thr):
        lo = 28 - 2 * i
        c1, c2, c3 = thr | (jnp.int32(1) << lo), thr | (jnp.int32(2) << lo), thr | (jnp.int32(3) << lo)
        return jnp.where(enough(c3), c3, jnp.where(enough(c2), c2, jnp.where(enough(c1), c1, thr)))

    thr = pltpu.bitcast(lax.fori_loop(0, 15, pick, thr0), F32)
    above = aff > thr
    tie = aff == thr
    need = cap - jnp.sum(above.astype(jnp.int32), axis=1, keepdims=True)

    blk = 256 if seq % 256 == 0 else V7X_LANES
    r = lax.broadcasted_iota(jnp.int32, (blk, blk), 0)
    c = lax.broadcasted_iota(jnp.int32, (blk, blk), 1)
    before = jnp.where(r < c, 1.0, 0.0).astype(BF16)

    def prefix(mask):
        m = jnp.where(mask, 1.0, 0.0).astype(BF16)
        run = jnp.zeros((mask.shape[0], 1), F32)
        parts = []
        for t in range(seq // blk):
            mb = m[:, t * blk:(t + 1) * blk]
            parts.append(_dot(mb, before) + run)
            run = run + jnp.sum(mb.astype(F32), axis=1, keepdims=True)
        return jnp.concatenate(parts, axis=1).astype(jnp.int32)

    chosen = above | (tie & (prefix(tie) < need))
    slot_ref[0] = jnp.where(chosen, prefix(chosen), -1)


def _route(x1, rw_t, cap):
    b, s, d = x1.shape
    e = rw_t.shape[0]
    return pl.pallas_call(
        functools.partial(_route_kernel, cap),
        grid=(b,),
        in_specs=[pl.BlockSpec((1, s, d), lambda i: (i, 0, 0)), pl.BlockSpec((e, d), lambda i: (0, 0))],
        out_specs=[pl.BlockSpec((1, e, s), lambda i: (i, 0, 0)), pl.BlockSpec((1, e, s), lambda i: (i, 0, 0))],
        out_shape=[jax.ShapeDtypeStruct((b, e, s), F32), jax.ShapeDtypeStruct((b, e, s), jnp.int32)],
        compiler_params=_params(("parallel",)),
        name="route",
    )(x1, rw_t)


def _moe_kernel(cap, ts, xt_ref, slot_ref, aff_ref, w1_ref, w3_ref, w2_ref, o_ref):
    e_id = pl.program_id(1)
    seq = xt_ref.shape[2]
    ff = w2_ref.shape[2]

    @pl.when(e_id == 0)
    def _():
        o_ref[...] = jnp.zeros_like(o_ref)

    c_row = lax.broadcasted_iota(jnp.int32, (cap, seq), 0)
    pick = jnp.where(slot_ref[0, 0] == c_row, 1.0, 0.0).astype(BF16)
    hd2 = xt_ref.shape[1] // 2
    xg = jnp.concatenate([_dot_nt(xt_ref[0, :hd2, :], pick), _dot_nt(xt_ref[0, hd2:, :], pick)],
                         axis=0).astype(BF16)
    h1 = _dot(w1_ref[0], xg)
    h3 = _dot(w3_ref[0], xg)
    hid = (h1 * _sigmoid(h1) * h3).astype(BF16)
    ye = jnp.concatenate([_dot(w2_ref[0, :hd2, :], hid), _dot(w2_ref[0, hd2:, :], hid)],
                         axis=0).astype(BF16)
    for t in range(seq // ts):
        cols = slice(t * ts, (t + 1) * ts)
        o_ref[0, :, cols] += _dot(ye, pick[:, cols]) * aff_ref[0, 0, :, cols]


def _moe(xt, slot, aff, w1t, w3t, w2t, cap, layer):
    b, d, s = xt.shape
    e = slot.shape[1]
    ff = w2t.shape[2]
    ts = min(SCATTER_BLOCK, s)
    w_blk = lambda i, j: (layer * e + j, 0, 0)
    return pl.pallas_call(
        functools.partial(_moe_kernel, cap, ts),
        grid=(b, e),
        in_specs=[pl.BlockSpec((1, d, s), lambda i, j: (i, 0, 0)),
                  pl.BlockSpec((1, 1, 1, s), lambda i, j: (i, j, 0, 0)),
                  pl.BlockSpec((1, 1, 1, s), lambda i, j: (i, j, 0, 0)),
                  pl.BlockSpec((1, ff, d), w_blk), pl.BlockSpec((1, ff, d), w_blk),
                  pl.BlockSpec((1, d, ff), w_blk)],
        out_specs=pl.BlockSpec((1, d, s), lambda i, j: (i, 0, 0)),
        out_shape=jax.ShapeDtypeStruct((b, d, s), F32),
        compiler_params=_params(("parallel", "arbitrary")),
        name="moe",
    )(xt, slot.reshape(b, e, 1, s), aff.reshape(b, e, 1, s), w1t, w3t, w2t)


def _transpose_cast_kernel(w_ref, o_ref):
    o_ref[0] = w_ref[0].T.astype(o_ref.dtype)


def _transpose_cast(w):
    n, r, c = w.shape
    return pl.pallas_call(
        _transpose_cast_kernel,
        grid=(n,),
        in_specs=[pl.BlockSpec((1, r, c), lambda i: (i, 0, 0))],
        out_specs=pl.BlockSpec((1, c, r), lambda i: (i, 0, 0)),
        out_shape=jax.ShapeDtypeStruct((n, c, r), BF16),
        compiler_params=_params(("parallel",)),
        name="transpose_cast",
    )(w)


def _ple_ln_kernel(alpha, x_ref, f_ref, p_ref, wg_ref, bg_ref, wp_ref, g_ref, b_ref, y_ref):
    x = x_ref[...]
    gate = _sigmoid(_dot(x.astype(BF16), wg_ref[...]) + bg_ref[...])
    ple = gate * _dot(p_ref[...].astype(BF16), wp_ref[...])
    ffn = f_ref[0].T
    y_ref[...] = _layer_norm(alpha * x + ffn + ple, g_ref[...], b_ref[...])


def _ple_ln(x1, ffn_t, p2, wg, bg, wp, g, b, alpha, tm, layer):
    t, d = x1.shape
    nsb = ffn_t.shape[2] // tm
    row = lambda i: (i, 0)
    p_row = lambda i: (layer * (t // tm) + i, 0)
    fixed = lambda i: (0, 0)
    full = lambda a: pl.BlockSpec(a.shape, fixed)
    return pl.pallas_call(
        functools.partial(_ple_ln_kernel, alpha),
        grid=(t // tm,),
        in_specs=[pl.BlockSpec((tm, d), row),
                  pl.BlockSpec((1, d, tm), lambda i: (i // nsb, 0, i % nsb)),
                  pl.BlockSpec((tm, p2.shape[1]), p_row), full(wg), full(bg), full(wp), full(g), full(b)],
        out_specs=pl.BlockSpec((tm, d), row),
        out_shape=jax.ShapeDtypeStruct((t, d), F32),
        compiler_params=_params(("parallel",)),
        name="ple_ln",
    )(x1, ffn_t, p2, wg, bg, wp, g, b)


def _rope_tables(seq, rot_dim, lo, width):
    rows = seq // GRID_W
    row = jnp.repeat(jnp.arange(rows, dtype=F32), GRID_W)
    col = jnp.tile(jnp.arange(GRID_W, dtype=F32), rows)
    axis_dim = rot_dim // 2
    inv = ROPE_THETA ** (-jnp.arange(0, axis_dim, 2, dtype=F32) / axis_dim)
    ang = jnp.concatenate([row[:, None] * inv, col[:, None] * inv], axis=-1)
    cos, sin = jnp.cos(ang), jnp.sin(ang)
    cos2 = jnp.concatenate([cos, cos], axis=-1)
    sin2 = jnp.concatenate([-sin, sin], axis=-1)
    if lo == 0:
        reps = width // rot_dim
        return jnp.tile(cos2, (1, reps)), jnp.tile(sin2, (1, reps))
    pad_l = jnp.ones((seq, lo), F32)
    pad_r = jnp.ones((seq, width - lo - rot_dim), F32)
    cos_t = jnp.concatenate([pad_l, cos2, pad_r], axis=-1)
    sin_t = jnp.concatenate([0 * pad_l, sin2, 0 * pad_r], axis=-1)
    return cos_t, sin_t


def _prep_even(w_in, w_uq, w_ukv, gw_f, gb_f, gw_b, gb_b):
    d = w_in.shape[0]
    offs = np.cumsum(EVEN_SPLITS)[:-1].tolist()
    c_q, c_kv, k_pe, gq, gk, gv, g_lr, gr = jnp.split(w_in, offs, axis=-1)
    z32 = jnp.zeros((d, 32), w_in.dtype)
    chunk = jnp.concatenate([g_lr, z32, k_pe, z32], axis=-1)
    w_in_p = jnp.concatenate([c_q, c_kv, chunk, gq, gk, gv, gr], axis=-1).astype(BF16)
    uq = w_uq.reshape(MLA_Q_LORA, MLA_HEADS, MLA_NOPE + MLA_ROPE)
    uq = jnp.pad(uq, ((0, 0), (0, 0), (0, MLA_PAD - MLA_NOPE - MLA_ROPE)))
    w_uq_p = uq.reshape(MLA_Q_LORA, MLA_HEADS * MLA_PAD).astype(BF16)
    ukv = w_ukv.reshape(MLA_KV_LORA, MLA_HEADS, MLA_NOPE + MLA_V)
    uk = jnp.pad(ukv[:, :, :MLA_NOPE], ((0, 0), (0, 0), (0, MLA_PAD - MLA_NOPE)))
    w_uk_p = uk.reshape(MLA_KV_LORA, MLA_HEADS * MLA_PAD).astype(BF16)
    w_uv = ukv[:, :, MLA_NOPE:].reshape(MLA_KV_LORA, MLA_HEADS * MLA_V).astype(BF16)
    dkw = GLA_HEADS * GLA_DK
    w_gate = jnp.zeros((MLA_PAD, 2 * dkw), F32)
    w_gate = w_gate.at[0:GLA_GATE_RANK, 0:dkw].set(gw_f)
    w_gate = w_gate.at[GLA_GATE_RANK:2 * GLA_GATE_RANK, dkw:].set(gw_b).astype(BF16)
    b_gate = jnp.concatenate([gb_f, gb_b])[None, :]
    return w_in_p, w_uq_p, w_uk_p, w_uv, w_gate, b_gate


def kernel(x, p, w_in_even, mla_q_norm, w_uq, mla_kv_norm, w_ukv, gla_gate_w_fwd, gla_gate_b_fwd,
           gla_gate_w_bwd, gla_gate_b_bwd, gla_norm, w_in_odd, gqa_q_norm, gqa_k_norm, w_o, ln1_g,
           ln1_b, router_w, w1, w3, w2, ple_gate_w, ple_gate_b, ple_w, ln2_g, ln2_b):
    b, s, d = x.shape
    depth = w_o.shape[0]
    t = b * s
    alpha = (2.0 * depth) ** 0.25
    cap = EC_CAPACITY_FACTOR * s // N_EXPERTS
    tm = min(ROW_BLOCK, s)
    tq = min(QUERY_BLOCK, s)
    assert s % tm == 0 and s % tq == 0 and s % GRID_W == 0 and tm % (2 * GLA_CHUNK) == 0
    assert (s // KEY_CHUNKS) % KEY_BLK == 0 and tq % V7X_LANES == 0 and s % min(SCATTER_BLOCK, s) == 0
    assert cap % V7X_BF16_SUBLANES == 0 and w1.shape[1] == N_EXPERTS and router_w.shape[2] == N_EXPERTS
    half = w_o.shape[1] // 2
    cos_a, sin_a = _rope_tables(s, MLA_ROPE, PE_LO, MLA_PAD)
    cos_c, sin_c = _rope_tables(s, GQA_HEAD_DIM, 0, GQA_KV_HEADS * GQA_HEAD_DIM)
    hw = GQA_KV_HEADS * GQA_HEAD_DIM
    head_of = np.arange(hw) // GQA_HEAD_DIM
    avg = jnp.asarray((head_of[:, None] == head_of[None, :]) / GQA_HEAD_DIM, BF16)

    n_e, ff = w1.shape[1], w1.shape[3]
    w1t = _transpose_cast(w1.reshape(depth * n_e, d, ff))
    w3t = _transpose_cast(w3.reshape(depth * n_e, d, ff))
    w2t = _transpose_cast(w2.reshape(depth * n_e, ff, d))
    p_all = p.reshape(depth * t, p.shape[-1])

    x2 = x.reshape(t, d)
    for i in range(depth):
        j = i // 2
        if i % 2 == 0:
            w_in_p, w_uq_p, w_uk_p, w_uv, w_gate, b_gate = _prep_even(
                w_in_even[j], w_uq[j], w_ukv[j], gla_gate_w_fwd[j], gla_gate_b_fwd[j],
                gla_gate_w_bwd[j], gla_gate_b_bwd[j])
            q, k, v, gq, gk, gv, la, gr = _even_proj(
                x2, w_in_p, mla_q_norm[j][None, :], w_uq_p, mla_kv_norm[j][None, :], w_uk_p, w_uv,
                w_gate, b_gate, cos_a, sin_a, s, tm)
            r3 = lambda a: a.reshape(b, s, a.shape[-1])
            o_mla = _mla_attn(r3(q), r3(k), r3(v), tq).reshape(t, -1)
            o_gla = _gla(r3(gq), r3(gk), r3(la), r3(gv), r3(gr), gla_norm[j][None, :]).reshape(t, -1)
            mix_a, mix_b, col_a, col_b = o_mla, o_gla, 0, 0
        else:
            q, k, v = _odd_proj(x2, w_in_odd[j].astype(BF16),
                                jnp.tile(gqa_q_norm[j], GQA_KV_HEADS)[None, :],
                                jnp.tile(gqa_k_norm[j], GQA_KV_HEADS)[None, :],
                                avg, cos_c, sin_c, s, tm)
            r3 = lambda a: a.reshape(b, s, a.shape[-1])
            o = _gqa_attn(r3(q), r3(k), r3(v), tq).reshape(t, -1)
            mix_a, mix_b, col_a, col_b = o, o, 0, 1
        wo = w_o[i].astype(BF16)
        x1, x1t = _out_ln(x2, mix_a, mix_b, col_a, col_b, wo[:half], wo[half:],
                          ln1_g[i][None, :], ln1_b[i][None, :], alpha, s, tm)
        aff, slot = _route(x1.reshape(b, s, d), router_w[i].T, cap)
        ffn_t = _moe(x1t, slot, aff, w1t, w3t, w2t, cap, i)
        x2 = _ple_ln(x1, ffn_t, p_all, ple_gate_w[i].astype(BF16),
                     ple_gate_b[i][None, :], ple_w[i].astype(BF16), ln2_g[i][None, :], ln2_b[i][None, :],
                     alpha, tm, i)
    return x2.reshape(b, s, d)
```

```python
import functools
import math

import jax
import jax.numpy as jnp
import numpy as np
from jax import lax
from jax.experimental import pallas as pl
from jax.experimental.pallas import tpu as pltpu

F32 = jnp.float32
BF16 = jnp.bfloat16

V7X_LANES = 128
V7X_BF16_SUBLANES = 16
V7X_VMEM_BYTES = 64 * 1024 * 1024
VMEM_LIMIT = V7X_VMEM_BYTES - 8 * 1024 * 1024

ROW_BLOCK = 1024
QUERY_BLOCK = 1024
SCATTER_BLOCK = 512

GRID_W = 64
ROPE_THETA = 10000.0
EPS = 1e-6
MLA_HEADS, MLA_Q_LORA, MLA_KV_LORA = 8, 256, 128
MLA_NOPE, MLA_ROPE, MLA_V = 64, 32, 64
GLA_HEADS, GLA_DK, GLA_DV = 4, 64, 128
GLA_GATE_RANK, GLA_GATE_NORM, GLA_CHUNK = 16, 16.0, 64
GQA_HEADS, GQA_KV_HEADS, GQA_HEAD_DIM = 16, 4, 64
N_EXPERTS, EC_CAPACITY_FACTOR = 16, 2
EVEN_SPLITS = (MLA_Q_LORA, MLA_KV_LORA, MLA_ROPE, GLA_HEADS * GLA_DK, GLA_HEADS * GLA_DK,
               GLA_HEADS * GLA_DV, 2 * GLA_GATE_RANK, GLA_HEADS * GLA_DV)
ODD_SPLITS = (GQA_HEADS * GQA_HEAD_DIM, GQA_KV_HEADS * GQA_HEAD_DIM, GQA_KV_HEADS * GQA_HEAD_DIM)
MLA_PAD = V7X_LANES
PE_LO, PE_HI = MLA_NOPE, MLA_NOPE + MLA_ROPE


def _dot(a, b):
    return jnp.dot(a, b, preferred_element_type=F32)


def _dot_nt(a, b):
    return lax.dot_general(a, b, (((1,), (1,)), ((), ())), preferred_element_type=F32)


def _dot_tn(a, b):
    return lax.dot_general(a, b, (((0,), (0,)), ((), ())), preferred_element_type=F32)


def _split2(x):
    hi = x.astype(BF16)
    lo = (x - hi.astype(F32)).astype(BF16)
    return hi, lo


def _dot_sel(sel, x):
    hi, lo = _split2(x)
    return _dot(sel, hi) + _dot(sel, lo)


def _dot_x_sel(x, sel):
    hi, lo = _split2(x)
    return _dot(hi, sel) + _dot(lo, sel)


def _params(sem):
    return pltpu.CompilerParams(dimension_semantics=sem, vmem_limit_bytes=VMEM_LIMIT)


def _rot_half(x, half):
    w = x.shape[-1]
    lane = lax.broadcasted_iota(jnp.int32, x.shape, x.ndim - 1)
    first = (lane % (2 * half)) < half
    return jnp.where(first, pltpu.roll(x, w - half, x.ndim - 1), pltpu.roll(x, half, x.ndim - 1))


def _layer_norm(y, g, b):
    mu = jnp.mean(y, axis=-1, keepdims=True)
    yc = y - mu
    var = jnp.mean(yc * yc, axis=-1, keepdims=True)
    return yc * lax.rsqrt(var + EPS) * g + b


def _rms(x, g):
    return x * lax.rsqrt(jnp.mean(x * x, axis=-1, keepdims=True) + EPS) * g


def _log_sigmoid(z):
    return jnp.minimum(z, 0.0) - jnp.log1p(jnp.exp(-jnp.abs(z)))


def _sigmoid(z):
    return 1.0 / (1.0 + jnp.exp(-z))


LOG2_E = math.log2(math.e)
KEY_BLK = 128


KEY_CHUNKS = 4


def _exp2_cols(m, s_scr, e_scr):
    seq, tq = s_scr.shape
    for c in range(tq // V7X_LANES):
        cols = slice(c * V7X_LANES, (c + 1) * V7X_LANES)
        m_b = jnp.broadcast_to(m[:, cols], (KEY_BLK, V7X_LANES))
        for r in range(seq // KEY_BLK):
            rows = slice(r * KEY_BLK, (r + 1) * KEY_BLK)
            e_scr[rows, cols] = jnp.exp2(s_scr[rows, cols] - m_b).astype(BF16)


VT_ROWS_PAD = V7X_BF16_SUBLANES


def _values_t(v_t):
    dv, seq = v_t.shape
    extra = jnp.where(lax.broadcasted_iota(jnp.int32, (VT_ROWS_PAD, seq), 0) == 0, 1.0, 0.0)
    return jnp.concatenate([v_t.astype(BF16), extra.astype(BF16)], axis=0)


SHIFT_SAMPLE = 128
DEN_LO, DEN_HI = 2.0 ** -90, 2.0 ** 90


def _attend_heads_one_pass(n_heads, k_of, q_of, vt_of, e_ref):
    seq = e_ref.shape[1]
    ck = seq // KEY_CHUNKS
    assert ck % SHIFT_SAMPLE == 0
    bounds = sorted({SHIFT_SAMPLE, *range(ck, seq + 1, ck)})
    outs, dens = [], []
    for h in range(n_heads):
        q = q_of(h)
        sample = slice(0, SHIFT_SAMPLE)
        s0 = _dot_nt(k_of(h, sample), q)
        shift = jnp.max(s0, axis=0, keepdims=True)
        e_ref[h, sample, :] = jnp.exp2(s0 - shift).astype(BF16)
        for lo, hi in zip(bounds[:-1], bounds[1:]):
            keys = slice(lo, hi)
            e_ref[h, keys, :] = jnp.exp2(_dot_nt(k_of(h, keys), q) - shift).astype(BF16)
        o = _dot(vt_of(h, slice(0, seq)), e_ref[h])
        dv = o.shape[0] - VT_ROWS_PAD
        outs.append(o[:dv] * (1.0 / o[dv:dv + 1]))
        dens.append(o[dv:dv + 1])
    return jnp.concatenate(outs, axis=0), jnp.concatenate(dens, axis=0)


def _attend_heads(n_heads, k_of, q_of, vt_of, s_ref, e_ref):
    seq = s_ref.shape[1]
    all_keys = slice(0, seq)
    outs = []
    for h in range(n_heads):
        s_scr = s_ref.at[h % 2]
        s = _dot_nt(k_of(h, all_keys), q_of(h))
        s_scr[...] = s
        _exp2_cols(jnp.max(s, axis=0, keepdims=True), s_scr, e_ref.at[h])
        o = _dot(vt_of(h, all_keys), e_ref[h])
        dv = o.shape[0] - VT_ROWS_PAD
        outs.append(o[:dv] * (1.0 / o[dv:dv + 1]))
    return jnp.concatenate(outs, axis=0)


def _attend_and_store(n_heads, k_of, q_of, vt_of, s_ref, e_ref, o_ref):
    out_t, den = _attend_heads_one_pass(n_heads, k_of, q_of, vt_of, e_ref)
    o_ref[0] = out_t.T.astype(o_ref.dtype)
    in_range = (den > DEN_LO) & (den < DEN_HI)
    n_bad = jnp.sum(jnp.where(in_range, 0.0, 1.0))

    @pl.when(n_bad > 0.0)
    def _():
        o_ref[0] = _attend_heads(n_heads, k_of, q_of, vt_of, s_ref, e_ref).T.astype(o_ref.dtype)


def _even_proj_kernel(x_ref, w_in_ref, qn_ref, w_uq_ref, kvn_ref, w_uk_ref, w_uv_ref,
                      w_gate_ref, b_gate_ref, cq_ref, sq_ref,
                      q_ref, k_ref, v_ref, gq_ref, gk_ref, gv_ref, la_ref, gr_ref):
    xb = x_ref[...].astype(BF16)
    h = _dot(xb, w_in_ref[:, 0:512])
    cos = cq_ref[...]
    sin = sq_ref[...]
    lane = lax.broadcasted_iota(jnp.int32, cos.shape, 1)
    pe_lane = (lane >= PE_LO) & (lane < PE_HI)

    def rope(t):
        return t * cos + _rot_half(t, MLA_ROPE // 2) * sin

    c_q = _rms(h[:, 0:256], qn_ref[...])
    q = _dot(c_q.astype(BF16), w_uq_ref[...]) * ((MLA_NOPE + MLA_ROPE) ** -0.5 * LOG2_E)
    c_kv = _rms(h[:, 256:384], kvn_ref[...]).astype(BF16)
    kn = _dot(c_kv, w_uk_ref[...])
    v_ref[...] = _dot(c_kv, w_uv_ref[...]).astype(v_ref.dtype)
    chunk = h[:, 384:512]
    k_pe = jnp.where(pe_lane, rope(chunk), 0.0)
    for hd in range(MLA_HEADS):
        sl = slice(hd * MLA_PAD, (hd + 1) * MLA_PAD)
        q_ref[:, sl] = rope(q[:, sl]).astype(q_ref.dtype)
        k_ref[:, sl] = (kn[:, sl] + k_pe).astype(k_ref.dtype)
    z = _dot(chunk.astype(BF16), w_gate_ref[...]) + b_gate_ref[...]
    la = _log_sigmoid(z) * (1.0 / GLA_GATE_NORM)
    blk = 2 * GLA_CHUNK
    r = lax.broadcasted_iota(jnp.int32, (blk, blk), 0)
    c = lax.broadcasted_iota(jnp.int32, (blk, blk), 1)
    same = (r // GLA_CHUNK) == (c // GLA_CHUNK)
    lower = jnp.where(same & (c <= r), 1.0, 0.0).astype(BF16)
    upper = jnp.where(same & (c >= r), 1.0, 0.0).astype(BF16)
    dkw = GLA_HEADS * GLA_DK
    for t in range(la.shape[0] // blk):
        rows = slice(t * blk, (t + 1) * blk)
        la_ref[rows, :dkw] = _dot_sel(lower, la[rows, :dkw])
        la_ref[rows, dkw:] = _dot_sel(upper, la[rows, dkw:])
    h_qk = _dot(xb, w_in_ref[:, 512:1024])
    gq_ref[...] = h_qk[:, 0:256] * (GLA_DK ** -0.5)
    gk_ref[...] = h_qk[:, 256:512]
    gv_ref[...] = _dot(xb, w_in_ref[:, 1024:1536]).astype(gv_ref.dtype)
    gr = _dot(xb, w_in_ref[:, 1536:2048])
    gr_ref[...] = gr * _sigmoid(gr)


def _even_proj(x2, w_in_p, qn, w_uq_p, kvn, w_uk_p, w_uv, w_gate, b_gate, cq, sq, seq, tm):
    t, d = x2.shape
    nsb = seq // tm
    row = lambda i: (i, 0)
    fixed = lambda i: (0, 0)
    pos = lambda i: (i % nsb, 0)
    full = lambda a: pl.BlockSpec(a.shape, fixed)
    outs = [(1024, BF16), (1024, BF16), (512, BF16), (256, F32), (256, F32), (512, BF16), (512, F32), (512, F32)]
    return pl.pallas_call(
        _even_proj_kernel,
        grid=(t // tm,),
        in_specs=[pl.BlockSpec((tm, d), row), full(w_in_p), full(qn), full(w_uq_p), full(kvn),
                  full(w_uk_p), full(w_uv), full(w_gate), full(b_gate),
                  pl.BlockSpec((tm, MLA_PAD), pos), pl.BlockSpec((tm, MLA_PAD), pos)],
        out_specs=[pl.BlockSpec((tm, n), row) for n, _ in outs],
        out_shape=[jax.ShapeDtypeStruct((t, n), dt) for n, dt in outs],
        compiler_params=_params(("parallel",)),
        name="even_proj",
    )(x2, w_in_p, qn, w_uq_p, kvn, w_uk_p, w_uv, w_gate, b_gate, cq, sq)


MLA_STEP_HEADS = 4


def _mla_attn_kernel(q_ref, k_ref, v_ref, o_ref, vt_ref, s_ref, e_ref):
    out_w = MLA_STEP_HEADS * MLA_V

    @pl.when(pl.program_id(2) == 0)
    def _():
        r = lax.broadcasted_iota(jnp.int32, (out_w, out_w), 0)
        c = lax.broadcasted_iota(jnp.int32, (out_w, out_w), 1)
        eye = jnp.where(r == c, 1.0, 0.0).astype(BF16)
        v_t = _dot_nt(eye, v_ref[0])
        for a in range(MLA_STEP_HEADS):
            vt_ref[a] = _values_t(v_t[a * MLA_V:(a + 1) * MLA_V])

    _attend_and_store(
        MLA_STEP_HEADS,
        lambda a, keys: k_ref[0, keys, a * MLA_PAD:(a + 1) * MLA_PAD],
        lambda a: q_ref[0, :, a * MLA_PAD:(a + 1) * MLA_PAD],
        lambda a, keys: vt_ref[a, :, keys],
        s_ref, e_ref, o_ref)


def _mla_attn(q, k, v, tq):
    b, s, _ = q.shape
    n = MLA_STEP_HEADS
    return pl.pallas_call(
        _mla_attn_kernel,
        grid=(b, MLA_HEADS // n, s // tq),
        in_specs=[pl.BlockSpec((1, tq, n * MLA_PAD), lambda i, j, t: (i, t, j)),
                  pl.BlockSpec((1, s, n * MLA_PAD), lambda i, j, t: (i, 0, j)),
                  pl.BlockSpec((1, s, n * MLA_V), lambda i, j, t: (i, 0, j))],
        out_specs=pl.BlockSpec((1, tq, n * MLA_V), lambda i, j, t: (i, t, j)),
        out_shape=jax.ShapeDtypeStruct((b, s, MLA_HEADS * MLA_V), BF16),
        scratch_shapes=[pltpu.VMEM((n, MLA_V + VT_ROWS_PAD, s), BF16),
                        pltpu.VMEM((2, s, tq), F32), pltpu.VMEM((n, s, tq), BF16)],
        compiler_params=_params(("parallel", "parallel", "arbitrary")),
        name="mla_attn",
    )(q, k, v)


def _gla_kernel(q_ref, k_ref, cum_ref, v_ref, gr_ref, g_ref, o_ref, state_ref, of_ref, ob_ref):
    seq = q_ref.shape[1]
    n_chunks = seq // GLA_CHUNK
    L = GLA_CHUNK
    dkw = GLA_HEADS * GLA_DK
    dvw = GLA_HEADS * GLA_DV

    def iota(shape, dim):
        return lax.broadcasted_iota(jnp.int32, shape, dim)

    row_l = iota((L, dkw), 0)
    col_m = iota((L, dkw), 1) % L
    k_own = (iota((dkw, dkw), 0) // L) == (iota((dkw, dkw), 1) // GLA_DK)
    v_own = (iota((dkw, dvw), 0) // L) == (iota((dkw, dvw), 1) // GLA_DV)
    s_own = (iota((dvw, dkw), 0) // GLA_DV) == (iota((dvw, dkw), 1) // GLA_DK)
    state_ref[...] = jnp.zeros_like(state_ref)

    def body(i, carry):
        for d, fwd in enumerate((True, False)):
            n = i if fwd else n_chunks - 1 - i
            rows = pl.ds(pl.multiple_of(n * L, L), L)
            cum = cum_ref[0, rows, d * dkw:(d + 1) * dkw]
            last = cum[L - 1:L, :] if fwd else cum[0:1, :]
            q = q_ref[0, rows, :]
            k = k_ref[0, rows, :]
            v = v_ref[0, rows, :]
            qe = (q * jnp.exp(cum)).astype(BF16)
            kg = (k * jnp.exp(-cum)).astype(BF16)
            kdec = (k * jnp.exp(last - cum)).astype(BF16)
            k_blk = jnp.where(k_own, jnp.concatenate([kg] * GLA_HEADS, axis=0), jnp.zeros((), BF16))
            att = _dot_nt(qe, k_blk)
            keep = (col_m <= row_l) if fwd else (col_m >= row_l)
            att = jnp.where(keep, att, 0.0).astype(BF16)
            v_blk = jnp.where(v_own, jnp.concatenate([v] * GLA_HEADS, axis=0), jnp.zeros((), BF16))
            st = state_ref[d]
            o = _dot(att, v_blk) + _dot_nt(qe, st.astype(BF16))
            state_ref[d] = st * jnp.exp(last) + jnp.where(s_own, _dot_tn(v, kdec), 0.0)
            if fwd:
                of_ref[rows, :] = o
            else:
                ob_ref[rows, :] = o
        return carry

    lax.fori_loop(0, n_chunks, body, 0, unroll=4)

    def finish(n, carry):
        rows = pl.ds(pl.multiple_of(n * L, L), L)
        tot = of_ref[rows, :] + ob_ref[rows, :]
        for hd in range(GLA_HEADS):
            sl = slice(hd * GLA_DV, (hd + 1) * GLA_DV)
            o_ref[0, rows, sl] = (_rms(tot[:, sl], g_ref[...]) * gr_ref[0, rows, sl]).astype(o_ref.dtype)
        return carry

    lax.fori_loop(0, n_chunks, finish, 0)


def _gla(gq, gk, cum, gv, gr, g_norm):
    b, s, _ = gq.shape
    dkw = GLA_HEADS * GLA_DK
    dvw = GLA_HEADS * GLA_DV
    blk = lambda w: pl.BlockSpec((1, s, w), lambda i: (i, 0, 0))
    return pl.pallas_call(
        _gla_kernel,
        grid=(b,),
        in_specs=[blk(dkw), blk(dkw), blk(2 * dkw), blk(dvw), blk(dvw),
                  pl.BlockSpec((1, GLA_DV), lambda i: (0, 0))],
        out_specs=blk(dvw),
        out_shape=jax.ShapeDtypeStruct((b, s, dvw), BF16),
        scratch_shapes=[pltpu.VMEM((2, dvw, dkw), F32), pltpu.VMEM((s, dvw), F32), pltpu.VMEM((s, dvw), F32)],
        compiler_params=_params(("parallel",)),
        name="gla",
    )(gq, gk, cum, gv, gr, g_norm)


def _odd_proj_kernel(x_ref, w_ref, gq_ref, gk_ref, avg_ref, cos_ref, sin_ref, q_ref, k_ref, v_ref):
    xb = x_ref[...].astype(BF16)
    cos = cos_ref[...]
    sin = sin_ref[...]
    avg = avg_ref[...]
    nq = GQA_HEADS * GQA_HEAD_DIM
    nk = GQA_KV_HEADS * GQA_HEAD_DIM

    def norm_rope(t, g):
        ms = _dot_x_sel(t * t, avg)
        t = t * lax.rsqrt(ms + EPS) * g
        return t * cos + _rot_half(t, GQA_HEAD_DIM // 2) * sin

    grp = 2 * nk
    for c in range(nq // grp):
        h = _dot(xb, w_ref[:, c * grp:(c + 1) * grp])
        for u in range(2):
            sl = slice(c * grp + u * nk, c * grp + (u + 1) * nk)
            q_ref[:, sl] = (norm_rope(h[:, u * nk:(u + 1) * nk], gq_ref[...])
                            * (GQA_HEAD_DIM ** -0.5 * LOG2_E)).astype(q_ref.dtype)
    h = _dot(xb, w_ref[:, nq:nq + grp])
    k_ref[...] = norm_rope(h[:, :nk], gk_ref[...]).astype(k_ref.dtype)
    v_ref[...] = h[:, nk:].astype(v_ref.dtype)


def _odd_proj(x2, w, gq, gk, avg, cos, sin, seq, tm):
    t, d = x2.shape
    nsb = seq // tm
    row = lambda i: (i, 0)
    fixed = lambda i: (0, 0)
    pos = lambda i: (i % nsb, 0)
    full = lambda a: pl.BlockSpec(a.shape, fixed)
    outs = [ODD_SPLITS[0], ODD_SPLITS[1], ODD_SPLITS[2]]
    return pl.pallas_call(
        _odd_proj_kernel,
        grid=(t // tm,),
        in_specs=[pl.BlockSpec((tm, d), row), full(w), full(gq), full(gk), full(avg),
                  pl.BlockSpec((tm, cos.shape[1]), pos), pl.BlockSpec((tm, sin.shape[1]), pos)],
        out_specs=[pl.BlockSpec((tm, n), row) for n in outs],
        out_shape=[jax.ShapeDtypeStruct((t, n), BF16) for n in outs],
        compiler_params=_params(("parallel",)),
        name="odd_proj",
    )(x2, w, gq, gk, avg, cos, sin)


def _gqa_attn_kernel(q_ref, k_ref, v_ref, o_ref, kp_ref, vt_ref, s_ref, e_ref):
    j = pl.program_id(1)
    group = GQA_HEADS // GQA_KV_HEADS
    width = GQA_KV_HEADS * GQA_HEAD_DIM
    pair = 2 * GQA_HEAD_DIM

    @pl.when(pl.program_id(2) == 0)
    def _():
        r = lax.broadcasted_iota(jnp.int32, (width, pair), 0)
        c = lax.broadcasted_iota(jnp.int32, (width, pair), 1)
        twice = (r // GQA_HEAD_DIM == j) & (r % GQA_HEAD_DIM == c % GQA_HEAD_DIM)
        kp_ref[...] = _dot(k_ref[0], jnp.where(twice, 1.0, 0.0).astype(BF16)).astype(BF16)
        rv = lax.broadcasted_iota(jnp.int32, (GQA_HEAD_DIM, width), 0)
        cv = lax.broadcasted_iota(jnp.int32, (GQA_HEAD_DIM, width), 1)
        mine = cv == j * GQA_HEAD_DIM + rv
        vt_ref[...] = _values_t(_dot_nt(jnp.where(mine, 1.0, 0.0).astype(BF16), v_ref[0]))

    lane = lax.broadcasted_iota(jnp.int32, (q_ref.shape[1], pair), 1)

    def q_of(g):
        qp = q_ref[0, :, (g // 2) * pair:(g // 2 + 1) * pair]
        return jnp.where(lane // GQA_HEAD_DIM == g % 2, qp, jnp.zeros_like(qp))

    _attend_and_store(group, lambda g, keys: kp_ref[keys, :], q_of, lambda g, keys: vt_ref[:, keys],
                      s_ref, e_ref, o_ref)


def _gqa_attn(q, k, v, tq):
    b, s, _ = q.shape
    width = GQA_KV_HEADS * GQA_HEAD_DIM
    group = GQA_HEADS // GQA_KV_HEADS
    return pl.pallas_call(
        _gqa_attn_kernel,
        grid=(b, GQA_KV_HEADS, s // tq),
        in_specs=[pl.BlockSpec((1, tq, width), lambda i, j, t: (i, t, j)),
                  pl.BlockSpec((1, s, width), lambda i, j, t: (i, 0, 0)),
                  pl.BlockSpec((1, s, width), lambda i, j, t: (i, 0, 0))],
        out_specs=pl.BlockSpec((1, tq, width), lambda i, j, t: (i, t, j)),
        out_shape=jax.ShapeDtypeStruct((b, s, GQA_HEADS * GQA_HEAD_DIM), BF16),
        scratch_shapes=[pltpu.VMEM((s, 2 * GQA_HEAD_DIM), BF16),
                        pltpu.VMEM((GQA_HEAD_DIM + VT_ROWS_PAD, s), BF16),
                        pltpu.VMEM((2, s, tq), F32), pltpu.VMEM((group, s, tq), BF16)],
        compiler_params=_params(("parallel", "parallel", "arbitrary")),
        name="gqa_attn",
    )(q, k, v)


def _out_ln_kernel(alpha, x_ref, ma_ref, mb_ref, wa_ref, wb_ref, g_ref, b_ref, y_ref, yt_ref):
    y = alpha * x_ref[...] + _dot(ma_ref[...].astype(BF16), wa_ref[...]) \
        + _dot(mb_ref[...].astype(BF16), wb_ref[...])
    y = _layer_norm(y, g_ref[...], b_ref[...])
    y_ref[...] = y
    yt_ref[0] = y.T.astype(BF16)


def _out_ln(x2, mix_a, mix_b, col_a, col_b, w_a, w_b, g, b, alpha, seq, tm):
    t, d = x2.shape
    half = w_a.shape[0]
    nsb = seq // tm
    row = lambda i: (i, 0)
    fixed = lambda i: (0, 0)
    return pl.pallas_call(
        functools.partial(_out_ln_kernel, alpha),
        grid=(t // tm,),
        in_specs=[pl.BlockSpec((tm, d), row),
                  pl.BlockSpec((tm, half), lambda i: (i, col_a)),
                  pl.BlockSpec((tm, half), lambda i: (i, col_b)),
                  pl.BlockSpec(w_a.shape, fixed), pl.BlockSpec(w_b.shape, fixed),
                  pl.BlockSpec(g.shape, fixed), pl.BlockSpec(b.shape, fixed)],
        out_specs=[pl.BlockSpec((tm, d), row),
                   pl.BlockSpec((1, d, tm), lambda i: (i // nsb, 0, i % nsb))],
        out_shape=[jax.ShapeDtypeStruct((t, d), F32), jax.ShapeDtypeStruct((t // seq, d, seq), BF16)],
        compiler_params=_params(("parallel",)),
        name="out_ln",
    )(x2, mix_a, mix_b, w_a, w_b, g, b)


def _route_kernel(cap, x_ref, rw_ref, aff_ref, slot_ref):
    x = x_ref[0]
    seq = x.shape[0]
    xh, xm = _split2(x)
    wh, wm = _split2(rw_ref[...])
    logits = _dot_nt(wh, xh) + (_dot_nt(wh, xm) + _dot_nt(wm, xh))
    e = jnp.exp(logits - jnp.max(logits, axis=0, keepdims=True))
    aff = e / jnp.sum(e, axis=0, keepdims=True)
    aff_ref[0] = aff

    def enough(cand):
        return jnp.sum((aff >= pltpu.bitcast(cand, F32)).astype(jnp.int32), axis=1, keepdims=True) >= cap

    top = jnp.full((aff.shape[0], 1), 1 << 30, jnp.int32)
    thr0 = jnp.where(enough(top), top, 0)

    def pick(i, thr):
        lo = 28 - 2 * i
        c1, c2, c3 = thr | (jnp.int32(1) << lo), thr | (jnp.int32(2) << lo), thr | (jnp.int32(3) << lo)
        return jnp.where(enough(c3), c3, jnp.where(enough(c2), c2, jnp.where(enough(c1), c1, thr)))

    thr = pltpu.bitcast(lax.fori_loop(0, 15, pick, thr0), F32)
    above = aff > thr
    tie = aff == thr
    need = cap - jnp.sum(above.astype(jnp.int32), axis=1, keepdims=True)

    blk = 256 if seq % 256 == 0 else V7X_LANES
    r = lax.broadcasted_iota(jnp.int32, (blk, blk), 0)
    c = lax.broadcasted_iota(jnp.int32, (blk, blk), 1)
    before = jnp.where(r < c, 1.0, 0.0).astype(BF16)

    def prefix(mask):
        m = jnp.where(mask, 1.0, 0.0).astype(BF16)
        run = jnp.zeros((mask.shape[0], 1), F32)
        parts = []
        for t in range(seq // blk):
            mb = m[:, t * blk:(t + 1) * blk]
            parts.append(_dot(mb, before) + run)
            run = run + jnp.sum(mb.astype(F32), axis=1, keepdims=True)
        return jnp.concatenate(parts, axis=1).astype(jnp.int32)

    chosen = above | (tie & (prefix(tie) < need))
    slot_ref[0] = jnp.where(chosen, prefix(chosen), -1)


def _route(x1, rw_t, cap):
    b, s, d = x1.shape
    e = rw_t.shape[0]
    return pl.pallas_call(
        functools.partial(_route_kernel, cap),
        grid=(b,),
        in_specs=[pl.BlockSpec((1, s, d), lambda i: (i, 0, 0)), pl.BlockSpec((e, d), lambda i: (0, 0))],
        out_specs=[pl.BlockSpec((1, e, s), lambda i: (i, 0, 0)), pl.BlockSpec((1, e, s), lambda i: (i, 0, 0))],
        out_shape=[jax.ShapeDtypeStruct((b, e, s), F32), jax.ShapeDtypeStruct((b, e, s), jnp.int32)],
        compiler_params=_params(("parallel",)),
        name="route",
    )(x1, rw_t)


def _moe_kernel(cap, ts, xt_ref, slot_ref, aff_ref, w1_ref, w3_ref, w2_ref, o_ref):
    e_id = pl.program_id(1)
    seq = xt_ref.shape[2]
    ff = w2_ref.shape[2]

    @pl.when(e_id == 0)
    def _():
        o_ref[...] = jnp.zeros_like(o_ref)

    c_row = lax.broadcasted_iota(jnp.int32, (cap, seq), 0)
    pick = jnp.where(slot_ref[0, 0] == c_row, 1.0, 0.0).astype(BF16)
    hd2 = xt_ref.shape[1] // 2
    xg = jnp.concatenate([_dot_nt(xt_ref[0, :hd2, :], pick), _dot_nt(xt_ref[0, hd2:, :], pick)],
                         axis=0).astype(BF16)
    h1 = _dot(w1_ref[0], xg)
    h3 = _dot(w3_ref[0], xg)
    hid = (h1 * _sigmoid(h1) * h3).astype(BF16)
    ye = jnp.concatenate([_dot(w2_ref[0, :hd2, :], hid), _dot(w2_ref[0, hd2:, :], hid)],
                         axis=0).astype(BF16)
    for t in range(seq // ts):
        cols = slice(t * ts, (t + 1) * ts)
        o_ref[0, :, cols] += _dot(ye, pick[:, cols]) * aff_ref[0, 0, :, cols]


def _moe(xt, slot, aff, w1t, w3t, w2t, cap, layer):
    b, d, s = xt.shape
    e = slot.shape[1]
    ff = w2t.shape[2]
    ts = min(SCATTER_BLOCK, s)
    w_blk = lambda i, j: (layer * e + j, 0, 0)
    return pl.pallas_call(
        functools.partial(_moe_kernel, cap, ts),
        grid=(b, e),
        in_specs=[pl.BlockSpec((1, d, s), lambda i, j: (i, 0, 0)),
                  pl.BlockSpec((1, 1, 1, s), lambda i, j: (i, j, 0, 0)),
                  pl.BlockSpec((1, 1, 1, s), lambda i, j: (i, j, 0, 0)),
                  pl.BlockSpec((1, ff, d), w_blk), pl.BlockSpec((1, ff, d), w_blk),
                  pl.BlockSpec((1, d, ff), w_blk)],
        out_specs=pl.BlockSpec((1, d, s), lambda i, j: (i, 0, 0)),
        out_shape=jax.ShapeDtypeStruct((b, d, s), F32),
        compiler_params=_params(("parallel", "arbitrary")),
        name="moe",
    )(xt, slot.reshape(b, e, 1, s), aff.reshape(b, e, 1, s), w1t, w3t, w2t)


def _transpose_cast_kernel(w_ref, o_ref):
    o_ref[0] = w_ref[0].T.astype(o_ref.dtype)


def _transpose_cast(w):
    n, r, c = w.shape
    return pl.pallas_call(
        _transpose_cast_kernel,
        grid=(n,),
        in_specs=[pl.BlockSpec((1, r, c), lambda i: (i, 0, 0))],
        out_specs=pl.BlockSpec((1, c, r), lambda i: (i, 0, 0)),
        out_shape=jax.ShapeDtypeStruct((n, c, r), BF16),
        compiler_params=_params(("parallel",)),
        name="transpose_cast",
    )(w)


def _ple_ln_kernel(alpha, x_ref, f_ref, p_ref, wg_ref, bg_ref, wp_ref, g_ref, b_ref, y_ref):
    xb = x_ref[...].astype(BF16)
    pb = p_ref[...].astype(BF16)
    grp = 512
    for c in range(x_ref.shape[1] // grp):
        cols = slice(c * grp, (c + 1) * grp)
        gate = _sigmoid(_dot(xb, wg_ref[:, cols]) + bg_ref[:, cols])
        ple = gate * _dot(pb, wp_ref[:, cols])
        ffn = f_ref[0, cols, :].T
        y_ref[:, cols] = alpha * x_ref[:, cols] + ffn + ple
    y_ref[...] = _layer_norm(y_ref[...], g_ref[...], b_ref[...])


def _ple_ln(x1, ffn_t, p2, wg, bg, wp, g, b, alpha, tm, layer):
    t, d = x1.shape
    nsb = ffn_t.shape[2] // tm
    row = lambda i: (i, 0)
    p_row = lambda i: (layer * (t // tm) + i, 0)
    fixed = lambda i: (0, 0)
    full = lambda a: pl.BlockSpec(a.shape, fixed)
    return pl.pallas_call(
        functools.partial(_ple_ln_kernel, alpha),
        grid=(t // tm,),
        in_specs=[pl.BlockSpec((tm, d), row),
                  pl.BlockSpec((1, d, tm), lambda i: (i // nsb, 0, i % nsb)),
                  pl.BlockSpec((tm, p2.shape[1]), p_row), full(wg), full(bg), full(wp), full(g), full(b)],
        out_specs=pl.BlockSpec((tm, d), row),
        out_shape=jax.ShapeDtypeStruct((t, d), F32),
        compiler_params=_params(("parallel",)),
        name="ple_ln",
    )(x1, ffn_t, p2, wg, bg, wp, g, b)


def _rope_tables(seq, rot_dim, lo, width):
    rows = seq // GRID_W
    row = jnp.repeat(jnp.arange(rows, dtype=F32), GRID_W)
    col = jnp.tile(jnp.arange(GRID_W, dtype=F32), rows)
    axis_dim = rot_dim // 2
    inv = ROPE_THETA ** (-jnp.arange(0, axis_dim, 2, dtype=F32) / axis_dim)
    ang = jnp.concatenate([row[:, None] * inv, col[:, None] * inv], axis=-1)
    cos, sin = jnp.cos(ang), jnp.sin(ang)
    cos2 = jnp.concatenate([cos, cos], axis=-1)
    sin2 = jnp.concatenate([-sin, sin], axis=-1)
    if lo == 0:
        reps = width // rot_dim
        return jnp.tile(cos2, (1, reps)), jnp.tile(sin2, (1, reps))
    pad_l = jnp.ones((seq, lo), F32)
    pad_r = jnp.ones((seq, width - lo - rot_dim), F32)
    cos_t = jnp.concatenate([pad_l, cos2, pad_r], axis=-1)
    sin_t = jnp.concatenate([0 * pad_l, sin2, 0 * pad_r], axis=-1)
    return cos_t, sin_t


def _prep_even(w_in, w_uq, w_ukv, gw_f, gb_f, gw_b, gb_b):
    d = w_in.shape[0]
    offs = np.cumsum(EVEN_SPLITS)[:-1].tolist()
    c_q, c_kv, k_pe, gq, gk, gv, g_lr, gr = jnp.split(w_in, offs, axis=-1)
    z32 = jnp.zeros((d, 32), w_in.dtype)
    chunk = jnp.concatenate([g_lr, z32, k_pe, z32], axis=-1)
    w_in_p = jnp.concatenate([c_q, c_kv, chunk, gq, gk, gv, gr], axis=-1).astype(BF16)
    uq = w_uq.reshape(MLA_Q_LORA, MLA_HEADS, MLA_NOPE + MLA_ROPE)
    uq = jnp.pad(uq, ((0, 0), (0, 0), (0, MLA_PAD - MLA_NOPE - MLA_ROPE)))
    w_uq_p = uq.reshape(MLA_Q_LORA, MLA_HEADS * MLA_PAD).astype(BF16)
    ukv = w_ukv.reshape(MLA_KV_LORA, MLA_HEADS, MLA_NOPE + MLA_V)
    uk = jnp.pad(ukv[:, :, :MLA_NOPE], ((0, 0), (0, 0), (0, MLA_PAD - MLA_NOPE)))
    w_uk_p = uk.reshape(MLA_KV_LORA, MLA_HEADS * MLA_PAD).astype(BF16)
    w_uv = ukv[:, :, MLA_NOPE:].reshape(MLA_KV_LORA, MLA_HEADS * MLA_V).astype(BF16)
    dkw = GLA_HEADS * GLA_DK
    w_gate = jnp.zeros((MLA_PAD, 2 * dkw), F32)
    w_gate = w_gate.at[0:GLA_GATE_RANK, 0:dkw].set(gw_f)
    w_gate = w_gate.at[GLA_GATE_RANK:2 * GLA_GATE_RANK, dkw:].set(gw_b).astype(BF16)
    b_gate = jnp.concatenate([gb_f, gb_b])[None, :]
    return w_in_p, w_uq_p, w_uk_p, w_uv, w_gate, b_gate


def kernel(x, p, w_in_even, mla_q_norm, w_uq, mla_kv_norm, w_ukv, gla_gate_w_fwd, gla_gate_b_fwd,
           gla_gate_w_bwd, gla_gate_b_bwd, gla_norm, w_in_odd, gqa_q_norm, gqa_k_norm, w_o, ln1_g,
           ln1_b, router_w, w1, w3, w2, ple_gate_w, ple_gate_b, ple_w, ln2_g, ln2_b):
    b, s, d = x.shape
    depth = w_o.shape[0]
    t = b * s
    alpha = (2.0 * depth) ** 0.25
    cap = EC_CAPACITY_FACTOR * s // N_EXPERTS
    tm = min(ROW_BLOCK, s)
    tq = min(QUERY_BLOCK, s)
    assert s % tm == 0 and s % tq == 0 and s % GRID_W == 0 and tm % (2 * GLA_CHUNK) == 0
    assert (s // KEY_CHUNKS) % KEY_BLK == 0 and tq % V7X_LANES == 0 and s % min(SCATTER_BLOCK, s) == 0
    assert cap % V7X_BF16_SUBLANES == 0 and w1.shape[1] == N_EXPERTS and router_w.shape[2] == N_EXPERTS
    half = w_o.shape[1] // 2
    cos_a, sin_a = _rope_tables(s, MLA_ROPE, PE_LO, MLA_PAD)
    cos_c, sin_c = _rope_tables(s, GQA_HEAD_DIM, 0, GQA_KV_HEADS * GQA_HEAD_DIM)
    hw = GQA_KV_HEADS * GQA_HEAD_DIM
    head_of = np.arange(hw) // GQA_HEAD_DIM
    avg = jnp.asarray((head_of[:, None] == head_of[None, :]) / GQA_HEAD_DIM, BF16)

    n_e, ff = w1.shape[1], w1.shape[3]
    w1t = _transpose_cast(w1.reshape(depth * n_e, d, ff))
    w3t = _transpose_cast(w3.reshape(depth * n_e, d, ff))
    w2t = _transpose_cast(w2.reshape(depth * n_e, ff, d))
    p_all = p.reshape(depth * t, p.shape[-1])

    x2 = x.reshape(t, d)
    for i in range(depth):
        j = i // 2
        if i % 2 == 0:
            w_in_p, w_uq_p, w_uk_p, w_uv, w_gate, b_gate = _prep_even(
                w_in_even[j], w_uq[j], w_ukv[j], gla_gate_w_fwd[j], gla_gate_b_fwd[j],
                gla_gate_w_bwd[j], gla_gate_b_bwd[j])
            q, k, v, gq, gk, gv, la, gr = _even_proj(
                x2, w_in_p, mla_q_norm[j][None, :], w_uq_p, mla_kv_norm[j][None, :], w_uk_p, w_uv,
                w_gate, b_gate, cos_a, sin_a, s, tm)
            r3 = lambda a: a.reshape(b, s, a.shape[-1])
            o_mla = _mla_attn(r3(q), r3(k), r3(v), tq).reshape(t, -1)
            o_gla = _gla(r3(gq), r3(gk), r3(la), r3(gv), r3(gr), gla_norm[j][None, :]).reshape(t, -1)
            mix_a, mix_b, col_a, col_b = o_mla, o_gla, 0, 0
        else:
            q, k, v = _odd_proj(x2, w_in_odd[j].astype(BF16),
                                jnp.tile(gqa_q_norm[j], GQA_KV_HEADS)[None, :],
                                jnp.tile(gqa_k_norm[j], GQA_KV_HEADS)[None, :],
                                avg, cos_c, sin_c, s, tm)
            r3 = lambda a: a.reshape(b, s, a.shape[-1])
            o = _gqa_attn(r3(q), r3(k), r3(v), tq).reshape(t, -1)
            mix_a, mix_b, col_a, col_b = o, o, 0, 1
        wo = w_o[i].astype(BF16)
        x1, x1t = _out_ln(x2, mix_a, mix_b, col_a, col_b, wo[:half], wo[half:],
                          ln1_g[i][None, :], ln1_b[i][None, :], alpha, s, tm)
        aff, slot = _route(x1.reshape(b, s, d), router_w[i].T, cap)
        ffn_t = _moe(x1t, slot, aff, w1t, w3t, w2t, cap, i)
        x2 = _ple_ln(x1, ffn_t, p_all, ple_gate_w[i].astype(BF16),
                     ple_gate_b[i][None, :], ple_w[i].astype(BF16), ln2_g[i][None, :], ln2_b[i][None, :],
                     alpha, tm, i)
    return x2.reshape(b, s, d)
```

```python
import functools
import math

import jax
import jax.numpy as jnp
import numpy as np
from jax import lax
from jax.experimental import pallas as pl
from jax.experimental.pallas import tpu as pltpu

F32 = jnp.float32
BF16 = jnp.bfloat16

V7X_LANES = 128
V7X_BF16_SUBLANES = 16
V7X_VMEM_BYTES = 64 * 1024 * 1024
VMEM_LIMIT = V7X_VMEM_BYTES - 8 * 1024 * 1024

ROW_BLOCK = 1024
QUERY_BLOCK = 1024
SCATTER_BLOCK = 512

GRID_W = 64
ROPE_THETA = 10000.0
EPS = 1e-6
MLA_HEADS, MLA_Q_LORA, MLA_KV_LORA = 8, 256, 128
MLA_NOPE, MLA_ROPE, MLA_V = 64, 32, 64
GLA_HEADS, GLA_DK, GLA_DV = 4, 64, 128
GLA_GATE_RANK, GLA_GATE_NORM, GLA_CHUNK = 16, 16.0, 64
GQA_HEADS, GQA_KV_HEADS, GQA_HEAD_DIM = 16, 4, 64
N_EXPERTS, EC_CAPACITY_FACTOR = 16, 2
EVEN_SPLITS = (MLA_Q_LORA, MLA_KV_LORA, MLA_ROPE, GLA_HEADS * GLA_DK, GLA_HEADS * GLA_DK,
               GLA_HEADS * GLA_DV, 2 * GLA_GATE_RANK, GLA_HEADS * GLA_DV)
ODD_SPLITS = (GQA_HEADS * GQA_HEAD_DIM, GQA_KV_HEADS * GQA_HEAD_DIM, GQA_KV_HEADS * GQA_HEAD_DIM)
MLA_PAD = V7X_LANES
PE_LO, PE_HI = MLA_NOPE, MLA_NOPE + MLA_ROPE


def _dot(a, b):
    return jnp.dot(a, b, preferred_element_type=F32)


def _dot_nt(a, b):
    return lax.dot_general(a, b, (((1,), (1,)), ((), ())), preferred_element_type=F32)


def _dot_tn(a, b):
    return lax.dot_general(a, b, (((0,), (0,)), ((), ())), preferred_element_type=F32)


def _split2(x):
    hi = x.astype(BF16)
    lo = (x - hi.astype(F32)).astype(BF16)
    return hi, lo


def _dot_sel(sel, x):
    hi, lo = _split2(x)
    return _dot(sel, hi) + _dot(sel, lo)


def _dot_x_sel(x, sel):
    hi, lo = _split2(x)
    return _dot(hi, sel) + _dot(lo, sel)


def _params(sem):
    return pltpu.CompilerParams(dimension_semantics=sem, vmem_limit_bytes=VMEM_LIMIT)


def _rot_half(x, half):
    w = x.shape[-1]
    lane = lax.broadcasted_iota(jnp.int32, x.shape, x.ndim - 1)
    first = (lane % (2 * half)) < half
    return jnp.where(first, pltpu.roll(x, w - half, x.ndim - 1), pltpu.roll(x, half, x.ndim - 1))


def _layer_norm(y, g, b):
    mu = jnp.mean(y, axis=-1, keepdims=True)
    yc = y - mu
    var = jnp.mean(yc * yc, axis=-1, keepdims=True)
    return yc * lax.rsqrt(var + EPS) * g + b


def _rms(x, g):
    return x * lax.rsqrt(jnp.mean(x * x, axis=-1, keepdims=True) + EPS) * g


def _log_sigmoid(z):
    return jnp.minimum(z, 0.0) - jnp.log1p(jnp.exp(-jnp.abs(z)))


def _sigmoid(z):
    return 1.0 / (1.0 + jnp.exp(-z))


LOG2_E = math.log2(math.e)
KEY_BLK = 128


KEY_CHUNKS = 4


def _exp2_cols(m, s_scr, e_scr):
    seq, tq = s_scr.shape
    for c in range(tq // V7X_LANES):
        cols = slice(c * V7X_LANES, (c + 1) * V7X_LANES)
        m_b = jnp.broadcast_to(m[:, cols], (KEY_BLK, V7X_LANES))
        for r in range(seq // KEY_BLK):
            rows = slice(r * KEY_BLK, (r + 1) * KEY_BLK)
            e_scr[rows, cols] = jnp.exp2(s_scr[rows, cols] - m_b).astype(BF16)


VT_ROWS_PAD = V7X_BF16_SUBLANES


def _values_t(v_t):
    dv, seq = v_t.shape
    extra = jnp.where(lax.broadcasted_iota(jnp.int32, (VT_ROWS_PAD, seq), 0) == 0, 1.0, 0.0)
    return jnp.concatenate([v_t.astype(BF16), extra.astype(BF16)], axis=0)


SHIFT_SAMPLE = 128
DEN_LO, DEN_HI = 2.0 ** -90, 2.0 ** 90


def _attend_heads_one_pass(n_heads, k_of, q_of, vt_of, e_ref):
    seq = e_ref.shape[1]
    ck = seq // KEY_CHUNKS
    assert ck % SHIFT_SAMPLE == 0
    bounds = sorted({SHIFT_SAMPLE, *range(ck, seq + 1, ck)})
    outs, dens = [], []
    for h in range(n_heads):
        q = q_of(h)
        sample = slice(0, SHIFT_SAMPLE)
        s0 = _dot_nt(k_of(h, sample), q)
        shift = jnp.max(s0, axis=0, keepdims=True)
        e_ref[h, sample, :] = jnp.exp2(s0 - shift).astype(BF16)
        for lo, hi in zip(bounds[:-1], bounds[1:]):
            keys = slice(lo, hi)
            e_ref[h, keys, :] = jnp.exp2(_dot_nt(k_of(h, keys), q) - shift).astype(BF16)
        o = _dot(vt_of(h, slice(0, seq)), e_ref[h])
        dv = o.shape[0] - VT_ROWS_PAD
        outs.append(o[:dv] * (1.0 / o[dv:dv + 1]))
        dens.append(o[dv:dv + 1])
    return jnp.concatenate(outs, axis=0), jnp.concatenate(dens, axis=0)


def _attend_heads(n_heads, k_of, q_of, vt_of, s_ref, e_ref):
    seq = s_ref.shape[1]
    all_keys = slice(0, seq)
    outs = []
    for h in range(n_heads):
        s_scr = s_ref.at[h % 2]
        s = _dot_nt(k_of(h, all_keys), q_of(h))
        s_scr[...] = s
        _exp2_cols(jnp.max(s, axis=0, keepdims=True), s_scr, e_ref.at[h])
        o = _dot(vt_of(h, all_keys), e_ref[h])
        dv = o.shape[0] - VT_ROWS_PAD
        outs.append(o[:dv] * (1.0 / o[dv:dv + 1]))
    return jnp.concatenate(outs, axis=0)


def _attend_and_store(n_heads, k_of, q_of, vt_of, s_ref, e_ref, o_ref):
    out_t, den = _attend_heads_one_pass(n_heads, k_of, q_of, vt_of, e_ref)
    o_ref[0] = out_t.T.astype(o_ref.dtype)
    in_range = (den > DEN_LO) & (den < DEN_HI)
    n_bad = jnp.sum(jnp.where(in_range, 0.0, 1.0))

    @pl.when(n_bad > 0.0)
    def _():
        o_ref[0] = _attend_heads(n_heads, k_of, q_of, vt_of, s_ref, e_ref).T.astype(o_ref.dtype)


def _even_proj_kernel(x_ref, w_in_ref, qn_ref, w_uq_ref, kvn_ref, w_uk_ref, w_uv_ref,
                      w_gate_ref, b_gate_ref, cq_ref, sq_ref,
                      q_ref, k_ref, v_ref, gq_ref, gk_ref, gv_ref, la_ref, gr_ref):
    xb = x_ref[...].astype(BF16)
    h = _dot(xb, w_in_ref[:, 0:512])
    cos = cq_ref[...]
    sin = sq_ref[...]
    lane = lax.broadcasted_iota(jnp.int32, cos.shape, 1)
    pe_lane = (lane >= PE_LO) & (lane < PE_HI)

    def rope(t):
        return t * cos + _rot_half(t, MLA_ROPE // 2) * sin

    c_q = _rms(h[:, 0:256], qn_ref[...])
    q = _dot(c_q.astype(BF16), w_uq_ref[...]) * ((MLA_NOPE + MLA_ROPE) ** -0.5 * LOG2_E)
    c_kv = _rms(h[:, 256:384], kvn_ref[...]).astype(BF16)
    kn = _dot(c_kv, w_uk_ref[...])
    v_ref[...] = _dot(c_kv, w_uv_ref[...]).astype(v_ref.dtype)
    chunk = h[:, 384:512]
    k_pe = jnp.where(pe_lane, rope(chunk), 0.0)
    for hd in range(MLA_HEADS):
        sl = slice(hd * MLA_PAD, (hd + 1) * MLA_PAD)
        q_ref[:, sl] = rope(q[:, sl]).astype(q_ref.dtype)
        k_ref[:, sl] = (kn[:, sl] + k_pe).astype(k_ref.dtype)
    z = _dot(chunk.astype(BF16), w_gate_ref[...]) + b_gate_ref[...]
    la = _log_sigmoid(z) * (1.0 / GLA_GATE_NORM)
    blk = 2 * GLA_CHUNK
    r = lax.broadcasted_iota(jnp.int32, (blk, blk), 0)
    c = lax.broadcasted_iota(jnp.int32, (blk, blk), 1)
    same = (r // GLA_CHUNK) == (c // GLA_CHUNK)
    lower = jnp.where(same & (c <= r), 1.0, 0.0).astype(BF16)
    upper = jnp.where(same & (c >= r), 1.0, 0.0).astype(BF16)
    dkw = GLA_HEADS * GLA_DK
    for t in range(la.shape[0] // blk):
        rows = slice(t * blk, (t + 1) * blk)
        la_ref[rows, :dkw] = _dot_sel(lower, la[rows, :dkw])
        la_ref[rows, dkw:] = _dot_sel(upper, la[rows, dkw:])
    h_qk = _dot(xb, w_in_ref[:, 512:1024])
    gq_ref[...] = h_qk[:, 0:256] * (GLA_DK ** -0.5)
    gk_ref[...] = h_qk[:, 256:512]
    gv_ref[...] = _dot(xb, w_in_ref[:, 1024:1536]).astype(gv_ref.dtype)
    gr = _dot(xb, w_in_ref[:, 1536:2048])
    gr_ref[...] = gr * _sigmoid(gr)


def _even_proj(x2, w_in_p, qn, w_uq_p, kvn, w_uk_p, w_uv, w_gate, b_gate, cq, sq, seq, tm):
    t, d = x2.shape
    nsb = seq // tm
    row = lambda i: (i, 0)
    fixed = lambda i: (0, 0)
    pos = lambda i: (i % nsb, 0)
    full = lambda a: pl.BlockSpec(a.shape, fixed)
    outs = [(1024, BF16), (1024, BF16), (512, BF16), (256, F32), (256, F32), (512, BF16), (512, F32), (512, F32)]
    return pl.pallas_call(
        _even_proj_kernel,
        grid=(t // tm,),
        in_specs=[pl.BlockSpec((tm, d), row), full(w_in_p), full(qn), full(w_uq_p), full(kvn),
                  full(w_uk_p), full(w_uv), full(w_gate), full(b_gate),
                  pl.BlockSpec((tm, MLA_PAD), pos), pl.BlockSpec((tm, MLA_PAD), pos)],
        out_specs=[pl.BlockSpec((tm, n), row) for n, _ in outs],
        out_shape=[jax.ShapeDtypeStruct((t, n), dt) for n, dt in outs],
        compiler_params=_params(("parallel",)),
        name="even_proj",
    )(x2, w_in_p, qn, w_uq_p, kvn, w_uk_p, w_uv, w_gate, b_gate, cq, sq)


MLA_STEP_HEADS = 4


def _mla_attn_kernel(q_ref, k_ref, v_ref, o_ref, vt_ref, s_ref, e_ref):
    out_w = MLA_STEP_HEADS * MLA_V

    @pl.when(pl.program_id(2) == 0)
    def _():
        r = lax.broadcasted_iota(jnp.int32, (out_w, out_w), 0)
        c = lax.broadcasted_iota(jnp.int32, (out_w, out_w), 1)
        eye = jnp.where(r == c, 1.0, 0.0).astype(BF16)
        v_t = _dot_nt(eye, v_ref[0])
        for a in range(MLA_STEP_HEADS):
            vt_ref[a] = _values_t(v_t[a * MLA_V:(a + 1) * MLA_V])

    _attend_and_store(
        MLA_STEP_HEADS,
        lambda a, keys: k_ref[0, keys, a * MLA_PAD:(a + 1) * MLA_PAD],
        lambda a: q_ref[0, :, a * MLA_PAD:(a + 1) * MLA_PAD],
        lambda a, keys: vt_ref[a, :, keys],
        s_ref, e_ref, o_ref)


def _mla_attn(q, k, v, tq):
    b, s, _ = q.shape
    n = MLA_STEP_HEADS
    return pl.pallas_call(
        _mla_attn_kernel,
        grid=(b, MLA_HEADS // n, s // tq),
        in_specs=[pl.BlockSpec((1, tq, n * MLA_PAD), lambda i, j, t: (i, t, j)),
                  pl.BlockSpec((1, s, n * MLA_PAD), lambda i, j, t: (i, 0, j)),
                  pl.BlockSpec((1, s, n * MLA_V), lambda i, j, t: (i, 0, j))],
        out_specs=pl.BlockSpec((1, tq, n * MLA_V), lambda i, j, t: (i, t, j)),
        out_shape=jax.ShapeDtypeStruct((b, s, MLA_HEADS * MLA_V), BF16),
        scratch_shapes=[pltpu.VMEM((n, MLA_V + VT_ROWS_PAD, s), BF16),
                        pltpu.VMEM((2, s, tq), F32), pltpu.VMEM((n, s, tq), BF16)],
        compiler_params=_params(("parallel", "parallel", "arbitrary")),
        name="mla_attn",
    )(q, k, v)


def _gla_kernel(q_ref, k_ref, cum_ref, v_ref, gr_ref, g_ref, o_ref, state_ref, of_ref, ob_ref):
    seq = q_ref.shape[1]
    n_chunks = seq // GLA_CHUNK
    L = GLA_CHUNK
    dkw = GLA_HEADS * GLA_DK
    dvw = GLA_HEADS * GLA_DV

    def iota(shape, dim):
        return lax.broadcasted_iota(jnp.int32, shape, dim)

    row_l = iota((L, dkw), 0)
    col_m = iota((L, dkw), 1) % L
    k_own = (iota((dkw, dkw), 0) // L) == (iota((dkw, dkw), 1) // GLA_DK)
    v_own = (iota((dkw, dvw), 0) // L) == (iota((dkw, dvw), 1) // GLA_DV)
    s_own = (iota((dvw, dkw), 0) // GLA_DV) == (iota((dvw, dkw), 1) // GLA_DK)
    state_ref[...] = jnp.zeros_like(state_ref)

    def body(i, carry):
        for d, fwd in enumerate((True, False)):
            n = i if fwd else n_chunks - 1 - i
            rows = pl.ds(pl.multiple_of(n * L, L), L)
            cum = cum_ref[0, rows, d * dkw:(d + 1) * dkw]
            last = cum[L - 1:L, :] if fwd else cum[0:1, :]
            q = q_ref[0, rows, :]
            k = k_ref[0, rows, :]
            v = v_ref[0, rows, :]
            qe = (q * jnp.exp(cum)).astype(BF16)
            kg = (k * jnp.exp(-cum)).astype(BF16)
            kdec = (k * jnp.exp(last - cum)).astype(BF16)
            k_blk = jnp.where(k_own, jnp.concatenate([kg] * GLA_HEADS, axis=0), jnp.zeros((), BF16))
            att = _dot_nt(qe, k_blk)
            keep = (col_m <= row_l) if fwd else (col_m >= row_l)
            att = jnp.where(keep, att, 0.0).astype(BF16)
            v_blk = jnp.where(v_own, jnp.concatenate([v] * GLA_HEADS, axis=0), jnp.zeros((), BF16))
            st = state_ref[d]
            o = _dot(att, v_blk) + _dot_nt(qe, st.astype(BF16))
            state_ref[d] = st * jnp.exp(last) + jnp.where(s_own, _dot_tn(v, kdec), 0.0)
            if fwd:
                of_ref[rows, :] = o
            else:
                ob_ref[rows, :] = o
        return carry

    lax.fori_loop(0, n_chunks, body, 0, unroll=4)

    def finish(n, carry):
        rows = pl.ds(pl.multiple_of(n * L, L), L)
        tot = of_ref[rows, :] + ob_ref[rows, :]
        for hd in range(GLA_HEADS):
            sl = slice(hd * GLA_DV, (hd + 1) * GLA_DV)
            o_ref[0, rows, sl] = (_rms(tot[:, sl], g_ref[...]) * gr_ref[0, rows, sl]).astype(o_ref.dtype)
        return carry

    lax.fori_loop(0, n_chunks, finish, 0)


def _gla(gq, gk, cum, gv, gr, g_norm):
    b, s, _ = gq.shape
    dkw = GLA_HEADS * GLA_DK
    dvw = GLA_HEADS * GLA_DV
    blk = lambda w: pl.BlockSpec((1, s, w), lambda i: (i, 0, 0))
    return pl.pallas_call(
        _gla_kernel,
        grid=(b,),
        in_specs=[blk(dkw), blk(dkw), blk(2 * dkw), blk(dvw), blk(dvw),
                  pl.BlockSpec((1, GLA_DV), lambda i: (0, 0))],
        out_specs=blk(dvw),
        out_shape=jax.ShapeDtypeStruct((b, s, dvw), BF16),
        scratch_shapes=[pltpu.VMEM((2, dvw, dkw), F32), pltpu.VMEM((s, dvw), F32), pltpu.VMEM((s, dvw), F32)],
        compiler_params=_params(("parallel",)),
        name="gla",
    )(gq, gk, cum, gv, gr, g_norm)


def _odd_proj_kernel(x_ref, w_ref, gq_ref, gk_ref, avg_ref, cos_ref, sin_ref, q_ref, k_ref, v_ref):
    xb = x_ref[...].astype(BF16)
    cos = cos_ref[...]
    sin = sin_ref[...]
    avg = avg_ref[...]
    nq = GQA_HEADS * GQA_HEAD_DIM
    nk = GQA_KV_HEADS * GQA_HEAD_DIM

    def norm_rope(t, g):
        ms = _dot_x_sel(t * t, avg)
        t = t * lax.rsqrt(ms + EPS) * g
        return t * cos + _rot_half(t, GQA_HEAD_DIM // 2) * sin

    grp = 2 * nk
    for c in range(nq // grp):
        h = _dot(xb, w_ref[:, c * grp:(c + 1) * grp])
        for u in range(2):
            sl = slice(c * grp + u * nk, c * grp + (u + 1) * nk)
            q_ref[:, sl] = (norm_rope(h[:, u * nk:(u + 1) * nk], gq_ref[...])
                            * (GQA_HEAD_DIM ** -0.5 * LOG2_E)).astype(q_ref.dtype)
    h = _dot(xb, w_ref[:, nq:nq + grp])
    k_ref[...] = norm_rope(h[:, :nk], gk_ref[...]).astype(k_ref.dtype)
    v_ref[...] = h[:, nk:].astype(v_ref.dtype)


def _odd_proj(x2, w, gq, gk, avg, cos, sin, seq, tm):
    t, d = x2.shape
    nsb = seq // tm
    row = lambda i: (i, 0)
    fixed = lambda i: (0, 0)
    pos = lambda i: (i % nsb, 0)
    full = lambda a: pl.BlockSpec(a.shape, fixed)
    outs = [ODD_SPLITS[0], ODD_SPLITS[1], ODD_SPLITS[2]]
    return pl.pallas_call(
        _odd_proj_kernel,
        grid=(t // tm,),
        in_specs=[pl.BlockSpec((tm, d), row), full(w), full(gq), full(gk), full(avg),
                  pl.BlockSpec((tm, cos.shape[1]), pos), pl.BlockSpec((tm, sin.shape[1]), pos)],
        out_specs=[pl.BlockSpec((tm, n), row) for n in outs],
        out_shape=[jax.ShapeDtypeStruct((t, n), BF16) for n in outs],
        compiler_params=_params(("parallel",)),
        name="odd_proj",
    )(x2, w, gq, gk, avg, cos, sin)


def _gqa_attn_kernel(q_ref, k_ref, v_ref, o_ref, kp_ref, vt_ref, s_ref, e_ref):
    j = pl.program_id(1)
    group = GQA_HEADS // GQA_KV_HEADS
    width = GQA_KV_HEADS * GQA_HEAD_DIM
    pair = 2 * GQA_HEAD_DIM

    @pl.when(pl.program_id(2) == 0)
    def _():
        r = lax.broadcasted_iota(jnp.int32, (width, pair), 0)
        c = lax.broadcasted_iota(jnp.int32, (width, pair), 1)
        twice = (r // GQA_HEAD_DIM == j) & (r % GQA_HEAD_DIM == c % GQA_HEAD_DIM)
        kp_ref[...] = _dot(k_ref[0], jnp.where(twice, 1.0, 0.0).astype(BF16)).astype(BF16)
        rv = lax.broadcasted_iota(jnp.int32, (GQA_HEAD_DIM, width), 0)
        cv = lax.broadcasted_iota(jnp.int32, (GQA_HEAD_DIM, width), 1)
        mine = cv == j * GQA_HEAD_DIM + rv
        vt_ref[...] = _values_t(_dot_nt(jnp.where(mine, 1.0, 0.0).astype(BF16), v_ref[0]))

    lane = lax.broadcasted_iota(jnp.int32, (q_ref.shape[1], pair), 1)

    def q_of(g):
        qp = q_ref[0, :, (g // 2) * pair:(g // 2 + 1) * pair]
        return jnp.where(lane // GQA_HEAD_DIM == g % 2, qp, jnp.zeros_like(qp))

    _attend_and_store(group, lambda g, keys: kp_ref[keys, :], q_of, lambda g, keys: vt_ref[:, keys],
                      s_ref, e_ref, o_ref)


def _gqa_attn(q, k, v, tq):
    b, s, _ = q.shape
    width = GQA_KV_HEADS * GQA_HEAD_DIM
    group = GQA_HEADS // GQA_KV_HEADS
    return pl.pallas_call(
        _gqa_attn_kernel,
        grid=(b, GQA_KV_HEADS, s // tq),
        in_specs=[pl.BlockSpec((1, tq, width), lambda i, j, t: (i, t, j)),
                  pl.BlockSpec((1, s, width), lambda i, j, t: (i, 0, 0)),
                  pl.BlockSpec((1, s, width), lambda i, j, t: (i, 0, 0))],
        out_specs=pl.BlockSpec((1, tq, width), lambda i, j, t: (i, t, j)),
        out_shape=jax.ShapeDtypeStruct((b, s, GQA_HEADS * GQA_HEAD_DIM), BF16),
        scratch_shapes=[pltpu.VMEM((s, 2 * GQA_HEAD_DIM), BF16),
                        pltpu.VMEM((GQA_HEAD_DIM + VT_ROWS_PAD, s), BF16),
                        pltpu.VMEM((2, s, tq), F32), pltpu.VMEM((group, s, tq), BF16)],
        compiler_params=_params(("parallel", "parallel", "arbitrary")),
        name="gqa_attn",
    )(q, k, v)


def _out_ln_kernel(alpha, x_ref, ma_ref, mb_ref, wa_ref, wb_ref, g_ref, b_ref, y_ref, yt_ref):
    hr = x_ref.shape[0] // 2
    for r in range(2):
        rows = slice(r * hr, (r + 1) * hr)
        y = alpha * x_ref[rows, :] + _dot(ma_ref[rows, :].astype(BF16), wa_ref[...]) \
            + _dot(mb_ref[rows, :].astype(BF16), wb_ref[...])
        y = _layer_norm(y, g_ref[...], b_ref[...])
        y_ref[rows, :] = y
        yt_ref[0, :, rows] = y.T.astype(BF16)


def _out_ln(x2, mix_a, mix_b, col_a, col_b, w_a, w_b, g, b, alpha, seq, tm):
    t, d = x2.shape
    half = w_a.shape[0]
    nsb = seq // tm
    row = lambda i: (i, 0)
    fixed = lambda i: (0, 0)
    return pl.pallas_call(
        functools.partial(_out_ln_kernel, alpha),
        grid=(t // tm,),
        in_specs=[pl.BlockSpec((tm, d), row),
                  pl.BlockSpec((tm, half), lambda i: (i, col_a)),
                  pl.BlockSpec((tm, half), lambda i: (i, col_b)),
                  pl.BlockSpec(w_a.shape, fixed), pl.BlockSpec(w_b.shape, fixed),
                  pl.BlockSpec(g.shape, fixed), pl.BlockSpec(b.shape, fixed)],
        out_specs=[pl.BlockSpec((tm, d), row),
                   pl.BlockSpec((1, d, tm), lambda i: (i // nsb, 0, i % nsb))],
        out_shape=[jax.ShapeDtypeStruct((t, d), F32), jax.ShapeDtypeStruct((t // seq, d, seq), BF16)],
        compiler_params=_params(("parallel",)),
        name="out_ln",
    )(x2, mix_a, mix_b, w_a, w_b, g, b)


def _route_kernel(cap, x_ref, rw_ref, aff_ref, slot_ref):
    x = x_ref[0]
    seq = x.shape[0]
    xh, xm = _split2(x)
    wh, wm = _split2(rw_ref[...])
    logits = _dot_nt(wh, xh) + (_dot_nt(wh, xm) + _dot_nt(wm, xh))
    e = jnp.exp(logits - jnp.max(logits, axis=0, keepdims=True))
    aff = e / jnp.sum(e, axis=0, keepdims=True)
    aff_ref[0] = aff

    def enough(cand):
        return jnp.sum((aff >= pltpu.bitcast(cand, F32)).astype(jnp.int32), axis=1, keepdims=True) >= cap

    top = jnp.full((aff.shape[0], 1), 1 << 30, jnp.int32)
    thr0 = jnp.where(enough(top), top, 0)

    def pick(i, thr):
        lo = 28 - 2 * i
        c1, c2, c3 = thr | (jnp.int32(1) << lo), thr | (jnp.int32(2) << lo), thr | (jnp.int32(3) << lo)
        return jnp.where(enough(c3), c3, jnp.where(enough(c2), c2, jnp.where(enough(c1), c1, thr)))

    thr = pltpu.bitcast(lax.fori_loop(0, 15, pick, thr0), F32)
    above = aff > thr
    tie = aff == thr
    need = cap - jnp.sum(above.astype(jnp.int32), axis=1, keepdims=True)

    blk = 256 if seq % 256 == 0 else V7X_LANES
    r = lax.broadcasted_iota(jnp.int32, (blk, blk), 0)
    c = lax.broadcasted_iota(jnp.int32, (blk, blk), 1)
    before = jnp.where(r < c, 1.0, 0.0).astype(BF16)

    def prefix(mask):
        m = jnp.where(mask, 1.0, 0.0).astype(BF16)
        run = jnp.zeros((mask.shape[0], 1), F32)
        parts = []
        for t in range(seq // blk):
            mb = m[:, t * blk:(t + 1) * blk]
            parts.append(_dot(mb, before) + run)
            run = run + jnp.sum(mb.astype(F32), axis=1, keepdims=True)
        return jnp.concatenate(parts, axis=1).astype(jnp.int32)

    chosen = above | (tie & (prefix(tie) < need))
    slot_ref[0] = jnp.where(chosen, prefix(chosen), -1)


def _route(x1, rw_t, cap):
    b, s, d = x1.shape
    e = rw_t.shape[0]
    return pl.pallas_call(
        functools.partial(_route_kernel, cap),
        grid=(b,),
        in_specs=[pl.BlockSpec((1, s, d), lambda i: (i, 0, 0)), pl.BlockSpec((e, d), lambda i: (0, 0))],
        out_specs=[pl.BlockSpec((1, e, s), lambda i: (i, 0, 0)), pl.BlockSpec((1, e, s), lambda i: (i, 0, 0))],
        out_shape=[jax.ShapeDtypeStruct((b, e, s), F32), jax.ShapeDtypeStruct((b, e, s), jnp.int32)],
        compiler_params=_params(("parallel",)),
        name="route",
    )(x1, rw_t)


def _moe_kernel(cap, ts, xt_ref, slot_ref, aff_ref, w1_ref, w3_ref, w2_ref, o_ref):
    e_id = pl.program_id(1)
    seq = xt_ref.shape[2]
    ff = w2_ref.shape[2]

    @pl.when(e_id == 0)
    def _():
        o_ref[...] = jnp.zeros_like(o_ref)

    c_row = lax.broadcasted_iota(jnp.int32, (cap, seq), 0)
    pick = jnp.where(slot_ref[0, 0] == c_row, 1.0, 0.0).astype(BF16)
    hd2 = xt_ref.shape[1] // 2
    xg = jnp.concatenate([_dot_nt(xt_ref[0, :hd2, :], pick), _dot_nt(xt_ref[0, hd2:, :], pick)],
                         axis=0).astype(BF16)
    h1 = _dot(w1_ref[0], xg)
    h3 = _dot(w3_ref[0], xg)
    hid = (h1 * _sigmoid(h1) * h3).astype(BF16)
    ye = jnp.concatenate([_dot(w2_ref[0, :hd2, :], hid), _dot(w2_ref[0, hd2:, :], hid)],
                         axis=0).astype(BF16)
    for t in range(seq // ts):
        cols = slice(t * ts, (t + 1) * ts)
        o_ref[0, :, cols] += _dot(ye, pick[:, cols]) * aff_ref[0, 0, :, cols]


def _moe(xt, slot, aff, w1t, w3t, w2t, cap, layer):
    b, d, s = xt.shape
    e = slot.shape[1]
    ff = w2t.shape[2]
    ts = min(SCATTER_BLOCK, s)
    w_blk = lambda i, j: (layer * e + j, 0, 0)
    return pl.pallas_call(
        functools.partial(_moe_kernel, cap, ts),
        grid=(b, e),
        in_specs=[pl.BlockSpec((1, d, s), lambda i, j: (i, 0, 0)),
                  pl.BlockSpec((1, 1, 1, s), lambda i, j: (i, j, 0, 0)),
                  pl.BlockSpec((1, 1, 1, s), lambda i, j: (i, j, 0, 0)),
                  pl.BlockSpec((1, ff, d), w_blk), pl.BlockSpec((1, ff, d), w_blk),
                  pl.BlockSpec((1, d, ff), w_blk)],
        out_specs=pl.BlockSpec((1, d, s), lambda i, j: (i, 0, 0)),
        out_shape=jax.ShapeDtypeStruct((b, d, s), F32),
        compiler_params=_params(("parallel", "arbitrary")),
        name="moe",
    )(xt, slot.reshape(b, e, 1, s), aff.reshape(b, e, 1, s), w1t, w3t, w2t)


def _transpose_cast_kernel(w_ref, o_ref):
    o_ref[0] = w_ref[0].T.astype(o_ref.dtype)


def _transpose_cast(w):
    n, r, c = w.shape
    return pl.pallas_call(
        _transpose_cast_kernel,
        grid=(n,),
        in_specs=[pl.BlockSpec((1, r, c), lambda i: (i, 0, 0))],
        out_specs=pl.BlockSpec((1, c, r), lambda i: (i, 0, 0)),
        out_shape=jax.ShapeDtypeStruct((n, c, r), BF16),
        compiler_params=_params(("parallel",)),
        name="transpose_cast",
    )(w)


def _ple_ln_kernel(alpha, x_ref, f_ref, p_ref, wg_ref, bg_ref, wp_ref, g_ref, b_ref, y_ref):
    hr = x_ref.shape[0] // 2
    for r in range(2):
        rows = slice(r * hr, (r + 1) * hr)
        x = x_ref[rows, :]
        gate = _sigmoid(_dot(x.astype(BF16), wg_ref[...]) + bg_ref[...])
        ple = gate * _dot(p_ref[rows, :].astype(BF16), wp_ref[...])
        ffn = f_ref[0, :, rows].T
        y_ref[rows, :] = _layer_norm(alpha * x + ffn + ple, g_ref[...], b_ref[...])


def _ple_ln(x1, ffn_t, p2, wg, bg, wp, g, b, alpha, tm, layer):
    t, d = x1.shape
    nsb = ffn_t.shape[2] // tm
    row = lambda i: (i, 0)
    p_row = lambda i: (layer * (t // tm) + i, 0)
    fixed = lambda i: (0, 0)
    full = lambda a: pl.BlockSpec(a.shape, fixed)
    return pl.pallas_call(
        functools.partial(_ple_ln_kernel, alpha),
        grid=(t // tm,),
        in_specs=[pl.BlockSpec((tm, d), row),
                  pl.BlockSpec((1, d, tm), lambda i: (i // nsb, 0, i % nsb)),
                  pl.BlockSpec((tm, p2.shape[1]), p_row), full(wg), full(bg), full(wp), full(g), full(b)],
        out_specs=pl.BlockSpec((tm, d), row),
        out_shape=jax.ShapeDtypeStruct((t, d), F32),
        compiler_params=_params(("parallel",)),
        name="ple_ln",
    )(x1, ffn_t, p2, wg, bg, wp, g, b)


def _rope_tables(seq, rot_dim, lo, width):
    rows = seq // GRID_W
    row = jnp.repeat(jnp.arange(rows, dtype=F32), GRID_W)
    col = jnp.tile(jnp.arange(GRID_W, dtype=F32), rows)
    axis_dim = rot_dim // 2
    inv = ROPE_THETA ** (-jnp.arange(0, axis_dim, 2, dtype=F32) / axis_dim)
    ang = jnp.concatenate([row[:, None] * inv, col[:, None] * inv], axis=-1)
    cos, sin = jnp.cos(ang), jnp.sin(ang)
    cos2 = jnp.concatenate([cos, cos], axis=-1)
    sin2 = jnp.concatenate([-sin, sin], axis=-1)
    if lo == 0:
        reps = width // rot_dim
        return jnp.tile(cos2, (1, reps)), jnp.tile(sin2, (1, reps))
    pad_l = jnp.ones((seq, lo), F32)
    pad_r = jnp.ones((seq, width - lo - rot_dim), F32)
    cos_t = jnp.concatenate([pad_l, cos2, pad_r], axis=-1)
    sin_t = jnp.concatenate([0 * pad_l, sin2, 0 * pad_r], axis=-1)
    return cos_t, sin_t


def _prep_even(w_in, w_uq, w_ukv, gw_f, gb_f, gw_b, gb_b):
    d = w_in.shape[0]
    offs = np.cumsum(EVEN_SPLITS)[:-1].tolist()
    c_q, c_kv, k_pe, gq, gk, gv, g_lr, gr = jnp.split(w_in, offs, axis=-1)
    z32 = jnp.zeros((d, 32), w_in.dtype)
    chunk = jnp.concatenate([g_lr, z32, k_pe, z32], axis=-1)
    w_in_p = jnp.concatenate([c_q, c_kv, chunk, gq, gk, gv, gr], axis=-1).astype(BF16)
    uq = w_uq.reshape(MLA_Q_LORA, MLA_HEADS, MLA_NOPE + MLA_ROPE)
    uq = jnp.pad(uq, ((0, 0), (0, 0), (0, MLA_PAD - MLA_NOPE - MLA_ROPE)))
    w_uq_p = uq.reshape(MLA_Q_LORA, MLA_HEADS * MLA_PAD).astype(BF16)
    ukv = w_ukv.reshape(MLA_KV_LORA, MLA_HEADS, MLA_NOPE + MLA_V)
    uk = jnp.pad(ukv[:, :, :MLA_NOPE], ((0, 0), (0, 0), (0, MLA_PAD - MLA_NOPE)))
    w_uk_p = uk.reshape(MLA_KV_LORA, MLA_HEADS * MLA_PAD).astype(BF16)
    w_uv = ukv[:, :, MLA_NOPE:].reshape(MLA_KV_LORA, MLA_HEADS * MLA_V).astype(BF16)
    dkw = GLA_HEADS * GLA_DK
    w_gate = jnp.zeros((MLA_PAD, 2 * dkw), F32)
    w_gate = w_gate.at[0:GLA_GATE_RANK, 0:dkw].set(gw_f)
    w_gate = w_gate.at[GLA_GATE_RANK:2 * GLA_GATE_RANK, dkw:].set(gw_b).astype(BF16)
    b_gate = jnp.concatenate([gb_f, gb_b])[None, :]
    return w_in_p, w_uq_p, w_uk_p, w_uv, w_gate, b_gate


def kernel(x, p, w_in_even, mla_q_norm, w_uq, mla_kv_norm, w_ukv, gla_gate_w_fwd, gla_gate_b_fwd,
           gla_gate_w_bwd, gla_gate_b_bwd, gla_norm, w_in_odd, gqa_q_norm, gqa_k_norm, w_o, ln1_g,
           ln1_b, router_w, w1, w3, w2, ple_gate_w, ple_gate_b, ple_w, ln2_g, ln2_b):
    b, s, d = x.shape
    depth = w_o.shape[0]
    t = b * s
    alpha = (2.0 * depth) ** 0.25
    cap = EC_CAPACITY_FACTOR * s // N_EXPERTS
    tm = min(ROW_BLOCK, s)
    tq = min(QUERY_BLOCK, s)
    assert s % tm == 0 and s % tq == 0 and s % GRID_W == 0 and tm % (2 * GLA_CHUNK) == 0
    assert (s // KEY_CHUNKS) % KEY_BLK == 0 and tq % V7X_LANES == 0 and s % min(SCATTER_BLOCK, s) == 0
    assert cap % V7X_BF16_SUBLANES == 0 and w1.shape[1] == N_EXPERTS and router_w.shape[2] == N_EXPERTS
    half = w_o.shape[1] // 2
    cos_a, sin_a = _rope_tables(s, MLA_ROPE, PE_LO, MLA_PAD)
    cos_c, sin_c = _rope_tables(s, GQA_HEAD_DIM, 0, GQA_KV_HEADS * GQA_HEAD_DIM)
    hw = GQA_KV_HEADS * GQA_HEAD_DIM
    head_of = np.arange(hw) // GQA_HEAD_DIM
    avg = jnp.asarray((head_of[:, None] == head_of[None, :]) / GQA_HEAD_DIM, BF16)

    n_e, ff = w1.shape[1], w1.shape[3]
    w1t = _transpose_cast(w1.reshape(depth * n_e, d, ff))
    w3t = _transpose_cast(w3.reshape(depth * n_e, d, ff))
    w2t = _transpose_cast(w2.reshape(depth * n_e, ff, d))
    p_all = p.reshape(depth * t, p.shape[-1])

    x2 = x.reshape(t, d)
    for i in range(depth):
        j = i // 2
        if i % 2 == 0:
            w_in_p, w_uq_p, w_uk_p, w_uv, w_gate, b_gate = _prep_even(
                w_in_even[j], w_uq[j], w_ukv[j], gla_gate_w_fwd[j], gla_gate_b_fwd[j],
                gla_gate_w_bwd[j], gla_gate_b_bwd[j])
            q, k, v, gq, gk, gv, la, gr = _even_proj(
                x2, w_in_p, mla_q_norm[j][None, :], w_uq_p, mla_kv_norm[j][None, :], w_uk_p, w_uv,
                w_gate, b_gate, cos_a, sin_a, s, tm)
            r3 = lambda a: a.reshape(b, s, a.shape[-1])
            o_mla = _mla_attn(r3(q), r3(k), r3(v), tq).reshape(t, -1)
            o_gla = _gla(r3(gq), r3(gk), r3(la), r3(gv), r3(gr), gla_norm[j][None, :]).reshape(t, -1)
            mix_a, mix_b, col_a, col_b = o_mla, o_gla, 0, 0
        else:
            q, k, v = _odd_proj(x2, w_in_odd[j].astype(BF16),
                                jnp.tile(gqa_q_norm[j], GQA_KV_HEADS)[None, :],
                                jnp.tile(gqa_k_norm[j], GQA_KV_HEADS)[None, :],
                                avg, cos_c, sin_c, s, tm)
            r3 = lambda a: a.reshape(b, s, a.shape[-1])
            o = _gqa_attn(r3(q), r3(k), r3(v), tq).reshape(t, -1)
            mix_a, mix_b, col_a, col_b = o, o, 0, 1
        wo = w_o[i].astype(BF16)
        x1, x1t = _out_ln(x2, mix_a, mix_b, col_a, col_b, wo[:half], wo[half:],
                          ln1_g[i][None, :], ln1_b[i][None, :], alpha, s, tm)
        aff, slot = _route(x1.reshape(b, s, d), router_w[i].T, cap)
        ffn_t = _moe(x1t, slot, aff, w1t, w3t, w2t, cap, i)
        x2 = _ple_ln(x1, ffn_t, p_all, ple_gate_w[i].astype(BF16),
                     ple_gate_b[i][None, :], ple_w[i].astype(BF16), ln2_g[i][None, :], ln2_b[i][None, :],
                     alpha, tm, i)
    return x2.reshape(b, s, d)
```
